```python
import numpy as np
import jax
import jax.numpy as jnp
from jax import lax

D_MODEL = 1024
BATCH = 8
SEQ = 2048
DEPTH = 1

HEAD_DIM = 64
NSA_HEADS = 8
NSA_KV_HEADS = 2
NSA_GROUP = NSA_HEADS // NSA_KV_HEADS
SWA_HEADS = 8
SWA_KV_HEADS = 2
SWA_GROUP = SWA_HEADS // SWA_KV_HEADS
NSA_WIDTH = NSA_HEADS * HEAD_DIM
SWA_WIDTH = SWA_HEADS * HEAD_DIM
NSA_KV_WIDTH = NSA_KV_HEADS * HEAD_DIM
SWA_KV_WIDTH = SWA_KV_HEADS * HEAD_DIM
CMP_LEN = 32
CMP_STRIDE = 16
CMP_HIDDEN = 256
SLC_LEN = 64
SLC_TOPK = 8
NSA_WINDOW = 256
SLC_QUERY_CHUNK = 64
SWA_WINDOW = 128
BAND_BLOCK = 128
N_GROUPS = 4
EXPERTS_PER_GROUP = 8
N_EXPERTS = N_GROUPS * EXPERTS_PER_GROUP
EXPERT_TOPK = 2
EXPERT_FF = 256
MOE_BLOCK = 128
IN_SIZES = (NSA_WIDTH, NSA_KV_WIDTH, NSA_KV_WIDTH, NSA_KV_WIDTH, NSA_KV_WIDTH, NSA_KV_WIDTH, NSA_KV_WIDTH,
            3 * NSA_HEADS, SWA_WIDTH, SWA_KV_WIDTH, SWA_KV_WIDTH, 2 * D_MODEL)
IN_WIDTH = sum(IN_SIZES)
RMS_EPS = 1e-6
NEG_INF = -1e30
FORCE_SCORE = 1e9

kernel_name = "hybrid_nsa_swa_sinks_hier_moe"


def rms_norm(x, g):
    x32 = x.astype(jnp.float32)
    y = x32 * lax.rsqrt(jnp.mean(x32 * x32, axis=-1, keepdims=True) + RMS_EPS)
    return (y * g.astype(jnp.float32)).astype(x.dtype)


def masked_softmax(s, mask):
    s = jnp.where(mask, s, NEG_INF)
    m = jnp.max(s, axis=-1, keepdims=True)
    e = jnp.where(mask, jnp.exp(s - m), 0.0)
    z = jnp.sum(e, axis=-1, keepdims=True)
    return e / jnp.where(z > 0, z, 1.0)


def alibi_slopes():
    n = NSA_HEADS + SWA_HEADS
    s = 2.0 ** (-8.0 * np.arange(1, n + 1) / n)
    return jnp.asarray(s[:SWA_HEADS], jnp.float32), jnp.asarray(s[SWA_HEADS:], jnp.float32)


def compress_blocks(kv, pos, w1, w2):
    B, S, H, d = kv.shape
    r = CMP_LEN // CMP_STRIDE
    nc = S // CMP_STRIDE - r + 1
    kc = kv.reshape(B, S // CMP_STRIDE, CMP_STRIDE, H, d)
    blocks = jnp.concatenate([kc[:, i:i + nc] for i in range(r)], axis=2)
    blocks = blocks + pos[None, None, :, None, :]
    flat = jnp.transpose(blocks, (0, 1, 3, 2, 4)).reshape(B, nc, H, CMP_LEN * d)
    return jax.nn.silu(flat @ w1) @ w2


def compressed_branch(q, kc, vc, slopes):
    B, S, Hkv, G, d = q.shape
    nc = kc.shape[1]
    s = jnp.einsum('bshgd,bchd->bhgsc', q, kc, preferred_element_type=jnp.float32) * (d ** -0.5)
    end = jnp.arange(nc) * CMP_STRIDE + CMP_LEN - 1
    dist = jnp.arange(S)[:, None] - end[None, :]
    s = s - slopes.reshape(Hkv, G, 1, 1) * dist.astype(jnp.float32)
    p = masked_softmax(s, dist >= 0)
    o = jnp.einsum('bhgsc,bchd->bshgd', p.astype(vc.dtype), vc)
    return o, p


def select_blocks(p_cmp, S):
    nc = p_cmp.shape[-1]
    ns = S // SLC_LEN
    c0 = np.arange(nc) * CMP_STRIDE
    s0 = np.arange(ns) * SLC_LEN
    ov = np.clip(np.minimum(c0[:, None] + CMP_LEN, s0[None, :] + SLC_LEN)
                 - np.maximum(c0[:, None], s0[None, :]), 0, None) / CMP_LEN
    imp = jnp.einsum('bhsc,cj->bhsj', jnp.sum(p_cmp, axis=2), jnp.asarray(ov, jnp.float32))
    t = jnp.arange(S)[:, None]
    j = jnp.arange(ns)[None, :]
    cur = t // SLC_LEN
    valid = j * SLC_LEN <= t
    forced = (j == 0) | (j == cur) | (j == cur - 1)
    score = jnp.where(forced, FORCE_SCORE, jnp.where(valid, imp, NEG_INF))
    vals, idx = lax.top_k(score, min(SLC_TOPK, ns))
    return idx, vals > 0.5 * NEG_INF


def selected_branch(q, k, v, idx, valid, slopes):
    B, S, Hkv, G, d = q.shape
    ns = S // SLC_LEN
    n = idx.shape[-1]
    QC = SLC_QUERY_CHUNK
    nq = S // QC
    kb = jnp.transpose(k.reshape(B, ns, SLC_LEN, Hkv, d), (0, 3, 1, 2, 4))
    vb = jnp.transpose(v.reshape(B, ns, SLC_LEN, Hkv, d), (0, 3, 1, 2, 4))
    qc = jnp.swapaxes(q.reshape(B, nq, QC, Hkv, G, d), 0, 1)
    ic = jnp.transpose(idx.reshape(B, Hkv, nq, QC, n), (2, 0, 1, 3, 4))
    vc = jnp.transpose(valid.reshape(B, Hkv, nq, QC, n), (2, 0, 1, 3, 4))
    tc = jnp.arange(S).reshape(nq, QC)
    bi = jnp.arange(B)[:, None, None, None]
    hi = jnp.arange(Hkv)[None, :, None, None]
    sl = slopes.reshape(1, Hkv, G, 1, 1, 1)
    scale = d ** -0.5

    def chunk(args):
        qq, ii, vv, tt = args
        kg = kb[bi, hi, ii]
        vg = vb[bi, hi, ii]
        s = jnp.einsum('bqhgd,bhqnld->bhgqnl', qq, kg, preferred_element_type=jnp.float32) * scale
        kpos = ii[..., None] * SLC_LEN + jnp.arange(SLC_LEN)
        dist = tt[None, None, :, None, None] - kpos
        mask = (vv[..., None] & (dist >= 0))[:, :, None]
        s = s - sl * dist[:, :, None].astype(jnp.float32)
        p = masked_softmax(s.reshape(B, Hkv, G, QC, n * SLC_LEN), mask.reshape(B, Hkv, 1, QC, n * SLC_LEN))
        return jnp.einsum('bhgqk,bhqkd->bqhgd', p.astype(vg.dtype), vg.reshape(B, Hkv, QC, n * SLC_LEN, d))

    out = lax.map(chunk, (qc, ic, vc, tc))
    return jnp.swapaxes(out, 0, 1).reshape(B, S, Hkv, G, d)


def banded_attention(q, k, v, slopes, window, sinks):
    B, S, Hkv, G, d = q.shape
    nb = S // BAND_BLOCK
    nprev = -(-(window - 1) // BAND_BLOCK)
    pad = nprev * BAND_BLOCK
    kw = (nprev + 1) * BAND_BLOCK

    def band(a):
        ap = jnp.pad(a, ((0, 0), (pad, 0), (0, 0), (0, 0))).reshape(B, nb + nprev, BAND_BLOCK, Hkv, d)
        return jnp.concatenate([ap[:, j:j + nb] for j in range(nprev + 1)], axis=2)

    kb, vb = band(k), band(v)
    qb = q.reshape(B, nb, BAND_BLOCK, Hkv, G, d)
    s = jnp.einsum('bnqhgd,bnkhd->bnhgqk', qb, kb, preferred_element_type=jnp.float32) * (d ** -0.5)
    blk = jnp.arange(nb)[:, None] * BAND_BLOCK
    tpos = blk + jnp.arange(BAND_BLOCK)
    kpos = blk - pad + jnp.arange(kw)
    dist = tpos[:, :, None] - kpos[:, None, :]
    mask = ((dist >= 0) & (dist < window) & (kpos[:, None, :] >= 0))[None, :, None, None]
    s = s - slopes.reshape(1, 1, Hkv, G, 1, 1) * dist.astype(jnp.float32)[None, :, None, None]
    s = jnp.where(mask, s, NEG_INF)
    m = jnp.max(s, axis=-1, keepdims=True)
    if sinks is not None:
        sk = sinks.astype(jnp.float32).reshape(1, 1, Hkv, G, 1, 1)
        m = jnp.maximum(m, sk)
    e = jnp.where(mask, jnp.exp(s - m), 0.0)
    z = jnp.sum(e, axis=-1, keepdims=True)
    if sinks is not None:
        z = z + jnp.exp(sk - m)
    p = e / z
    o = jnp.einsum('bnhgqk,bnkhd->bnqhgd', p.astype(v.dtype), vb)
    return o.reshape(B, S, Hkv, G, d)


def hier_moe(h, w_group, b_group, w_expert, b_expert, w_gate_e, w_up_e, w_down_e):
    B, S, D = h.shape
    N = B * S
    hf = h.reshape(N, D)
    g_logits = (hf @ w_group).astype(jnp.float32) + b_group.astype(jnp.float32)
    g_prob = jax.nn.softmax(g_logits, axis=-1)
    grp = jnp.argmax(g_logits, axis=-1)
    p_grp = jnp.take_along_axis(g_prob, grp[:, None], axis=-1)[:, 0]
    e_logits = ((hf @ w_expert).astype(jnp.float32) + b_expert.astype(jnp.float32)).reshape(N, N_GROUPS, EXPERTS_PER_GROUP)
    e_in = jnp.take_along_axis(e_logits, grp[:, None, None], axis=1)[:, 0]
    top_v, top_i = lax.top_k(e_in, EXPERT_TOPK)
    weight = p_grp[:, None] * jax.nn.softmax(top_v, axis=-1)
    eid = (grp[:, None] * EXPERTS_PER_GROUP + top_i).reshape(-1)
    w_f = weight.reshape(-1)
    NK = N * EXPERT_TOPK
    tok = jnp.arange(NK) // EXPERT_TOPK
    order = jnp.argsort(eid)
    e_sorted = eid[order]
    counts = jnp.bincount(eid, length=N_EXPERTS)
    padded = (counts + MOE_BLOCK - 1) // MOE_BLOCK * MOE_BLOCK
    pad_end = jnp.cumsum(padded)
    pad_start = pad_end - padded
    cnt_start = jnp.cumsum(counts) - counts
    dest = pad_start[e_sorted] + jnp.arange(NK) - cnt_start[e_sorted]
    n_blocks = -(-(NK + N_EXPERTS * (MOE_BLOCK - 1)) // MOE_BLOCK)
    buf = jnp.zeros((n_blocks * MOE_BLOCK, D), h.dtype).at[dest].set(hf[tok[order]])
    block_e = jnp.minimum(jnp.searchsorted(pad_end, jnp.arange(n_blocks) * MOE_BLOCK, side='right'), N_EXPERTS - 1)

    def expert_block(args):
        xb, e = args
        return (jax.nn.silu(xb @ w_gate_e[e]) * (xb @ w_up_e[e])) @ w_down_e[e]

    yb = lax.map(expert_block, (buf.reshape(n_blocks, MOE_BLOCK, D), block_e)).reshape(-1, D)
    y_slot = yb[dest] * w_f[order][:, None].astype(yb.dtype)
    y = jnp.zeros((N, D), h.dtype).at[tok[order]].add(y_slot)
    return y.reshape(B, S, D)


def setup_inputs(seed: int = 0) -> dict:
    key = jax.random.key(seed)
    ks = jax.random.split(key, 24)
    f32 = jnp.float32
    L = DEPTH
    d = HEAD_DIM

    def nrm(k, shape, scale):
        return jax.random.normal(k, shape, f32) * scale

    return {
        "x": nrm(ks[0], (BATCH, SEQ, D_MODEL), 1.0),
        "norm_mix_g": 1.0 + nrm(ks[1], (L, D_MODEL), 0.02),
        "w_in": nrm(ks[2], (L, D_MODEL, IN_WIDTH), D_MODEL ** -0.5),
        "cmp_pos_k": nrm(ks[3], (L, CMP_LEN, d), 0.1),
        "cmp_w1_k": nrm(ks[4], (L, CMP_LEN * d, CMP_HIDDEN), (CMP_LEN * d) ** -0.5),
        "cmp_w2_k": nrm(ks[5], (L, CMP_HIDDEN, d), CMP_HIDDEN ** -0.5),
        "cmp_pos_v": nrm(ks[6], (L, CMP_LEN, d), 0.1),
        "cmp_w1_v": nrm(ks[7], (L, CMP_LEN * d, CMP_HIDDEN), (CMP_LEN * d) ** -0.5),
        "cmp_w2_v": nrm(ks[8], (L, CMP_HIDDEN, d), CMP_HIDDEN ** -0.5),
        "sinks": nrm(ks[9], (L, SWA_HEADS), 0.5),
        "w_a": nrm(ks[10], (L, NSA_WIDTH, D_MODEL), NSA_WIDTH ** -0.5),
        "w_b": nrm(ks[11], (L, SWA_WIDTH, D_MODEL), SWA_WIDTH ** -0.5),
        "w_o": nrm(ks[12], (L, D_MODEL, D_MODEL), D_MODEL ** -0.5),
        "norm_ffn_g": 1.0 + nrm(ks[13], (L, D_MODEL), 0.02),
        "w_group": nrm(ks[14], (L, D_MODEL, N_GROUPS), D_MODEL ** -0.5),
        "b_group": nrm(ks[15], (L, N_GROUPS), 0.01),
        "w_expert": nrm(ks[16], (L, D_MODEL, N_EXPERTS), D_MODEL ** -0.5),
        "b_expert": nrm(ks[17], (L, N_EXPERTS), 0.01),
        "w_gate_e": nrm(ks[18], (L, N_EXPERTS, D_MODEL, EXPERT_FF), D_MODEL ** -0.5),
        "w_up_e": nrm(ks[19], (L, N_EXPERTS, D_MODEL, EXPERT_FF), D_MODEL ** -0.5),
        "w_down_e": nrm(ks[20], (L, N_EXPERTS, EXPERT_FF, D_MODEL), EXPERT_FF ** -0.5),
        "norm_final_g": 1.0 + nrm(ks[21], (D_MODEL,), 0.02),
    }


def reference(x, norm_mix_g, w_in, cmp_pos_k, cmp_w1_k, cmp_w2_k, cmp_pos_v, cmp_w1_v, cmp_w2_v,
              sinks, w_a, w_b, w_o, norm_ffn_g, w_group, b_group, w_expert, b_expert,
              w_gate_e, w_up_e, w_down_e, norm_final_g):
    B, S, _ = x.shape
    slopes_swa, slopes_nsa = alibi_slopes()
    split_points = [int(c) for c in np.cumsum(IN_SIZES)[:-1]]
    for l in range(DEPTH):
        h = rms_norm(x, norm_mix_g[l])
        proj = h @ w_in[l]
        (q_a, k_cmp, v_cmp, k_slc, v_slc, k_win, v_win, g_nsa,
         q_b, k_b, v_b, g_merge) = jnp.split(proj, split_points, axis=-1)
        q_a = q_a.reshape(B, S, NSA_KV_HEADS, NSA_GROUP, HEAD_DIM)
        kv_a = [t.reshape(B, S, NSA_KV_HEADS, HEAD_DIM) for t in (k_cmp, v_cmp, k_slc, v_slc, k_win, v_win)]
        kc = compress_blocks(kv_a[0], cmp_pos_k[l], cmp_w1_k[l], cmp_w2_k[l])
        vc = compress_blocks(kv_a[1], cmp_pos_v[l], cmp_w1_v[l], cmp_w2_v[l])
        o_cmp, p_cmp = compressed_branch(q_a, kc, vc, slopes_nsa)
        idx, valid = select_blocks(p_cmp, S)
        o_slc = selected_branch(q_a, kv_a[2], kv_a[3], idx, valid, slopes_nsa)
        o_win = banded_attention(q_a, kv_a[4], kv_a[5], slopes_nsa, NSA_WINDOW, None)
        g = jax.nn.sigmoid(g_nsa).reshape(B, S, NSA_KV_HEADS, NSA_GROUP, 3)
        o_a = (g[..., 0:1] * o_cmp + g[..., 1:2] * o_slc + g[..., 2:3] * o_win).reshape(B, S, NSA_WIDTH)
        q_b = q_b.reshape(B, S, SWA_KV_HEADS, SWA_GROUP, HEAD_DIM)
        k_b = k_b.reshape(B, S, SWA_KV_HEADS, HEAD_DIM)
        v_b = v_b.reshape(B, S, SWA_KV_HEADS, HEAD_DIM)
        o_b = banded_attention(q_b, k_b, v_b, slopes_swa, SWA_WINDOW, sinks[l]).reshape(B, S, SWA_WIDTH)
        gate_a, gate_b = jnp.split(jax.nn.sigmoid(g_merge), 2, axis=-1)
        mix = (gate_a * (o_a @ w_a[l]) + gate_b * (o_b @ w_b[l])) @ w_o[l]
        x = x + mix
        x = x + hier_moe(rms_norm(x, norm_ffn_g[l]), w_group[l], b_group[l], w_expert[l], b_expert[l],
                         w_gate_e[l], w_up_e[l], w_down_e[l])
    return rms_norm(x, norm_final_g)
```

```python
import functools

import numpy as np
import jax
import jax.numpy as jnp
from jax import lax
from jax.experimental import pallas as pl
from jax.experimental.pallas import tpu as pltpu

F32 = jnp.float32
BF16 = jnp.bfloat16

D_MODEL = 1024
HEAD_DIM = 64
N_HEADS = 8
N_KV = 2
GROUP = N_HEADS // N_KV
CMP_LEN = 32
CMP_STRIDE = 16
CMP_HIDDEN = 256
SLC_LEN = 64
SLC_TOPK = 8
NSA_WINDOW = 256
SWA_WINDOW = 128
N_GROUPS = 4
EPG = 8
N_EXPERTS = N_GROUPS * EPG
EXPERT_FF = 256
RMS_EPS = 1e-6
NEG_INF = -1e30
FORCE_SCORE = 1e9

LANES = 128
TQ = 128
PROJ_W = 2176
COL_KV = 1024
COL_GN = 2048
VMEM_LIMIT = 56 * 1024 * 1024


def _alibi_slopes():
    n = 2 * N_HEADS
    s = 2.0 ** (-8.0 * np.arange(1, n + 1) / n)
    return [float(v) for v in s[:N_HEADS]], [float(v) for v in s[N_HEADS:]]


SLOPES_SWA, SLOPES_NSA = _alibi_slopes()


def _rms(x, g):
    return x * lax.rsqrt(jnp.mean(x * x, axis=-1, keepdims=True) + RMS_EPS) * g


def _dot(a, b):
    return jnp.dot(a, b, preferred_element_type=F32)


def _dot_nt(a, b):
    return lax.dot_general(a, b, (((1,), (1,)), ((), ())), preferred_element_type=F32)


def _proj_kernel(x_ref, g_ref, w_ref, o_ref):
    h = _rms(x_ref[...], g_ref[...]).astype(BF16)
    o_ref[...] = _dot(h, w_ref[...]).astype(o_ref.dtype)


def _proj(x2d, g, w, tm):
    n = x2d.shape[0]
    return pl.pallas_call(
        _proj_kernel,
        grid=(n // tm,),
        in_specs=[pl.BlockSpec((tm, D_MODEL), lambda i: (i, 0)),
                  pl.BlockSpec((1, D_MODEL), lambda i: (0, 0)),
                  pl.BlockSpec((D_MODEL, PROJ_W), lambda i: (0, 0))],
        out_specs=pl.BlockSpec((tm, PROJ_W), lambda i: (i, 0)),
        out_shape=jax.ShapeDtypeStruct((n, PROJ_W), BF16),
        compiler_params=pltpu.CompilerParams(dimension_semantics=("parallel",),
                                             vmem_limit_bytes=VMEM_LIMIT),
        name="proj",
    )(x2d, g, w)


def _compress_kernel(r_ref, pa_ref, pb_ref, w1a_ref, w1b_ref, w2_ref, o_ref):
    for h in range(N_KV):
        r = r_ref[0, 0, h].astype(F32)
        a = _dot((r + pa_ref[0]).astype(BF16), w1a_ref[0])
        b = _dot((r + pb_ref[0]).astype(BF16), w1b_ref[0])
        nrow = b.shape[0]
        hid = a + pltpu.roll(b, nrow - 1, 0)
        hid = hid * jax.nn.sigmoid(hid)
        o_ref[0, 0, :, h * HEAD_DIM:(h + 1) * HEAD_DIM] = _dot(hid.astype(BF16), w2_ref[0]).astype(o_ref.dtype)


def _compress(r, pos_a, pos_b, w1a, w1b, w2):
    bsz, _, _, nch, width = r.shape
    wspec = lambda shape: pl.BlockSpec((1,) + shape, lambda b, j: (j, 0, 0))
    return pl.pallas_call(
        _compress_kernel,
        grid=(bsz, 2),
        in_specs=[pl.BlockSpec((1, 1, N_KV, nch, width), lambda b, j: (b, j, 0, 0, 0)),
                  wspec((1, width)), wspec((1, width)),
                  wspec((width, CMP_HIDDEN)), wspec((width, CMP_HIDDEN)),
                  wspec((CMP_HIDDEN, HEAD_DIM))],
        out_specs=pl.BlockSpec((1, 1, nch, N_KV * HEAD_DIM), lambda b, j: (b, j, 0, 0)),
        out_shape=jax.ShapeDtypeStruct((bsz, 2, nch, N_KV * HEAD_DIM), BF16),
        compiler_params=pltpu.CompilerParams(dimension_semantics=("parallel", "parallel"),
                                             vmem_limit_bytes=VMEM_LIMIT),
        name="compress",
    )(r, pos_a, pos_b, w1a, w1b, w2)


def _softmax_pv(s, mask, v, sink=None):
    m = jnp.max(s, axis=-1, keepdims=True)
    if sink is not None:
        m = jnp.maximum(m, sink)
    e = jnp.where(mask, jnp.exp(s - m), 0.0)
    z = jnp.sum(e, axis=-1, keepdims=True)
    if sink is not None:
        z = z + jnp.exp(sink - m)
    return _dot((e / z).astype(BF16), v)


def _attn_kernel(sinks_ref, qa_ref, qb_ref, kc_ref, vc_ref, ksl_ref, vsl_ref, kw_ref, vw_ref,
                 kb_ref, vb_ref, gn_ref, ov_ref, oa_ref, ob_ref, *, n_cmp):
    qi = pl.program_id(1)
    q0 = qi * TQ
    t_col = q0 + lax.broadcasted_iota(jnp.int32, (TQ, 1), 0)
    lane = lax.broadcasted_iota(jnp.int32, (1, LANES), 1)
    gates = jax.nn.sigmoid(gn_ref[0].astype(F32))

    def head_cols(h):
        return slice(h * HEAD_DIM, (h + 1) * HEAD_DIM)

    for h in range(N_KV):
        heads = [h * GROUP + g for g in range(GROUP)]
        qs = [qa_ref[0, :, head_cols(hh)] for hh in heads]
        q_all = jnp.concatenate(qs, axis=0)

        kc = kc_ref[0, 0, :, head_cols(h)]
        vc = vc_ref[0, 0, :, head_cols(h)]
        end_c = lane * CMP_STRIDE + (CMP_LEN - 1)
        cmask = (t_col >= end_c) & (lane < n_cmp)
        s_all = _dot_nt(q_all, kc)
        o_cmp = []
        psum = jnp.zeros((TQ, LANES), F32)
        for g in range(GROUP):
            s = s_all[g * TQ:(g + 1) * TQ] + SLOPES_NSA[heads[g]] * end_c.astype(F32)
            s = jnp.where(cmask, s, NEG_INF)
            m = jnp.max(s, axis=-1, keepdims=True)
            e = jnp.where(cmask, jnp.exp(s - m), 0.0)
            z = jnp.sum(e, axis=-1, keepdims=True)
            p = e / jnp.where(z > 0, z, 1.0)
            o_cmp.append(_dot(p.astype(BF16), vc))
            psum = psum + p

        p_hi = psum.astype(BF16)
        p_lo = (psum - p_hi.astype(F32)).astype(BF16)
        imp = _dot(p_hi, ov_ref[...]) + _dot(p_lo, ov_ref[...])
        cur = lax.shift_right_logical(t_col, int(np.log2(SLC_LEN)))
        valid = (lane * SLC_LEN <= t_col) & (lane < LANES // 4)
        forced = (lane == 0) | (lane == cur) | (lane == cur - 1)
        score = jnp.where(forced, FORCE_SCORE, jnp.where(valid, imp, NEG_INF))
        rank = jnp.zeros((TQ, LANES), F32)
        for i in range(LANES // 4):
            si = score[:, i:i + 1]
            ahead = (si > score) | ((si == score) & (lane > i))
            rank = rank + jnp.where(ahead, 1.0, 0.0)
        sel = (rank < SLC_TOPK) & (score > 0.5 * NEG_INF)
        sel_b = jnp.where(sel, 1.0, 0.0).astype(BF16)

        blk_row = lax.broadcasted_iota(jnp.int32, (LANES, TQ), 0)
        key_blk = lax.broadcasted_iota(jnp.int32, (LANES, TQ), 1) // SLC_LEN

        def slc_step(c, carry):
            m_i, l_i, acc = carry
            start = pl.multiple_of(c * TQ, TQ)
            k = ksl_ref[0, pl.ds(start, TQ), :][:, head_cols(h)]
            v = vsl_ref[0, pl.ds(start, TQ), :][:, head_cols(h)]
            expand = jnp.where(blk_row == key_blk + c * (TQ // SLC_LEN), 1.0, 0.0).astype(BF16)
            kpos = c * TQ + lane
            mask = (_dot(sel_b, expand) > 0.5) & (kpos <= t_col)
            s_all = _dot_nt(q_all, k)
            kposf = kpos.astype(F32)
            m_new, l_new, es, alphas = [], [], [], []
            for g in range(GROUP):
                rows = slice(g * TQ, (g + 1) * TQ)
                s = jnp.where(mask, s_all[rows] + SLOPES_NSA[heads[g]] * kposf, NEG_INF)
                mg = jnp.maximum(m_i[rows], jnp.max(s, axis=-1, keepdims=True))
                alpha = jnp.exp(m_i[rows] - mg)
                e = jnp.where(mask, jnp.exp(s - mg), 0.0)
                m_new.append(mg)
                l_new.append(alpha * l_i[rows] + jnp.sum(e, axis=-1, keepdims=True))
                es.append(e.astype(BF16))
                alphas.append(alpha)
            pv = _dot(jnp.concatenate(es, axis=0), v)
            acc = jnp.concatenate(alphas, axis=0) * acc + pv
            return jnp.concatenate(m_new, axis=0), jnp.concatenate(l_new, axis=0), acc

        init = (jnp.full((GROUP * TQ, 1), NEG_INF, F32), jnp.zeros((GROUP * TQ, 1), F32),
                jnp.zeros((GROUP * TQ, HEAD_DIM), F32))
        _, l_f, acc_f = lax.fori_loop(0, qi + 1, slc_step, init)
        o_slc_all = acc_f / l_f

        nw = NSA_WINDOW // TQ + 1
        start = pl.multiple_of(jnp.maximum(qi - (nw - 1), 0) * TQ, TQ)
        kw = kw_ref[0, pl.ds(start, nw * TQ), :][:, head_cols(h)]
        vw = vw_ref[0, pl.ds(start, nw * TQ), :][:, head_cols(h)]
        kpos = start + lax.broadcasted_iota(jnp.int32, (1, nw * TQ), 1)
        dist = t_col - kpos
        wmask = (dist >= 0) & (dist < NSA_WINDOW)
        s_all = _dot_nt(q_all, kw)
        for g in range(GROUP):
            hh = heads[g]
            s = jnp.where(wmask, s_all[g * TQ:(g + 1) * TQ] + SLOPES_NSA[hh] * kpos.astype(F32), NEG_INF)
            o_win = _softmax_pv(s, wmask, vw)
            o_slc = o_slc_all[g * TQ:(g + 1) * TQ]
            o = (gates[:, 3 * hh:3 * hh + 1] * o_cmp[g] + gates[:, 3 * hh + 1:3 * hh + 2] * o_slc
                 + gates[:, 3 * hh + 2:3 * hh + 3] * o_win)
            oa_ref[0, :, head_cols(hh)] = o.astype(oa_ref.dtype)

    nb = (SWA_WINDOW - 1 + TQ - 1) // TQ + 1
    start = pl.multiple_of(jnp.maximum(qi - (nb - 1), 0) * TQ, TQ)
    kpos = start + lax.broadcasted_iota(jnp.int32, (1, nb * TQ), 1)
    dist = t_col - kpos
    bmask = (dist >= 0) & (dist < SWA_WINDOW)
    for h in range(N_KV):
        kb = kb_ref[0, pl.ds(start, nb * TQ), :][:, head_cols(h)]
        vb = vb_ref[0, pl.ds(start, nb * TQ), :][:, head_cols(h)]
        for g in range(GROUP):
            hh = h * GROUP + g
            q = qb_ref[0, :, head_cols(hh)]
            s = jnp.where(bmask, _dot_nt(q, kb) - SLOPES_SWA[hh] * dist.astype(F32), NEG_INF)
            ob_ref[0, :, head_cols(hh)] = _softmax_pv(s, bmask, vb, sink=sinks_ref[hh]).astype(ob_ref.dtype)


def _attention(proj3, kvc, ov, sinks):
    bsz, seq, _ = proj3.shape
    n_cmp = seq // CMP_STRIDE - CMP_LEN // CMP_STRIDE + 1
    width = N_HEADS * HEAD_DIM
    kvw = N_KV * HEAD_DIM
    qspec = lambda col: pl.BlockSpec((1, TQ, width), lambda b, q, s: (b, q, col))
    kvspec = lambda col: pl.BlockSpec((1, seq, kvw), lambda b, q, s: (b, 0, col))
    cspec = lambda j: pl.BlockSpec((1, 1, kvc.shape[2], kvw), lambda b, q, s: (b, j, 0, 0))
    kv0 = COL_KV // kvw
    in_specs = [qspec(0), qspec(1), cspec(0), cspec(1)]
    in_specs += [kvspec(kv0 + 2 + i) for i in range(4)]
    in_specs += [kvspec(kv0 + 6), kvspec(kv0 + 7)]
    in_specs += [pl.BlockSpec((1, TQ, LANES), lambda b, q, s: (b, q, COL_GN // LANES)),
                 pl.BlockSpec((LANES, LANES), lambda b, q, s: (0, 0))]
    ospec = pl.BlockSpec((1, TQ, width), lambda b, q, s: (b, q, 0))
    return pl.pallas_call(
        functools.partial(_attn_kernel, n_cmp=n_cmp),
        grid_spec=pltpu.PrefetchScalarGridSpec(
            num_scalar_prefetch=1, grid=(bsz, seq // TQ), in_specs=in_specs, out_specs=[ospec, ospec]),
        out_shape=[jax.ShapeDtypeStruct((bsz, seq, width), BF16)] * 2,
        compiler_params=pltpu.CompilerParams(dimension_semantics=("parallel", "parallel"),
                                             vmem_limit_bytes=VMEM_LIMIT),
        name="attn",
    )(sinks, proj3, proj3, kvc, kvc, proj3, proj3, proj3, proj3, proj3, proj3, proj3, ov)


def _merge_kernel(x_ref, oa_ref, ob_ref, g1_ref, g2_ref, wg_ref, wa_ref, wb_ref, wo_ref,
                  wrh_ref, wrl_ref, br_ref, x2_ref, hn_ref, route_ref, cnt_ref):
    tm = x_ref.shape[0]

    @pl.when(pl.program_id(0) == 0)
    def _():
        cnt_ref[...] = jnp.zeros_like(cnt_ref)

    x = x_ref[...]
    h = _rms(x, g1_ref[...]).astype(BF16)
    gm = jax.nn.sigmoid(_dot(h, wg_ref[...]))
    a = _dot(oa_ref[...], wa_ref[...])
    b = _dot(ob_ref[...], wb_ref[...])
    mixin = gm[:, :D_MODEL] * a + gm[:, D_MODEL:] * b
    x2 = x + _dot(mixin.astype(BF16), wo_ref[...])
    x2_ref[...] = x2
    hn = _rms(x2, g2_ref[...])
    hn_b = hn.astype(BF16)
    hn_ref[...] = hn

    hn_lo = (hn - hn_b.astype(F32)).astype(BF16)
    logits = (_dot(hn_b, wrh_ref[...]) + _dot(hn_lo, wrh_ref[...]) + _dot(hn_b, wrl_ref[...])
              + br_ref[...])
    lane = lax.broadcasted_iota(jnp.int32, (1, LANES), 1)
    lanef = lane.astype(F32)
    big = float(LANES)
    is_g = (lane >= N_EXPERTS) & (lane < N_EXPERTS + N_GROUPS)
    gl = jnp.where(is_g, logits, NEG_INF)
    gmax = jnp.max(gl, axis=-1, keepdims=True)
    grp = jnp.min(jnp.where(gl == gmax, lanef, big), axis=-1, keepdims=True) - N_EXPERTS
    p_grp = 1.0 / jnp.sum(jnp.where(is_g, jnp.exp(gl - gmax), 0.0), axis=-1, keepdims=True)
    in_grp = (lanef >= grp * EPG) & (lanef < grp * EPG + EPG)
    el = jnp.where(in_grp, logits, NEG_INF)
    v0 = jnp.max(el, axis=-1, keepdims=True)
    i0 = jnp.min(jnp.where(el == v0, lanef, big), axis=-1, keepdims=True)
    el1 = jnp.where(lanef == i0, NEG_INF, el)
    v1 = jnp.max(el1, axis=-1, keepdims=True)
    i1 = jnp.min(jnp.where(el1 == v1, lanef, big), axis=-1, keepdims=True)
    e1 = jnp.exp(v1 - v0)
    w0 = p_grp / (1.0 + e1)
    w1 = p_grp * e1 / (1.0 + e1)

    oh0 = jnp.where(lanef == i0, 1.0, 0.0)
    oh1 = jnp.where(lanef == i1, 1.0, 0.0)
    oh = oh0 + oh1
    r_i = lax.broadcasted_iota(jnp.int32, (tm, tm), 0)
    c_i = lax.broadcasted_iota(jnp.int32, (tm, tm), 1)
    lower = jnp.where(c_i < r_i, 1.0, 0.0).astype(BF16)
    before = cnt_ref[...] + _dot(lower, oh.astype(BF16))
    rank0 = jnp.sum(oh0 * before, axis=-1, keepdims=True)
    rank1 = jnp.sum(oh1 * before, axis=-1, keepdims=True)
    cnt_ref[...] = cnt_ref[...] + jnp.sum(oh, axis=0, keepdims=True)

    route = jnp.where(lane == 0, i0, jnp.where(lane == 1, i1, jnp.where(lane == 2, rank0,
            jnp.where(lane == 3, rank1, jnp.where(lane == 4, w0, jnp.where(lane == 5, w1, 0.0))))))
    route_ref[...] = route


def _merge(x2d, oa, ob, g1, g2, wg, wa, wb, wo, wrh, wrl, br, tm):
    n = x2d.shape[0]
    width = N_HEADS * HEAD_DIM
    row = lambda w: pl.BlockSpec((tm, w), lambda i: (i, 0))
    full = lambda a: pl.BlockSpec(a.shape, lambda i: (0, 0))
    return pl.pallas_call(
        _merge_kernel,
        grid=(n // tm,),
        in_specs=[row(D_MODEL), row(width), row(width), full(g1), full(g2), full(wg), full(wa), full(wb),
                  full(wo), full(wrh), full(wrl), full(br)],
        out_specs=[row(D_MODEL), row(D_MODEL), row(LANES), pl.BlockSpec((1, LANES), lambda i: (0, 0))],
        out_shape=[jax.ShapeDtypeStruct((n, D_MODEL), F32), jax.ShapeDtypeStruct((n, D_MODEL), F32),
                   jax.ShapeDtypeStruct((n, LANES), F32), jax.ShapeDtypeStruct((1, LANES), F32)],
        compiler_params=pltpu.CompilerParams(dimension_semantics=("arbitrary",),
                                             vmem_limit_bytes=VMEM_LIMIT),
        name="merge",
    )(x2d, oa, ob, g1, g2, wg, wa, wb, wo, wrh, wrl, br)


def _dispatch_kernel(dest_ref, hn_ref, buf_in_ref, buf_ref, sem, *, td):
    del buf_in_ref
    base = pl.program_id(0) * td

    def row_copy(t, d):
        return pltpu.make_async_copy(hn_ref.at[pl.ds(t, 1)], buf_ref.at[pl.ds(d, 1)], sem)

    def issue(i, c):
        row_copy(base + i, dest_ref[0, 0, 2 * i]).start()
        row_copy(base + i, dest_ref[0, 0, 2 * i + 1]).start()
        return c

    lax.fori_loop(0, td, issue, 0)

    def drain(i, c):
        row_copy(0, 0).wait()
        return c

    lax.fori_loop(0, 2 * td, drain, 0)


def _dispatch(dest2d, hn, buf0, td):
    n = hn.shape[0]
    return pl.pallas_call(
        functools.partial(_dispatch_kernel, td=td),
        grid=(n // td,),
        in_specs=[pl.BlockSpec((1, 1, 2 * td), lambda i: (i, 0, 0), memory_space=pltpu.SMEM),
                  pl.BlockSpec(memory_space=pl.ANY), pl.BlockSpec(memory_space=pl.ANY)],
        out_specs=pl.BlockSpec(memory_space=pl.ANY),
        out_shape=jax.ShapeDtypeStruct(buf0.shape, buf0.dtype),
        scratch_shapes=[pltpu.SemaphoreType.DMA(())],
        input_output_aliases={2: 0},
        compiler_params=pltpu.CompilerParams(dimension_semantics=("arbitrary",)),
        name="dispatch",
    )(dest2d, hn, buf0)


def _expert_kernel(be_ref, nu_ref, xb_ref, wg_ref, wu_ref, wd_ref, y_ref):
    del be_ref

    @pl.when(pl.program_id(0) < nu_ref[0])
    def _():
        xb = xb_ref[...].astype(BF16)
        g = _dot(xb, wg_ref[0])
        u = _dot(xb, wu_ref[0])
        y_ref[...] = _dot((g * jax.nn.sigmoid(g) * u).astype(BF16), wd_ref[0])

    @pl.when(pl.program_id(0) >= nu_ref[0])
    def _():
        y_ref[...] = jnp.zeros_like(y_ref)


def _experts(block_e, n_used, buf, wg, wu, wd, tb):
    nblk = buf.shape[0] // tb
    live = lambda b, be, nu: jnp.minimum(b, nu[0] - 1)
    return pl.pallas_call(
        _expert_kernel,
        grid_spec=pltpu.PrefetchScalarGridSpec(
            num_scalar_prefetch=2, grid=(nblk,),
            in_specs=[pl.BlockSpec((tb, D_MODEL), lambda b, be, nu: (live(b, be, nu), 0)),
                      pl.BlockSpec((1, D_MODEL, EXPERT_FF), lambda b, be, nu: (be[live(b, be, nu)], 0, 0)),
                      pl.BlockSpec((1, D_MODEL, EXPERT_FF), lambda b, be, nu: (be[live(b, be, nu)], 0, 0)),
                      pl.BlockSpec((1, EXPERT_FF, D_MODEL), lambda b, be, nu: (be[live(b, be, nu)], 0, 0))],
            out_specs=pl.BlockSpec((tb, D_MODEL), lambda b, be, nu: (b, 0))),
        out_shape=jax.ShapeDtypeStruct((buf.shape[0], D_MODEL), F32),
        compiler_params=pltpu.CompilerParams(dimension_semantics=("arbitrary",),
                                             vmem_limit_bytes=VMEM_LIMIT),
        name="experts",
    )(block_e, n_used, buf, wg, wu, wd)


def _combine_kernel(dest_ref, x2_ref, route_ref, gf_ref, yb_ref, o_ref, rows, sem, *, tc):
    def row_copy(d, k, i):
        return pltpu.make_async_copy(yb_ref.at[pl.ds(d, 1)], rows.at[k, pl.ds(i, 1)], sem)

    def issue(i, c):
        row_copy(dest_ref[0, 0, 2 * i], 0, i).start()
        row_copy(dest_ref[0, 0, 2 * i + 1], 1, i).start()
        return c

    lax.fori_loop(0, tc, issue, 0)

    def drain(i, c):
        row_copy(0, 0, 0).wait()
        return c

    lax.fori_loop(0, 2 * tc, drain, 0)
    route = route_ref[...]
    y = x2_ref[...] + (route[:, 4:5] * rows[0] + route[:, 5:6] * rows[1])
    o_ref[...] = _rms(y, gf_ref[...])


def _combine(dest2d, x2, route, gf, yb, tc):
    n = x2.shape[0]
    return pl.pallas_call(
        functools.partial(_combine_kernel, tc=tc),
        grid=(n // tc,),
        in_specs=[pl.BlockSpec((1, 1, 2 * tc), lambda i: (i, 0, 0), memory_space=pltpu.SMEM),
                  pl.BlockSpec((tc, D_MODEL), lambda i: (i, 0)),
                  pl.BlockSpec((tc, LANES), lambda i: (i, 0)),
                  pl.BlockSpec((1, D_MODEL), lambda i: (0, 0)),
                  pl.BlockSpec(memory_space=pl.ANY)],
        out_specs=pl.BlockSpec((tc, D_MODEL), lambda i: (i, 0)),
        out_shape=jax.ShapeDtypeStruct((n, D_MODEL), F32),
        scratch_shapes=[pltpu.VMEM((2, tc, D_MODEL), F32), pltpu.SemaphoreType.DMA(())],
        compiler_params=pltpu.CompilerParams(dimension_semantics=("arbitrary",),
                                             vmem_limit_bytes=VMEM_LIMIT),
        name="combine",
    )(dest2d, x2, route, gf, yb)


def _overlap_matrix(seq):
    nc = seq // CMP_STRIDE - CMP_LEN // CMP_STRIDE + 1
    ns = seq // SLC_LEN
    c0 = np.arange(nc) * CMP_STRIDE
    s0 = np.arange(ns) * SLC_LEN
    ov = np.clip(np.minimum(c0[:, None] + CMP_LEN, s0[None, :] + SLC_LEN)
                 - np.maximum(c0[:, None], s0[None, :]), 0, None) / CMP_LEN
    out = np.zeros((LANES, LANES), np.float32)
    out[:nc, :ns] = ov
    return jnp.asarray(out, BF16)


def _pick_tile(n, pref):
    t = pref
    while n % t:
        t //= 2
    return t


def kernel(x, norm_mix_g, w_in, cmp_pos_k, cmp_w1_k, cmp_w2_k, cmp_pos_v, cmp_w1_v, cmp_w2_v, sinks, w_a, w_b,
           w_o, norm_ffn_g, w_group, b_group, w_expert, b_expert, w_gate_e, w_up_e, w_down_e, norm_final_g):
    bsz, seq, _ = x.shape
    n = bsz * seq
    assert seq % TQ == 0 and seq // SLC_LEN <= LANES // 4 and seq // CMP_STRIDE <= LANES
    assert seq >= (NSA_WINDOW // TQ + 1) * TQ and w_in.shape[0] == 1
    x2d = x.reshape(n, D_MODEL)

    w = w_in[0]
    scale = HEAD_DIM ** -0.5
    nsa_w, kvw = N_HEADS * HEAD_DIM, N_KV * HEAD_DIM
    o_qa, o_kva, o_gn = 0, nsa_w, nsa_w + 6 * kvw
    o_qb = o_gn + 3 * N_HEADS
    o_kvb = o_qb + nsa_w
    o_gm = o_kvb + 2 * kvw
    w_attn = jnp.concatenate([
        w[:, o_qa:o_qa + nsa_w] * scale, w[:, o_qb:o_qb + nsa_w] * scale,
        w[:, o_kva:o_kva + 6 * kvw], w[:, o_kvb:o_kvb + 2 * kvw],
        w[:, o_gn:o_gn + 3 * N_HEADS], jnp.zeros((D_MODEL, LANES - 3 * N_HEADS), F32)], axis=1).astype(BF16)
    w_gm = w[:, o_gm:o_gm + 2 * D_MODEL].astype(BF16)

    proj = _proj(x2d, norm_mix_g[0][None], w_attn, _pick_tile(n, 512))
    proj3 = proj.reshape(bsz, seq, PROJ_W)

    nch = seq // CMP_STRIDE
    r = proj3[:, :, COL_KV:COL_KV + 2 * kvw].reshape(bsz, nch, CMP_STRIDE, 2, N_KV, HEAD_DIM)
    r = jnp.transpose(r, (0, 3, 4, 1, 2, 5)).reshape(bsz, 2, N_KV, nch, CMP_STRIDE * HEAD_DIM)
    half = CMP_STRIDE * HEAD_DIM
    pos = jnp.stack([cmp_pos_k[0], cmp_pos_v[0]])
    pos_a = pos[:, :CMP_STRIDE].reshape(2, 1, half)
    pos_b = pos[:, CMP_STRIDE:].reshape(2, 1, half)
    w1 = jnp.stack([cmp_w1_k[0], cmp_w1_v[0]]).astype(BF16)
    w2 = jnp.stack([cmp_w2_k[0], cmp_w2_v[0]]).astype(BF16)
    kvc = _compress(r, pos_a, pos_b, w1[:, :half], w1[:, half:], w2)
    kvc = jnp.pad(kvc, ((0, 0), (0, 0), (0, LANES - nch), (0, 0)))

    o_a, o_b = _attention(proj3, kvc, _overlap_matrix(seq), sinks[0])

    w_r = jnp.concatenate([w_expert[0], w_group[0],
                           jnp.zeros((D_MODEL, LANES - N_EXPERTS - N_GROUPS), F32)], axis=1)
    w_rh = w_r.astype(BF16)
    w_rl = (w_r - w_rh.astype(F32)).astype(BF16)
    b_r = jnp.concatenate([b_expert[0], b_group[0], jnp.zeros((LANES - N_EXPERTS - N_GROUPS,), F32)])[None]
    x2, hn, route, counts = _merge(
        x2d, o_a.reshape(n, nsa_w), o_b.reshape(n, nsa_w), norm_mix_g[0][None], norm_ffn_g[0][None], w_gm,
        w_a[0].astype(BF16), w_b[0].astype(BF16), w_o[0].astype(BF16), w_rh, w_rl, b_r, _pick_tile(n, 256))

    tb = 256
    nblk = -(-(2 * n + N_EXPERTS * (tb - 1)) // tb)
    cnt = counts[0, :N_EXPERTS].astype(jnp.int32)
    padded = (cnt + tb - 1) // tb * tb
    pad_end = jnp.cumsum(padded)
    pad_start = pad_end - padded
    block_e = jnp.minimum(jnp.sum(pad_end[None, :] <= (jnp.arange(nblk) * tb)[:, None], axis=1), N_EXPERTS - 1)
    n_used = (pad_end[-1:] // tb).astype(jnp.int32)
    eid = route[:, 0:2].astype(jnp.int32)
    dest = pad_start[eid] + route[:, 2:4].astype(jnp.int32)

    td = _pick_tile(n, 512)
    buf = _dispatch(dest.reshape(n // td, 1, 2 * td), hn, jnp.zeros((nblk * tb, D_MODEL), F32), td)
    yb = _experts(block_e.astype(jnp.int32), n_used, buf, w_gate_e[0].astype(BF16), w_up_e[0].astype(BF16),
                  w_down_e[0].astype(BF16), tb)
    tc = _pick_tile(n, 256)
    out = _combine(dest.reshape(n // tc, 1, 2 * tc), x2, route, norm_final_g[None], yb, tc)
    return out.reshape(bsz, seq, D_MODEL)
```

```python
import functools

import numpy as np
import jax
import jax.numpy as jnp
from jax import lax
from jax.experimental import pallas as pl
from jax.experimental.pallas import tpu as pltpu

F32 = jnp.float32
BF16 = jnp.bfloat16

D_MODEL = 1024
HEAD_DIM = 64
N_HEADS = 8
N_KV = 2
GROUP = N_HEADS // N_KV
CMP_LEN = 32
CMP_STRIDE = 16
CMP_HIDDEN = 256
SLC_LEN = 64
SLC_TOPK = 8
NSA_WINDOW = 256
SWA_WINDOW = 128
N_GROUPS = 4
EPG = 8
N_EXPERTS = N_GROUPS * EPG
EXPERT_FF = 256
RMS_EPS = 1e-6
NEG_INF = -1e30
FORCE_SCORE = 1e9

LANES = 128
TQ = 128
PROJ_W = 2176
COL_KV = 1024
COL_GN = 2048
VMEM_LIMIT = 56 * 1024 * 1024


def _alibi_slopes():
    n = 2 * N_HEADS
    s = 2.0 ** (-8.0 * np.arange(1, n + 1) / n)
    return [float(v) for v in s[:N_HEADS]], [float(v) for v in s[N_HEADS:]]


SLOPES_SWA, SLOPES_NSA = _alibi_slopes()


def _rms(x, g):
    return x * lax.rsqrt(jnp.mean(x * x, axis=-1, keepdims=True) + RMS_EPS) * g


def _dot(a, b):
    return jnp.dot(a, b, preferred_element_type=F32)


def _dot_nt(a, b):
    return lax.dot_general(a, b, (((1,), (1,)), ((), ())), preferred_element_type=F32)


def _proj_kernel(x_ref, g_ref, w_ref, o_ref):
    h = _rms(x_ref[...], g_ref[...]).astype(BF16)
    o_ref[...] = _dot(h, w_ref[...]).astype(o_ref.dtype)


def _proj(x2d, g, w, tm):
    n = x2d.shape[0]
    return pl.pallas_call(
        _proj_kernel,
        grid=(n // tm,),
        in_specs=[pl.BlockSpec((tm, D_MODEL), lambda i: (i, 0)),
                  pl.BlockSpec((1, D_MODEL), lambda i: (0, 0)),
                  pl.BlockSpec((D_MODEL, PROJ_W), lambda i: (0, 0))],
        out_specs=pl.BlockSpec((tm, PROJ_W), lambda i: (i, 0)),
        out_shape=jax.ShapeDtypeStruct((n, PROJ_W), BF16),
        compiler_params=pltpu.CompilerParams(dimension_semantics=("parallel",),
                                             vmem_limit_bytes=VMEM_LIMIT),
        name="proj",
    )(x2d, g, w)


def _compress_kernel(r_ref, pa_ref, pb_ref, w1a_ref, w1b_ref, w2_ref, o_ref):
    for h in range(N_KV):
        r = r_ref[0, 0, h].astype(F32)
        a = _dot((r + pa_ref[0]).astype(BF16), w1a_ref[0])
        b = _dot((r + pb_ref[0]).astype(BF16), w1b_ref[0])
        nrow = b.shape[0]
        hid = a + pltpu.roll(b, nrow - 1, 0)
        hid = hid * jax.nn.sigmoid(hid)
        o_ref[0, 0, :, h * HEAD_DIM:(h + 1) * HEAD_DIM] = _dot(hid.astype(BF16), w2_ref[0]).astype(o_ref.dtype)


def _compress(r, pos_a, pos_b, w1a, w1b, w2):
    bsz, _, _, nch, width = r.shape
    wspec = lambda shape: pl.BlockSpec((1,) + shape, lambda b, j: (j, 0, 0))
    return pl.pallas_call(
        _compress_kernel,
        grid=(bsz, 2),
        in_specs=[pl.BlockSpec((1, 1, N_KV, nch, width), lambda b, j: (b, j, 0, 0, 0)),
                  wspec((1, width)), wspec((1, width)),
                  wspec((width, CMP_HIDDEN)), wspec((width, CMP_HIDDEN)),
                  wspec((CMP_HIDDEN, HEAD_DIM))],
        out_specs=pl.BlockSpec((1, 1, nch, N_KV * HEAD_DIM), lambda b, j: (b, j, 0, 0)),
        out_shape=jax.ShapeDtypeStruct((bsz, 2, nch, N_KV * HEAD_DIM), BF16),
        compiler_params=pltpu.CompilerParams(dimension_semantics=("parallel", "parallel"),
                                             vmem_limit_bytes=VMEM_LIMIT),
        name="compress",
    )(r, pos_a, pos_b, w1a, w1b, w2)


def _softmax_pv(s, mask, v, sink=None):
    m = jnp.max(s, axis=-1, keepdims=True)
    if sink is not None:
        m = jnp.maximum(m, sink)
    e = jnp.where(mask, jnp.exp(s - m), 0.0)
    z = jnp.sum(e, axis=-1, keepdims=True)
    if sink is not None:
        z = z + jnp.exp(sink - m)
    return _dot((e / z).astype(BF16), v)


def _attn_kernel(sinks_ref, qa_ref, qb_ref, kc_ref, vc_ref, ksl_ref, vsl_ref, kw_ref, vw_ref,
                 kb_ref, vb_ref, gn_ref, ov_ref, oa_ref, ob_ref, *, n_cmp):
    qi = pl.program_id(1)
    q0 = qi * TQ
    t_col = q0 + lax.broadcasted_iota(jnp.int32, (TQ, 1), 0)
    lane = lax.broadcasted_iota(jnp.int32, (1, LANES), 1)
    gates = jax.nn.sigmoid(gn_ref[0].astype(F32))

    def head_cols(h):
        return slice(h * HEAD_DIM, (h + 1) * HEAD_DIM)

    for h in range(N_KV):
        heads = [h * GROUP + g for g in range(GROUP)]
        qs = [qa_ref[0, :, head_cols(hh)] for hh in heads]
        q_all = jnp.concatenate(qs, axis=0)

        kc = kc_ref[0, 0, :, head_cols(h)]
        vc = vc_ref[0, 0, :, head_cols(h)]
        end_c = lane * CMP_STRIDE + (CMP_LEN - 1)
        cmask = (t_col >= end_c) & (lane < n_cmp)
        s_all = _dot_nt(q_all, kc)
        o_cmp = []
        psum = jnp.zeros((TQ, LANES), F32)
        for g in range(GROUP):
            s = s_all[g * TQ:(g + 1) * TQ] + SLOPES_NSA[heads[g]] * end_c.astype(F32)
            s = jnp.where(cmask, s, NEG_INF)
            m = jnp.max(s, axis=-1, keepdims=True)
            e = jnp.where(cmask, jnp.exp(s - m), 0.0)
            z = jnp.sum(e, axis=-1, keepdims=True)
            p = e / jnp.where(z > 0, z, 1.0)
            o_cmp.append(_dot(p.astype(BF16), vc))
            psum = psum + p

        p_hi = psum.astype(BF16)
        p_lo = (psum - p_hi.astype(F32)).astype(BF16)
        imp = _dot(p_hi, ov_ref[...]) + _dot(p_lo, ov_ref[...])
        cur = lax.shift_right_logical(t_col, int(np.log2(SLC_LEN)))
        valid = (lane * SLC_LEN <= t_col) & (lane < LANES // 4)
        forced = (lane == 0) | (lane == cur) | (lane == cur - 1)
        score = jnp.where(forced, FORCE_SCORE, jnp.where(valid, imp, NEG_INF))
        rank = jnp.zeros((TQ, LANES), F32)
        for i in range(LANES // 4):
            si = score[:, i:i + 1]
            ahead = (si > score) | ((si == score) & (lane > i))
            rank = rank + jnp.where(ahead, 1.0, 0.0)
        sel = (rank < SLC_TOPK) & (score > 0.5 * NEG_INF)
        sel_b = jnp.where(sel, 1.0, 0.0).astype(BF16)

        blk_row = lax.broadcasted_iota(jnp.int32, (LANES, TQ), 0)
        key_blk = lax.broadcasted_iota(jnp.int32, (LANES, TQ), 1) // SLC_LEN

        def slc_step(c, carry):
            m_i, l_i, acc = carry
            start = pl.multiple_of(c * TQ, TQ)
            k = ksl_ref[0, pl.ds(start, TQ), :][:, head_cols(h)]
            v = vsl_ref[0, pl.ds(start, TQ), :][:, head_cols(h)]
            expand = jnp.where(blk_row == key_blk + c * (TQ // SLC_LEN), 1.0, 0.0).astype(BF16)
            kpos = c * TQ + lane
            mask = (_dot(sel_b, expand) > 0.5) & (kpos <= t_col)
            s_all = _dot_nt(q_all, k)
            kposf = kpos.astype(F32)
            m_new, l_new, es, alphas = [], [], [], []
            for g in range(GROUP):
                rows = slice(g * TQ, (g + 1) * TQ)
                s = jnp.where(mask, s_all[rows] + SLOPES_NSA[heads[g]] * kposf, NEG_INF)
                mg = jnp.maximum(m_i[rows], jnp.max(s, axis=-1, keepdims=True))
                alpha = jnp.exp(m_i[rows] - mg)
                e = jnp.where(mask, jnp.exp(s - mg), 0.0)
                m_new.append(mg)
                l_new.append(alpha * l_i[rows] + jnp.sum(e, axis=-1, keepdims=True))
                es.append(e.astype(BF16))
                alphas.append(alpha)
            pv = _dot(jnp.concatenate(es, axis=0), v)
            acc = jnp.concatenate(alphas, axis=0) * acc + pv
            return jnp.concatenate(m_new, axis=0), jnp.concatenate(l_new, axis=0), acc

        init = (jnp.full((GROUP * TQ, 1), NEG_INF, F32), jnp.zeros((GROUP * TQ, 1), F32),
                jnp.zeros((GROUP * TQ, HEAD_DIM), F32))
        _, l_f, acc_f = lax.fori_loop(0, qi + 1, slc_step, init)
        o_slc_all = acc_f / l_f

        nw = NSA_WINDOW // TQ + 1
        start = pl.multiple_of(jnp.maximum(qi - (nw - 1), 0) * TQ, TQ)
        kw = kw_ref[0, pl.ds(start, nw * TQ), :][:, head_cols(h)]
        vw = vw_ref[0, pl.ds(start, nw * TQ), :][:, head_cols(h)]
        kpos = start + lax.broadcasted_iota(jnp.int32, (1, nw * TQ), 1)
        dist = t_col - kpos
        wmask = (dist >= 0) & (dist < NSA_WINDOW)
        s_all = _dot_nt(q_all, kw)
        for g in range(GROUP):
            hh = heads[g]
            s = jnp.where(wmask, s_all[g * TQ:(g + 1) * TQ] + SLOPES_NSA[hh] * kpos.astype(F32), NEG_INF)
            o_win = _softmax_pv(s, wmask, vw)
            o_slc = o_slc_all[g * TQ:(g + 1) * TQ]
            o = (gates[:, 3 * hh:3 * hh + 1] * o_cmp[g] + gates[:, 3 * hh + 1:3 * hh + 2] * o_slc
                 + gates[:, 3 * hh + 2:3 * hh + 3] * o_win)
            oa_ref[0, :, head_cols(hh)] = o.astype(oa_ref.dtype)

    nb = (SWA_WINDOW - 1 + TQ - 1) // TQ + 1
    start = pl.multiple_of(jnp.maximum(qi - (nb - 1), 0) * TQ, TQ)
    kpos = start + lax.broadcasted_iota(jnp.int32, (1, nb * TQ), 1)
    dist = t_col - kpos
    bmask = (dist >= 0) & (dist < SWA_WINDOW)
    for h in range(N_KV):
        kb = kb_ref[0, pl.ds(start, nb * TQ), :][:, head_cols(h)]
        vb = vb_ref[0, pl.ds(start, nb * TQ), :][:, head_cols(h)]
        for g in range(GROUP):
            hh = h * GROUP + g
            q = qb_ref[0, :, head_cols(hh)]
            s = jnp.where(bmask, _dot_nt(q, kb) - SLOPES_SWA[hh] * dist.astype(F32), NEG_INF)
            ob_ref[0, :, head_cols(hh)] = _softmax_pv(s, bmask, vb, sink=sinks_ref[hh]).astype(ob_ref.dtype)


def _attention(proj3, kvc, ov, sinks):
    bsz, seq, _ = proj3.shape
    n_cmp = seq // CMP_STRIDE - CMP_LEN // CMP_STRIDE + 1
    width = N_HEADS * HEAD_DIM
    kvw = N_KV * HEAD_DIM
    qspec = lambda col: pl.BlockSpec((1, TQ, width), lambda b, q, s: (b, q, col))
    kvspec = lambda col: pl.BlockSpec((1, seq, kvw), lambda b, q, s: (b, 0, col))
    cspec = lambda j: pl.BlockSpec((1, 1, kvc.shape[2], kvw), lambda b, q, s: (b, j, 0, 0))
    kv0 = COL_KV // kvw
    in_specs = [qspec(0), qspec(1), cspec(0), cspec(1)]
    in_specs += [kvspec(kv0 + 2 + i) for i in range(4)]
    in_specs += [kvspec(kv0 + 6), kvspec(kv0 + 7)]
    in_specs += [pl.BlockSpec((1, TQ, LANES), lambda b, q, s: (b, q, COL_GN // LANES)),
                 pl.BlockSpec((LANES, LANES), lambda b, q, s: (0, 0))]
    ospec = pl.BlockSpec((1, TQ, width), lambda b, q, s: (b, q, 0))
    return pl.pallas_call(
        functools.partial(_attn_kernel, n_cmp=n_cmp),
        grid_spec=pltpu.PrefetchScalarGridSpec(
            num_scalar_prefetch=1, grid=(bsz, seq // TQ), in_specs=in_specs, out_specs=[ospec, ospec]),
        out_shape=[jax.ShapeDtypeStruct((bsz, seq, width), BF16)] * 2,
        compiler_params=pltpu.CompilerParams(dimension_semantics=("parallel", "parallel"),
                                             vmem_limit_bytes=VMEM_LIMIT),
        name="attn",
    )(sinks, proj3, proj3, kvc, kvc, proj3, proj3, proj3, proj3, proj3, proj3, proj3, ov)


def _merge_kernel(x_ref, oa_ref, ob_ref, g1_ref, g2_ref, wg_ref, wa_ref, wb_ref, wo_ref,
                  wrh_ref, wrl_ref, br_ref, x2_ref, hn_ref, route_ref, cnt_ref):
    tm = x_ref.shape[0]

    @pl.when(pl.program_id(0) == 0)
    def _():
        cnt_ref[...] = jnp.zeros_like(cnt_ref)

    x = x_ref[...]
    h = _rms(x, g1_ref[...]).astype(BF16)
    gm = jax.nn.sigmoid(_dot(h, wg_ref[...]))
    a = _dot(oa_ref[...], wa_ref[...])
    b = _dot(ob_ref[...], wb_ref[...])
    mixin = gm[:, :D_MODEL] * a + gm[:, D_MODEL:] * b
    x2 = x + _dot(mixin.astype(BF16), wo_ref[...])
    x2_ref[...] = x2
    hn = _rms(x2, g2_ref[...])
    hn_b = hn.astype(BF16)
    for j in range(D_MODEL // LANES):
        hn_ref[:, j, :] = hn[:, j * LANES:(j + 1) * LANES]

    hn_lo = (hn - hn_b.astype(F32)).astype(BF16)
    logits = (_dot(hn_b, wrh_ref[...]) + _dot(hn_lo, wrh_ref[...]) + _dot(hn_b, wrl_ref[...])
              + br_ref[...])
    lane = lax.broadcasted_iota(jnp.int32, (1, LANES), 1)
    lanef = lane.astype(F32)
    big = float(LANES)
    is_g = (lane >= N_EXPERTS) & (lane < N_EXPERTS + N_GROUPS)
    gl = jnp.where(is_g, logits, NEG_INF)
    gmax = jnp.max(gl, axis=-1, keepdims=True)
    grp = jnp.min(jnp.where(gl == gmax, lanef, big), axis=-1, keepdims=True) - N_EXPERTS
    p_grp = 1.0 / jnp.sum(jnp.where(is_g, jnp.exp(gl - gmax), 0.0), axis=-1, keepdims=True)
    in_grp = (lanef >= grp * EPG) & (lanef < grp * EPG + EPG)
    el = jnp.where(in_grp, logits, NEG_INF)
    v0 = jnp.max(el, axis=-1, keepdims=True)
    i0 = jnp.min(jnp.where(el == v0, lanef, big), axis=-1, keepdims=True)
    el1 = jnp.where(lanef == i0, NEG_INF, el)
    v1 = jnp.max(el1, axis=-1, keepdims=True)
    i1 = jnp.min(jnp.where(el1 == v1, lanef, big), axis=-1, keepdims=True)
    e1 = jnp.exp(v1 - v0)
    w0 = p_grp / (1.0 + e1)
    w1 = p_grp * e1 / (1.0 + e1)

    oh0 = jnp.where(lanef == i0, 1.0, 0.0)
    oh1 = jnp.where(lanef == i1, 1.0, 0.0)
    oh = oh0 + oh1
    r_i = lax.broadcasted_iota(jnp.int32, (tm, tm), 0)
    c_i = lax.broadcasted_iota(jnp.int32, (tm, tm), 1)
    lower = jnp.where(c_i < r_i, 1.0, 0.0).astype(BF16)
    before = cnt_ref[...] + _dot(lower, oh.astype(BF16))
    rank0 = jnp.sum(oh0 * before, axis=-1, keepdims=True)
    rank1 = jnp.sum(oh1 * before, axis=-1, keepdims=True)
    cnt_ref[...] = cnt_ref[...] + jnp.sum(oh, axis=0, keepdims=True)

    route = jnp.where(lane == 0, i0, jnp.where(lane == 1, i1, jnp.where(lane == 2, rank0,
            jnp.where(lane == 3, rank1, jnp.where(lane == 4, w0, jnp.where(lane == 5, w1, 0.0))))))
    route_ref[...] = route


def _merge(x2d, oa, ob, g1, g2, wg, wa, wb, wo, wrh, wrl, br, tm):
    n = x2d.shape[0]
    width = N_HEADS * HEAD_DIM
    row = lambda w: pl.BlockSpec((tm, w), lambda i: (i, 0))
    full = lambda a: pl.BlockSpec(a.shape, lambda i: (0, 0))
    return pl.pallas_call(
        _merge_kernel,
        grid=(n // tm,),
        in_specs=[row(D_MODEL), row(width), row(width), full(g1), full(g2), full(wg), full(wa), full(wb),
                  full(wo), full(wrh), full(wrl), full(br)],
        out_specs=[row(D_MODEL), pl.BlockSpec((tm, D_MODEL // LANES, LANES), lambda i: (i, 0, 0)), row(LANES),
                   pl.BlockSpec((1, LANES), lambda i: (0, 0))],
        out_shape=[jax.ShapeDtypeStruct((n, D_MODEL), F32), jax.ShapeDtypeStruct((n, D_MODEL // LANES, LANES), F32),
                   jax.ShapeDtypeStruct((n, LANES), F32), jax.ShapeDtypeStruct((1, LANES), F32)],
        compiler_params=pltpu.CompilerParams(dimension_semantics=("arbitrary",),
                                             vmem_limit_bytes=VMEM_LIMIT),
        name="merge",
    )(x2d, oa, ob, g1, g2, wg, wa, wb, wo, wrh, wrl, br)


def _dispatch_kernel(dest_ref, hn_ref, buf_in_ref, buf_ref, sem, *, td):
    del buf_in_ref

    def row_copy(i, d):
        return pltpu.make_async_copy(hn_ref.at[i], buf_ref.at[d], sem)

    def issue(i, c):
        row_copy(i, dest_ref[0, 0, 2 * i]).start()
        row_copy(i, dest_ref[0, 0, 2 * i + 1]).start()
        return c

    lax.fori_loop(0, td, issue, 0, unroll=8)

    def drain(i, c):
        row_copy(0, 0).wait()
        return c

    lax.fori_loop(0, 2 * td, drain, 0, unroll=8)


def _dispatch(dest2d, hn, buf0, td):
    n = hn.shape[0]
    return pl.pallas_call(
        functools.partial(_dispatch_kernel, td=td),
        grid=(n // td,),
        in_specs=[pl.BlockSpec((1, 1, 2 * td), lambda i: (i, 0, 0), memory_space=pltpu.SMEM),
                  pl.BlockSpec((td,) + hn.shape[1:], lambda i: (i, 0, 0)), pl.BlockSpec(memory_space=pl.ANY)],
        out_specs=pl.BlockSpec(memory_space=pl.ANY),
        out_shape=jax.ShapeDtypeStruct(buf0.shape, buf0.dtype),
        scratch_shapes=[pltpu.SemaphoreType.DMA(())],
        input_output_aliases={2: 0},
        compiler_params=pltpu.CompilerParams(dimension_semantics=("arbitrary",)),
        name="dispatch",
    )(dest2d, hn, buf0)


def _expert_kernel(be_ref, nu_ref, xb_ref, wg_ref, wu_ref, wd_ref, y_ref):
    del be_ref

    @pl.when(pl.program_id(0) < nu_ref[0])
    def _():
        xb = jnp.concatenate([xb_ref[:, j, :] for j in range(D_MODEL // LANES)], axis=1).astype(BF16)
        g = _dot(xb, wg_ref[0])
        u = _dot(xb, wu_ref[0])
        y = _dot((g * jax.nn.sigmoid(g) * u).astype(BF16), wd_ref[0])
        for j in range(D_MODEL // LANES):
            y_ref[:, j, :] = y[:, j * LANES:(j + 1) * LANES]

    @pl.when(pl.program_id(0) >= nu_ref[0])
    def _():
        y_ref[...] = jnp.zeros_like(y_ref)


def _experts(block_e, n_used, buf, wg, wu, wd, tb):
    nblk = buf.shape[0] // tb
    live = lambda b, be, nu: jnp.minimum(b, nu[0] - 1)
    return pl.pallas_call(
        _expert_kernel,
        grid_spec=pltpu.PrefetchScalarGridSpec(
            num_scalar_prefetch=2, grid=(nblk,),
            in_specs=[pl.BlockSpec((tb, D_MODEL // LANES, LANES), lambda b, be, nu: (live(b, be, nu), 0, 0)),
                      pl.BlockSpec((1, D_MODEL, EXPERT_FF), lambda b, be, nu: (be[live(b, be, nu)], 0, 0)),
                      pl.BlockSpec((1, D_MODEL, EXPERT_FF), lambda b, be, nu: (be[live(b, be, nu)], 0, 0)),
                      pl.BlockSpec((1, EXPERT_FF, D_MODEL), lambda b, be, nu: (be[live(b, be, nu)], 0, 0))],
            out_specs=pl.BlockSpec((tb, D_MODEL // LANES, LANES), lambda b, be, nu: (b, 0, 0))),
        out_shape=jax.ShapeDtypeStruct(buf.shape, F32),
        compiler_params=pltpu.CompilerParams(dimension_semantics=("arbitrary",),
                                             vmem_limit_bytes=VMEM_LIMIT),
        name="experts",
    )(block_e, n_used, buf, wg, wu, wd)


def _combine_kernel(dest_ref, x2_ref, route_ref, gf_ref, yb_ref, o_ref, rows, sem, *, tc):
    def row_copy(d, k, i):
        return pltpu.make_async_copy(yb_ref.at[d], rows.at[k, i], sem)

    def issue(i, c):
        row_copy(dest_ref[0, 0, 2 * i], 0, i).start()
        row_copy(dest_ref[0, 0, 2 * i + 1], 1, i).start()
        return c

    lax.fori_loop(0, tc, issue, 0, unroll=8)

    def drain(i, c):
        row_copy(0, 0, 0).wait()
        return c

    lax.fori_loop(0, 2 * tc, drain, 0, unroll=8)
    route = route_ref[...]
    y0 = jnp.concatenate([rows[0, :, j, :] for j in range(D_MODEL // LANES)], axis=1)
    y1 = jnp.concatenate([rows[1, :, j, :] for j in range(D_MODEL // LANES)], axis=1)
    y = x2_ref[...] + (route[:, 4:5] * y0 + route[:, 5:6] * y1)
    o_ref[...] = _rms(y, gf_ref[...])


def _combine(dest2d, x2, route, gf, yb, tc):
    n = x2.shape[0]
    return pl.pallas_call(
        functools.partial(_combine_kernel, tc=tc),
        grid=(n // tc,),
        in_specs=[pl.BlockSpec((1, 1, 2 * tc), lambda i: (i, 0, 0), memory_space=pltpu.SMEM),
                  pl.BlockSpec((tc, D_MODEL), lambda i: (i, 0)),
                  pl.BlockSpec((tc, LANES), lambda i: (i, 0)),
                  pl.BlockSpec((1, D_MODEL), lambda i: (0, 0)),
                  pl.BlockSpec(memory_space=pl.ANY)],
        out_specs=pl.BlockSpec((tc, D_MODEL), lambda i: (i, 0)),
        out_shape=jax.ShapeDtypeStruct((n, D_MODEL), F32),
        scratch_shapes=[pltpu.VMEM((2, tc, D_MODEL // LANES, LANES), F32), pltpu.SemaphoreType.DMA(())],
        compiler_params=pltpu.CompilerParams(dimension_semantics=("arbitrary",),
                                             vmem_limit_bytes=VMEM_LIMIT),
        name="combine",
    )(dest2d, x2, route, gf, yb)


def _overlap_matrix(seq):
    nc = seq // CMP_STRIDE - CMP_LEN // CMP_STRIDE + 1
    ns = seq // SLC_LEN
    c0 = np.arange(nc) * CMP_STRIDE
    s0 = np.arange(ns) * SLC_LEN
    ov = np.clip(np.minimum(c0[:, None] + CMP_LEN, s0[None, :] + SLC_LEN)
                 - np.maximum(c0[:, None], s0[None, :]), 0, None) / CMP_LEN
    out = np.zeros((LANES, LANES), np.float32)
    out[:nc, :ns] = ov
    return jnp.asarray(out, BF16)


def _pick_tile(n, pref):
    t = pref
    while n % t:
        t //= 2
    return t


def kernel(x, norm_mix_g, w_in, cmp_pos_k, cmp_w1_k, cmp_w2_k, cmp_pos_v, cmp_w1_v, cmp_w2_v, sinks, w_a, w_b,
           w_o, norm_ffn_g, w_group, b_group, w_expert, b_expert, w_gate_e, w_up_e, w_down_e, norm_final_g):
    bsz, seq, _ = x.shape
    n = bsz * seq
    assert seq % TQ == 0 and seq // SLC_LEN <= LANES // 4 and seq // CMP_STRIDE <= LANES
    assert seq >= (NSA_WINDOW // TQ + 1) * TQ and w_in.shape[0] == 1
    x2d = x.reshape(n, D_MODEL)

    w = w_in[0]
    scale = HEAD_DIM ** -0.5
    nsa_w, kvw = N_HEADS * HEAD_DIM, N_KV * HEAD_DIM
    o_qa, o_kva, o_gn = 0, nsa_w, nsa_w + 6 * kvw
    o_qb = o_gn + 3 * N_HEADS
    o_kvb = o_qb + nsa_w
    o_gm = o_kvb + 2 * kvw
    w_attn = jnp.concatenate([
        w[:, o_qa:o_qa + nsa_w] * scale, w[:, o_qb:o_qb + nsa_w] * scale,
        w[:, o_kva:o_kva + 6 * kvw], w[:, o_kvb:o_kvb + 2 * kvw],
        w[:, o_gn:o_gn + 3 * N_HEADS], jnp.zeros((D_MODEL, LANES - 3 * N_HEADS), F32)], axis=1).astype(BF16)
    w_gm = w[:, o_gm:o_gm + 2 * D_MODEL].astype(BF16)

    proj = _proj(x2d, norm_mix_g[0][None], w_attn, _pick_tile(n, 512))
    proj3 = proj.reshape(bsz, seq, PROJ_W)

    nch = seq // CMP_STRIDE
    r = proj3[:, :, COL_KV:COL_KV + 2 * kvw].reshape(bsz, nch, CMP_STRIDE, 2, N_KV, HEAD_DIM)
    r = jnp.transpose(r, (0, 3, 4, 1, 2, 5)).reshape(bsz, 2, N_KV, nch, CMP_STRIDE * HEAD_DIM)
    half = CMP_STRIDE * HEAD_DIM
    pos = jnp.stack([cmp_pos_k[0], cmp_pos_v[0]])
    pos_a = pos[:, :CMP_STRIDE].reshape(2, 1, half)
    pos_b = pos[:, CMP_STRIDE:].reshape(2, 1, half)
    w1 = jnp.stack([cmp_w1_k[0], cmp_w1_v[0]]).astype(BF16)
    w2 = jnp.stack([cmp_w2_k[0], cmp_w2_v[0]]).astype(BF16)
    kvc = _compress(r, pos_a, pos_b, w1[:, :half], w1[:, half:], w2)
    kvc = jnp.pad(kvc, ((0, 0), (0, 0), (0, LANES - nch), (0, 0)))

    o_a, o_b = _attention(proj3, kvc, _overlap_matrix(seq), sinks[0])

    w_r = jnp.concatenate([w_expert[0], w_group[0],
                           jnp.zeros((D_MODEL, LANES - N_EXPERTS - N_GROUPS), F32)], axis=1)
    w_rh = w_r.astype(BF16)
    w_rl = (w_r - w_rh.astype(F32)).astype(BF16)
    b_r = jnp.concatenate([b_expert[0], b_group[0], jnp.zeros((LANES - N_EXPERTS - N_GROUPS,), F32)])[None]
    x2, hn, route, counts = _merge(
        x2d, o_a.reshape(n, nsa_w), o_b.reshape(n, nsa_w), norm_mix_g[0][None], norm_ffn_g[0][None], w_gm,
        w_a[0].astype(BF16), w_b[0].astype(BF16), w_o[0].astype(BF16), w_rh, w_rl, b_r, _pick_tile(n, 256))

    tb = 256
    nblk = -(-(2 * n + N_EXPERTS * (tb - 1)) // tb)
    cnt = counts[0, :N_EXPERTS].astype(jnp.int32)
    padded = (cnt + tb - 1) // tb * tb
    pad_end = jnp.cumsum(padded)
    pad_start = pad_end - padded
    block_e = jnp.minimum(jnp.sum(pad_end[None, :] <= (jnp.arange(nblk) * tb)[:, None], axis=1), N_EXPERTS - 1)
    n_used = (pad_end[-1:] // tb).astype(jnp.int32)
    eid = route[:, 0:2].astype(jnp.int32)
    dest = pad_start[eid] + route[:, 2:4].astype(jnp.int32)

    td = _pick_tile(n, 512)
    buf = _dispatch(dest.reshape(n // td, 1, 2 * td), hn, jnp.zeros((nblk * tb, D_MODEL // LANES, LANES), F32), td)
    yb = _experts(block_e.astype(jnp.int32), n_used, buf, w_gate_e[0].astype(BF16), w_up_e[0].astype(BF16),
                  w_down_e[0].astype(BF16), tb)
    tc = _pick_tile(n, 256)
    out = _combine(dest.reshape(n // tc, 1, 2 * tc), x2, route, norm_final_g[None], yb, tc)
    return out.reshape(bsz, seq, D_MODEL)
```

```python
import functools

import numpy as np
import jax
import jax.numpy as jnp
from jax import lax
from jax.experimental import pallas as pl
from jax.experimental.pallas import tpu as pltpu

F32 = jnp.float32
BF16 = jnp.bfloat16

D_MODEL = 1024
HEAD_DIM = 64
N_HEADS = 8
N_KV = 2
GROUP = N_HEADS // N_KV
CMP_LEN = 32
CMP_STRIDE = 16
CMP_HIDDEN = 256
SLC_LEN = 64
SLC_TOPK = 8
NSA_WINDOW = 256
SWA_WINDOW = 128
N_GROUPS = 4
EPG = 8
N_EXPERTS = N_GROUPS * EPG
EXPERT_FF = 256
RMS_EPS = 1e-6
NEG_INF = -1e30
FORCE_SCORE = 1e9

LANES = 128
TQ = 128
KCHUNK = 512
N_SLC_BLK = LANES // 4
FEAT_POS = HEAD_DIM
FEAT_SEL = HEAD_DIM + 6
ROWS_LO, ROWS_HI = 64, 104
N_KV_SECT = 12
COL_QB = 512
COL_KV = 1024
COL_CMP = COL_KV + N_KV_SECT * LANES
COL_GN = COL_CMP + 2 * LANES
PROJ_W = COL_GN + LANES
VMEM_LIMIT = 56 * 1024 * 1024


def _alibi_slopes():
    n = 2 * N_HEADS
    s = 2.0 ** (-8.0 * np.arange(1, n + 1) / n)
    return [float(v) for v in s[:N_HEADS]], [float(v) for v in s[N_HEADS:]]


SLOPES_SWA, SLOPES_NSA = _alibi_slopes()


def _bf16_pieces(v):
    out, rem = [], np.float32(v)
    for _ in range(3):
        p = np.float32(np.asarray(rem, np.float32).astype(BF16).astype(np.float32))
        out.append(float(p))
        rem = np.float32(rem - p)
    return out


def _rms(x, g):
    return x * lax.rsqrt(jnp.mean(x * x, axis=-1, keepdims=True) + RMS_EPS) * g


def _dot(a, b):
    return jnp.dot(a, b, preferred_element_type=F32)


def _dot_nt(a, b):
    return lax.dot_general(a, b, (((1,), (1,)), ((), ())), preferred_element_type=F32)


def _proj_kernel(x_ref, g_ref, w_ref, fk_ref, fv_ref, o_ref):
    h = _rms(x_ref[...], g_ref[...]).astype(BF16)
    res = _dot(h, w_ref[...])
    o_ref[:, :COL_KV] = res[:, :COL_KV].astype(o_ref.dtype)
    for j in range(N_KV_SECT):
        feat = fk_ref[...] if (j // N_KV) % 2 == 0 else fv_ref[...]
        c0 = COL_KV + j * LANES
        o_ref[:, c0:c0 + LANES] = (res[:, c0:c0 + LANES] + feat).astype(o_ref.dtype)
    o_ref[:, COL_CMP:] = res[:, COL_CMP:].astype(o_ref.dtype)


def _proj(x2d, g, w, feat_k, feat_v, tm):
    n = x2d.shape[0]
    nper = feat_k.shape[0] // tm
    return pl.pallas_call(
        _proj_kernel,
        grid=(n // tm,),
        in_specs=[pl.BlockSpec((tm, D_MODEL), lambda i: (i, 0)),
                  pl.BlockSpec((1, D_MODEL), lambda i: (0, 0)),
                  pl.BlockSpec((D_MODEL, PROJ_W), lambda i: (0, 0)),
                  pl.BlockSpec((tm, LANES), lambda i: (i % nper, 0)),
                  pl.BlockSpec((tm, LANES), lambda i: (i % nper, 0))],
        out_specs=pl.BlockSpec((tm, PROJ_W), lambda i: (i, 0)),
        out_shape=jax.ShapeDtypeStruct((n, PROJ_W), BF16),
        compiler_params=pltpu.CompilerParams(dimension_semantics=("parallel",),
                                             vmem_limit_bytes=VMEM_LIMIT),
        name="proj",
    )(x2d, g, w, feat_k, feat_v)


def _compress_kernel(r_ref, pa_ref, pb_ref, w1a_ref, w1b_ref, w2_ref, o_ref):
    for h in range(N_KV):
        r = r_ref[0, 0, h].astype(F32)
        a = _dot((r + pa_ref[0]).astype(BF16), w1a_ref[0])
        b = _dot((r + pb_ref[0]).astype(BF16), w1b_ref[0])
        nrow = b.shape[0]
        hid = a + pltpu.roll(b, nrow - 1, 0)
        hid = hid * jax.nn.sigmoid(hid)
        o_ref[0, 0, :, h * HEAD_DIM:(h + 1) * HEAD_DIM] = _dot(hid.astype(BF16), w2_ref[0]).astype(o_ref.dtype)


def _compress(r, pos_a, pos_b, w1a, w1b, w2):
    bsz, _, _, nch, width = r.shape
    wspec = lambda shape: pl.BlockSpec((1,) + shape, lambda b, j: (j, 0, 0))
    return pl.pallas_call(
        _compress_kernel,
        grid=(bsz, 2),
        in_specs=[pl.BlockSpec((1, 1, N_KV, nch, width), lambda b, j: (b, j, 0, 0, 0)),
                  wspec((1, width)), wspec((1, width)),
                  wspec((width, CMP_HIDDEN)), wspec((width, CMP_HIDDEN)),
                  wspec((CMP_HIDDEN, HEAD_DIM))],
        out_specs=pl.BlockSpec((1, 1, nch, N_KV * HEAD_DIM), lambda b, j: (b, j, 0, 0)),
        out_shape=jax.ShapeDtypeStruct((bsz, 2, nch, N_KV * HEAD_DIM), BF16),
        compiler_params=pltpu.CompilerParams(dimension_semantics=("parallel", "parallel"),
                                             vmem_limit_bytes=VMEM_LIMIT),
        name="compress",
    )(r, pos_a, pos_b, w1a, w1b, w2)


def _slope_row(slope, lane):
    hi, mid, lo = _bf16_pieces(slope)
    row = jnp.zeros((1, LANES), F32)
    for i, v in enumerate([SLC_LEN * hi, SLC_LEN * mid, SLC_LEN * lo, hi, mid, lo]):
        row = jnp.where(lane == FEAT_POS + i, v, row)
    return row.astype(BF16)


def _masked_rows(s, mask):
    return jnp.concatenate([jnp.where(mask, s[g * TQ:(g + 1) * TQ], NEG_INF) for g in range(GROUP)], axis=0)


def _attn_kernel(sinks_ref, qa_ref, qb_ref, kc_ref, vc_ref, *rest, n_cmp):
    kv = rest[:N_KV_SECT]
    gn_ref, ovt_ref, oa_ref, ob_ref = rest[N_KV_SECT:]
    sect = lambda is_v, branch, h: kv[(2 * branch + is_v) * N_KV + h]
    qi = pl.program_id(1)
    q0 = pl.multiple_of(qi * TQ, TQ)
    lane = lax.broadcasted_iota(jnp.int32, (1, LANES), 1)
    t_col = q0 + lax.broadcasted_iota(jnp.int32, (TQ, 1), 0)
    t_row = q0 + lane
    eye = jnp.where(lax.broadcasted_iota(jnp.int32, (TQ, TQ), 0) == lax.broadcasted_iota(jnp.int32, (TQ, TQ), 1),
                    1.0, 0.0).astype(BF16)
    gates = jax.nn.sigmoid(gn_ref[0].astype(F32))
    blk = lax.broadcasted_iota(jnp.int32, (ROWS_HI - ROWS_LO, 1), 0) + (ROWS_LO - FEAT_SEL)
    in_rng = (blk >= 0) & (blk < N_SLC_BLK)

    def head_cols(h):
        return slice(h * HEAD_DIM, (h + 1) * HEAD_DIM)

    def q_low(ref, hh):
        pair = ref[0, :, (hh // 2) * LANES:(hh // 2 + 1) * LANES]
        if hh % 2 == 0:
            return pair
        return pltpu.roll(pair.astype(F32), HEAD_DIM, 1).astype(BF16)

    def q_aug(q_lo, tails):
        return jnp.concatenate([jnp.where(lane < HEAD_DIM, q, t) for q, t in zip(q_lo, tails)], axis=0)

    def to_lanes(bias_rows):
        full = jnp.concatenate([jnp.zeros((ROWS_LO, TQ), F32), bias_rows,
                                jnp.zeros((LANES - ROWS_HI, TQ), F32)], axis=0).astype(BF16)
        return _dot_nt(eye, full).astype(BF16)

    for h in range(N_KV):
        heads = [h * GROUP + g for g in range(GROUP)]
        q_lo = [q_low(qa_ref, hh) for hh in heads]
        srow = [_slope_row(SLOPES_NSA[hh], lane) for hh in heads]

        q64 = jnp.concatenate([q[:, :HEAD_DIM] for q in q_lo], axis=0)
        kc = kc_ref[0, 0, :, head_cols(h)]
        vc = vc_ref[0, 0, :, head_cols(h)]
        end_c = lane * CMP_STRIDE + (CMP_LEN - 1)
        cmask = (t_col >= end_c) & (lane < n_cmp)
        s_all = _dot_nt(q64, kc)
        o_cmp = []
        psum = jnp.zeros((TQ, LANES), F32)
        for g in range(GROUP):
            s = s_all[g * TQ:(g + 1) * TQ] + SLOPES_NSA[heads[g]] * end_c.astype(F32)
            s = jnp.where(cmask, s, NEG_INF)
            m = jnp.max(s, axis=-1, keepdims=True)
            e = jnp.where(cmask, jnp.exp(s - m), 0.0)
            z = jnp.sum(e, axis=-1, keepdims=True)
            p = e / jnp.where(z > 0, z, 1.0)
            o_cmp.append(_dot(p.astype(BF16), vc))
            psum = psum + p

        p_hi = psum.astype(BF16)
        p_lo = (psum - p_hi.astype(F32)).astype(BF16)
        imp = (_dot_nt(ovt_ref[...], p_hi) + _dot_nt(ovt_ref[...], p_lo))[ROWS_LO:ROWS_HI]
        cur = lax.shift_right_logical(t_row, int(np.log2(SLC_LEN)))
        valid = in_rng & (blk * SLC_LEN <= t_row)
        forced = in_rng & ((blk == 0) | (blk == cur) | (blk == cur - 1))
        score = jnp.where(forced, FORCE_SCORE, jnp.where(valid, imp, NEG_INF))
        rank = jnp.zeros(score.shape, F32)
        for i in range(N_SLC_BLK):
            r = FEAT_SEL - ROWS_LO + i
            si = score[r:r + 1, :]
            rank = rank + jnp.where((si > score) | ((si == score) & (blk > i)), 1.0, 0.0)
        sel = in_rng & (rank < SLC_TOPK) & (score > 0.5 * NEG_INF)
        tail_diag = to_lanes(jnp.where(in_rng & jnp.logical_not(sel), NEG_INF, 0.0))
        tail_main = to_lanes(jnp.where(in_rng & jnp.logical_not(sel & (blk < 2 * qi)), NEG_INF, 0.0))
        is_pos = (lane >= FEAT_POS) & (lane < FEAT_SEL)

        q_d = q_aug(q_lo, [jnp.where(is_pos, srow[g], tail_diag) for g in range(GROUP)])
        q_m = q_aug(q_lo, [jnp.where(is_pos, srow[g], tail_main) for g in range(GROUP)])
        k_d = sect(0, 0, h)[0, pl.ds(q0, TQ), :]
        v_d = sect(1, 0, h)[0, pl.ds(q0, TQ), :]
        s = _masked_rows(_dot_nt(q_d, k_d), t_row <= t_col)
        m0 = jnp.max(s, axis=-1, keepdims=True)
        acc0 = _dot(jnp.exp(s - m0).astype(BF16), v_d)

        def slc_step(c, carry):
            m_i, acc = carry
            start = pl.multiple_of(c * KCHUNK, KCHUNK)
            s = _dot_nt(q_m, sect(0, 0, h)[0, pl.ds(start, KCHUNK), :])
            m_new = jnp.maximum(m_i, jnp.max(s, axis=-1, keepdims=True))
            pv = _dot(jnp.exp(s - m_new).astype(BF16), sect(1, 0, h)[0, pl.ds(start, KCHUNK), :])
            return m_new, jnp.exp(m_i - m_new) * acc + pv

        n_main = lax.shift_right_logical(q0 + (KCHUNK - 1), int(np.log2(KCHUNK)))
        _, acc = lax.fori_loop(0, n_main, slc_step, (m0, acc0))
        o_slc_all = acc[:, :HEAD_DIM] / acc[:, HEAD_DIM:HEAD_DIM + 1]

        nw = NSA_WINDOW // TQ + 1
        start = pl.multiple_of(jnp.maximum(qi - (nw - 1), 0) * TQ, TQ)
        dist = t_col - (start + lax.broadcasted_iota(jnp.int32, (1, nw * TQ), 1))
        q_w = q_aug(q_lo, srow)
        s = _masked_rows(_dot_nt(q_w, sect(0, 1, h)[0, pl.ds(start, nw * TQ), :]), (dist >= 0) & (dist < NSA_WINDOW))
        m = jnp.max(s, axis=-1, keepdims=True)
        acc = _dot(jnp.exp(s - m).astype(BF16), sect(1, 1, h)[0, pl.ds(start, nw * TQ), :])
        o_win_all = acc[:, :HEAD_DIM] / acc[:, HEAD_DIM:HEAD_DIM + 1]

        for g in range(GROUP):
            hh = heads[g]
            rows = slice(g * TQ, (g + 1) * TQ)
            o = (gates[:, 3 * hh:3 * hh + 1] * o_cmp[g] + gates[:, 3 * hh + 1:3 * hh + 2] * o_slc_all[rows]
                 + gates[:, 3 * hh + 2:3 * hh + 3] * o_win_all[rows])
            oa_ref[0, :, head_cols(hh)] = o.astype(oa_ref.dtype)

    nb = (SWA_WINDOW - 1 + TQ - 1) // TQ + 1
    start = pl.multiple_of(jnp.maximum(qi - (nb - 1), 0) * TQ, TQ)
    dist = t_col - (start + lax.broadcasted_iota(jnp.int32, (1, nb * TQ), 1))
    bmask = (dist >= 0) & (dist < SWA_WINDOW)
    for h in range(N_KV):
        heads = [h * GROUP + g for g in range(GROUP)]
        q_b = q_aug([q_low(qb_ref, hh) for hh in heads], [_slope_row(SLOPES_SWA[hh], lane) for hh in heads])
        s = _masked_rows(_dot_nt(q_b, sect(0, 2, h)[0, pl.ds(start, nb * TQ), :]), bmask)
        sink = jnp.concatenate([sinks_ref[hh] + SLOPES_SWA[hh] * t_col.astype(F32) for hh in heads], axis=0)
        m = jnp.maximum(jnp.max(s, axis=-1, keepdims=True), sink)
        acc = _dot(jnp.exp(s - m).astype(BF16), sect(1, 2, h)[0, pl.ds(start, nb * TQ), :])
        o_all = acc[:, :HEAD_DIM] / (acc[:, HEAD_DIM:HEAD_DIM + 1] + jnp.exp(sink - m))
        for g in range(GROUP):
            ob_ref[0, :, head_cols(heads[g])] = o_all[g * TQ:(g + 1) * TQ].astype(ob_ref.dtype)


def _attention(proj3, kvc, ovt, sinks):
    bsz, seq, _ = proj3.shape
    n_cmp = seq // CMP_STRIDE - CMP_LEN // CMP_STRIDE + 1
    width = N_HEADS * HEAD_DIM
    kvw = N_KV * HEAD_DIM
    qspec = lambda col: pl.BlockSpec((1, TQ, width), lambda b, q, s: (b, q, col))
    cspec = lambda j: pl.BlockSpec((1, 1, kvc.shape[2], kvw), lambda b, q, s: (b, j, 0, 0))
    in_specs = [qspec(0), qspec(COL_QB // width), cspec(0), cspec(1)]
    in_specs += [pl.BlockSpec((1, seq, LANES), lambda b, q, s, j=j: (b, 0, COL_KV // LANES + j))
                 for j in range(N_KV_SECT)]
    in_specs += [pl.BlockSpec((1, TQ, LANES), lambda b, q, s: (b, q, COL_GN // LANES)),
                 pl.BlockSpec((LANES, LANES), lambda b, q, s: (0, 0))]
    ospec = pl.BlockSpec((1, TQ, width), lambda b, q, s: (b, q, 0))
    return pl.pallas_call(
        functools.partial(_attn_kernel, n_cmp=n_cmp),
        grid_spec=pltpu.PrefetchScalarGridSpec(
            num_scalar_prefetch=1, grid=(bsz, seq // TQ), in_specs=in_specs, out_specs=[ospec, ospec]),
        out_shape=[jax.ShapeDtypeStruct((bsz, seq, width), BF16)] * 2,
        compiler_params=pltpu.CompilerParams(dimension_semantics=("parallel", "parallel"),
                                             vmem_limit_bytes=VMEM_LIMIT),
        name="attn",
    )(sinks, proj3, proj3, kvc, kvc, *([proj3] * N_KV_SECT), proj3, ovt)


def _merge_kernel(x_ref, oa_ref, ob_ref, g1_ref, g2_ref, wg_ref, wa_ref, wb_ref, wo_ref,
                  wrh_ref, wrl_ref, br_ref, x2_ref, hn_ref, route_ref, cnt_ref):
    tm = x_ref.shape[0]

    @pl.when(pl.program_id(0) == 0)
    def _():
        cnt_ref[...] = jnp.zeros_like(cnt_ref)

    x = x_ref[...]
    h = _rms(x, g1_ref[...]).astype(BF16)
    gm = jax.nn.sigmoid(_dot(h, wg_ref[...]))
    a = _dot(oa_ref[...], wa_ref[...])
    b = _dot(ob_ref[...], wb_ref[...])
    mixin = gm[:, :D_MODEL] * a + gm[:, D_MODEL:] * b
    x2 = x + _dot(mixin.astype(BF16), wo_ref[...])
    x2_ref[...] = x2
    hn = _rms(x2, g2_ref[...])
    hn_b = hn.astype(BF16)
    for j in range(D_MODEL // LANES):
        hn_ref[:, j, :] = hn[:, j * LANES:(j + 1) * LANES]

    hn_lo = (hn - hn_b.astype(F32)).astype(BF16)
    logits = (_dot(hn_b, wrh_ref[...]) + _dot(hn_lo, wrh_ref[...]) + _dot(hn_b, wrl_ref[...])
              + br_ref[...])
    lane = lax.broadcasted_iota(jnp.int32, (1, LANES), 1)
    lanef = lane.astype(F32)
    big = float(LANES)
    is_g = (lane >= N_EXPERTS) & (lane < N_EXPERTS + N_GROUPS)
    gl = jnp.where(is_g, logits, NEG_INF)
    gmax = jnp.max(gl, axis=-1, keepdims=True)
    grp = jnp.min(jnp.where(gl == gmax, lanef, big), axis=-1, keepdims=True) - N_EXPERTS
    p_grp = 1.0 / jnp.sum(jnp.where(is_g, jnp.exp(gl - gmax), 0.0), axis=-1, keepdims=True)
    in_grp = (lanef >= grp * EPG) & (lanef < grp * EPG + EPG)
    el = jnp.where(in_grp, logits, NEG_INF)
    v0 = jnp.max(el, axis=-1, keepdims=True)
    i0 = jnp.min(jnp.where(el == v0, lanef, big), axis=-1, keepdims=True)
    el1 = jnp.where(lanef == i0, NEG_INF, el)
    v1 = jnp.max(el1, axis=-1, keepdims=True)
    i1 = jnp.min(jnp.where(el1 == v1, lanef, big), axis=-1, keepdims=True)
    e1 = jnp.exp(v1 - v0)
    w0 = p_grp / (1.0 + e1)
    w1 = p_grp * e1 / (1.0 + e1)

    oh0 = jnp.where(lanef == i0, 1.0, 0.0)
    oh1 = jnp.where(lanef == i1, 1.0, 0.0)
    oh = oh0 + oh1
    r_i = lax.broadcasted_iota(jnp.int32, (tm, tm), 0)
    c_i = lax.broadcasted_iota(jnp.int32, (tm, tm), 1)
    lower = jnp.where(c_i < r_i, 1.0, 0.0).astype(BF16)
    before = cnt_ref[...] + _dot(lower, oh.astype(BF16))
    rank0 = jnp.sum(oh0 * before, axis=-1, keepdims=True)
    rank1 = jnp.sum(oh1 * before, axis=-1, keepdims=True)
    cnt_ref[...] = cnt_ref[...] + jnp.sum(oh, axis=0, keepdims=True)

    route = jnp.where(lane == 0, i0, jnp.where(lane == 1, i1, jnp.where(lane == 2, rank0,
            jnp.where(lane == 3, rank1, jnp.where(lane == 4, w0, jnp.where(lane == 5, w1, 0.0))))))
    route_ref[...] = route


def _merge(x2d, oa, ob, g1, g2, wg, wa, wb, wo, wrh, wrl, br, tm):
    n = x2d.shape[0]
    width = N_HEADS * HEAD_DIM
    row = lambda w: pl.BlockSpec((tm, w), lambda i: (i, 0))
    full = lambda a: pl.BlockSpec(a.shape, lambda i: (0, 0))
    return pl.pallas_call(
        _merge_kernel,
        grid=(n // tm,),
        in_specs=[row(D_MODEL), row(width), row(width), full(g1), full(g2), full(wg), full(wa), full(wb),
                  full(wo), full(wrh), full(wrl), full(br)],
        out_specs=[row(D_MODEL), pl.BlockSpec((tm, D_MODEL // LANES, LANES), lambda i: (i, 0, 0)), row(LANES),
                   pl.BlockSpec((1, LANES), lambda i: (0, 0))],
        out_shape=[jax.ShapeDtypeStruct((n, D_MODEL), F32), jax.ShapeDtypeStruct((n, D_MODEL // LANES, LANES), F32),
                   jax.ShapeDtypeStruct((n, LANES), F32), jax.ShapeDtypeStruct((1, LANES), F32)],
        compiler_params=pltpu.CompilerParams(dimension_semantics=("arbitrary",),
                                             vmem_limit_bytes=VMEM_LIMIT),
        name="merge",
    )(x2d, oa, ob, g1, g2, wg, wa, wb, wo, wrh, wrl, br)


def _dispatch_kernel(dest_ref, hn_ref, buf_in_ref, buf_ref, sem, *, td):
    del buf_in_ref

    def row_copy(i, d):
        return pltpu.make_async_copy(hn_ref.at[i], buf_ref.at[d], sem)

    def issue(i, c):
        row_copy(i, dest_ref[0, 0, 2 * i]).start()
        row_copy(i, dest_ref[0, 0, 2 * i + 1]).start()
        return c

    lax.fori_loop(0, td, issue, 0, unroll=8)

    def drain(i, c):
        row_copy(0, 0).wait()
        return c

    lax.fori_loop(0, 2 * td, drain, 0, unroll=8)


def _dispatch(dest2d, hn, buf0, td):
    n = hn.shape[0]
    return pl.pallas_call(
        functools.partial(_dispatch_kernel, td=td),
        grid=(n // td,),
        in_specs=[pl.BlockSpec((1, 1, 2 * td), lambda i: (i, 0, 0), memory_space=pltpu.SMEM),
                  pl.BlockSpec((td,) + hn.shape[1:], lambda i: (i, 0, 0)), pl.BlockSpec(memory_space=pl.ANY)],
        out_specs=pl.BlockSpec(memory_space=pl.ANY),
        out_shape=jax.ShapeDtypeStruct(buf0.shape, buf0.dtype),
        scratch_shapes=[pltpu.SemaphoreType.DMA(())],
        input_output_aliases={2: 0},
        compiler_params=pltpu.CompilerParams(dimension_semantics=("arbitrary",)),
        name="dispatch",
    )(dest2d, hn, buf0)


def _expert_kernel(be_ref, nu_ref, xb_ref, wg_ref, wu_ref, wd_ref, y_ref):
    del be_ref

    @pl.when(pl.program_id(0) < nu_ref[0])
    def _():
        xb = jnp.concatenate([xb_ref[:, j, :] for j in range(D_MODEL // LANES)], axis=1).astype(BF16)
        g = _dot(xb, wg_ref[0])
        u = _dot(xb, wu_ref[0])
        y = _dot((g * jax.nn.sigmoid(g) * u).astype(BF16), wd_ref[0])
        for j in range(D_MODEL // LANES):
            y_ref[:, j, :] = y[:, j * LANES:(j + 1) * LANES]

    @pl.when(pl.program_id(0) >= nu_ref[0])
    def _():
        y_ref[...] = jnp.zeros_like(y_ref)


def _experts(block_e, n_used, buf, wg, wu, wd, tb):
    nblk = buf.shape[0] // tb
    live = lambda b, be, nu: jnp.minimum(b, nu[0] - 1)
    return pl.pallas_call(
        _expert_kernel,
        grid_spec=pltpu.PrefetchScalarGridSpec(
            num_scalar_prefetch=2, grid=(nblk,),
            in_specs=[pl.BlockSpec((tb, D_MODEL // LANES, LANES), lambda b, be, nu: (live(b, be, nu), 0, 0)),
                      pl.BlockSpec((1, D_MODEL, EXPERT_FF), lambda b, be, nu: (be[live(b, be, nu)], 0, 0)),
                      pl.BlockSpec((1, D_MODEL, EXPERT_FF), lambda b, be, nu: (be[live(b, be, nu)], 0, 0)),
                      pl.BlockSpec((1, EXPERT_FF, D_MODEL), lambda b, be, nu: (be[live(b, be, nu)], 0, 0))],
            out_specs=pl.BlockSpec((tb, D_MODEL // LANES, LANES), lambda b, be, nu: (b, 0, 0))),
        out_shape=jax.ShapeDtypeStruct(buf.shape, F32),
        compiler_params=pltpu.CompilerParams(dimension_semantics=("arbitrary",),
                                             vmem_limit_bytes=VMEM_LIMIT),
        name="experts",
    )(block_e, n_used, buf, wg, wu, wd)


def _combine_kernel(dest_ref, x2_ref, route_ref, gf_ref, yb_ref, o_ref, rows, sem, *, tc):
    def row_copy(d, k, i):
        return pltpu.make_async_copy(yb_ref.at[d], rows.at[k, i], sem)

    def issue(i, c):
        row_copy(dest_ref[0, 0, 2 * i], 0, i).start()
        row_copy(dest_ref[0, 0, 2 * i + 1], 1, i).start()
        return c

    lax.fori_loop(0, tc, issue, 0, unroll=8)

    def drain(i, c):
        row_copy(0, 0, 0).wait()
        return c

    lax.fori_loop(0, 2 * tc, drain, 0, unroll=8)
    route = route_ref[...]
    y0 = jnp.concatenate([rows[0, :, j, :] for j in range(D_MODEL // LANES)], axis=1)
    y1 = jnp.concatenate([rows[1, :, j, :] for j in range(D_MODEL // LANES)], axis=1)
    y = x2_ref[...] + (route[:, 4:5] * y0 + route[:, 5:6] * y1)
    o_ref[...] = _rms(y, gf_ref[...])


def _combine(dest2d, x2, route, gf, yb, tc):
    n = x2.shape[0]
    return pl.pallas_call(
        functools.partial(_combine_kernel, tc=tc),
        grid=(n // tc,),
        in_specs=[pl.BlockSpec((1, 1, 2 * tc), lambda i: (i, 0, 0), memory_space=pltpu.SMEM),
                  pl.BlockSpec((tc, D_MODEL), lambda i: (i, 0)),
                  pl.BlockSpec((tc, LANES), lambda i: (i, 0)),
                  pl.BlockSpec((1, D_MODEL), lambda i: (0, 0)),
                  pl.BlockSpec(memory_space=pl.ANY)],
        out_specs=pl.BlockSpec((tc, D_MODEL), lambda i: (i, 0)),
        out_shape=jax.ShapeDtypeStruct((n, D_MODEL), F32),
        scratch_shapes=[pltpu.VMEM((2, tc, D_MODEL // LANES, LANES), F32), pltpu.SemaphoreType.DMA(())],
        compiler_params=pltpu.CompilerParams(dimension_semantics=("arbitrary",),
                                             vmem_limit_bytes=VMEM_LIMIT),
        name="combine",
    )(dest2d, x2, route, gf, yb)


def _overlap_matrix_t(seq):
    nc = seq // CMP_STRIDE - CMP_LEN // CMP_STRIDE + 1
    ns = seq // SLC_LEN
    c0 = np.arange(nc) * CMP_STRIDE
    s0 = np.arange(ns) * SLC_LEN
    ov = np.clip(np.minimum(c0[:, None] + CMP_LEN, s0[None, :] + SLC_LEN)
                 - np.maximum(c0[:, None], s0[None, :]), 0, None) / CMP_LEN
    out = np.zeros((LANES, LANES), np.float32)
    out[FEAT_SEL:FEAT_SEL + ns, :nc] = ov.T
    return jnp.asarray(out, BF16)


def _position_features(seq):
    pos = np.arange(seq)
    fk = np.zeros((seq, LANES), np.float32)
    fk[:, FEAT_POS:FEAT_POS + 3] = (pos // SLC_LEN)[:, None]
    fk[:, FEAT_POS + 3:FEAT_POS + 6] = (pos % SLC_LEN)[:, None]
    fk[pos, FEAT_SEL + pos // SLC_LEN] = 1.0
    fv = np.zeros((seq, LANES), np.float32)
    fv[:, HEAD_DIM] = 1.0
    return jnp.asarray(fk), jnp.asarray(fv)


def _pick_tile(n, pref):
    t = pref
    while n % t:
        t //= 2
    return t


def kernel(x, norm_mix_g, w_in, cmp_pos_k, cmp_w1_k, cmp_w2_k, cmp_pos_v, cmp_w1_v, cmp_w2_v, sinks, w_a, w_b,
           w_o, norm_ffn_g, w_group, b_group, w_expert, b_expert, w_gate_e, w_up_e, w_down_e, norm_final_g):
    bsz, seq, _ = x.shape
    n = bsz * seq
    assert TQ == LANES and seq % KCHUNK == 0 and seq // SLC_LEN <= N_SLC_BLK and seq // CMP_STRIDE <= LANES
    assert seq >= (NSA_WINDOW // TQ + 1) * TQ and w_in.shape[0] == 1
    x2d = x.reshape(n, D_MODEL)

    w = w_in[0]
    scale = HEAD_DIM ** -0.5
    nsa_w, kvw = N_HEADS * HEAD_DIM, N_KV * HEAD_DIM
    o_qa, o_kva, o_gn = 0, nsa_w, nsa_w + 6 * kvw
    o_qb = o_gn + 3 * N_HEADS
    o_kvb = o_qb + nsa_w
    o_gm = o_kvb + 2 * kvw
    zpad = jnp.zeros((D_MODEL, LANES - HEAD_DIM), F32)
    sect_cols = []
    for off in (o_kva + 2 * kvw, o_kva + 3 * kvw, o_kva + 4 * kvw, o_kva + 5 * kvw, o_kvb, o_kvb + kvw):
        for h in range(N_KV):
            sect_cols += [w[:, off + h * HEAD_DIM:off + (h + 1) * HEAD_DIM], zpad]
    w_attn = jnp.concatenate(
        [w[:, o_qa:o_qa + nsa_w] * scale, w[:, o_qb:o_qb + nsa_w] * scale] + sect_cols
        + [w[:, o_kva:o_kva + 2 * kvw], w[:, o_gn:o_gn + 3 * N_HEADS],
           jnp.zeros((D_MODEL, LANES - 3 * N_HEADS), F32)], axis=1).astype(BF16)
    w_gm = w[:, o_gm:o_gm + 2 * D_MODEL].astype(BF16)

    tm = _pick_tile(seq, 512)
    feat_k, feat_v = _position_features(seq)
    proj = _proj(x2d, norm_mix_g[0][None], w_attn, feat_k, feat_v, tm)
    proj3 = proj.reshape(bsz, seq, PROJ_W)

    nch = seq // CMP_STRIDE
    r = proj3[:, :, COL_CMP:COL_CMP + 2 * kvw].reshape(bsz, nch, CMP_STRIDE, 2, N_KV, HEAD_DIM)
    r = jnp.transpose(r, (0, 3, 4, 1, 2, 5)).reshape(bsz, 2, N_KV, nch, CMP_STRIDE * HEAD_DIM)
    half = CMP_STRIDE * HEAD_DIM
    pos = jnp.stack([cmp_pos_k[0], cmp_pos_v[0]])
    pos_a = pos[:, :CMP_STRIDE].reshape(2, 1, half)
    pos_b = pos[:, CMP_STRIDE:].reshape(2, 1, half)
    w1 = jnp.stack([cmp_w1_k[0], cmp_w1_v[0]]).astype(BF16)
    w2 = jnp.stack([cmp_w2_k[0], cmp_w2_v[0]]).astype(BF16)
    kvc = _compress(r, pos_a, pos_b, w1[:, :half], w1[:, half:], w2)
    kvc = jnp.pad(kvc, ((0, 0), (0, 0), (0, LANES - nch), (0, 0)))

    o_a, o_b = _attention(proj3, kvc, _overlap_matrix_t(seq), sinks[0])

    w_r = jnp.concatenate([w_expert[0], w_group[0],
                           jnp.zeros((D_MODEL, LANES - N_EXPERTS - N_GROUPS), F32)], axis=1)
    w_rh = w_r.astype(BF16)
    w_rl = (w_r - w_rh.astype(F32)).astype(BF16)
    b_r = jnp.concatenate([b_expert[0], b_group[0], jnp.zeros((LANES - N_EXPERTS - N_GROUPS,), F32)])[None]
    x2, hn, route, counts = _merge(
        x2d, o_a.reshape(n, nsa_w), o_b.reshape(n, nsa_w), norm_mix_g[0][None], norm_ffn_g[0][None], w_gm,
        w_a[0].astype(BF16), w_b[0].astype(BF16), w_o[0].astype(BF16), w_rh, w_rl, b_r, _pick_tile(n, 256))

    tb = 256
    nblk = -(-(2 * n + N_EXPERTS * (tb - 1)) // tb)
    cnt = counts[0, :N_EXPERTS].astype(jnp.int32)
    padded = (cnt + tb - 1) // tb * tb
    pad_end = jnp.cumsum(padded)
    pad_start = pad_end - padded
    block_e = jnp.minimum(jnp.sum(pad_end[None, :] <= (jnp.arange(nblk) * tb)[:, None], axis=1), N_EXPERTS - 1)
    n_used = (pad_end[-1:] // tb).astype(jnp.int32)
    eid = route[:, 0:2].astype(jnp.int32)
    dest = pad_start[eid] + route[:, 2:4].astype(jnp.int32)

    td = _pick_tile(n, 512)
    buf = _dispatch(dest.reshape(n // td, 1, 2 * td), hn, jnp.zeros((nblk * tb, D_MODEL // LANES, LANES), F32), td)
    yb = _experts(block_e.astype(jnp.int32), n_used, buf, w_gate_e[0].astype(BF16), w_up_e[0].astype(BF16),
                  w_down_e[0].astype(BF16), tb)
    tc = _pick_tile(n, 256)
    out = _combine(dest.reshape(n // tc, 1, 2 * tc), x2, route, norm_final_g[None], yb, tc)
    return out.reshape(bsz, seq, D_MODEL)
```

```python
import functools

import numpy as np
import jax
import jax.numpy as jnp
from jax import lax
from jax.experimental import pallas as pl
from jax.experimental.pallas import tpu as pltpu

F32 = jnp.float32
BF16 = jnp.bfloat16

D_MODEL = 1024
HEAD_DIM = 64
N_HEADS = 8
N_KV = 2
GROUP = N_HEADS // N_KV
CMP_LEN = 32
CMP_STRIDE = 16
CMP_HIDDEN = 256
SLC_LEN = 64
SLC_TOPK = 8
NSA_WINDOW = 256
SWA_WINDOW = 128
N_GROUPS = 4
EPG = 8
N_EXPERTS = N_GROUPS * EPG
EXPERT_FF = 256
RMS_EPS = 1e-6
NEG_INF = -1e30
FORCE_SCORE = 1e9

LANES = 128
TQ = 128
KCHUNK = 512
N_SLC_BLK = LANES // 4
FEAT_POS = HEAD_DIM
FEAT_SEL = HEAD_DIM + 6
ROWS_LO, ROWS_HI = 64, 104
N_KV_SECT = 12
COL_QB = 512
COL_KV = 1024
COL_GN = COL_KV + N_KV_SECT * LANES
PROJ_W = COL_GN + LANES
CMP_W = 2 * LANES
VMEM_LIMIT = 56 * 1024 * 1024


def _alibi_slopes():
    n = 2 * N_HEADS
    s = 2.0 ** (-8.0 * np.arange(1, n + 1) / n)
    return [float(v) for v in s[:N_HEADS]], [float(v) for v in s[N_HEADS:]]


SLOPES_SWA, SLOPES_NSA = _alibi_slopes()


def _bf16_pieces(v):
    out, rem = [], np.float32(v)
    for _ in range(3):
        p = np.float32(np.asarray(rem, np.float32).astype(BF16).astype(np.float32))
        out.append(float(p))
        rem = np.float32(rem - p)
    return out


def _rms(x, g):
    return x * lax.rsqrt(jnp.mean(x * x, axis=-1, keepdims=True) + RMS_EPS) * g


def _dot(a, b):
    return jnp.dot(a, b, preferred_element_type=F32)


def _dot_nt(a, b):
    return lax.dot_general(a, b, (((1,), (1,)), ((), ())), preferred_element_type=F32)


def _proj_kernel(x_ref, g_ref, w_ref, fk_ref, fv_ref, o_ref, cmp_ref):
    h = _rms(x_ref[...], g_ref[...]).astype(BF16)
    res = _dot(h, w_ref[...])
    o_ref[:, :COL_KV] = res[:, :COL_KV].astype(o_ref.dtype)
    for j in range(N_KV_SECT):
        feat = fk_ref[...] if (j // N_KV) % 2 == 0 else fv_ref[...]
        c0 = COL_KV + j * LANES
        o_ref[:, c0:c0 + LANES] = (res[:, c0:c0 + LANES] + feat).astype(o_ref.dtype)
    o_ref[:, COL_GN:] = res[:, COL_GN:PROJ_W].astype(o_ref.dtype)
    cmp_ref[...] = res[:, PROJ_W:]


def _proj(x2d, g, w, feat_k, feat_v, tm):
    n = x2d.shape[0]
    nper = feat_k.shape[0] // tm
    return pl.pallas_call(
        _proj_kernel,
        grid=(n // tm,),
        in_specs=[pl.BlockSpec((tm, D_MODEL), lambda i: (i, 0)),
                  pl.BlockSpec((1, D_MODEL), lambda i: (0, 0)),
                  pl.BlockSpec((D_MODEL, PROJ_W + CMP_W), lambda i: (0, 0)),
                  pl.BlockSpec((tm, LANES), lambda i: (i % nper, 0)),
                  pl.BlockSpec((tm, LANES), lambda i: (i % nper, 0))],
        out_specs=[pl.BlockSpec((tm, PROJ_W), lambda i: (i, 0)), pl.BlockSpec((tm, CMP_W), lambda i: (i, 0))],
        out_shape=[jax.ShapeDtypeStruct((n, PROJ_W), BF16), jax.ShapeDtypeStruct((n, CMP_W), F32)],
        compiler_params=pltpu.CompilerParams(dimension_semantics=("parallel",),
                                             vmem_limit_bytes=VMEM_LIMIT),
        name="proj",
    )(x2d, g, w, feat_k, feat_v)


def _compress_kernel(x_ref, pa_ref, pb_ref, w1a_ref, w1b_ref, w2_ref, o_ref, *, nch):
    r = jnp.concatenate([x_ref[0, pl.ds(j, nch, stride=CMP_STRIDE), :] for j in range(CMP_STRIDE)], axis=1)
    a = _dot((r + pa_ref[0]).astype(BF16), w1a_ref[0])
    b = _dot((r + pb_ref[0]).astype(BF16), w1b_ref[0])
    hid = a + pltpu.roll(b, nch - 1, 0)
    hid = hid * jax.nn.sigmoid(hid)
    for h in range(N_KV):
        o_ref[0, 0, :, h * HEAD_DIM:(h + 1) * HEAD_DIM] = _dot(
            hid[:, h * CMP_HIDDEN:(h + 1) * CMP_HIDDEN].astype(BF16), w2_ref[0]).astype(o_ref.dtype)


def _compress(cmp3, pos_a, pos_b, w1a, w1b, w2):
    bsz, seq, _ = cmp3.shape
    nch = seq // CMP_STRIDE
    wspec = lambda a: pl.BlockSpec((1,) + a.shape[1:], lambda b, j: (j, 0, 0))
    return pl.pallas_call(
        functools.partial(_compress_kernel, nch=nch),
        grid=(bsz, 2),
        in_specs=[pl.BlockSpec((1, seq, LANES), lambda b, j: (b, 0, j)),
                  wspec(pos_a), wspec(pos_b), wspec(w1a), wspec(w1b), wspec(w2)],
        out_specs=pl.BlockSpec((1, 1, nch, N_KV * HEAD_DIM), lambda b, j: (b, j, 0, 0)),
        out_shape=jax.ShapeDtypeStruct((bsz, 2, nch, N_KV * HEAD_DIM), BF16),
        compiler_params=pltpu.CompilerParams(dimension_semantics=("parallel", "parallel"),
                                             vmem_limit_bytes=VMEM_LIMIT),
        name="compress",
    )(cmp3, pos_a, pos_b, w1a, w1b, w2)


def _slope_row(slope, lane):
    hi, mid, lo = _bf16_pieces(slope)
    row = jnp.zeros((1, LANES), F32)
    for i, v in enumerate([SLC_LEN * hi, SLC_LEN * mid, SLC_LEN * lo, hi, mid, lo]):
        row = jnp.where(lane == FEAT_POS + i, v, row)
    return row.astype(BF16)


def _masked_rows(s, mask):
    return jnp.concatenate([jnp.where(mask, s[g * TQ:(g + 1) * TQ], NEG_INF) for g in range(GROUP)], axis=0)


def _attn_kernel(sinks_ref, qa_ref, qb_ref, kc_ref, vc_ref, *rest, n_cmp):
    kv = rest[:N_KV_SECT]
    gn_ref, ovt_ref, oa_ref, ob_ref = rest[N_KV_SECT:]
    sect = lambda is_v, branch, h: kv[(2 * branch + is_v) * N_KV + h]
    qi = pl.program_id(1)
    q0 = pl.multiple_of(qi * TQ, TQ)
    lane = lax.broadcasted_iota(jnp.int32, (1, LANES), 1)
    t_col = q0 + lax.broadcasted_iota(jnp.int32, (TQ, 1), 0)
    t_row = q0 + lane
    eye = jnp.where(lax.broadcasted_iota(jnp.int32, (TQ, TQ), 0) == lax.broadcasted_iota(jnp.int32, (TQ, TQ), 1),
                    1.0, 0.0).astype(BF16)
    gates = jax.nn.sigmoid(gn_ref[0].astype(F32))
    blk = lax.broadcasted_iota(jnp.int32, (ROWS_HI - ROWS_LO, 1), 0) + (ROWS_LO - FEAT_SEL)
    in_rng = (blk >= 0) & (blk < N_SLC_BLK)

    def head_cols(h):
        return slice(h * HEAD_DIM, (h + 1) * HEAD_DIM)

    def q_low(ref, hh):
        pair = ref[0, :, (hh // 2) * LANES:(hh // 2 + 1) * LANES]
        if hh % 2 == 0:
            return pair
        return pltpu.roll(pair.astype(F32), HEAD_DIM, 1).astype(BF16)

    def q_aug(q_lo, tails):
        return jnp.concatenate([jnp.where(lane < HEAD_DIM, q, t) for q, t in zip(q_lo, tails)], axis=0)

    def to_lanes(bias_rows):
        full = jnp.concatenate([jnp.zeros((ROWS_LO, TQ), F32), bias_rows,
                                jnp.zeros((LANES - ROWS_HI, TQ), F32)], axis=0).astype(BF16)
        return _dot_nt(eye, full).astype(BF16)

    def nsa_front(h):
        heads = [h * GROUP + g for g in range(GROUP)]
        q_lo = [q_low(qa_ref, hh) for hh in heads]
        srow = [_slope_row(SLOPES_NSA[hh], lane) for hh in heads]

        q64 = jnp.concatenate([q[:, :HEAD_DIM] for q in q_lo], axis=0)
        kc = kc_ref[0, 0, :, head_cols(h)]
        vc = vc_ref[0, 0, :, head_cols(h)]
        end_c = lane * CMP_STRIDE + (CMP_LEN - 1)
        cmask = (t_col >= end_c) & (lane < n_cmp)
        s_all = _dot_nt(q64, kc)
        o_cmp = []
        psum = jnp.zeros((TQ, LANES), F32)
        for g in range(GROUP):
            s = s_all[g * TQ:(g + 1) * TQ] + SLOPES_NSA[heads[g]] * end_c.astype(F32)
            s = jnp.where(cmask, s, NEG_INF)
            m = jnp.max(s, axis=-1, keepdims=True)
            e = jnp.where(cmask, jnp.exp(s - m), 0.0)
            z = jnp.sum(e, axis=-1, keepdims=True)
            p = e / jnp.where(z > 0, z, 1.0)
            o_cmp.append(_dot(p.astype(BF16), vc))
            psum = psum + p

        p_hi = psum.astype(BF16)
        p_lo = (psum - p_hi.astype(F32)).astype(BF16)
        imp = (_dot_nt(ovt_ref[...], p_hi) + _dot_nt(ovt_ref[...], p_lo))[ROWS_LO:ROWS_HI]
        cur = lax.shift_right_logical(t_row, int(np.log2(SLC_LEN)))
        valid = in_rng & (blk * SLC_LEN <= t_row)
        forced = in_rng & ((blk == 0) | (blk == cur) | (blk == cur - 1))
        score = jnp.where(forced, FORCE_SCORE, jnp.where(valid, imp, NEG_INF))
        rank = jnp.zeros(score.shape, F32)
        for i in range(N_SLC_BLK):
            r = FEAT_SEL - ROWS_LO + i
            si = score[r:r + 1, :]
            rank = rank + jnp.where((si > score) | ((si == score) & (blk > i)), 1.0, 0.0)
        sel = in_rng & (rank < SLC_TOPK) & (score > 0.5 * NEG_INF)
        tail_diag = to_lanes(jnp.where(in_rng & jnp.logical_not(sel), NEG_INF, 0.0))
        tail_main = to_lanes(jnp.where(in_rng & jnp.logical_not(sel & (blk < 2 * qi)), NEG_INF, 0.0))
        is_pos = (lane >= FEAT_POS) & (lane < FEAT_SEL)

        q_d = q_aug(q_lo, [jnp.where(is_pos, srow[g], tail_diag) for g in range(GROUP)])
        q_m = q_aug(q_lo, [jnp.where(is_pos, srow[g], tail_main) for g in range(GROUP)])
        k_d = sect(0, 0, h)[0, pl.ds(q0, TQ), :]
        v_d = sect(1, 0, h)[0, pl.ds(q0, TQ), :]
        s = _masked_rows(_dot_nt(q_d, k_d), t_row <= t_col)
        m0 = jnp.max(s, axis=-1, keepdims=True)
        acc0 = _dot(jnp.exp(s - m0).astype(BF16), v_d)

        nw = NSA_WINDOW // TQ + 1
        start = pl.multiple_of(jnp.maximum(qi - (nw - 1), 0) * TQ, TQ)
        dist = t_col - (start + lax.broadcasted_iota(jnp.int32, (1, nw * TQ), 1))
        q_w = q_aug(q_lo, srow)
        s = _masked_rows(_dot_nt(q_w, sect(0, 1, h)[0, pl.ds(start, nw * TQ), :]), (dist >= 0) & (dist < NSA_WINDOW))
        m = jnp.max(s, axis=-1, keepdims=True)
        acc = _dot(jnp.exp(s - m).astype(BF16), sect(1, 1, h)[0, pl.ds(start, nw * TQ), :])
        o_win_all = acc[:, :HEAD_DIM] / acc[:, HEAD_DIM:HEAD_DIM + 1]

        part = [gates[:, 3 * hh:3 * hh + 1] * o_cmp[g]
                + gates[:, 3 * hh + 2:3 * hh + 3] * o_win_all[g * TQ:(g + 1) * TQ] for g, hh in enumerate(heads)]
        return q_m, m0, acc0, part

    fronts = [nsa_front(h) for h in range(N_KV)]

    nb = (SWA_WINDOW - 1 + TQ - 1) // TQ + 1
    start = pl.multiple_of(jnp.maximum(qi - (nb - 1), 0) * TQ, TQ)
    dist = t_col - (start + lax.broadcasted_iota(jnp.int32, (1, nb * TQ), 1))
    bmask = (dist >= 0) & (dist < SWA_WINDOW)
    for h in range(N_KV):
        heads = [h * GROUP + g for g in range(GROUP)]
        q_b = q_aug([q_low(qb_ref, hh) for hh in heads], [_slope_row(SLOPES_SWA[hh], lane) for hh in heads])
        s = _masked_rows(_dot_nt(q_b, sect(0, 2, h)[0, pl.ds(start, nb * TQ), :]), bmask)
        sink = jnp.concatenate([sinks_ref[hh] + SLOPES_SWA[hh] * t_col.astype(F32) for hh in heads], axis=0)
        m = jnp.maximum(jnp.max(s, axis=-1, keepdims=True), sink)
        acc = _dot(jnp.exp(s - m).astype(BF16), sect(1, 2, h)[0, pl.ds(start, nb * TQ), :])
        o_all = acc[:, :HEAD_DIM] / (acc[:, HEAD_DIM:HEAD_DIM + 1] + jnp.exp(sink - m))
        for g in range(GROUP):
            ob_ref[0, :, head_cols(heads[g])] = o_all[g * TQ:(g + 1) * TQ].astype(ob_ref.dtype)

    def slc_step(c, carry):
        start = pl.multiple_of(c * KCHUNK, KCHUNK)
        out = []
        for h, (m_i, acc) in enumerate(carry):
            s = _dot_nt(fronts[h][0], sect(0, 0, h)[0, pl.ds(start, KCHUNK), :])
            m_new = jnp.maximum(m_i, jnp.max(s, axis=-1, keepdims=True))
            pv = _dot(jnp.exp(s - m_new).astype(BF16), sect(1, 0, h)[0, pl.ds(start, KCHUNK), :])
            out.append((m_new, jnp.exp(m_i - m_new) * acc + pv))
        return tuple(out)

    n_main = lax.shift_right_logical(q0 + (KCHUNK - 1), int(np.log2(KCHUNK)))
    swept = lax.fori_loop(0, n_main, slc_step, tuple((f[1], f[2]) for f in fronts))
    for h in range(N_KV):
        acc = swept[h][1]
        o_slc_all = acc[:, :HEAD_DIM] / acc[:, HEAD_DIM:HEAD_DIM + 1]
        for g in range(GROUP):
            hh = h * GROUP + g
            o = fronts[h][3][g] + gates[:, 3 * hh + 1:3 * hh + 2] * o_slc_all[g * TQ:(g + 1) * TQ]
            oa_ref[0, :, head_cols(hh)] = o.astype(oa_ref.dtype)


def _attention(proj3, kvc, ovt, sinks):
    bsz, seq, _ = proj3.shape
    n_cmp = seq // CMP_STRIDE - CMP_LEN // CMP_STRIDE + 1
    width = N_HEADS * HEAD_DIM
    kvw = N_KV * HEAD_DIM
    qspec = lambda col: pl.BlockSpec((1, TQ, width), lambda b, q, s: (b, q, col))
    cspec = lambda j: pl.BlockSpec((1, 1, kvc.shape[2], kvw), lambda b, q, s: (b, j, 0, 0))
    in_specs = [qspec(0), qspec(COL_QB // width), cspec(0), cspec(1)]
    in_specs += [pl.BlockSpec((1, seq, LANES), lambda b, q, s, j=j: (b, 0, COL_KV // LANES + j))
                 for j in range(N_KV_SECT)]
    in_specs += [pl.BlockSpec((1, TQ, LANES), lambda b, q, s: (b, q, COL_GN // LANES)),
                 pl.BlockSpec((LANES, LANES), lambda b, q, s: (0, 0))]
    ospec = pl.BlockSpec((1, TQ, width), lambda b, q, s: (b, q, 0))
    return pl.pallas_call(
        functools.partial(_attn_kernel, n_cmp=n_cmp),
        grid_spec=pltpu.PrefetchScalarGridSpec(
            num_scalar_prefetch=1, grid=(bsz, seq // TQ), in_specs=in_specs, out_specs=[ospec, ospec]),
        out_shape=[jax.ShapeDtypeStruct((bsz, seq, width), BF16)] * 2,
        compiler_params=pltpu.CompilerParams(dimension_semantics=("parallel", "parallel"),
                                             vmem_limit_bytes=VMEM_LIMIT),
        name="attn",
    )(sinks, proj3, proj3, kvc, kvc, *([proj3] * N_KV_SECT), proj3, ovt)


def _merge_kernel(x_ref, oa_ref, ob_ref, g1_ref, g2_ref, wg_ref, wa_ref, wb_ref, wo_ref,
                  wrh_ref, wrl_ref, br_ref, x2_ref, hn_ref, route_ref, cnt_ref):
    tm = x_ref.shape[0]

    @pl.when(pl.program_id(0) == 0)
    def _():
        cnt_ref[...] = jnp.zeros_like(cnt_ref)

    x = x_ref[...]
    h = _rms(x, g1_ref[...]).astype(BF16)
    gm = jax.nn.sigmoid(_dot(h, wg_ref[...]))
    a = _dot(oa_ref[...], wa_ref[...])
    b = _dot(ob_ref[...], wb_ref[...])
    mixin = gm[:, :D_MODEL] * a + gm[:, D_MODEL:] * b
    x2 = x + _dot(mixin.astype(BF16), wo_ref[...])
    x2_ref[...] = x2
    hn = _rms(x2, g2_ref[...])
    hn_b = hn.astype(BF16)
    for j in range(D_MODEL // LANES):
        hn_ref[:, j, :] = hn[:, j * LANES:(j + 1) * LANES]

    hn_lo = (hn - hn_b.astype(F32)).astype(BF16)
    logits = (_dot(hn_b, wrh_ref[...]) + _dot(hn_lo, wrh_ref[...]) + _dot(hn_b, wrl_ref[...])
              + br_ref[...])
    lane = lax.broadcasted_iota(jnp.int32, (1, LANES), 1)
    lanef = lane.astype(F32)
    big = float(LANES)
    is_g = (lane >= N_EXPERTS) & (lane < N_EXPERTS + N_GROUPS)
    gl = jnp.where(is_g, logits, NEG_INF)
    gmax = jnp.max(gl, axis=-1, keepdims=True)
    grp = jnp.min(jnp.where(gl == gmax, lanef, big), axis=-1, keepdims=True) - N_EXPERTS
    p_grp = 1.0 / jnp.sum(jnp.where(is_g, jnp.exp(gl - gmax), 0.0), axis=-1, keepdims=True)
    in_grp = (lanef >= grp * EPG) & (lanef < grp * EPG + EPG)
    el = jnp.where(in_grp, logits, NEG_INF)
    v0 = jnp.max(el, axis=-1, keepdims=True)
    i0 = jnp.min(jnp.where(el == v0, lanef, big), axis=-1, keepdims=True)
    el1 = jnp.where(lanef == i0, NEG_INF, el)
    v1 = jnp.max(el1, axis=-1, keepdims=True)
    i1 = jnp.min(jnp.where(el1 == v1, lanef, big), axis=-1, keepdims=True)
    e1 = jnp.exp(v1 - v0)
    w0 = p_grp / (1.0 + e1)
    w1 = p_grp * e1 / (1.0 + e1)

    oh0 = jnp.where(lanef == i0, 1.0, 0.0)
    oh1 = jnp.where(lanef == i1, 1.0, 0.0)
    oh = oh0 + oh1
    r_i = lax.broadcasted_iota(jnp.int32, (tm, tm), 0)
    c_i = lax.broadcasted_iota(jnp.int32, (tm, tm), 1)
    lower = jnp.where(c_i < r_i, 1.0, 0.0).astype(BF16)
    before = cnt_ref[...] + _dot(lower, oh.astype(BF16))
    rank0 = jnp.sum(oh0 * before, axis=-1, keepdims=True)
    rank1 = jnp.sum(oh1 * before, axis=-1, keepdims=True)
    cnt_ref[...] = cnt_ref[...] + jnp.sum(oh, axis=0, keepdims=True)

    route = jnp.where(lane == 0, i0, jnp.where(lane == 1, i1, jnp.where(lane == 2, rank0,
            jnp.where(lane == 3, rank1, jnp.where(lane == 4, w0, jnp.where(lane == 5, w1, 0.0))))))
    route_ref[...] = route


def _merge(x2d, oa, ob, g1, g2, wg, wa, wb, wo, wrh, wrl, br, tm):
    n = x2d.shape[0]
    width = N_HEADS * HEAD_DIM
    row = lambda w: pl.BlockSpec((tm, w), lambda i: (i, 0))
    full = lambda a: pl.BlockSpec(a.shape, lambda i: (0, 0))
    return pl.pallas_call(
        _merge_kernel,
        grid=(n // tm,),
        in_specs=[row(D_MODEL), row(width), row(width), full(g1), full(g2), full(wg), full(wa), full(wb),
                  full(wo), full(wrh), full(wrl), full(br)],
        out_specs=[row(D_MODEL), pl.BlockSpec((tm, D_MODEL // LANES, LANES), lambda i: (i, 0, 0)), row(LANES),
                   pl.BlockSpec((1, LANES), lambda i: (0, 0))],
        out_shape=[jax.ShapeDtypeStruct((n, D_MODEL), F32), jax.ShapeDtypeStruct((n, D_MODEL // LANES, LANES), F32),
                   jax.ShapeDtypeStruct((n, LANES), F32), jax.ShapeDtypeStruct((1, LANES), F32)],
        compiler_params=pltpu.CompilerParams(dimension_semantics=("arbitrary",),
                                             vmem_limit_bytes=VMEM_LIMIT),
        name="merge",
    )(x2d, oa, ob, g1, g2, wg, wa, wb, wo, wrh, wrl, br)


def _slot_rows(ps_ref, tab_ref, i):
    return (ps_ref[tab_ref[0, 0, 4 * i]] + tab_ref[0, 0, 4 * i + 2],
            ps_ref[tab_ref[0, 0, 4 * i + 1]] + tab_ref[0, 0, 4 * i + 3])


def _dispatch_kernel(ps_ref, tab_ref, hn_ref, buf_in_ref, buf_ref, sem, *, td):
    del buf_in_ref

    def row_copy(i, d):
        return pltpu.make_async_copy(hn_ref.at[i], buf_ref.at[d], sem)

    def issue(i, c):
        d0, d1 = _slot_rows(ps_ref, tab_ref, i)
        row_copy(i, d0).start()
        row_copy(i, d1).start()
        return c

    lax.fori_loop(0, td, issue, 0, unroll=8)

    def drain(i, c):
        row_copy(0, 0).wait()
        return c

    lax.fori_loop(0, 2 * td, drain, 0, unroll=8)


def _dispatch(pad_start, tab, hn, buf0, td):
    n = hn.shape[0]
    return pl.pallas_call(
        functools.partial(_dispatch_kernel, td=td),
        grid_spec=pltpu.PrefetchScalarGridSpec(
            num_scalar_prefetch=1, grid=(n // td,),
            in_specs=[pl.BlockSpec((1, 1, 4 * td), lambda i, ps: (i, 0, 0), memory_space=pltpu.SMEM),
                      pl.BlockSpec((td,) + hn.shape[1:], lambda i, ps: (i, 0, 0)),
                      pl.BlockSpec(memory_space=pl.ANY)],
            out_specs=pl.BlockSpec(memory_space=pl.ANY),
            scratch_shapes=[pltpu.SemaphoreType.DMA(())]),
        out_shape=jax.ShapeDtypeStruct(buf0.shape, buf0.dtype),
        input_output_aliases={3: 0},
        compiler_params=pltpu.CompilerParams(dimension_semantics=("arbitrary",)),
        name="dispatch",
    )(pad_start, tab.reshape(n // td, 1, 4 * td), hn, buf0)


def _expert_kernel(be_ref, nu_ref, xb_ref, wg_ref, wu_ref, wd_ref, y_ref, wg_s, wu_s, wd_s):
    b = pl.program_id(0)
    live = b < nu_ref[0]

    @pl.when(live & ((b == 0) | (be_ref[b] != be_ref[jnp.maximum(b - 1, 0)])))
    def _():
        wg_s[...] = wg_ref[0].astype(BF16)
        wu_s[...] = wu_ref[0].astype(BF16)
        wd_s[...] = wd_ref[0].astype(BF16)

    @pl.when(live)
    def _():
        xb = jnp.concatenate([xb_ref[:, j, :] for j in range(D_MODEL // LANES)], axis=1).astype(BF16)
        g = _dot(xb, wg_s[...])
        u = _dot(xb, wu_s[...])
        y = _dot((g * jax.nn.sigmoid(g) * u).astype(BF16), wd_s[...])
        for j in range(D_MODEL // LANES):
            y_ref[:, j, :] = y[:, j * LANES:(j + 1) * LANES]

    @pl.when(jnp.logical_not(live))
    def _():
        y_ref[...] = jnp.zeros_like(y_ref)


def _experts(block_e, n_used, buf, wg, wu, wd, tb):
    nblk = buf.shape[0] // tb
    live = lambda b, be, nu: jnp.minimum(b, nu[0] - 1)
    return pl.pallas_call(
        _expert_kernel,
        grid_spec=pltpu.PrefetchScalarGridSpec(
            num_scalar_prefetch=2, grid=(nblk,),
            in_specs=[pl.BlockSpec((tb, D_MODEL // LANES, LANES), lambda b, be, nu: (live(b, be, nu), 0, 0)),
                      pl.BlockSpec((1, D_MODEL, EXPERT_FF), lambda b, be, nu: (be[live(b, be, nu)], 0, 0)),
                      pl.BlockSpec((1, D_MODEL, EXPERT_FF), lambda b, be, nu: (be[live(b, be, nu)], 0, 0)),
                      pl.BlockSpec((1, EXPERT_FF, D_MODEL), lambda b, be, nu: (be[live(b, be, nu)], 0, 0))],
            out_specs=pl.BlockSpec((tb, D_MODEL // LANES, LANES), lambda b, be, nu: (b, 0, 0)),
            scratch_shapes=[pltpu.VMEM((D_MODEL, EXPERT_FF), BF16), pltpu.VMEM((D_MODEL, EXPERT_FF), BF16),
                            pltpu.VMEM((EXPERT_FF, D_MODEL), BF16)]),
        out_shape=jax.ShapeDtypeStruct(buf.shape, F32),
        compiler_params=pltpu.CompilerParams(dimension_semantics=("arbitrary",),
                                             vmem_limit_bytes=VMEM_LIMIT),
        name="experts",
    )(block_e, n_used, buf, wg, wu, wd)


def _combine_kernel(ps_ref, tab_ref, tab_next_ref, x2_ref, route_ref, gf_ref, yb_ref, o_ref, rows, sems, *, tc):
    step = pl.program_id(0)
    slot = lax.rem(step, 2)

    def row_copy(d, s, k, i):
        return pltpu.make_async_copy(yb_ref.at[d], rows.at[s, k, i], sems.at[s])

    def gather(tab, s):
        def issue(i, c):
            d0, d1 = _slot_rows(ps_ref, tab, i)
            row_copy(d0, s, 0, i).start()
            row_copy(d1, s, 1, i).start()
            return c
        lax.fori_loop(0, tc, issue, 0, unroll=8)

    @pl.when(step == 0)
    def _():
        gather(tab_ref, 0)

    @pl.when(step + 1 < pl.num_programs(0))
    def _():
        gather(tab_next_ref, 1 - slot)

    def drain(i, c):
        row_copy(0, slot, 0, 0).wait()
        return c

    lax.fori_loop(0, 2 * tc, drain, 0, unroll=8)
    route = route_ref[...]
    y0 = jnp.concatenate([rows[slot, 0, :, j, :] for j in range(D_MODEL // LANES)], axis=1)
    y1 = jnp.concatenate([rows[slot, 1, :, j, :] for j in range(D_MODEL // LANES)], axis=1)
    y = x2_ref[...] + (route[:, 4:5] * y0 + route[:, 5:6] * y1)
    o_ref[...] = _rms(y, gf_ref[...])


def _combine(pad_start, tab, x2, route, gf, yb, tc):
    n = x2.shape[0]
    nt = n // tc
    return pl.pallas_call(
        functools.partial(_combine_kernel, tc=tc),
        grid_spec=pltpu.PrefetchScalarGridSpec(
            num_scalar_prefetch=1, grid=(nt,),
            in_specs=[pl.BlockSpec((1, 1, 4 * tc), lambda i, ps: (i, 0, 0), memory_space=pltpu.SMEM),
                      pl.BlockSpec((1, 1, 4 * tc), lambda i, ps: (jnp.minimum(i + 1, nt - 1), 0, 0),
                                   memory_space=pltpu.SMEM),
                      pl.BlockSpec((tc, D_MODEL), lambda i, ps: (i, 0)),
                      pl.BlockSpec((tc, LANES), lambda i, ps: (i, 0)),
                      pl.BlockSpec((1, D_MODEL), lambda i, ps: (0, 0)),
                      pl.BlockSpec(memory_space=pl.ANY)],
            out_specs=pl.BlockSpec((tc, D_MODEL), lambda i, ps: (i, 0)),
            scratch_shapes=[pltpu.VMEM((2, 2, tc, D_MODEL // LANES, LANES), F32), pltpu.SemaphoreType.DMA((2,))]),
        out_shape=jax.ShapeDtypeStruct((n, D_MODEL), F32),
        compiler_params=pltpu.CompilerParams(dimension_semantics=("arbitrary",),
                                             vmem_limit_bytes=VMEM_LIMIT),
        name="combine",
    )(pad_start, tab.reshape(nt, 1, 4 * tc), tab.reshape(nt, 1, 4 * tc), x2, route, gf, yb)


def _overlap_matrix_t(seq):
    nc = seq // CMP_STRIDE - CMP_LEN // CMP_STRIDE + 1
    ns = seq // SLC_LEN
    c0 = np.arange(nc) * CMP_STRIDE
    s0 = np.arange(ns) * SLC_LEN
    ov = np.clip(np.minimum(c0[:, None] + CMP_LEN, s0[None, :] + SLC_LEN)
                 - np.maximum(c0[:, None], s0[None, :]), 0, None) / CMP_LEN
    out = np.zeros((LANES, LANES), np.float32)
    out[FEAT_SEL:FEAT_SEL + ns, :nc] = ov.T
    return jnp.asarray(out, BF16)


def _position_features(seq):
    pos = np.arange(seq)
    fk = np.zeros((seq, LANES), np.float32)
    fk[:, FEAT_POS:FEAT_POS + 3] = (pos // SLC_LEN)[:, None]
    fk[:, FEAT_POS + 3:FEAT_POS + 6] = (pos % SLC_LEN)[:, None]
    fk[pos, FEAT_SEL + pos // SLC_LEN] = 1.0
    fv = np.zeros((seq, LANES), np.float32)
    fv[:, HEAD_DIM] = 1.0
    return jnp.asarray(fk), jnp.asarray(fv)


def _pick_tile(n, pref):
    t = pref
    while n % t:
        t //= 2
    return t


def kernel(x, norm_mix_g, w_in, cmp_pos_k, cmp_w1_k, cmp_w2_k, cmp_pos_v, cmp_w1_v, cmp_w2_v, sinks, w_a, w_b,
           w_o, norm_ffn_g, w_group, b_group, w_expert, b_expert, w_gate_e, w_up_e, w_down_e, norm_final_g):
    bsz, seq, _ = x.shape
    n = bsz * seq
    assert TQ == LANES and seq % KCHUNK == 0 and seq // SLC_LEN <= N_SLC_BLK and seq // CMP_STRIDE <= LANES
    assert seq >= (NSA_WINDOW // TQ + 1) * TQ and w_in.shape[0] == 1
    x2d = x.reshape(n, D_MODEL)

    w = w_in[0]
    scale = HEAD_DIM ** -0.5
    nsa_w, kvw = N_HEADS * HEAD_DIM, N_KV * HEAD_DIM
    o_qa, o_kva, o_gn = 0, nsa_w, nsa_w + 6 * kvw
    o_qb = o_gn + 3 * N_HEADS
    o_kvb = o_qb + nsa_w
    o_gm = o_kvb + 2 * kvw
    zpad = jnp.zeros((D_MODEL, LANES - HEAD_DIM), F32)
    sect_cols = []
    for off in (o_kva + 2 * kvw, o_kva + 3 * kvw, o_kva + 4 * kvw, o_kva + 5 * kvw, o_kvb, o_kvb + kvw):
        for h in range(N_KV):
            sect_cols += [w[:, off + h * HEAD_DIM:off + (h + 1) * HEAD_DIM], zpad]
    w_attn = jnp.concatenate(
        [w[:, o_qa:o_qa + nsa_w] * scale, w[:, o_qb:o_qb + nsa_w] * scale] + sect_cols
        + [w[:, o_gn:o_gn + 3 * N_HEADS], jnp.zeros((D_MODEL, LANES - 3 * N_HEADS), F32),
           w[:, o_kva:o_kva + 2 * kvw]], axis=1).astype(BF16)
    w_gm = w[:, o_gm:o_gm + 2 * D_MODEL].astype(BF16)

    tm = _pick_tile(seq, 512)
    feat_k, feat_v = _position_features(seq)
    proj, cmp_in = _proj(x2d, norm_mix_g[0][None], w_attn, feat_k, feat_v, tm)
    proj3 = proj.reshape(bsz, seq, PROJ_W)

    nch = seq // CMP_STRIDE
    pos = jnp.stack([cmp_pos_k[0], cmp_pos_v[0]])
    pos = jnp.broadcast_to(pos[:, :, None, :], (2, CMP_LEN, N_KV, HEAD_DIM))
    pos_a = pos[:, :CMP_STRIDE].reshape(2, 1, CMP_STRIDE * kvw)
    pos_b = pos[:, CMP_STRIDE:].reshape(2, 1, CMP_STRIDE * kvw)
    w1 = jnp.stack([cmp_w1_k[0], cmp_w1_v[0]]).reshape(2, CMP_LEN, HEAD_DIM, CMP_HIDDEN)
    w1 = jnp.einsum('kjdc,hg->kjhdgc', w1, jnp.eye(N_KV, dtype=F32)).astype(BF16)
    w1 = w1.reshape(2, CMP_LEN * kvw, N_KV * CMP_HIDDEN)
    w2 = jnp.stack([cmp_w2_k[0], cmp_w2_v[0]]).astype(BF16)
    kvc = _compress(cmp_in.reshape(bsz, seq, CMP_W), pos_a, pos_b, w1[:, :CMP_STRIDE * kvw],
                    w1[:, CMP_STRIDE * kvw:], w2)
    kvc = jnp.pad(kvc, ((0, 0), (0, 0), (0, LANES - nch), (0, 0)))

    o_a, o_b = _attention(proj3, kvc, _overlap_matrix_t(seq), sinks[0])

    w_r = jnp.concatenate([w_expert[0], w_group[0],
                           jnp.zeros((D_MODEL, LANES - N_EXPERTS - N_GROUPS), F32)], axis=1)
    w_rh = w_r.astype(BF16)
    w_rl = (w_r - w_rh.astype(F32)).astype(BF16)
    b_r = jnp.concatenate([b_expert[0], b_group[0], jnp.zeros((LANES - N_EXPERTS - N_GROUPS,), F32)])[None]
    x2, hn, route, counts = _merge(
        x2d, o_a.reshape(n, nsa_w), o_b.reshape(n, nsa_w), norm_mix_g[0][None], norm_ffn_g[0][None], w_gm,
        w_a[0].astype(BF16), w_b[0].astype(BF16), w_o[0].astype(BF16), w_rh, w_rl, b_r, _pick_tile(n, 256))

    tb = 256
    nblk = -(-(2 * n + N_EXPERTS * (tb - 1)) // tb)
    cnt = counts[0, :N_EXPERTS].astype(jnp.int32)
    padded = (cnt + tb - 1) // tb * tb
    pad_end = jnp.cumsum(padded)
    pad_start = pad_end - padded
    block_e = jnp.minimum(jnp.sum(pad_end[None, :] <= (jnp.arange(nblk) * tb)[:, None], axis=1), N_EXPERTS - 1)
    n_used = (pad_end[-1:] // tb).astype(jnp.int32)
    tab = route[:, 0:4].astype(jnp.int32)

    buf = _dispatch(pad_start, tab, hn, jnp.zeros((nblk * tb, D_MODEL // LANES, LANES), F32), _pick_tile(n, 512))
    yb = _experts(block_e.astype(jnp.int32), n_used, buf, w_gate_e[0], w_up_e[0], w_down_e[0], tb)
    out = _combine(pad_start, tab, x2, route, norm_final_g[None], yb, _pick_tile(n, 256))
    return out.reshape(bsz, seq, D_MODEL)
```

```python
import functools

import numpy as np
import jax
import jax.numpy as jnp
from jax import lax
from jax.experimental import pallas as pl
from jax.experimental.pallas import tpu as pltpu

F32 = jnp.float32
BF16 = jnp.bfloat16

D_MODEL = 1024
HEAD_DIM = 64
N_HEADS = 8
N_KV = 2
GROUP = N_HEADS // N_KV
CMP_LEN = 32
CMP_STRIDE = 16
CMP_HIDDEN = 256
SLC_LEN = 64
SLC_TOPK = 8
NSA_WINDOW = 256
SWA_WINDOW = 128
N_GROUPS = 4
EPG = 8
N_EXPERTS = N_GROUPS * EPG
EXPERT_FF = 256
RMS_EPS = 1e-6
NEG_INF = -1e30
FORCE_SCORE = 1e9

LANES = 128
TQ = 128
KCHUNK = 512
N_SLC_BLK = LANES // 4
FEAT_POS = HEAD_DIM
FEAT_SEL = HEAD_DIM + 6
ROWS_LO, ROWS_HI = 64, 104
N_KV_SECT = 12
COL_QB = 512
COL_KV = 1024
COL_GN = COL_KV + N_KV_SECT * LANES
PROJ_W = COL_GN + LANES
CMP_W = 2 * LANES
ROW_TILE = D_MODEL // LANES
VMEM_LIMIT = 56 * 1024 * 1024


def _alibi_slopes():
    n = 2 * N_HEADS
    s = 2.0 ** (-8.0 * np.arange(1, n + 1) / n)
    return [float(v) for v in s[:N_HEADS]], [float(v) for v in s[N_HEADS:]]


SLOPES_SWA, SLOPES_NSA = _alibi_slopes()


def _bf16_pieces(v):
    out, rem = [], np.float32(v)
    for _ in range(3):
        p = np.float32(np.asarray(rem, np.float32).astype(BF16).astype(np.float32))
        out.append(float(p))
        rem = np.float32(rem - p)
    return out


def _rms(x, g):
    return x * lax.rsqrt(jnp.mean(x * x, axis=-1, keepdims=True) + RMS_EPS) * g


def _dot(a, b):
    return jnp.dot(a, b, preferred_element_type=F32)


def _dot3(x, w, pieces=3):
    out, rem = None, x
    for i in range(pieces):
        p = rem.astype(BF16)
        out = _dot(p, w) if out is None else out + _dot(p, w)
        if i + 1 < pieces:
            rem = rem - p.astype(F32)
    return out


def _tiles_to_rows(ref, n, lead=()):
    return jnp.concatenate([ref[lead + (pl.ds(j, n, stride=ROW_TILE), slice(None))] for j in range(ROW_TILE)], axis=1)


def _rows_to_tiles(ref, val):
    n = val.shape[0]
    for j in range(ROW_TILE):
        ref[pl.ds(j, n, stride=ROW_TILE), :] = val[:, j * LANES:(j + 1) * LANES]


def _dot_nt(a, b):
    return lax.dot_general(a, b, (((1,), (1,)), ((), ())), preferred_element_type=F32)


def _proj_kernel(x_ref, g_ref, w_ref, fk_ref, fv_ref, o_ref, cmp_ref):
    h = _rms(x_ref[...], g_ref[...]).astype(BF16)
    res = _dot(h, w_ref[...])
    o_ref[:, :COL_KV] = res[:, :COL_KV].astype(o_ref.dtype)
    for j in range(N_KV_SECT):
        feat = fk_ref[...] if (j // N_KV) % 2 == 0 else fv_ref[...]
        c0 = COL_KV + j * LANES
        o_ref[:, c0:c0 + LANES] = (res[:, c0:c0 + LANES] + feat).astype(o_ref.dtype)
    o_ref[:, COL_GN:] = res[:, COL_GN:PROJ_W].astype(o_ref.dtype)
    cmp_ref[...] = res[:, PROJ_W:]


def _proj(x2d, g, w, feat_k, feat_v, tm):
    n = x2d.shape[0]
    nper = feat_k.shape[0] // tm
    return pl.pallas_call(
        _proj_kernel,
        grid=(n // tm,),
        in_specs=[pl.BlockSpec((tm, D_MODEL), lambda i: (i, 0)),
                  pl.BlockSpec((1, D_MODEL), lambda i: (0, 0)),
                  pl.BlockSpec((D_MODEL, PROJ_W + CMP_W), lambda i: (0, 0)),
                  pl.BlockSpec((tm, LANES), lambda i: (i % nper, 0)),
                  pl.BlockSpec((tm, LANES), lambda i: (i % nper, 0))],
        out_specs=[pl.BlockSpec((tm, PROJ_W), lambda i: (i, 0)), pl.BlockSpec((tm, CMP_W), lambda i: (i, 0))],
        out_shape=[jax.ShapeDtypeStruct((n, PROJ_W), BF16), jax.ShapeDtypeStruct((n, CMP_W), F32)],
        compiler_params=pltpu.CompilerParams(dimension_semantics=("parallel",),
                                             vmem_limit_bytes=VMEM_LIMIT),
        name="proj",
    )(x2d, g, w, feat_k, feat_v)


def _compress_kernel(x_ref, pa_ref, pb_ref, w1a_ref, w1b_ref, w2_ref, o_ref, *, nch):
    r = jnp.concatenate([x_ref[0, pl.ds(j, nch, stride=CMP_STRIDE), :] for j in range(CMP_STRIDE)], axis=1)
    a = _dot((r + pa_ref[0]).astype(BF16), w1a_ref[0])
    b = _dot((r + pb_ref[0]).astype(BF16), w1b_ref[0])
    hid = a + pltpu.roll(b, nch - 1, 0)
    hid = hid * jax.nn.sigmoid(hid)
    for h in range(N_KV):
        o_ref[0, 0, h] = _dot(hid[:, h * CMP_HIDDEN:(h + 1) * CMP_HIDDEN].astype(BF16), w2_ref[0]).astype(o_ref.dtype)


def _compress(cmp3, pos_a, pos_b, w1a, w1b, w2):
    bsz, seq, _ = cmp3.shape
    nch = seq // CMP_STRIDE
    wspec = lambda a: pl.BlockSpec((1,) + a.shape[1:], lambda b, j: (j, 0, 0))
    return pl.pallas_call(
        functools.partial(_compress_kernel, nch=nch),
        grid=(bsz, 2),
        in_specs=[pl.BlockSpec((1, seq, LANES), lambda b, j: (b, 0, j)),
                  wspec(pos_a), wspec(pos_b), wspec(w1a), wspec(w1b), wspec(w2)],
        out_specs=pl.BlockSpec((1, 1, N_KV, nch, LANES), lambda b, j: (b, j, 0, 0, 0)),
        out_shape=jax.ShapeDtypeStruct((bsz, 2, N_KV, nch, LANES), BF16),
        compiler_params=pltpu.CompilerParams(dimension_semantics=("parallel", "parallel"),
                                             vmem_limit_bytes=VMEM_LIMIT),
        name="compress",
    )(cmp3, pos_a, pos_b, w1a, w1b, w2)


def _slope_row(slope, lane):
    hi, mid, lo = _bf16_pieces(slope)
    row = jnp.zeros((1, LANES), F32)
    for i, v in enumerate([SLC_LEN * hi, SLC_LEN * mid, SLC_LEN * lo, hi, mid, lo]):
        row = jnp.where(lane == FEAT_POS + i, v, row)
    return row.astype(BF16)


def _masked_rows(s, mask):
    return jnp.concatenate([jnp.where(mask, s[g * TQ:(g + 1) * TQ], NEG_INF) for g in range(GROUP)], axis=0)


def _attn_kernel(sinks_ref, qa_ref, qb_ref, kc_ref, vc_ref, *rest, n_cmp):
    kv = rest[:N_KV_SECT]
    gn_ref, ovt_ref, b64_ref, gexp_ref, place_ref, oa_ref, ob_ref = rest[N_KV_SECT:]
    sect = lambda is_v, branch, h: kv[(2 * branch + is_v) * N_KV + h]
    qi = pl.program_id(1)
    q0 = pl.multiple_of(qi * TQ, TQ)
    lane = lax.broadcasted_iota(jnp.int32, (1, LANES), 1)
    t_col = q0 + lax.broadcasted_iota(jnp.int32, (TQ, 1), 0)
    t_row = q0 + lane
    eye = jnp.where(lax.broadcasted_iota(jnp.int32, (TQ, TQ), 0) == lax.broadcasted_iota(jnp.int32, (TQ, TQ), 1),
                    1.0, 0.0).astype(BF16)
    gates = _dot3(jax.nn.sigmoid(gn_ref[0].astype(F32)), gexp_ref[...])
    gate = lambda hh, c: gates[:, (3 * hh + c) * LANES:(3 * hh + c + 1) * LANES]
    t_full = (q0 + lax.broadcasted_iota(jnp.int32, (TQ, LANES), 0)).astype(F32)
    blk = lax.broadcasted_iota(jnp.int32, (ROWS_HI - ROWS_LO, 1), 0) + (ROWS_LO - FEAT_SEL)
    in_rng = (blk >= 0) & (blk < N_SLC_BLK)

    def exp_rows(s, m):
        return jnp.concatenate([jnp.exp(s[:, j * LANES:(j + 1) * LANES] - m) for j in range(s.shape[1] // LANES)],
                               axis=1)

    def row_sums(acc):
        return _dot3(acc, b64_ref[...], pieces=2)

    def store_heads(ref, outs):
        for p in range(N_HEADS // 2):
            both = jnp.concatenate([outs[2 * p].astype(BF16), outs[2 * p + 1].astype(BF16)], axis=1)
            ref[0, :, p * LANES:(p + 1) * LANES] = _dot(both, place_ref[...]).astype(ref.dtype)

    def q_low(ref, hh):
        pair = ref[0, :, (hh // 2) * LANES:(hh // 2 + 1) * LANES]
        if hh % 2 == 0:
            return pair
        return pltpu.roll(pair.astype(F32), HEAD_DIM, 1).astype(BF16)

    def q_aug(q_lo, tails):
        return jnp.concatenate([jnp.where(lane < HEAD_DIM, q, t) for q, t in zip(q_lo, tails)], axis=0)

    def to_lanes(bias_rows):
        full = jnp.concatenate([jnp.zeros((ROWS_LO, TQ), F32), bias_rows,
                                jnp.zeros((LANES - ROWS_HI, TQ), F32)], axis=0).astype(BF16)
        return _dot_nt(eye, full).astype(BF16)

    def nsa_front(h):
        heads = [h * GROUP + g for g in range(GROUP)]
        q_lo = [q_low(qa_ref, hh) for hh in heads]
        srow = [_slope_row(SLOPES_NSA[hh], lane) for hh in heads]

        kc = kc_ref[0, 0, h]
        vc = vc_ref[0, 0, h]
        end_c = lane * CMP_STRIDE + (CMP_LEN - 1)
        cmask = (t_col >= end_c) & (lane < n_cmp)
        s_all = _dot_nt(q_aug(q_lo, [jnp.zeros((1, LANES), BF16)] * GROUP), kc)
        o_cmp = []
        psum = jnp.zeros((TQ, LANES), F32)
        for g in range(GROUP):
            s = s_all[g * TQ:(g + 1) * TQ] + SLOPES_NSA[heads[g]] * end_c.astype(F32)
            s = jnp.where(cmask, s, NEG_INF)
            m = jnp.max(s, axis=-1, keepdims=True)
            e = jnp.where(cmask, jnp.exp(s - m), 0.0)
            z = jnp.sum(e, axis=-1, keepdims=True)
            p = e / jnp.where(z > 0, z, 1.0)
            o_cmp.append(_dot(p.astype(BF16), vc))
            psum = psum + p

        p_hi = psum.astype(BF16)
        p_lo = (psum - p_hi.astype(F32)).astype(BF16)
        imp = (_dot_nt(ovt_ref[...], p_hi) + _dot_nt(ovt_ref[...], p_lo))[ROWS_LO:ROWS_HI]
        cur = lax.shift_right_logical(t_row, int(np.log2(SLC_LEN)))
        valid = in_rng & (blk * SLC_LEN <= t_row)
        forced = in_rng & ((blk == 0) | (blk == cur) | (blk == cur - 1))
        score = jnp.where(forced, FORCE_SCORE, jnp.where(valid, imp, NEG_INF))
        rank = jnp.zeros(score.shape, F32)
        for i in range(N_SLC_BLK):
            r = FEAT_SEL - ROWS_LO + i
            si = score[r:r + 1, :]
            rank = rank + jnp.where((si > score) | ((si == score) & (blk > i)), 1.0, 0.0)
        sel = in_rng & (rank < SLC_TOPK) & (score > 0.5 * NEG_INF)
        tail_diag = to_lanes(jnp.where(in_rng & jnp.logical_not(sel), NEG_INF, 0.0))
        tail_main = to_lanes(jnp.where(in_rng & jnp.logical_not(sel & (blk < 2 * qi)), NEG_INF, 0.0))
        is_pos = (lane >= FEAT_POS) & (lane < FEAT_SEL)

        q_d = q_aug(q_lo, [jnp.where(is_pos, srow[g], tail_diag) for g in range(GROUP)])
        q_m = q_aug(q_lo, [jnp.where(is_pos, srow[g], tail_main) for g in range(GROUP)])
        k_d = sect(0, 0, h)[0, pl.ds(q0, TQ), :]
        v_d = sect(1, 0, h)[0, pl.ds(q0, TQ), :]
        s = _masked_rows(_dot_nt(q_d, k_d), t_row <= t_col)
        m0 = jnp.broadcast_to(jnp.max(s, axis=-1, keepdims=True), s.shape)
        acc0 = _dot(jnp.exp(s - m0).astype(BF16), v_d)

        nw = NSA_WINDOW // TQ + 1
        start = pl.multiple_of(jnp.maximum(qi - (nw - 1), 0) * TQ, TQ)
        dist = t_col - (start + lax.broadcasted_iota(jnp.int32, (1, nw * TQ), 1))
        q_w = q_aug(q_lo, srow)
        s = _masked_rows(_dot_nt(q_w, sect(0, 1, h)[0, pl.ds(start, nw * TQ), :]), (dist >= 0) & (dist < NSA_WINDOW))
        m = jnp.broadcast_to(jnp.max(s, axis=-1, keepdims=True), (GROUP * TQ, LANES))
        acc = _dot(exp_rows(s, m).astype(BF16), sect(1, 1, h)[0, pl.ds(start, nw * TQ), :])
        o_win_all = acc / row_sums(acc)

        part = [gate(hh, 0) * o_cmp[g] + gate(hh, 2) * o_win_all[g * TQ:(g + 1) * TQ] for g, hh in enumerate(heads)]
        return q_m, m0, acc0, part

    fronts = [nsa_front(h) for h in range(N_KV)]

    nb = (SWA_WINDOW - 1 + TQ - 1) // TQ + 1
    start = pl.multiple_of(jnp.maximum(qi - (nb - 1), 0) * TQ, TQ)
    dist = t_col - (start + lax.broadcasted_iota(jnp.int32, (1, nb * TQ), 1))
    bmask = (dist >= 0) & (dist < SWA_WINDOW)
    outs_b = []
    for h in range(N_KV):
        heads = [h * GROUP + g for g in range(GROUP)]
        q_b = q_aug([q_low(qb_ref, hh) for hh in heads], [_slope_row(SLOPES_SWA[hh], lane) for hh in heads])
        s = _masked_rows(_dot_nt(q_b, sect(0, 2, h)[0, pl.ds(start, nb * TQ), :]), bmask)
        sink = jnp.concatenate([sinks_ref[hh] + SLOPES_SWA[hh] * t_full for hh in heads], axis=0)
        m = jnp.maximum(jnp.max(s, axis=-1, keepdims=True), sink)
        acc = _dot(exp_rows(s, m).astype(BF16), sect(1, 2, h)[0, pl.ds(start, nb * TQ), :])
        o_all = acc / (row_sums(acc) + jnp.exp(sink - m))
        outs_b += [o_all[g * TQ:(g + 1) * TQ] for g in range(GROUP)]
    store_heads(ob_ref, outs_b)

    def slc_step(c, carry):
        start = pl.multiple_of(c * KCHUNK, KCHUNK)
        out = []
        for h, (m_i, acc) in enumerate(carry):
            s = _dot_nt(fronts[h][0], sect(0, 0, h)[0, pl.ds(start, KCHUNK), :])
            m_new = jnp.maximum(m_i, jnp.max(s, axis=-1, keepdims=True))
            pv = _dot(exp_rows(s, m_new).astype(BF16), sect(1, 0, h)[0, pl.ds(start, KCHUNK), :])
            out.append((m_new, jnp.exp(m_i - m_new) * acc + pv))
        return tuple(out)

    n_main = lax.shift_right_logical(q0 + (KCHUNK - 1), int(np.log2(KCHUNK)))
    swept = lax.fori_loop(0, n_main, slc_step, tuple((f[1], f[2]) for f in fronts))
    outs_a = []
    for h in range(N_KV):
        acc = swept[h][1]
        o_slc_all = acc / row_sums(acc)
        outs_a += [fronts[h][3][g] + gate(h * GROUP + g, 1) * o_slc_all[g * TQ:(g + 1) * TQ] for g in range(GROUP)]
    store_heads(oa_ref, outs_a)


def _lane_matrices():
    b64 = np.zeros((LANES, LANES), np.float32)
    b64[HEAD_DIM, :] = 1.0
    gexp = np.zeros((LANES, 3 * N_HEADS * LANES), np.float32)
    for j in range(3 * N_HEADS):
        gexp[j, j * LANES:(j + 1) * LANES] = 1.0
    place = np.zeros((2 * LANES, LANES), np.float32)
    idx = np.arange(HEAD_DIM)
    place[idx, idx] = 1.0
    place[LANES + idx, HEAD_DIM + idx] = 1.0
    return [jnp.asarray(a, BF16) for a in (b64, gexp, place)]


def _attention(proj3, kvc, ovt, sinks):
    bsz, seq, _ = proj3.shape
    n_cmp = seq // CMP_STRIDE - CMP_LEN // CMP_STRIDE + 1
    width = N_HEADS * HEAD_DIM
    consts = [ovt] + _lane_matrices()
    qspec = lambda col: pl.BlockSpec((1, TQ, width), lambda b, q, s: (b, q, col))
    cspec = lambda j: pl.BlockSpec((1, 1) + kvc.shape[2:], lambda b, q, s: (b, j, 0, 0, 0))
    in_specs = [qspec(0), qspec(COL_QB // width), cspec(0), cspec(1)]
    in_specs += [pl.BlockSpec((1, seq, LANES), lambda b, q, s, j=j: (b, 0, COL_KV // LANES + j))
                 for j in range(N_KV_SECT)]
    in_specs += [pl.BlockSpec((1, TQ, LANES), lambda b, q, s: (b, q, COL_GN // LANES))]
    in_specs += [pl.BlockSpec(c.shape, lambda b, q, s: (0, 0)) for c in consts]
    ospec = pl.BlockSpec((1, TQ, width), lambda b, q, s: (b, q, 0))
    return pl.pallas_call(
        functools.partial(_attn_kernel, n_cmp=n_cmp),
        grid_spec=pltpu.PrefetchScalarGridSpec(
            num_scalar_prefetch=1, grid=(bsz, seq // TQ), in_specs=in_specs, out_specs=[ospec, ospec]),
        out_shape=[jax.ShapeDtypeStruct((bsz, seq, width), BF16)] * 2,
        compiler_params=pltpu.CompilerParams(dimension_semantics=("parallel", "parallel"),
                                             vmem_limit_bytes=VMEM_LIMIT),
        name="attn",
    )(sinks, proj3, proj3, kvc, kvc, *([proj3] * N_KV_SECT), proj3, *consts)


def _merge_kernel(x_ref, oa_ref, ob_ref, g1_ref, g2_ref, wg_ref, wa_ref, wb_ref, wo_ref,
                  wrh_ref, wrl_ref, br_ref, x2_ref, hn_ref, route_ref, cnt_ref):
    tm = x_ref.shape[0]

    @pl.when(pl.program_id(0) == 0)
    def _():
        cnt_ref[...] = jnp.zeros_like(cnt_ref)

    x = x_ref[...]
    h = _rms(x, g1_ref[...]).astype(BF16)
    gm = jax.nn.sigmoid(_dot(h, wg_ref[...]))
    a = _dot(oa_ref[...], wa_ref[...])
    b = _dot(ob_ref[...], wb_ref[...])
    mixin = gm[:, :D_MODEL] * a + gm[:, D_MODEL:] * b
    x2 = x + _dot(mixin.astype(BF16), wo_ref[...])
    x2_ref[...] = x2
    hn = _rms(x2, g2_ref[...])
    hn_b = hn.astype(BF16)
    _rows_to_tiles(hn_ref, hn)

    hn_lo = (hn - hn_b.astype(F32)).astype(BF16)
    logits = (_dot(hn_b, wrh_ref[...]) + _dot(hn_lo, wrh_ref[...]) + _dot(hn_b, wrl_ref[...])
              + br_ref[...])
    lane = lax.broadcasted_iota(jnp.int32, (1, LANES), 1)
    lanef = lane.astype(F32)
    big = float(LANES)
    is_g = (lane >= N_EXPERTS) & (lane < N_EXPERTS + N_GROUPS)
    gl = jnp.where(is_g, logits, NEG_INF)
    gmax = jnp.max(gl, axis=-1, keepdims=True)
    grp = jnp.min(jnp.where(gl == gmax, lanef, big), axis=-1, keepdims=True) - N_EXPERTS
    p_grp = 1.0 / jnp.sum(jnp.where(is_g, jnp.exp(gl - gmax), 0.0), axis=-1, keepdims=True)
    in_grp = (lanef >= grp * EPG) & (lanef < grp * EPG + EPG)
    el = jnp.where(in_grp, logits, NEG_INF)
    v0 = jnp.max(el, axis=-1, keepdims=True)
    i0 = jnp.min(jnp.where(el == v0, lanef, big), axis=-1, keepdims=True)
    el1 = jnp.where(lanef == i0, NEG_INF, el)
    v1 = jnp.max(el1, axis=-1, keepdims=True)
    i1 = jnp.min(jnp.where(el1 == v1, lanef, big), axis=-1, keepdims=True)
    e1 = jnp.exp(v1 - v0)
    w0 = p_grp / (1.0 + e1)
    w1 = p_grp * e1 / (1.0 + e1)

    oh0 = jnp.where(lanef == i0, 1.0, 0.0)
    oh1 = jnp.where(lanef == i1, 1.0, 0.0)
    oh = oh0 + oh1
    r_i = lax.broadcasted_iota(jnp.int32, (tm, tm), 0)
    c_i = lax.broadcasted_iota(jnp.int32, (tm, tm), 1)
    lower = jnp.where(c_i < r_i, 1.0, 0.0).astype(BF16)
    before = cnt_ref[...] + _dot(lower, oh.astype(BF16))
    rank0 = jnp.sum(oh0 * before, axis=-1, keepdims=True)
    rank1 = jnp.sum(oh1 * before, axis=-1, keepdims=True)
    cnt_ref[...] = cnt_ref[...] + jnp.sum(oh, axis=0, keepdims=True)

    route = jnp.where(lane == 0, i0, jnp.where(lane == 1, i1, jnp.where(lane == 2, rank0,
            jnp.where(lane == 3, rank1, jnp.where(lane == 4, w0, jnp.where(lane == 5, w1, 0.0))))))
    route_ref[...] = route


def _merge(x2d, oa, ob, g1, g2, wg, wa, wb, wo, wrh, wrl, br, tm):
    n = x2d.shape[0]
    width = N_HEADS * HEAD_DIM
    row = lambda w: pl.BlockSpec((tm, w), lambda i: (i, 0))
    full = lambda a: pl.BlockSpec(a.shape, lambda i: (0, 0))
    return pl.pallas_call(
        _merge_kernel,
        grid=(n // tm,),
        in_specs=[row(D_MODEL), row(width), row(width), full(g1), full(g2), full(wg), full(wa), full(wb),
                  full(wo), full(wrh), full(wrl), full(br)],
        out_specs=[row(D_MODEL), pl.BlockSpec((tm * ROW_TILE, LANES), lambda i: (i, 0)), row(LANES),
                   pl.BlockSpec((1, LANES), lambda i: (0, 0))],
        out_shape=[jax.ShapeDtypeStruct((n, D_MODEL), F32), jax.ShapeDtypeStruct((n * ROW_TILE, LANES), F32),
                   jax.ShapeDtypeStruct((n, LANES), F32), jax.ShapeDtypeStruct((1, LANES), F32)],
        compiler_params=pltpu.CompilerParams(dimension_semantics=("arbitrary",),
                                             vmem_limit_bytes=VMEM_LIMIT),
        name="merge",
    )(x2d, oa, ob, g1, g2, wg, wa, wb, wo, wrh, wrl, br)


def _tile(t):
    return pl.ds(pl.multiple_of(t * ROW_TILE, ROW_TILE), ROW_TILE)


def _slot_rows(ps_ref, tab_ref, i):
    return (ps_ref[tab_ref[0, 0, 4 * i]] + tab_ref[0, 0, 4 * i + 2],
            ps_ref[tab_ref[0, 0, 4 * i + 1]] + tab_ref[0, 0, 4 * i + 3])


def _dispatch_kernel(ps_ref, tab_ref, hn_ref, buf_in_ref, buf_ref, sem, *, td):
    del buf_in_ref

    def row_copy(i, d):
        return pltpu.make_async_copy(hn_ref.at[_tile(i)], buf_ref.at[_tile(d)], sem)

    def issue(i, c):
        d0, d1 = _slot_rows(ps_ref, tab_ref, i)
        row_copy(i, d0).start(priority=0)
        row_copy(i, d1).start(priority=1)
        return c

    lax.fori_loop(0, td, issue, 0, unroll=8)
    for _ in range(2):
        pltpu.make_async_copy(hn_ref, buf_ref.at[pl.ds(0, td * ROW_TILE)], sem).wait()


def _dispatch(pad_start, tab, hn, buf0, td):
    n = hn.shape[0] // ROW_TILE
    return pl.pallas_call(
        functools.partial(_dispatch_kernel, td=td),
        grid_spec=pltpu.PrefetchScalarGridSpec(
            num_scalar_prefetch=1, grid=(n // td,),
            in_specs=[pl.BlockSpec((1, 1, 4 * td), lambda i, ps: (i, 0, 0), memory_space=pltpu.SMEM),
                      pl.BlockSpec((td * ROW_TILE, LANES), lambda i, ps: (i, 0)),
                      pl.BlockSpec(memory_space=pl.ANY)],
            out_specs=pl.BlockSpec(memory_space=pl.ANY),
            scratch_shapes=[pltpu.SemaphoreType.DMA(())]),
        out_shape=jax.ShapeDtypeStruct(buf0.shape, buf0.dtype),
        input_output_aliases={3: 0},
        compiler_params=pltpu.CompilerParams(dimension_semantics=("arbitrary",)),
        name="dispatch",
    )(pad_start, tab.reshape(n // td, 1, 4 * td), hn, buf0)


def _expert_kernel(be_ref, nu_ref, xb_ref, wg_ref, wu_ref, wd_ref, y_ref, wg_s, wu_s, wd_s):
    b = pl.program_id(0)
    live = b < nu_ref[0]

    @pl.when(live & ((b == 0) | (be_ref[b] != be_ref[jnp.maximum(b - 1, 0)])))
    def _():
        wg_s[...] = wg_ref[0].astype(BF16)
        wu_s[...] = wu_ref[0].astype(BF16)
        wd_s[...] = wd_ref[0].astype(BF16)

    @pl.when(live)
    def _():
        xb = _tiles_to_rows(xb_ref, xb_ref.shape[0] // ROW_TILE).astype(BF16)
        g = _dot(xb, wg_s[...])
        u = _dot(xb, wu_s[...])
        _rows_to_tiles(y_ref, _dot((g * jax.nn.sigmoid(g) * u).astype(BF16), wd_s[...]))

    @pl.when(jnp.logical_not(live))
    def _():
        y_ref[...] = jnp.zeros_like(y_ref)


def _experts(block_e, n_used, buf, wg, wu, wd, tb):
    nblk = buf.shape[0] // (tb * ROW_TILE)
    live = lambda b, be, nu: jnp.minimum(b, nu[0] - 1)
    return pl.pallas_call(
        _expert_kernel,
        grid_spec=pltpu.PrefetchScalarGridSpec(
            num_scalar_prefetch=2, grid=(nblk,),
            in_specs=[pl.BlockSpec((tb * ROW_TILE, LANES), lambda b, be, nu: (live(b, be, nu), 0)),
                      pl.BlockSpec((1, D_MODEL, EXPERT_FF), lambda b, be, nu: (be[live(b, be, nu)], 0, 0)),
                      pl.BlockSpec((1, D_MODEL, EXPERT_FF), lambda b, be, nu: (be[live(b, be, nu)], 0, 0)),
                      pl.BlockSpec((1, EXPERT_FF, D_MODEL), lambda b, be, nu: (be[live(b, be, nu)], 0, 0))],
            out_specs=pl.BlockSpec((tb * ROW_TILE, LANES), lambda b, be, nu: (b, 0)),
            scratch_shapes=[pltpu.VMEM((D_MODEL, EXPERT_FF), BF16), pltpu.VMEM((D_MODEL, EXPERT_FF), BF16),
                            pltpu.VMEM((EXPERT_FF, D_MODEL), BF16)]),
        out_shape=jax.ShapeDtypeStruct(buf.shape, F32),
        compiler_params=pltpu.CompilerParams(dimension_semantics=("arbitrary",),
                                             vmem_limit_bytes=VMEM_LIMIT),
        name="experts",
    )(block_e, n_used, buf, wg, wu, wd)


def _combine_kernel(ps_ref, tab_ref, tab_next_ref, x2_ref, route_ref, gf_ref, yb_ref, o_ref, rows, sems, *, tc):
    step = pl.program_id(0)
    slot = lax.rem(step, 2)

    def row_copy(d, s, k, i):
        return pltpu.make_async_copy(yb_ref.at[_tile(d)], rows.at[s, k, _tile(i)], sems.at[s])

    def gather(tab, s):
        def issue(i, c):
            d0, d1 = _slot_rows(ps_ref, tab, i)
            row_copy(d0, s, 0, i).start(priority=0)
            row_copy(d1, s, 1, i).start(priority=1)
            return c
        lax.fori_loop(0, tc, issue, 0, unroll=8)

    @pl.when(step == 0)
    def _():
        gather(tab_ref, 0)

    @pl.when(step + 1 < pl.num_programs(0))
    def _():
        gather(tab_next_ref, 1 - slot)

    for k in range(2):
        pltpu.make_async_copy(yb_ref.at[pl.ds(0, tc * ROW_TILE)], rows.at[slot, k], sems.at[slot]).wait()
    route = route_ref[...]
    y0 = _tiles_to_rows(rows, tc, (slot, 0))
    y1 = _tiles_to_rows(rows, tc, (slot, 1))
    y = x2_ref[...] + (route[:, 4:5] * y0 + route[:, 5:6] * y1)
    o_ref[...] = _rms(y, gf_ref[...])


def _combine(pad_start, tab, x2, route, gf, yb, tc):
    n = x2.shape[0]
    nt = n // tc
    return pl.pallas_call(
        functools.partial(_combine_kernel, tc=tc),
        grid_spec=pltpu.PrefetchScalarGridSpec(
            num_scalar_prefetch=1, grid=(nt,),
            in_specs=[pl.BlockSpec((1, 1, 4 * tc), lambda i, ps: (i, 0, 0), memory_space=pltpu.SMEM),
                      pl.BlockSpec((1, 1, 4 * tc), lambda i, ps: (jnp.minimum(i + 1, nt - 1), 0, 0),
                                   memory_space=pltpu.SMEM),
                      pl.BlockSpec((tc, D_MODEL), lambda i, ps: (i, 0)),
                      pl.BlockSpec((tc, LANES), lambda i, ps: (i, 0)),
                      pl.BlockSpec((1, D_MODEL), lambda i, ps: (0, 0)),
                      pl.BlockSpec(memory_space=pl.ANY)],
            out_specs=pl.BlockSpec((tc, D_MODEL), lambda i, ps: (i, 0)),
            scratch_shapes=[pltpu.VMEM((2, 2, tc * ROW_TILE, LANES), F32), pltpu.SemaphoreType.DMA((2,))]),
        out_shape=jax.ShapeDtypeStruct((n, D_MODEL), F32),
        compiler_params=pltpu.CompilerParams(dimension_semantics=("arbitrary",),
                                             vmem_limit_bytes=VMEM_LIMIT),
        name="combine",
    )(pad_start, tab.reshape(nt, 1, 4 * tc), tab.reshape(nt, 1, 4 * tc), x2, route, gf, yb)


def _overlap_matrix_t(seq):
    nc = seq // CMP_STRIDE - CMP_LEN // CMP_STRIDE + 1
    ns = seq // SLC_LEN
    c0 = np.arange(nc) * CMP_STRIDE
    s0 = np.arange(ns) * SLC_LEN
    ov = np.clip(np.minimum(c0[:, None] + CMP_LEN, s0[None, :] + SLC_LEN)
                 - np.maximum(c0[:, None], s0[None, :]), 0, None) / CMP_LEN
    out = np.zeros((LANES, LANES), np.float32)
    out[FEAT_SEL:FEAT_SEL + ns, :nc] = ov.T
    return jnp.asarray(out, BF16)


def _position_features(seq):
    pos = np.arange(seq)
    fk = np.zeros((seq, LANES), np.float32)
    fk[:, FEAT_POS:FEAT_POS + 3] = (pos // SLC_LEN)[:, None]
    fk[:, FEAT_POS + 3:FEAT_POS + 6] = (pos % SLC_LEN)[:, None]
    fk[pos, FEAT_SEL + pos // SLC_LEN] = 1.0
    fv = np.zeros((seq, LANES), np.float32)
    fv[:, HEAD_DIM] = 1.0
    return jnp.asarray(fk), jnp.asarray(fv)


def _pick_tile(n, pref):
    t = pref
    while n % t:
        t //= 2
    return t


def kernel(x, norm_mix_g, w_in, cmp_pos_k, cmp_w1_k, cmp_w2_k, cmp_pos_v, cmp_w1_v, cmp_w2_v, sinks, w_a, w_b,
           w_o, norm_ffn_g, w_group, b_group, w_expert, b_expert, w_gate_e, w_up_e, w_down_e, norm_final_g):
    bsz, seq, _ = x.shape
    n = bsz * seq
    assert TQ == LANES and seq % KCHUNK == 0 and seq // SLC_LEN <= N_SLC_BLK and seq // CMP_STRIDE <= LANES
    assert seq >= (NSA_WINDOW // TQ + 1) * TQ and w_in.shape[0] == 1
    x2d = x.reshape(n, D_MODEL)

    w = w_in[0]
    scale = HEAD_DIM ** -0.5
    nsa_w, kvw = N_HEADS * HEAD_DIM, N_KV * HEAD_DIM
    o_qa, o_kva, o_gn = 0, nsa_w, nsa_w + 6 * kvw
    o_qb = o_gn + 3 * N_HEADS
    o_kvb = o_qb + nsa_w
    o_gm = o_kvb + 2 * kvw
    zpad = jnp.zeros((D_MODEL, LANES - HEAD_DIM), F32)
    sect_cols = []
    for off in (o_kva + 2 * kvw, o_kva + 3 * kvw, o_kva + 4 * kvw, o_kva + 5 * kvw, o_kvb, o_kvb + kvw):
        for h in range(N_KV):
            sect_cols += [w[:, off + h * HEAD_DIM:off + (h + 1) * HEAD_DIM], zpad]
    w_attn = jnp.concatenate(
        [w[:, o_qa:o_qa + nsa_w] * scale, w[:, o_qb:o_qb + nsa_w] * scale] + sect_cols
        + [w[:, o_gn:o_gn + 3 * N_HEADS], jnp.zeros((D_MODEL, LANES - 3 * N_HEADS), F32),
           w[:, o_kva:o_kva + 2 * kvw]], axis=1).astype(BF16)
    w_gm = w[:, o_gm:o_gm + 2 * D_MODEL].astype(BF16)

    tm = _pick_tile(seq, 512)
    feat_k, feat_v = _position_features(seq)
    proj, cmp_in = _proj(x2d, norm_mix_g[0][None], w_attn, feat_k, feat_v, tm)
    proj3 = proj.reshape(bsz, seq, PROJ_W)

    nch = seq // CMP_STRIDE
    pos = jnp.stack([cmp_pos_k[0], cmp_pos_v[0]])
    pos = jnp.broadcast_to(pos[:, :, None, :], (2, CMP_LEN, N_KV, HEAD_DIM))
    pos_a = pos[:, :CMP_STRIDE].reshape(2, 1, CMP_STRIDE * kvw)
    pos_b = pos[:, CMP_STRIDE:].reshape(2, 1, CMP_STRIDE * kvw)
    w1 = jnp.stack([cmp_w1_k[0], cmp_w1_v[0]]).reshape(2, CMP_LEN, HEAD_DIM, CMP_HIDDEN)
    w1 = jnp.einsum('kjdc,hg->kjhdgc', w1, jnp.eye(N_KV, dtype=F32)).astype(BF16)
    w1 = w1.reshape(2, CMP_LEN * kvw, N_KV * CMP_HIDDEN)
    w2 = jnp.pad(jnp.stack([cmp_w2_k[0], cmp_w2_v[0]]), ((0, 0), (0, 0), (0, LANES - HEAD_DIM))).astype(BF16)
    kvc = _compress(cmp_in.reshape(bsz, seq, CMP_W), pos_a, pos_b, w1[:, :CMP_STRIDE * kvw],
                    w1[:, CMP_STRIDE * kvw:], w2)
    kvc = jnp.pad(kvc, ((0, 0), (0, 0), (0, 0), (0, LANES - nch), (0, 0)))

    o_a, o_b = _attention(proj3, kvc, _overlap_matrix_t(seq), sinks[0])

    w_r = jnp.concatenate([w_expert[0], w_group[0],
                           jnp.zeros((D_MODEL, LANES - N_EXPERTS - N_GROUPS), F32)], axis=1)
    w_rh = w_r.astype(BF16)
    w_rl = (w_r - w_rh.astype(F32)).astype(BF16)
    b_r = jnp.concatenate([b_expert[0], b_group[0], jnp.zeros((LANES - N_EXPERTS - N_GROUPS,), F32)])[None]
    x2, hn, route, counts = _merge(
        x2d, o_a.reshape(n, nsa_w), o_b.reshape(n, nsa_w), norm_mix_g[0][None], norm_ffn_g[0][None], w_gm,
        w_a[0].astype(BF16), w_b[0].astype(BF16), w_o[0].astype(BF16), w_rh, w_rl, b_r, _pick_tile(n, 256))

    tb = 256
    nblk = -(-(2 * n + N_EXPERTS * (tb - 1)) // tb)
    cnt = counts[0, :N_EXPERTS].astype(jnp.int32)
    padded = (cnt + tb - 1) // tb * tb
    pad_end = jnp.cumsum(padded)
    pad_start = pad_end - padded
    block_e = jnp.minimum(jnp.sum(pad_end[None, :] <= (jnp.arange(nblk) * tb)[:, None], axis=1), N_EXPERTS - 1)
    n_used = (pad_end[-1:] // tb).astype(jnp.int32)
    tab = route[:, 0:4].astype(jnp.int32)

    buf = _dispatch(pad_start, tab, hn, jnp.zeros((nblk * tb * ROW_TILE, LANES), F32), _pick_tile(n, 512))
    yb = _experts(block_e.astype(jnp.int32), n_used, buf, w_gate_e[0], w_up_e[0], w_down_e[0], tb)
    out = _combine(pad_start, tab, x2, route, norm_final_g[None], yb, _pick_tile(n, 256))
    return out.reshape(bsz, seq, D_MODEL)
```

```python
import functools

import numpy as np
import jax
import jax.numpy as jnp
from jax import lax
from jax.experimental import pallas as pl
from jax.experimental.pallas import tpu as pltpu

F32 = jnp.float32
BF16 = jnp.bfloat16

D_MODEL = 1024
HEAD_DIM = 64
N_HEADS = 8
N_KV = 2
GROUP = N_HEADS // N_KV
CMP_LEN = 32
CMP_STRIDE = 16
CMP_HIDDEN = 256
SLC_LEN = 64
SLC_TOPK = 8
NSA_WINDOW = 256
SWA_WINDOW = 128
N_GROUPS = 4
EPG = 8
N_EXPERTS = N_GROUPS * EPG
EXPERT_FF = 256
RMS_EPS = 1e-6
NEG_INF = -1e30
FORCE_SCORE = 1e9

LANES = 128
TQ = 128
KCHUNK = 512
N_SLC_BLK = LANES // 4
FEAT_POS = HEAD_DIM
FEAT_SEL = HEAD_DIM + 6
ROWS_LO, ROWS_HI = 64, 104
N_KV_SECT = 6
COL_QB = 512
COL_KV = 1024
COL_GN = COL_KV + N_KV_SECT * LANES
PROJ_W = COL_GN + LANES
VT_W = N_KV_SECT * LANES
CMP_W = 2 * LANES
ROW_TILE = D_MODEL // LANES
VMEM_LIMIT = 56 * 1024 * 1024


def _alibi_slopes():
    n = 2 * N_HEADS
    s = 2.0 ** (-8.0 * np.arange(1, n + 1) / n)
    return [float(v) for v in s[:N_HEADS]], [float(v) for v in s[N_HEADS:]]


SLOPES_SWA, SLOPES_NSA = _alibi_slopes()


def _bf16_pieces(v):
    out, rem = [], np.float32(v)
    for _ in range(3):
        p = np.float32(np.asarray(rem, np.float32).astype(BF16).astype(np.float32))
        out.append(float(p))
        rem = np.float32(rem - p)
    return out


def _rms(x, g):
    return x * lax.rsqrt(jnp.mean(x * x, axis=-1, keepdims=True) + RMS_EPS) * g


def _dot(a, b):
    return jnp.dot(a, b, preferred_element_type=F32)


def _tiles_to_rows(ref, n, lead=()):
    return jnp.concatenate([ref[lead + (pl.ds(j, n, stride=ROW_TILE), slice(None))] for j in range(ROW_TILE)], axis=1)


def _rows_to_tiles(ref, val):
    n = val.shape[0]
    for j in range(ROW_TILE):
        ref[pl.ds(j, n, stride=ROW_TILE), :] = val[:, j * LANES:(j + 1) * LANES]


def _dot_nt(a, b):
    return lax.dot_general(a, b, (((1,), (1,)), ((), ())), preferred_element_type=F32)


def _proj_kernel(x_ref, g_ref, w_ref, fk_ref, fv_ref, eye_ref, o_ref, vt_ref, cmp_ref):
    h = _rms(x_ref[...], g_ref[...]).astype(BF16)
    res = _dot(h, w_ref[...])
    o_ref[:, :COL_KV] = res[:, :COL_KV].astype(o_ref.dtype)
    for j in range(N_KV_SECT):
        c0 = COL_KV + j * LANES
        o_ref[:, c0:c0 + LANES] = (res[:, c0:c0 + LANES] + fk_ref[...]).astype(o_ref.dtype)
    o_ref[:, COL_GN:] = res[:, COL_GN:PROJ_W].astype(o_ref.dtype)
    for j in range(N_KV_SECT):
        c0 = PROJ_W + j * LANES
        v_t = _dot_nt(eye_ref[...], (res[:, c0:c0 + LANES] + fv_ref[...]).astype(BF16))
        for k in range(v_t.shape[1] // LANES):
            vt_ref[j, 0, k] = v_t[:, k * LANES:(k + 1) * LANES].astype(vt_ref.dtype)
    cmp_ref[...] = res[:, PROJ_W + VT_W:]


def _proj(x2d, g, w, feat_k, feat_v, eye, tm):
    n = x2d.shape[0]
    seq = feat_k.shape[0]
    nper = seq // tm
    kb = tm // LANES
    return pl.pallas_call(
        _proj_kernel,
        grid=(n // tm,),
        in_specs=[pl.BlockSpec((tm, D_MODEL), lambda i: (i, 0)),
                  pl.BlockSpec((1, D_MODEL), lambda i: (0, 0)),
                  pl.BlockSpec((D_MODEL, PROJ_W + VT_W + CMP_W), lambda i: (0, 0)),
                  pl.BlockSpec((tm, LANES), lambda i: (i % nper, 0)),
                  pl.BlockSpec((tm, LANES), lambda i: (i % nper, 0)),
                  pl.BlockSpec((LANES, LANES), lambda i: (0, 0))],
        out_specs=[pl.BlockSpec((tm, PROJ_W), lambda i: (i, 0)),
                   pl.BlockSpec((N_KV_SECT, 1, kb, LANES, LANES), lambda i: (0, i // nper, i % nper, 0, 0)),
                   pl.BlockSpec((tm, CMP_W), lambda i: (i, 0))],
        out_shape=[jax.ShapeDtypeStruct((n, PROJ_W), BF16),
                   jax.ShapeDtypeStruct((N_KV_SECT, n // seq, seq // LANES, LANES, LANES), BF16),
                   jax.ShapeDtypeStruct((n, CMP_W), F32)],
        compiler_params=pltpu.CompilerParams(dimension_semantics=("parallel",),
                                             vmem_limit_bytes=VMEM_LIMIT),
        name="proj",
    )(x2d, g, w, feat_k, feat_v, eye)


def _compress_kernel(x_ref, pa_ref, pb_ref, w1a_ref, w1b_ref, w2_ref, w2t_ref, o_ref, ot_ref, *, nch):
    r = jnp.concatenate([x_ref[0, pl.ds(j, nch, stride=CMP_STRIDE), :] for j in range(CMP_STRIDE)], axis=1)
    a = _dot((r + pa_ref[0]).astype(BF16), w1a_ref[0])
    b = _dot((r + pb_ref[0]).astype(BF16), w1b_ref[0])
    hid = a + pltpu.roll(b, nch - 1, 0)
    hid = hid * jax.nn.sigmoid(hid)
    for h in range(N_KV):
        hid_h = hid[:, h * CMP_HIDDEN:(h + 1) * CMP_HIDDEN].astype(BF16)
        o_ref[0, 0, h] = _dot(hid_h, w2_ref[0]).astype(o_ref.dtype)
        ot_ref[0, 0, h] = _dot_nt(w2t_ref[0], hid_h).astype(ot_ref.dtype)


def _compress(cmp3, pos_a, pos_b, w1a, w1b, w2, w2t):
    bsz, seq, _ = cmp3.shape
    nch = seq // CMP_STRIDE
    wspec = lambda a: pl.BlockSpec((1,) + a.shape[1:], lambda b, j: (j, 0, 0))
    return pl.pallas_call(
        functools.partial(_compress_kernel, nch=nch),
        grid=(bsz, 2),
        in_specs=[pl.BlockSpec((1, seq, LANES), lambda b, j: (b, 0, j)),
                  wspec(pos_a), wspec(pos_b), wspec(w1a), wspec(w1b), wspec(w2), wspec(w2t)],
        out_specs=[pl.BlockSpec((1, 1, N_KV, nch, LANES), lambda b, j: (b, j, 0, 0, 0)),
                   pl.BlockSpec((1, 1, N_KV, LANES, nch), lambda b, j: (b, j, 0, 0, 0))],
        out_shape=[jax.ShapeDtypeStruct((bsz, 2, N_KV, nch, LANES), BF16),
                   jax.ShapeDtypeStruct((bsz, 2, N_KV, LANES, nch), BF16)],
        compiler_params=pltpu.CompilerParams(dimension_semantics=("parallel", "parallel"),
                                             vmem_limit_bytes=VMEM_LIMIT),
        name="compress",
    )(cmp3, pos_a, pos_b, w1a, w1b, w2, w2t)


def _attn_t_kernel(sinks_ref, qa_ref, qb_ref, kc_ref, vct_ref, *rest, n_cmp):
    n_br = 3
    ks = rest[:n_br * N_KV]
    vts = rest[n_br * N_KV:2 * n_br * N_KV]
    gn_ref, ovt_ref, eye_ref, oa_ref, ob_ref = rest[2 * n_br * N_KV:]
    ksect = lambda branch, h: ks[branch * N_KV + h]
    vsect = lambda branch, h: vts[branch * N_KV + h]
    qi = pl.program_id(1)
    q0 = pl.multiple_of(qi * TQ, TQ)
    lane = lax.broadcasted_iota(jnp.int32, (1, TQ), 1)
    sub = lax.broadcasted_iota(jnp.int32, (LANES, 1), 0)
    t_row = q0 + lane
    eye = eye_ref[...]
    gates = jax.nn.sigmoid(_dot_nt(eye, gn_ref[0]))
    gate = lambda hh, c: gates[3 * hh + c:3 * hh + c + 1, :]
    sub40 = sub[ROWS_LO:ROWS_HI]
    blk = sub40 - FEAT_SEL
    in_rng = (blk >= 0) & (blk < N_SLC_BLK)
    is_pos = (sub40 >= FEAT_POS) & (sub40 < FEAT_SEL)
    zeros_lo = jnp.zeros((LANES - ROWS_HI, TQ), F32)

    def q_t(ref, hh):
        both = _dot_nt(eye, ref[0, :, (hh // 2) * LANES:(hh // 2 + 1) * LANES])
        return both[(hh % 2) * HEAD_DIM:(hh % 2 + 1) * HEAD_DIM]

    def slope_col(slope):
        hi, mid, lo = _bf16_pieces(slope)
        col = jnp.zeros(sub40.shape, F32)
        for i, v in enumerate([SLC_LEN * hi, SLC_LEN * mid, SLC_LEN * lo, hi, mid, lo]):
            col = jnp.where(sub40 == FEAT_POS + i, v, col)
        return col

    def q_aug_t(q_ts, tails):
        return jnp.concatenate([jnp.concatenate([q, jnp.broadcast_to(t, (ROWS_HI - ROWS_LO, TQ)), zeros_lo], axis=0)
                                for q, t in zip(q_ts, tails)], axis=1).astype(BF16)

    def v_t(ref, first, n):
        return jnp.concatenate([ref[0, 0, first + j] for j in range(n)], axis=1)

    def masked(s, mask):
        return jnp.concatenate([jnp.where(mask, s[:, g * TQ:(g + 1) * TQ], NEG_INF) for g in range(GROUP)], axis=1)

    def store_heads(ref, outs):
        for p in range(N_HEADS // 2):
            pair = jnp.concatenate([outs[2 * p][:HEAD_DIM], outs[2 * p + 1][:HEAD_DIM]], axis=0).astype(BF16)
            ref[0, :, p * LANES:(p + 1) * LANES] = _dot_nt(eye, pair).astype(ref.dtype)

    nw = NSA_WINDOW // TQ + 1
    w_first = jnp.maximum(qi - (nw - 1), 0)
    w_start = pl.multiple_of(w_first * TQ, TQ)
    nb = (SWA_WINDOW - 1 + TQ - 1) // TQ + 1
    b_first = jnp.maximum(qi - (nb - 1), 0)
    b_start = pl.multiple_of(b_first * TQ, TQ)
    pre = []
    for h in range(N_KV):
        heads = [h * GROUP + g for g in range(GROUP)]
        q_ts = [q_t(qa_ref, hh) for hh in heads]
        scol = [slope_col(SLOPES_NSA[hh]) for hh in heads]
        s_cmp = _dot(kc_ref[0, 0, h], q_aug_t(q_ts, [jnp.zeros((1, 1), F32)] * GROUP))
        s_win = _dot(ksect(1, h)[0, pl.ds(w_start, nw * TQ), :], q_aug_t(q_ts, scol))
        q_b = q_aug_t([q_t(qb_ref, hh) for hh in heads], [slope_col(SLOPES_SWA[hh]) for hh in heads])
        s_swa = _dot(ksect(2, h)[0, pl.ds(b_start, nb * TQ), :], q_b)
        pre.append((q_ts, scol, s_cmp, s_win, s_swa))

    def nsa_local(h):
        heads = [h * GROUP + g for g in range(GROUP)]
        _, _, s_all, s_win, _ = pre[h]

        end_c = sub * CMP_STRIDE + (CMP_LEN - 1)
        cmask = (t_row >= end_c) & (sub < n_cmp)
        ps = []
        for g in range(GROUP):
            s = s_all[:, g * TQ:(g + 1) * TQ] + SLOPES_NSA[heads[g]] * end_c.astype(F32)
            s = jnp.where(cmask, s, NEG_INF)
            m = jnp.max(s, axis=0, keepdims=True)
            e = jnp.where(cmask, jnp.exp(s - m), 0.0)
            z = jnp.sum(e, axis=0, keepdims=True)
            ps.append(e / jnp.where(z > 0, z, 1.0))
        o_cmp = _dot(vct_ref[0, 0, h], jnp.concatenate(ps, axis=1).astype(BF16))

        dist = t_row - (w_start + lax.broadcasted_iota(jnp.int32, (nw * TQ, 1), 0))
        s = masked(s_win, (dist >= 0) & (dist < NSA_WINDOW))
        m = jnp.max(s, axis=0, keepdims=True)
        acc = _dot(v_t(vsect(1, h), w_first, nw), jnp.exp(s - m).astype(BF16))
        o_win = acc / acc[HEAD_DIM:HEAD_DIM + 1, :]

        psum = ps[0] + ps[1] + ps[2] + ps[3]
        p_hi = psum.astype(BF16)
        p_lo = (psum - p_hi.astype(F32)).astype(BF16)
        imp = (_dot(ovt_ref[...], p_hi) + _dot(ovt_ref[...], p_lo))[ROWS_LO:ROWS_HI]
        part = [gate(hh, 0) * o_cmp[:, g * TQ:(g + 1) * TQ] + gate(hh, 2) * o_win[:, g * TQ:(g + 1) * TQ]
                for g, hh in enumerate(heads)]
        return imp, part

    def nsa_select(h, imp):
        q_ts, scol = pre[h][:2]
        cur = lax.shift_right_logical(t_row, int(np.log2(SLC_LEN)))
        valid = in_rng & (blk * SLC_LEN <= t_row)
        forced = in_rng & ((blk == 0) | (blk == cur) | (blk == cur - 1))
        score = jnp.where(forced, FORCE_SCORE, jnp.where(valid, imp, NEG_INF))
        rank = jnp.zeros(score.shape, F32)
        for i in range(N_SLC_BLK):
            r = FEAT_SEL - ROWS_LO + i
            si = score[r:r + 1, :]
            rank = rank + jnp.where((si > score) | ((si == score) & (blk > i)), 1.0, 0.0)
        sel = in_rng & (rank < SLC_TOPK) & (score > 0.5 * NEG_INF)
        bias_diag = jnp.where(in_rng & jnp.logical_not(sel), NEG_INF, 0.0)
        bias_main = jnp.where(in_rng & jnp.logical_not(sel & (blk < 2 * qi)), NEG_INF, 0.0)

        q_d = q_aug_t(q_ts, [jnp.where(is_pos, scol[g], bias_diag) for g in range(GROUP)])
        q_m = q_aug_t(q_ts, [jnp.where(is_pos, scol[g], bias_main) for g in range(GROUP)])
        kpos = q0 + sub
        s = masked(_dot(ksect(0, h)[0, pl.ds(q0, TQ), :], q_d), kpos <= t_row)
        m0 = jnp.max(s, axis=0, keepdims=True)
        acc0 = _dot(v_t(vsect(0, h), qi, 1), jnp.exp(s - m0).astype(BF16))
        return q_m, m0, acc0

    local = [nsa_local(h) for h in range(N_KV)]

    dist = t_row - (b_start + lax.broadcasted_iota(jnp.int32, (nb * TQ, 1), 0))
    bmask = (dist >= 0) & (dist < SWA_WINDOW)
    outs_b = []
    for h in range(N_KV):
        heads = [h * GROUP + g for g in range(GROUP)]
        s = masked(pre[h][4], bmask)
        sink = jnp.concatenate([sinks_ref[hh] + SLOPES_SWA[hh] * t_row.astype(F32) for hh in heads], axis=1)
        m = jnp.maximum(jnp.max(s, axis=0, keepdims=True), sink)
        acc = _dot(v_t(vsect(2, h), b_first, nb), jnp.exp(s - m).astype(BF16))
        o_all = acc / (acc[HEAD_DIM:HEAD_DIM + 1, :] + jnp.exp(sink - m))
        outs_b += [o_all[:, g * TQ:(g + 1) * TQ] for g in range(GROUP)]
    store_heads(ob_ref, outs_b)

    fronts = [nsa_select(h, local[h][0]) + (local[h][1],) for h in range(N_KV)]
    nblk = KCHUNK // TQ

    def slc_step(c, carry):
        start = pl.multiple_of(c * KCHUNK, KCHUNK)
        scores = [_dot(ksect(0, h)[0, pl.ds(start, KCHUNK), :], fronts[h][0]) for h in range(N_KV)]
        out = []
        for h, (m_i, acc) in enumerate(carry):
            s = scores[h]
            m_new = jnp.maximum(m_i, jnp.max(s, axis=0, keepdims=True))
            pv = _dot(v_t(vsect(0, h), c * nblk, nblk), jnp.exp(s - m_new).astype(BF16))
            out.append((m_new, jnp.exp(m_i - m_new) * acc + pv))
        return tuple(out)

    n_main = lax.shift_right_logical(q0 + (KCHUNK - 1), int(np.log2(KCHUNK)))
    swept = lax.fori_loop(0, n_main, slc_step, tuple((f[1], f[2]) for f in fronts))
    outs_a = []
    for h in range(N_KV):
        acc = swept[h][1]
        o_slc = acc / acc[HEAD_DIM:HEAD_DIM + 1, :]
        outs_a += [fronts[h][3][g] + gate(h * GROUP + g, 1) * o_slc[:, g * TQ:(g + 1) * TQ] for g in range(GROUP)]
    store_heads(oa_ref, outs_a)


def _attention(proj3, v_t, kc, vc_t, ovt, eye, sinks):
    bsz, seq, _ = proj3.shape
    n_cmp = seq // CMP_STRIDE - CMP_LEN // CMP_STRIDE + 1
    width = N_HEADS * HEAD_DIM
    consts = [ovt, eye]
    qspec = lambda col: pl.BlockSpec((1, TQ, width), lambda b, q, s: (b, q, col))
    cspec = lambda a, j: pl.BlockSpec((1, 1) + a.shape[2:], lambda b, q, s: (b, j, 0, 0, 0))
    in_specs = [qspec(0), qspec(COL_QB // width), cspec(kc, 0), cspec(vc_t, 1)]
    in_specs += [pl.BlockSpec((1, seq, LANES), lambda b, q, s, j=j: (b, 0, COL_KV // LANES + j))
                 for j in range(N_KV_SECT)]
    in_specs += [pl.BlockSpec((1, 1) + v_t.shape[2:], lambda b, q, s, j=j: (j, b, 0, 0, 0)) for j in range(N_KV_SECT)]
    in_specs += [pl.BlockSpec((1, TQ, LANES), lambda b, q, s: (b, q, COL_GN // LANES))]
    in_specs += [pl.BlockSpec(c.shape, lambda b, q, s: (0, 0)) for c in consts]
    ospec = pl.BlockSpec((1, TQ, width), lambda b, q, s: (b, q, 0))
    return pl.pallas_call(
        functools.partial(_attn_t_kernel, n_cmp=n_cmp),
        grid_spec=pltpu.PrefetchScalarGridSpec(
            num_scalar_prefetch=1, grid=(bsz, seq // TQ), in_specs=in_specs, out_specs=[ospec, ospec]),
        out_shape=[jax.ShapeDtypeStruct((bsz, seq, width), BF16)] * 2,
        compiler_params=pltpu.CompilerParams(dimension_semantics=("parallel", "parallel"),
                                             vmem_limit_bytes=VMEM_LIMIT),
        name="attn",
    )(sinks, proj3, proj3, kc, vc_t, *([proj3] * N_KV_SECT), *([v_t] * N_KV_SECT), proj3, *consts)


def _merge_kernel(x_ref, oa_ref, ob_ref, g1_ref, g2_ref, wg_ref, wa_ref, wb_ref, wo_ref,
                  wrh_ref, wrl_ref, br_ref, x2_ref, hn_ref, route_ref, cnt_ref):
    tm = x_ref.shape[0]

    @pl.when(pl.program_id(0) == 0)
    def _():
        cnt_ref[...] = jnp.zeros_like(cnt_ref)

    x = x_ref[...]
    h = _rms(x, g1_ref[...]).astype(BF16)
    gm = jax.nn.sigmoid(_dot(h, wg_ref[...]))
    a = _dot(oa_ref[...], wa_ref[...])
    b = _dot(ob_ref[...], wb_ref[...])
    mixin = gm[:, :D_MODEL] * a + gm[:, D_MODEL:] * b
    x2 = x + _dot(mixin.astype(BF16), wo_ref[...])
    x2_ref[...] = x2
    hn = _rms(x2, g2_ref[...])
    hn_b = hn.astype(BF16)
    _rows_to_tiles(hn_ref, hn)

    hn_lo = (hn - hn_b.astype(F32)).astype(BF16)
    logits = (_dot(hn_b, wrh_ref[...]) + _dot(hn_lo, wrh_ref[...]) + _dot(hn_b, wrl_ref[...])
              + br_ref[...])
    lane = lax.broadcasted_iota(jnp.int32, (1, LANES), 1)
    lanef = lane.astype(F32)
    big = float(LANES)
    is_g = (lane >= N_EXPERTS) & (lane < N_EXPERTS + N_GROUPS)
    gl = jnp.where(is_g, logits, NEG_INF)
    gmax = jnp.max(gl, axis=-1, keepdims=True)
    grp = jnp.min(jnp.where(gl == gmax, lanef, big), axis=-1, keepdims=True) - N_EXPERTS
    p_grp = 1.0 / jnp.sum(jnp.where(is_g, jnp.exp(gl - gmax), 0.0), axis=-1, keepdims=True)
    in_grp = (lanef >= grp * EPG) & (lanef < grp * EPG + EPG)
    el = jnp.where(in_grp, logits, NEG_INF)
    v0 = jnp.max(el, axis=-1, keepdims=True)
    i0 = jnp.min(jnp.where(el == v0, lanef, big), axis=-1, keepdims=True)
    el1 = jnp.where(lanef == i0, NEG_INF, el)
    v1 = jnp.max(el1, axis=-1, keepdims=True)
    i1 = jnp.min(jnp.where(el1 == v1, lanef, big), axis=-1, keepdims=True)
    e1 = jnp.exp(v1 - v0)
    w0 = p_grp / (1.0 + e1)
    w1 = p_grp * e1 / (1.0 + e1)

    oh0 = jnp.where(lanef == i0, 1.0, 0.0)
    oh1 = jnp.where(lanef == i1, 1.0, 0.0)
    oh = oh0 + oh1
    r_i = lax.broadcasted_iota(jnp.int32, (tm, tm), 0)
    c_i = lax.broadcasted_iota(jnp.int32, (tm, tm), 1)
    lower = jnp.where(c_i < r_i, 1.0, 0.0).astype(BF16)
    before = cnt_ref[...] + _dot(lower, oh.astype(BF16))
    rank0 = jnp.sum(oh0 * before, axis=-1, keepdims=True)
    rank1 = jnp.sum(oh1 * before, axis=-1, keepdims=True)
    cnt_ref[...] = cnt_ref[...] + jnp.sum(oh, axis=0, keepdims=True)

    route = jnp.where(lane == 0, i0, jnp.where(lane == 1, i1, jnp.where(lane == 2, rank0,
            jnp.where(lane == 3, rank1, jnp.where(lane == 4, w0, jnp.where(lane == 5, w1, 0.0))))))
    route_ref[...] = route


def _merge(x2d, oa, ob, g1, g2, wg, wa, wb, wo, wrh, wrl, br, tm):
    n = x2d.shape[0]
    width = N_HEADS * HEAD_DIM
    row = lambda w: pl.BlockSpec((tm, w), lambda i: (i, 0))
    full = lambda a: pl.BlockSpec(a.shape, lambda i: (0, 0))
    return pl.pallas_call(
        _merge_kernel,
        grid=(n // tm,),
        in_specs=[row(D_MODEL), row(width), row(width), full(g1), full(g2), full(wg), full(wa), full(wb),
                  full(wo), full(wrh), full(wrl), full(br)],
        out_specs=[row(D_MODEL), pl.BlockSpec((tm * ROW_TILE, LANES), lambda i: (i, 0)), row(LANES),
                   pl.BlockSpec((1, LANES), lambda i: (0, 0))],
        out_shape=[jax.ShapeDtypeStruct((n, D_MODEL), F32), jax.ShapeDtypeStruct((n * ROW_TILE, LANES), F32),
                   jax.ShapeDtypeStruct((n, LANES), F32), jax.ShapeDtypeStruct((1, LANES), F32)],
        compiler_params=pltpu.CompilerParams(dimension_semantics=("arbitrary",),
                                             vmem_limit_bytes=VMEM_LIMIT),
        name="merge",
    )(x2d, oa, ob, g1, g2, wg, wa, wb, wo, wrh, wrl, br)


def _tile(t):
    return pl.ds(pl.multiple_of(t * ROW_TILE, ROW_TILE), ROW_TILE)


def _slot_rows(ps_ref, tab_ref, i):
    return (ps_ref[tab_ref[0, 0, 4 * i]] + tab_ref[0, 0, 4 * i + 2],
            ps_ref[tab_ref[0, 0, 4 * i + 1]] + tab_ref[0, 0, 4 * i + 3])


def _dispatch_kernel(ps_ref, tab_ref, hn_ref, buf_in_ref, buf_ref, sem, *, td):
    del buf_in_ref

    def row_copy(i, d):
        return pltpu.make_async_copy(hn_ref.at[_tile(i)], buf_ref.at[_tile(d)], sem)

    def issue(i, c):
        d0, d1 = _slot_rows(ps_ref, tab_ref, i)
        row_copy(i, d0).start(priority=0)
        row_copy(i, d1).start(priority=1)
        return c

    lax.fori_loop(0, td, issue, 0, unroll=8)
    for _ in range(2):
        pltpu.make_async_copy(hn_ref, buf_ref.at[pl.ds(0, td * ROW_TILE)], sem).wait()


def _dispatch(pad_start, tab, hn, buf0, td):
    n = hn.shape[0] // ROW_TILE
    return pl.pallas_call(
        functools.partial(_dispatch_kernel, td=td),
        grid_spec=pltpu.PrefetchScalarGridSpec(
            num_scalar_prefetch=1, grid=(n // td,),
            in_specs=[pl.BlockSpec((1, 1, 4 * td), lambda i, ps: (i, 0, 0), memory_space=pltpu.SMEM),
                      pl.BlockSpec((td * ROW_TILE, LANES), lambda i, ps: (i, 0)),
                      pl.BlockSpec(memory_space=pl.ANY)],
            out_specs=pl.BlockSpec(memory_space=pl.ANY),
            scratch_shapes=[pltpu.SemaphoreType.DMA(())]),
        out_shape=jax.ShapeDtypeStruct(buf0.shape, buf0.dtype),
        input_output_aliases={3: 0},
        compiler_params=pltpu.CompilerParams(dimension_semantics=("arbitrary",)),
        name="dispatch",
    )(pad_start, tab.reshape(n // td, 1, 4 * td), hn, buf0)


def _expert_kernel(be_ref, nu_ref, xb_ref, wg_ref, wu_ref, wd_ref, y_ref, wg_s, wu_s, wd_s):
    b = pl.program_id(0)
    live = b < nu_ref[0]

    @pl.when(live & ((b == 0) | (be_ref[b] != be_ref[jnp.maximum(b - 1, 0)])))
    def _():
        wg_s[...] = wg_ref[0].astype(BF16)
        wu_s[...] = wu_ref[0].astype(BF16)
        wd_s[...] = wd_ref[0].astype(BF16)

    @pl.when(live)
    def _():
        xb = _tiles_to_rows(xb_ref, xb_ref.shape[0] // ROW_TILE).astype(BF16)
        g = _dot(xb, wg_s[...])
        u = _dot(xb, wu_s[...])
        _rows_to_tiles(y_ref, _dot((g * jax.nn.sigmoid(g) * u).astype(BF16), wd_s[...]))

    @pl.when(jnp.logical_not(live))
    def _():
        y_ref[...] = jnp.zeros_like(y_ref)


def _experts(block_e, n_used, buf, wg, wu, wd, tb):
    nblk = buf.shape[0] // (tb * ROW_TILE)
    live = lambda b, be, nu: jnp.minimum(b, nu[0] - 1)
    return pl.pallas_call(
        _expert_kernel,
        grid_spec=pltpu.PrefetchScalarGridSpec(
            num_scalar_prefetch=2, grid=(nblk,),
            in_specs=[pl.BlockSpec((tb * ROW_TILE, LANES), lambda b, be, nu: (live(b, be, nu), 0)),
                      pl.BlockSpec((1, D_MODEL, EXPERT_FF), lambda b, be, nu: (be[live(b, be, nu)], 0, 0)),
                      pl.BlockSpec((1, D_MODEL, EXPERT_FF), lambda b, be, nu: (be[live(b, be, nu)], 0, 0)),
                      pl.BlockSpec((1, EXPERT_FF, D_MODEL), lambda b, be, nu: (be[live(b, be, nu)], 0, 0))],
            out_specs=pl.BlockSpec((tb * ROW_TILE, LANES), lambda b, be, nu: (b, 0)),
            scratch_shapes=[pltpu.VMEM((D_MODEL, EXPERT_FF), BF16), pltpu.VMEM((D_MODEL, EXPERT_FF), BF16),
                            pltpu.VMEM((EXPERT_FF, D_MODEL), BF16)]),
        out_shape=jax.ShapeDtypeStruct(buf.shape, F32),
        compiler_params=pltpu.CompilerParams(dimension_semantics=("arbitrary",),
                                             vmem_limit_bytes=VMEM_LIMIT),
        name="experts",
    )(block_e, n_used, buf, wg, wu, wd)


def _combine_kernel(ps_ref, tab_ref, tab_next_ref, x2_ref, route_ref, gf_ref, yb_ref, o_ref, rows, sems, *, tc):
    step = pl.program_id(0)
    slot = lax.rem(step, 2)

    def row_copy(d, s, k, i):
        return pltpu.make_async_copy(yb_ref.at[_tile(d)], rows.at[s, k, _tile(i)], sems.at[s])

    def gather(tab, s):
        def issue(i, c):
            d0, d1 = _slot_rows(ps_ref, tab, i)
            row_copy(d0, s, 0, i).start(priority=0)
            row_copy(d1, s, 1, i).start(priority=1)
            return c
        lax.fori_loop(0, tc, issue, 0, unroll=8)

    @pl.when(step == 0)
    def _():
        gather(tab_ref, 0)

    @pl.when(step + 1 < pl.num_programs(0))
    def _():
        gather(tab_next_ref, 1 - slot)

    for k in range(2):
        pltpu.make_async_copy(yb_ref.at[pl.ds(0, tc * ROW_TILE)], rows.at[slot, k], sems.at[slot]).wait()
    route = route_ref[...]
    y0 = _tiles_to_rows(rows, tc, (slot, 0))
    y1 = _tiles_to_rows(rows, tc, (slot, 1))
    y = x2_ref[...] + (route[:, 4:5] * y0 + route[:, 5:6] * y1)
    o_ref[...] = _rms(y, gf_ref[...])


def _combine(pad_start, tab, x2, route, gf, yb, tc):
    n = x2.shape[0]
    nt = n // tc
    return pl.pallas_call(
        functools.partial(_combine_kernel, tc=tc),
        grid_spec=pltpu.PrefetchScalarGridSpec(
            num_scalar_prefetch=1, grid=(nt,),
            in_specs=[pl.BlockSpec((1, 1, 4 * tc), lambda i, ps: (i, 0, 0), memory_space=pltpu.SMEM),
                      pl.BlockSpec((1, 1, 4 * tc), lambda i, ps: (jnp.minimum(i + 1, nt - 1), 0, 0),
                                   memory_space=pltpu.SMEM),
                      pl.BlockSpec((tc, D_MODEL), lambda i, ps: (i, 0)),
                      pl.BlockSpec((tc, LANES), lambda i, ps: (i, 0)),
                      pl.BlockSpec((1, D_MODEL), lambda i, ps: (0, 0)),
                      pl.BlockSpec(memory_space=pl.ANY)],
            out_specs=pl.BlockSpec((tc, D_MODEL), lambda i, ps: (i, 0)),
            scratch_shapes=[pltpu.VMEM((2, 2, tc * ROW_TILE, LANES), F32), pltpu.SemaphoreType.DMA((2,))]),
        out_shape=jax.ShapeDtypeStruct((n, D_MODEL), F32),
        compiler_params=pltpu.CompilerParams(dimension_semantics=("arbitrary",),
                                             vmem_limit_bytes=VMEM_LIMIT),
        name="combine",
    )(pad_start, tab.reshape(nt, 1, 4 * tc), tab.reshape(nt, 1, 4 * tc), x2, route, gf, yb)


def _overlap_matrix_t(seq):
    nc = seq // CMP_STRIDE - CMP_LEN // CMP_STRIDE + 1
    ns = seq // SLC_LEN
    c0 = np.arange(nc) * CMP_STRIDE
    s0 = np.arange(ns) * SLC_LEN
    ov = np.clip(np.minimum(c0[:, None] + CMP_LEN, s0[None, :] + SLC_LEN)
                 - np.maximum(c0[:, None], s0[None, :]), 0, None) / CMP_LEN
    out = np.zeros((LANES, LANES), np.float32)
    out[FEAT_SEL:FEAT_SEL + ns, :nc] = ov.T
    return jnp.asarray(out, BF16)


def _position_features(seq):
    pos = np.arange(seq)
    fk = np.zeros((seq, LANES), np.float32)
    fk[:, FEAT_POS:FEAT_POS + 3] = (pos // SLC_LEN)[:, None]
    fk[:, FEAT_POS + 3:FEAT_POS + 6] = (pos % SLC_LEN)[:, None]
    fk[pos, FEAT_SEL + pos // SLC_LEN] = 1.0
    fv = np.zeros((seq, LANES), np.float32)
    fv[:, HEAD_DIM] = 1.0
    return jnp.asarray(fk), jnp.asarray(fv)


def _pick_tile(n, pref):
    t = pref
    while n % t:
        t //= 2
    return t


def kernel(x, norm_mix_g, w_in, cmp_pos_k, cmp_w1_k, cmp_w2_k, cmp_pos_v, cmp_w1_v, cmp_w2_v, sinks, w_a, w_b,
           w_o, norm_ffn_g, w_group, b_group, w_expert, b_expert, w_gate_e, w_up_e, w_down_e, norm_final_g):
    bsz, seq, _ = x.shape
    n = bsz * seq
    assert TQ == LANES and seq % KCHUNK == 0 and seq // SLC_LEN <= N_SLC_BLK and seq // CMP_STRIDE <= LANES
    assert seq >= (NSA_WINDOW // TQ + 1) * TQ and w_in.shape[0] == 1
    x2d = x.reshape(n, D_MODEL)

    w = w_in[0]
    scale = HEAD_DIM ** -0.5
    nsa_w, kvw = N_HEADS * HEAD_DIM, N_KV * HEAD_DIM
    o_qa, o_kva, o_gn = 0, nsa_w, nsa_w + 6 * kvw
    o_qb = o_gn + 3 * N_HEADS
    o_kvb = o_qb + nsa_w
    o_gm = o_kvb + 2 * kvw
    zpad = jnp.zeros((D_MODEL, LANES - HEAD_DIM), F32)
    def sections(offsets):
        return [c for off in offsets for h in range(N_KV)
                for c in (w[:, off + h * HEAD_DIM:off + (h + 1) * HEAD_DIM], zpad)]

    w_attn = jnp.concatenate(
        [w[:, o_qa:o_qa + nsa_w] * scale, w[:, o_qb:o_qb + nsa_w] * scale]
        + sections((o_kva + 2 * kvw, o_kva + 4 * kvw, o_kvb))
        + [w[:, o_gn:o_gn + 3 * N_HEADS], jnp.zeros((D_MODEL, LANES - 3 * N_HEADS), F32)]
        + sections((o_kva + 3 * kvw, o_kva + 5 * kvw, o_kvb + kvw))
        + [w[:, o_kva:o_kva + 2 * kvw]], axis=1).astype(BF16)
    w_gm = w[:, o_gm:o_gm + 2 * D_MODEL].astype(BF16)

    tm = _pick_tile(seq, 512)
    feat_k, feat_v = _position_features(seq)
    eye = jnp.eye(LANES, dtype=BF16)
    proj, v_t, cmp_in = _proj(x2d, norm_mix_g[0][None], w_attn, feat_k, feat_v, eye, tm)
    proj3 = proj.reshape(bsz, seq, PROJ_W)

    nch = seq // CMP_STRIDE
    pos = jnp.stack([cmp_pos_k[0], cmp_pos_v[0]])
    pos = jnp.broadcast_to(pos[:, :, None, :], (2, CMP_LEN, N_KV, HEAD_DIM))
    pos_a = pos[:, :CMP_STRIDE].reshape(2, 1, CMP_STRIDE * kvw)
    pos_b = pos[:, CMP_STRIDE:].reshape(2, 1, CMP_STRIDE * kvw)
    w1 = jnp.stack([cmp_w1_k[0], cmp_w1_v[0]]).reshape(2, CMP_LEN, HEAD_DIM, CMP_HIDDEN)
    w1 = jnp.einsum('kjdc,hg->kjhdgc', w1, jnp.eye(N_KV, dtype=F32)).astype(BF16)
    w1 = w1.reshape(2, CMP_LEN * kvw, N_KV * CMP_HIDDEN)
    w2 = jnp.pad(jnp.stack([cmp_w2_k[0], cmp_w2_v[0]]), ((0, 0), (0, 0), (0, LANES - HEAD_DIM))).astype(BF16)
    kvc, kvc_t = _compress(cmp_in.reshape(bsz, seq, CMP_W), pos_a, pos_b, w1[:, :CMP_STRIDE * kvw],
                           w1[:, CMP_STRIDE * kvw:], w2, jnp.swapaxes(w2, 1, 2))
    kvc = jnp.pad(kvc, ((0, 0), (0, 0), (0, 0), (0, LANES - nch), (0, 0)))
    kvc_t = jnp.pad(kvc_t, ((0, 0), (0, 0), (0, 0), (0, 0), (0, LANES - nch)))

    o_a, o_b = _attention(proj3, v_t, kvc, kvc_t, _overlap_matrix_t(seq), eye, sinks[0])

    w_r = jnp.concatenate([w_expert[0], w_group[0],
                           jnp.zeros((D_MODEL, LANES - N_EXPERTS - N_GROUPS), F32)], axis=1)
    w_rh = w_r.astype(BF16)
    w_rl = (w_r - w_rh.astype(F32)).astype(BF16)
    b_r = jnp.concatenate([b_expert[0], b_group[0], jnp.zeros((LANES - N_EXPERTS - N_GROUPS,), F32)])[None]
    x2, hn, route, counts = _merge(
        x2d, o_a.reshape(n, nsa_w), o_b.reshape(n, nsa_w), norm_mix_g[0][None], norm_ffn_g[0][None], w_gm,
        w_a[0].astype(BF16), w_b[0].astype(BF16), w_o[0].astype(BF16), w_rh, w_rl, b_r, _pick_tile(n, 256))

    tb = 256
    nblk = -(-(2 * n + N_EXPERTS * (tb - 1)) // tb)
    cnt = counts[0, :N_EXPERTS].astype(jnp.int32)
    padded = (cnt + tb - 1) // tb * tb
    pad_end = jnp.cumsum(padded)
    pad_start = pad_end - padded
    block_e = jnp.minimum(jnp.sum(pad_end[None, :] <= (jnp.arange(nblk) * tb)[:, None], axis=1), N_EXPERTS - 1)
    n_used = (pad_end[-1:] // tb).astype(jnp.int32)
    tab = route[:, 0:4].astype(jnp.int32)

    buf = _dispatch(pad_start, tab, hn, jnp.zeros((nblk * tb * ROW_TILE, LANES), F32), _pick_tile(n, 512))
    yb = _experts(block_e.astype(jnp.int32), n_used, buf, w_gate_e[0], w_up_e[0], w_down_e[0], tb)
    out = _combine(pad_start, tab, x2, route, norm_final_g[None], yb, _pick_tile(n, 256))
    return out.reshape(bsz, seq, D_MODEL)
```

```python
import functools

import numpy as np
import jax
import jax.numpy as jnp
from jax import lax
from jax.experimental import pallas as pl
from jax.experimental.pallas import tpu as pltpu

F32 = jnp.float32
BF16 = jnp.bfloat16

D_MODEL = 1024
HEAD_DIM = 64
N_HEADS = 8
N_KV = 2
GROUP = N_HEADS // N_KV
CMP_LEN = 32
CMP_STRIDE = 16
CMP_HIDDEN = 256
SLC_LEN = 64
SLC_TOPK = 8
NSA_WINDOW = 256
SWA_WINDOW = 128
N_GROUPS = 4
EPG = 8
N_EXPERTS = N_GROUPS * EPG
EXPERT_FF = 256
RMS_EPS = 1e-6
NEG_INF = -1e30
FORCE_SCORE = 1e9

LANES = 128
TQ = 128
KCHUNK = 512
SUB = 128
N_SLC_BLK = LANES // 4
FEAT_POS = HEAD_DIM
FEAT_SEL = HEAD_DIM + 6
ROWS_LO, ROWS_HI = 64, 104
N_KV_SECT = 6
COL_QB = 512
COL_KV = 1024
COL_GN = COL_KV + N_KV_SECT * LANES
PROJ_W = COL_GN + LANES
VT_W = N_KV_SECT * LANES
CMP_W = 2 * LANES
ROW_TILE = D_MODEL // LANES
VMEM_LIMIT = 56 * 1024 * 1024


def _alibi_slopes():
    n = 2 * N_HEADS
    s = 2.0 ** (-8.0 * np.arange(1, n + 1) / n)
    return [float(v) for v in s[:N_HEADS]], [float(v) for v in s[N_HEADS:]]


SLOPES_SWA, SLOPES_NSA = _alibi_slopes()


def _bf16_pieces(v):
    out, rem = [], np.float32(v)
    for _ in range(3):
        p = np.float32(np.asarray(rem, np.float32).astype(BF16).astype(np.float32))
        out.append(float(p))
        rem = np.float32(rem - p)
    return out


def _rms(x, g):
    return x * lax.rsqrt(jnp.mean(x * x, axis=-1, keepdims=True) + RMS_EPS) * g


def _dot(a, b):
    return jnp.dot(a, b, preferred_element_type=F32)


def _tiles_to_rows(ref, n, lead=()):
    return jnp.concatenate([ref[lead + (pl.ds(j, n, stride=ROW_TILE), slice(None))] for j in range(ROW_TILE)], axis=1)


def _rows_to_tiles(ref, val):
    n = val.shape[0]
    for j in range(ROW_TILE):
        ref[pl.ds(j, n, stride=ROW_TILE), :] = val[:, j * LANES:(j + 1) * LANES]


def _dot_nt(a, b):
    return lax.dot_general(a, b, (((1,), (1,)), ((), ())), preferred_element_type=F32)


def _proj_kernel(x_ref, g_ref, w_ref, fk_ref, fv_ref, eye_ref, o_ref, vt_ref, cmp_ref):
    h = _rms(x_ref[...], g_ref[...]).astype(BF16)
    res = _dot(h, w_ref[...])
    o_ref[:, :COL_KV] = res[:, :COL_KV].astype(o_ref.dtype)
    for j in range(N_KV_SECT):
        c0 = COL_KV + j * LANES
        o_ref[:, c0:c0 + LANES] = (res[:, c0:c0 + LANES] + fk_ref[...]).astype(o_ref.dtype)
    o_ref[:, COL_GN:] = res[:, COL_GN:PROJ_W].astype(o_ref.dtype)
    for j in range(N_KV_SECT):
        c0 = PROJ_W + j * LANES
        v_t = _dot_nt(eye_ref[...], (res[:, c0:c0 + LANES] + fv_ref[...]).astype(BF16))
        for k in range(v_t.shape[1] // LANES):
            vt_ref[j, 0, k] = v_t[:, k * LANES:(k + 1) * LANES].astype(vt_ref.dtype)
    cmp_ref[...] = res[:, PROJ_W + VT_W:]


def _proj(x2d, g, w, feat_k, feat_v, eye, tm):
    n = x2d.shape[0]
    seq = feat_k.shape[0]
    nper = seq // tm
    kb = tm // LANES
    return pl.pallas_call(
        _proj_kernel,
        grid=(n // tm,),
        in_specs=[pl.BlockSpec((tm, D_MODEL), lambda i: (i, 0)),
                  pl.BlockSpec((1, D_MODEL), lambda i: (0, 0)),
                  pl.BlockSpec((D_MODEL, PROJ_W + VT_W + CMP_W), lambda i: (0, 0)),
                  pl.BlockSpec((tm, LANES), lambda i: (i % nper, 0)),
                  pl.BlockSpec((tm, LANES), lambda i: (i % nper, 0)),
                  pl.BlockSpec((LANES, LANES), lambda i: (0, 0))],
        out_specs=[pl.BlockSpec((tm, PROJ_W), lambda i: (i, 0)),
                   pl.BlockSpec((N_KV_SECT, 1, kb, LANES, LANES), lambda i: (0, i // nper, i % nper, 0, 0)),
                   pl.BlockSpec((tm, CMP_W), lambda i: (i, 0))],
        out_shape=[jax.ShapeDtypeStruct((n, PROJ_W), BF16),
                   jax.ShapeDtypeStruct((N_KV_SECT, n // seq, seq // LANES, LANES, LANES), BF16),
                   jax.ShapeDtypeStruct((n, CMP_W), F32)],
        compiler_params=pltpu.CompilerParams(dimension_semantics=("parallel",),
                                             vmem_limit_bytes=VMEM_LIMIT),
        name="proj",
    )(x2d, g, w, feat_k, feat_v, eye)


def _compress_kernel(x_ref, pa_ref, pb_ref, w1a_ref, w1b_ref, w2_ref, w2t_ref, o_ref, ot_ref, *, nch):
    r = jnp.concatenate([x_ref[0, pl.ds(j, nch, stride=CMP_STRIDE), :] for j in range(CMP_STRIDE)], axis=1)
    a = _dot((r + pa_ref[0]).astype(BF16), w1a_ref[0])
    b = _dot((r + pb_ref[0]).astype(BF16), w1b_ref[0])
    hid = a + pltpu.roll(b, nch - 1, 0)
    hid = hid * jax.nn.sigmoid(hid)
    for h in range(N_KV):
        hid_h = hid[:, h * CMP_HIDDEN:(h + 1) * CMP_HIDDEN].astype(BF16)
        o_ref[0, 0, h] = _dot(hid_h, w2_ref[0]).astype(o_ref.dtype)
        ot_ref[0, 0, h] = _dot_nt(w2t_ref[0], hid_h).astype(ot_ref.dtype)


def _compress(cmp3, pos_a, pos_b, w1a, w1b, w2, w2t):
    bsz, seq, _ = cmp3.shape
    nch = seq // CMP_STRIDE
    wspec = lambda a: pl.BlockSpec((1,) + a.shape[1:], lambda b, j: (j, 0, 0))
    return pl.pallas_call(
        functools.partial(_compress_kernel, nch=nch),
        grid=(bsz, 2),
        in_specs=[pl.BlockSpec((1, seq, LANES), lambda b, j: (b, 0, j)),
                  wspec(pos_a), wspec(pos_b), wspec(w1a), wspec(w1b), wspec(w2), wspec(w2t)],
        out_specs=[pl.BlockSpec((1, 1, N_KV, nch, LANES), lambda b, j: (b, j, 0, 0, 0)),
                   pl.BlockSpec((1, 1, N_KV, LANES, nch), lambda b, j: (b, j, 0, 0, 0))],
        out_shape=[jax.ShapeDtypeStruct((bsz, 2, N_KV, nch, LANES), BF16),
                   jax.ShapeDtypeStruct((bsz, 2, N_KV, LANES, nch), BF16)],
        compiler_params=pltpu.CompilerParams(dimension_semantics=("parallel", "parallel"),
                                             vmem_limit_bytes=VMEM_LIMIT),
        name="compress",
    )(cmp3, pos_a, pos_b, w1a, w1b, w2, w2t)


def _attn_t_kernel(sinks_ref, qa_ref, qb_ref, kc_ref, vct_ref, *rest, n_cmp):
    n_br = 3
    ks = rest[:n_br * N_KV]
    vts = rest[n_br * N_KV:2 * n_br * N_KV]
    gn_ref, ovt_ref, eye_ref, oa_ref, ob_ref = rest[2 * n_br * N_KV:]
    ksect = lambda branch, h: ks[branch * N_KV + h]
    vsect = lambda branch, h: vts[branch * N_KV + h]
    qi = pl.program_id(1)
    q0 = pl.multiple_of(qi * TQ, TQ)
    lane = lax.broadcasted_iota(jnp.int32, (1, TQ), 1)
    sub = lax.broadcasted_iota(jnp.int32, (LANES, 1), 0)
    t_row = q0 + lane
    eye = eye_ref[...]
    gates = jax.nn.sigmoid(_dot_nt(eye, gn_ref[0]))
    gate = lambda hh, c: gates[3 * hh + c:3 * hh + c + 1, :]
    sub40 = sub[ROWS_LO:ROWS_HI]
    blk = sub40 - FEAT_SEL
    in_rng = (blk >= 0) & (blk < N_SLC_BLK)
    is_pos = (sub40 >= FEAT_POS) & (sub40 < FEAT_SEL)
    zeros_lo = jnp.zeros((LANES - ROWS_HI, TQ), F32)

    def q_t(ref, hh):
        both = _dot_nt(eye, ref[0, :, (hh // 2) * LANES:(hh // 2 + 1) * LANES])
        return both[(hh % 2) * HEAD_DIM:(hh % 2 + 1) * HEAD_DIM]

    def slope_col(slope):
        hi, mid, lo = _bf16_pieces(slope)
        col = jnp.zeros(sub40.shape, F32)
        for i, v in enumerate([SLC_LEN * hi, SLC_LEN * mid, SLC_LEN * lo, hi, mid, lo]):
            col = jnp.where(sub40 == FEAT_POS + i, v, col)
        return col

    def q_aug_t(q_ts, tails):
        return jnp.concatenate([jnp.concatenate([q, jnp.broadcast_to(t, (ROWS_HI - ROWS_LO, TQ)), zeros_lo], axis=0)
                                for q, t in zip(q_ts, tails)], axis=1).astype(BF16)

    def v_t(ref, first, n):
        return jnp.concatenate([ref[0, 0, first + j] for j in range(n)], axis=1)

    def masked(s, mask):
        return jnp.concatenate([jnp.where(mask, s[:, g * TQ:(g + 1) * TQ], NEG_INF) for g in range(GROUP)], axis=1)

    def store_heads(ref, outs):
        for p in range(N_HEADS // 2):
            pair = jnp.concatenate([outs[2 * p][:HEAD_DIM], outs[2 * p + 1][:HEAD_DIM]], axis=0).astype(BF16)
            ref[0, :, p * LANES:(p + 1) * LANES] = _dot_nt(eye, pair).astype(ref.dtype)

    nw = NSA_WINDOW // TQ + 1
    w_first = jnp.maximum(qi - (nw - 1), 0)
    w_start = pl.multiple_of(w_first * TQ, TQ)
    nb = (SWA_WINDOW - 1 + TQ - 1) // TQ + 1
    b_first = jnp.maximum(qi - (nb - 1), 0)
    b_start = pl.multiple_of(b_first * TQ, TQ)
    pre = []
    for h in range(N_KV):
        heads = [h * GROUP + g for g in range(GROUP)]
        q_ts = [q_t(qa_ref, hh) for hh in heads]
        scol = [slope_col(SLOPES_NSA[hh]) for hh in heads]
        s_cmp = _dot(kc_ref[0, 0, h], q_aug_t(q_ts, [jnp.zeros((1, 1), F32)] * GROUP))
        s_win = _dot(ksect(1, h)[0, pl.ds(w_start, nw * TQ), :], q_aug_t(q_ts, scol))
        q_b = q_aug_t([q_t(qb_ref, hh) for hh in heads], [slope_col(SLOPES_SWA[hh]) for hh in heads])
        s_swa = _dot(ksect(2, h)[0, pl.ds(b_start, nb * TQ), :], q_b)
        pre.append((q_ts, scol, s_cmp, s_win, s_swa))

    def nsa_local(h):
        heads = [h * GROUP + g for g in range(GROUP)]
        _, _, s_all, s_win, _ = pre[h]

        end_c = sub * CMP_STRIDE + (CMP_LEN - 1)
        cmask = (t_row >= end_c) & (sub < n_cmp)
        ps = []
        for g in range(GROUP):
            s = s_all[:, g * TQ:(g + 1) * TQ] + SLOPES_NSA[heads[g]] * end_c.astype(F32)
            s = jnp.where(cmask, s, NEG_INF)
            m = jnp.max(s, axis=0, keepdims=True)
            e = jnp.where(cmask, jnp.exp(s - m), 0.0)
            z = jnp.sum(e, axis=0, keepdims=True)
            ps.append(e / jnp.where(z > 0, z, 1.0))
        o_cmp = _dot(vct_ref[0, 0, h], jnp.concatenate(ps, axis=1).astype(BF16))

        dist = t_row - (w_start + lax.broadcasted_iota(jnp.int32, (nw * TQ, 1), 0))
        s = masked(s_win, (dist >= 0) & (dist < NSA_WINDOW))
        m = jnp.max(s, axis=0, keepdims=True)
        acc = _dot(v_t(vsect(1, h), w_first, nw), jnp.exp(s - m).astype(BF16))
        o_win = acc / acc[HEAD_DIM:HEAD_DIM + 1, :]

        psum = ps[0] + ps[1] + ps[2] + ps[3]
        p_hi = psum.astype(BF16)
        p_lo = (psum - p_hi.astype(F32)).astype(BF16)
        imp = (_dot(ovt_ref[...], p_hi) + _dot(ovt_ref[...], p_lo))[ROWS_LO:ROWS_HI]
        part = [gate(hh, 0) * o_cmp[:, g * TQ:(g + 1) * TQ] + gate(hh, 2) * o_win[:, g * TQ:(g + 1) * TQ]
                for g, hh in enumerate(heads)]
        return imp, part

    def nsa_select(h, imp):
        q_ts, scol = pre[h][:2]
        cur = lax.shift_right_logical(t_row, int(np.log2(SLC_LEN)))
        valid = in_rng & (blk * SLC_LEN <= t_row)
        forced = in_rng & ((blk == 0) | (blk == cur) | (blk == cur - 1))
        score = jnp.where(forced, FORCE_SCORE, jnp.where(valid, imp, NEG_INF))
        rank = jnp.zeros(score.shape, F32)
        for i in range(N_SLC_BLK):
            r = FEAT_SEL - ROWS_LO + i
            si = score[r:r + 1, :]
            rank = rank + jnp.where((si > score) | ((si == score) & (blk > i)), 1.0, 0.0)
        sel = in_rng & (rank < SLC_TOPK) & (score > 0.5 * NEG_INF)
        bias_diag = jnp.where(in_rng & jnp.logical_not(sel), NEG_INF, 0.0)
        bias_main = jnp.where(in_rng & jnp.logical_not(sel & (blk < 2 * qi)), NEG_INF, 0.0)

        q_d = q_aug_t(q_ts, [jnp.where(is_pos, scol[g], bias_diag) for g in range(GROUP)])
        q_m = q_aug_t(q_ts, [jnp.where(is_pos, scol[g], bias_main) for g in range(GROUP)])
        return q_m, _dot(ksect(0, h)[0, pl.ds(q0, TQ), :], q_d)

    def nsa_diag(h, q_m, s_diag):
        s = masked(s_diag, q0 + sub <= t_row)
        m0 = jnp.max(s, axis=0, keepdims=True)
        acc0 = _dot(v_t(vsect(0, h), qi, 1), jnp.exp(s - m0).astype(BF16))
        return q_m, m0, acc0

    local = [nsa_local(h) for h in range(N_KV)]

    dist = t_row - (b_start + lax.broadcasted_iota(jnp.int32, (nb * TQ, 1), 0))
    bmask = (dist >= 0) & (dist < SWA_WINDOW)
    outs_b = []
    for h in range(N_KV):
        heads = [h * GROUP + g for g in range(GROUP)]
        s = masked(pre[h][4], bmask)
        sink = jnp.concatenate([sinks_ref[hh] + SLOPES_SWA[hh] * t_row.astype(F32) for hh in heads], axis=1)
        m = jnp.maximum(jnp.max(s, axis=0, keepdims=True), sink)
        acc = _dot(v_t(vsect(2, h), b_first, nb), jnp.exp(s - m).astype(BF16))
        o_all = acc / (acc[HEAD_DIM:HEAD_DIM + 1, :] + jnp.exp(sink - m))
        outs_b += [o_all[:, g * TQ:(g + 1) * TQ] for g in range(GROUP)]
    store_heads(ob_ref, outs_b)

    selected = [nsa_select(h, local[h][0]) for h in range(N_KV)]
    fronts = [nsa_diag(h, *selected[h]) + (local[h][1],) for h in range(N_KV)]
    nblk = KCHUNK // TQ

    def slc_step(c, carry):
        state = list(carry)
        subs = [(j, h) for j in range(KCHUNK // SUB) for h in range(N_KV)]
        scores = [_dot(ksect(0, h)[0, pl.ds(pl.multiple_of(c * KCHUNK + j * SUB, SUB), SUB), :], fronts[h][0])
                  for j, h in subs]
        for (j, h), s in zip(subs, scores):
            m_i, acc = state[h]
            m_new = jnp.maximum(m_i, jnp.max(s, axis=0, keepdims=True))
            pv = _dot(v_t(vsect(0, h), c * nblk + j * (SUB // TQ), SUB // TQ), jnp.exp(s - m_new).astype(BF16))
            state[h] = (m_new, jnp.exp(m_i - m_new) * acc + pv)
        return tuple(state)

    n_main = lax.shift_right_logical(q0 + (KCHUNK - 1), int(np.log2(KCHUNK)))
    swept = lax.fori_loop(0, n_main, slc_step, tuple((f[1], f[2]) for f in fronts))
    outs_a = []
    for h in range(N_KV):
        acc = swept[h][1]
        o_slc = acc / acc[HEAD_DIM:HEAD_DIM + 1, :]
        outs_a += [fronts[h][3][g] + gate(h * GROUP + g, 1) * o_slc[:, g * TQ:(g + 1) * TQ] for g in range(GROUP)]
    store_heads(oa_ref, outs_a)


def _attention(proj3, v_t, kc, vc_t, ovt, eye, sinks):
    bsz, seq, _ = proj3.shape
    n_cmp = seq // CMP_STRIDE - CMP_LEN // CMP_STRIDE + 1
    width = N_HEADS * HEAD_DIM
    consts = [ovt, eye]
    qspec = lambda col: pl.BlockSpec((1, TQ, width), lambda b, q, s: (b, q, col))
    cspec = lambda a, j: pl.BlockSpec((1, 1) + a.shape[2:], lambda b, q, s: (b, j, 0, 0, 0))
    in_specs = [qspec(0), qspec(COL_QB // width), cspec(kc, 0), cspec(vc_t, 1)]
    in_specs += [pl.BlockSpec((1, seq, LANES), lambda b, q, s, j=j: (b, 0, COL_KV // LANES + j))
                 for j in range(N_KV_SECT)]
    in_specs += [pl.BlockSpec((1, 1) + v_t.shape[2:], lambda b, q, s, j=j: (j, b, 0, 0, 0)) for j in range(N_KV_SECT)]
    in_specs += [pl.BlockSpec((1, TQ, LANES), lambda b, q, s: (b, q, COL_GN // LANES))]
    in_specs += [pl.BlockSpec(c.shape, lambda b, q, s: (0, 0)) for c in consts]
    ospec = pl.BlockSpec((1, TQ, width), lambda b, q, s: (b, q, 0))
    return pl.pallas_call(
        functools.partial(_attn_t_kernel, n_cmp=n_cmp),
        grid_spec=pltpu.PrefetchScalarGridSpec(
            num_scalar_prefetch=1, grid=(bsz, seq // TQ), in_specs=in_specs, out_specs=[ospec, ospec]),
        out_shape=[jax.ShapeDtypeStruct((bsz, seq, width), BF16)] * 2,
        compiler_params=pltpu.CompilerParams(dimension_semantics=("parallel", "parallel"),
                                             vmem_limit_bytes=VMEM_LIMIT),
        name="attn",
    )(sinks, proj3, proj3, kc, vc_t, *([proj3] * N_KV_SECT), *([v_t] * N_KV_SECT), proj3, *consts)


def _merge_kernel(x_ref, oa_ref, ob_ref, g1_ref, g2_ref, wg_ref, wa_ref, wb_ref, wo_ref,
                  wrh_ref, wrl_ref, br_ref, x2_ref, hn_ref, route_ref, cnt_ref):
    tm = x_ref.shape[0]

    @pl.when(pl.program_id(0) == 0)
    def _():
        cnt_ref[...] = jnp.zeros_like(cnt_ref)

    halves = [slice(i * (tm // 2), (i + 1) * (tm // 2)) for i in range(2)]
    xs = [x_ref[r, :] for r in halves]
    pre = []
    for r, x in zip(halves, xs):
        h = _rms(x, g1_ref[...]).astype(BF16)
        pre.append((_dot(h, wg_ref[...]), _dot(oa_ref[r, :], wa_ref[...]), _dot(ob_ref[r, :], wb_ref[...])))
    mixes = []
    for g_pre, a, b in pre:
        gm = jax.nn.sigmoid(g_pre)
        mixin = gm[:, :D_MODEL] * a + gm[:, D_MODEL:] * b
        mixes.append(_dot(mixin.astype(BF16), wo_ref[...]))
    logit_halves = []
    for r, x, mix in zip(halves, xs, mixes):
        x2 = x + mix
        x2_ref[r, :] = x2
        hn = _rms(x2, g2_ref[...])
        for j in range(ROW_TILE):
            hn_ref[pl.ds(r.start * ROW_TILE + j, tm // 2, stride=ROW_TILE), :] = hn[:, j * LANES:(j + 1) * LANES]
        hn_b = hn.astype(BF16)
        hn_lo = (hn - hn_b.astype(F32)).astype(BF16)
        logit_halves.append(_dot_nt(wrh_ref[...], hn_b) + _dot_nt(wrh_ref[...], hn_lo) + _dot_nt(wrl_ref[...], hn_b))
    logits = jnp.concatenate(logit_halves, axis=1) + br_ref[...]
    row = lax.broadcasted_iota(jnp.int32, (LANES, 1), 0)
    rowf = row.astype(F32)
    big = float(LANES)
    top = lambda a: jnp.max(a, axis=0, keepdims=True)
    first = lambda hit: jnp.min(jnp.where(hit, rowf, big), axis=0, keepdims=True)
    is_g = (row >= N_EXPERTS) & (row < N_EXPERTS + N_GROUPS)
    gl = jnp.where(is_g, logits, NEG_INF)
    gmax = top(gl)
    grp = first(gl == gmax) - N_EXPERTS
    p_grp = 1.0 / jnp.sum(jnp.where(is_g, jnp.exp(gl - gmax), 0.0), axis=0, keepdims=True)
    in_grp = (rowf >= grp * EPG) & (rowf < grp * EPG + EPG)
    el = jnp.where(in_grp, logits, NEG_INF)
    v0 = top(el)
    i0 = first(el == v0)
    el1 = jnp.where(rowf == i0, NEG_INF, el)
    v1 = top(el1)
    i1 = first(el1 == v1)
    e1 = jnp.exp(v1 - v0)
    w0 = p_grp / (1.0 + e1)
    w1 = p_grp * e1 / (1.0 + e1)

    oh0 = jnp.where(rowf == i0, 1.0, 0.0)
    oh1 = jnp.where(rowf == i1, 1.0, 0.0)
    oh = oh0 + oh1
    r_i = lax.broadcasted_iota(jnp.int32, (tm, tm), 0)
    c_i = lax.broadcasted_iota(jnp.int32, (tm, tm), 1)
    earlier = jnp.where(r_i < c_i, 1.0, 0.0).astype(BF16)
    before = cnt_ref[...] + _dot(oh.astype(BF16), earlier)
    rank0 = jnp.sum(oh0 * before, axis=0, keepdims=True)
    rank1 = jnp.sum(oh1 * before, axis=0, keepdims=True)
    cnt_ref[...] = cnt_ref[...] + jnp.sum(oh, axis=1, keepdims=True)
    row8 = row[:8]
    route = jnp.zeros((8, tm), F32)
    for k, v in enumerate((i0, i1, rank0, rank1, w0, w1)):
        route = jnp.where(row8 == k, v, route)
    route_ref[0] = route


def _merge(x2d, oa, ob, g1, g2, wg, wa, wb, wo, wrh, wrl, br, tm):
    n = x2d.shape[0]
    width = N_HEADS * HEAD_DIM
    row = lambda w: pl.BlockSpec((tm, w), lambda i: (i, 0))
    full = lambda a: pl.BlockSpec(a.shape, lambda i: (0, 0))
    return pl.pallas_call(
        _merge_kernel,
        grid=(n // tm,),
        in_specs=[row(D_MODEL), row(width), row(width), full(g1), full(g2), full(wg), full(wa), full(wb),
                  full(wo), full(wrh), full(wrl), full(br)],
        out_specs=[row(D_MODEL), pl.BlockSpec((tm * ROW_TILE, LANES), lambda i: (i, 0)),
                   pl.BlockSpec((1, 8, tm), lambda i: (i, 0, 0)), pl.BlockSpec((LANES, 1), lambda i: (0, 0))],
        out_shape=[jax.ShapeDtypeStruct((n, D_MODEL), F32), jax.ShapeDtypeStruct((n * ROW_TILE, LANES), F32),
                   jax.ShapeDtypeStruct((n // tm, 8, tm), F32), jax.ShapeDtypeStruct((LANES, 1), F32)],
        compiler_params=pltpu.CompilerParams(dimension_semantics=("arbitrary",),
                                             vmem_limit_bytes=VMEM_LIMIT),
        name="merge",
    )(x2d, oa, ob, g1, g2, wg, wa, wb, wo, wrh, wrl, br)


def _tile(t):
    return pl.ds(pl.multiple_of(t * ROW_TILE, ROW_TILE), ROW_TILE)


def _slot_rows(ps_ref, tab_ref, i):
    return (ps_ref[tab_ref[0, 0, i]] + tab_ref[0, 2, i], ps_ref[tab_ref[0, 1, i]] + tab_ref[0, 3, i])


def _dispatch_kernel(ps_ref, tab_ref, hn_ref, buf_in_ref, buf_ref, sem, *, td):
    del buf_in_ref

    def row_copy(i, d):
        return pltpu.make_async_copy(hn_ref.at[_tile(i)], buf_ref.at[_tile(d)], sem)

    def issue(i, c):
        d0, d1 = _slot_rows(ps_ref, tab_ref, i)
        row_copy(i, d0).start(priority=0)
        row_copy(i, d1).start(priority=1)
        return c

    lax.fori_loop(0, td, issue, 0, unroll=8)
    for _ in range(2):
        pltpu.make_async_copy(hn_ref, buf_ref.at[pl.ds(0, td * ROW_TILE)], sem).wait()


def _dispatch(pad_start, tab, hn, buf0, td):
    n = hn.shape[0] // ROW_TILE
    return pl.pallas_call(
        functools.partial(_dispatch_kernel, td=td),
        grid_spec=pltpu.PrefetchScalarGridSpec(
            num_scalar_prefetch=1, grid=(n // td,),
            in_specs=[pl.BlockSpec((1, 4, td), lambda i, ps: (i, 0, 0), memory_space=pltpu.SMEM),
                      pl.BlockSpec((td * ROW_TILE, LANES), lambda i, ps: (i, 0)),
                      pl.BlockSpec(memory_space=pl.ANY)],
            out_specs=pl.BlockSpec(memory_space=pl.ANY),
            scratch_shapes=[pltpu.SemaphoreType.DMA(())]),
        out_shape=jax.ShapeDtypeStruct(buf0.shape, buf0.dtype),
        input_output_aliases={3: 0},
        compiler_params=pltpu.CompilerParams(dimension_semantics=("arbitrary",)),
        name="dispatch",
    )(pad_start, tab, hn, buf0)


def _expert_kernel(be_ref, nu_ref, xb_ref, wg_ref, wu_ref, wd_ref, y_ref, wg_s, wu_s, wd_s):
    b = pl.program_id(0)
    live = b < nu_ref[0]

    @pl.when(live & ((b == 0) | (be_ref[b] != be_ref[jnp.maximum(b - 1, 0)])))
    def _():
        wg_s[...] = wg_ref[0].astype(BF16)
        wu_s[...] = wu_ref[0].astype(BF16)
        wd_s[...] = wd_ref[0].astype(BF16)

    @pl.when(live)
    def _():
        half = xb_ref.shape[0] // ROW_TILE // 2
        gu = []
        for i in range(2):
            xb = jnp.concatenate([xb_ref[pl.ds(i * half * ROW_TILE + j, half, stride=ROW_TILE), :]
                                  for j in range(ROW_TILE)], axis=1).astype(BF16)
            gu.append((_dot(xb, wg_s[...]), _dot(xb, wu_s[...])))
        for i, (g, u) in enumerate(gu):
            y = _dot((g * jax.nn.sigmoid(g) * u).astype(BF16), wd_s[...])
            for j in range(ROW_TILE):
                y_ref[pl.ds(i * half * ROW_TILE + j, half, stride=ROW_TILE), :] = y[:, j * LANES:(j + 1) * LANES]

    @pl.when(jnp.logical_not(live))
    def _():
        y_ref[...] = jnp.zeros_like(y_ref)


def _experts(block_e, n_used, buf, wg, wu, wd, tb):
    nblk = buf.shape[0] // (tb * ROW_TILE)
    live = lambda b, be, nu: jnp.minimum(b, nu[0] - 1)
    return pl.pallas_call(
        _expert_kernel,
        grid_spec=pltpu.PrefetchScalarGridSpec(
            num_scalar_prefetch=2, grid=(nblk,),
            in_specs=[pl.BlockSpec((tb * ROW_TILE, LANES), lambda b, be, nu: (live(b, be, nu), 0)),
                      pl.BlockSpec((1, D_MODEL, EXPERT_FF), lambda b, be, nu: (be[live(b, be, nu)], 0, 0)),
                      pl.BlockSpec((1, D_MODEL, EXPERT_FF), lambda b, be, nu: (be[live(b, be, nu)], 0, 0)),
                      pl.BlockSpec((1, EXPERT_FF, D_MODEL), lambda b, be, nu: (be[live(b, be, nu)], 0, 0))],
            out_specs=pl.BlockSpec((tb * ROW_TILE, LANES), lambda b, be, nu: (b, 0)),
            scratch_shapes=[pltpu.VMEM((D_MODEL, EXPERT_FF), BF16), pltpu.VMEM((D_MODEL, EXPERT_FF), BF16),
                            pltpu.VMEM((EXPERT_FF, D_MODEL), BF16)]),
        out_shape=jax.ShapeDtypeStruct(buf.shape, F32),
        compiler_params=pltpu.CompilerParams(dimension_semantics=("arbitrary",),
                                             vmem_limit_bytes=VMEM_LIMIT),
        name="experts",
    )(block_e, n_used, buf, wg, wu, wd)


def _combine_kernel(ps_ref, tab_ref, tab_next_ref, x2_ref, route_ref, gf_ref, yb_ref, o_ref, rows, sems, *, tc):
    step = pl.program_id(0)
    slot = lax.rem(step, 2)

    def row_copy(d, s, k, i):
        return pltpu.make_async_copy(yb_ref.at[_tile(d)], rows.at[s, k, _tile(i)], sems.at[s])

    def gather(tab, s):
        def issue(i, c):
            d0, d1 = _slot_rows(ps_ref, tab, i)
            row_copy(d0, s, 0, i).start(priority=0)
            row_copy(d1, s, 1, i).start(priority=1)
            return c
        lax.fori_loop(0, tc, issue, 0, unroll=8)

    @pl.when(step == 0)
    def _():
        gather(tab_ref, 0)

    @pl.when(step + 1 < pl.num_programs(0))
    def _():
        gather(tab_next_ref, 1 - slot)

    for k in range(2):
        pltpu.make_async_copy(yb_ref.at[pl.ds(0, tc * ROW_TILE)], rows.at[slot, k], sems.at[slot]).wait()
    w = route_ref[...]
    y0 = _tiles_to_rows(rows, tc, (slot, 0))
    y1 = _tiles_to_rows(rows, tc, (slot, 1))
    y = x2_ref[...] + (w[:, 0:1] * y0 + w[:, 1:2] * y1)
    o_ref[...] = _rms(y, gf_ref[...])


def _combine(pad_start, tab, x2, route, gf, yb, tc):
    n = x2.shape[0]
    nt = n // tc
    return pl.pallas_call(
        functools.partial(_combine_kernel, tc=tc),
        grid_spec=pltpu.PrefetchScalarGridSpec(
            num_scalar_prefetch=1, grid=(nt,),
            in_specs=[pl.BlockSpec((1, 4, tc), lambda i, ps: (i, 0, 0), memory_space=pltpu.SMEM),
                      pl.BlockSpec((1, 4, tc), lambda i, ps: (jnp.minimum(i + 1, nt - 1), 0, 0),
                                   memory_space=pltpu.SMEM),
                      pl.BlockSpec((tc, D_MODEL), lambda i, ps: (i, 0)),
                      pl.BlockSpec((tc, 2), lambda i, ps: (i, 0)),
                      pl.BlockSpec((1, D_MODEL), lambda i, ps: (0, 0)),
                      pl.BlockSpec(memory_space=pl.ANY)],
            out_specs=pl.BlockSpec((tc, D_MODEL), lambda i, ps: (i, 0)),
            scratch_shapes=[pltpu.VMEM((2, 2, tc * ROW_TILE, LANES), F32), pltpu.SemaphoreType.DMA((2,))]),
        out_shape=jax.ShapeDtypeStruct((n, D_MODEL), F32),
        compiler_params=pltpu.CompilerParams(dimension_semantics=("arbitrary",),
                                             vmem_limit_bytes=VMEM_LIMIT),
        name="combine",
    )(pad_start, tab, tab, x2, route, gf, yb)


def _overlap_matrix_t(seq):
    nc = seq // CMP_STRIDE - CMP_LEN // CMP_STRIDE + 1
    ns = seq // SLC_LEN
    c0 = np.arange(nc) * CMP_STRIDE
    s0 = np.arange(ns) * SLC_LEN
    ov = np.clip(np.minimum(c0[:, None] + CMP_LEN, s0[None, :] + SLC_LEN)
                 - np.maximum(c0[:, None], s0[None, :]), 0, None) / CMP_LEN
    out = np.zeros((LANES, LANES), np.float32)
    out[FEAT_SEL:FEAT_SEL + ns, :nc] = ov.T
    return jnp.asarray(out, BF16)


def _position_features(seq):
    pos = np.arange(seq)
    fk = np.zeros((seq, LANES), np.float32)
    fk[:, FEAT_POS:FEAT_POS + 3] = (pos // SLC_LEN)[:, None]
    fk[:, FEAT_POS + 3:FEAT_POS + 6] = (pos % SLC_LEN)[:, None]
    fk[pos, FEAT_SEL + pos // SLC_LEN] = 1.0
    fv = np.zeros((seq, LANES), np.float32)
    fv[:, HEAD_DIM] = 1.0
    return jnp.asarray(fk), jnp.asarray(fv)


def _pick_tile(n, pref):
    t = pref
    while n % t:
        t //= 2
    return t


def kernel(x, norm_mix_g, w_in, cmp_pos_k, cmp_w1_k, cmp_w2_k, cmp_pos_v, cmp_w1_v, cmp_w2_v, sinks, w_a, w_b,
           w_o, norm_ffn_g, w_group, b_group, w_expert, b_expert, w_gate_e, w_up_e, w_down_e, norm_final_g):
    bsz, seq, _ = x.shape
    n = bsz * seq
    assert TQ == LANES and seq % KCHUNK == 0 and seq // SLC_LEN <= N_SLC_BLK and seq // CMP_STRIDE <= LANES
    assert seq >= (NSA_WINDOW // TQ + 1) * TQ and w_in.shape[0] == 1
    x2d = x.reshape(n, D_MODEL)

    w = w_in[0]
    scale = HEAD_DIM ** -0.5
    nsa_w, kvw = N_HEADS * HEAD_DIM, N_KV * HEAD_DIM
    o_qa, o_kva, o_gn = 0, nsa_w, nsa_w + 6 * kvw
    o_qb = o_gn + 3 * N_HEADS
    o_kvb = o_qb + nsa_w
    o_gm = o_kvb + 2 * kvw
    zpad = jnp.zeros((D_MODEL, LANES - HEAD_DIM), F32)
    def sections(offsets):
        return [c for off in offsets for h in range(N_KV)
                for c in (w[:, off + h * HEAD_DIM:off + (h + 1) * HEAD_DIM], zpad)]

    w_attn = jnp.concatenate(
        [w[:, o_qa:o_qa + nsa_w] * scale, w[:, o_qb:o_qb + nsa_w] * scale]
        + sections((o_kva + 2 * kvw, o_kva + 4 * kvw, o_kvb))
        + [w[:, o_gn:o_gn + 3 * N_HEADS], jnp.zeros((D_MODEL, LANES - 3 * N_HEADS), F32)]
        + sections((o_kva + 3 * kvw, o_kva + 5 * kvw, o_kvb + kvw))
        + [w[:, o_kva:o_kva + 2 * kvw]], axis=1).astype(BF16)
    w_gm = w[:, o_gm:o_gm + 2 * D_MODEL].astype(BF16)

    tm = _pick_tile(seq, 512)
    feat_k, feat_v = _position_features(seq)
    eye = jnp.eye(LANES, dtype=BF16)
    proj, v_t, cmp_in = _proj(x2d, norm_mix_g[0][None], w_attn, feat_k, feat_v, eye, tm)
    proj3 = proj.reshape(bsz, seq, PROJ_W)

    nch = seq // CMP_STRIDE
    pos = jnp.stack([cmp_pos_k[0], cmp_pos_v[0]])
    pos = jnp.broadcast_to(pos[:, :, None, :], (2, CMP_LEN, N_KV, HEAD_DIM))
    pos_a = pos[:, :CMP_STRIDE].reshape(2, 1, CMP_STRIDE * kvw)
    pos_b = pos[:, CMP_STRIDE:].reshape(2, 1, CMP_STRIDE * kvw)
    w1 = jnp.stack([cmp_w1_k[0], cmp_w1_v[0]]).reshape(2, CMP_LEN, HEAD_DIM, CMP_HIDDEN)
    w1 = jnp.einsum('kjdc,hg->kjhdgc', w1, jnp.eye(N_KV, dtype=F32))
    w1 = w1.reshape(2, CMP_LEN * kvw, N_KV * CMP_HIDDEN).astype(BF16)
    w2 = jnp.pad(jnp.stack([cmp_w2_k[0], cmp_w2_v[0]]), ((0, 0), (0, 0), (0, LANES - HEAD_DIM))).astype(BF16)
    kvc, kvc_t = _compress(cmp_in.reshape(bsz, seq, CMP_W), pos_a, pos_b, w1[:, :CMP_STRIDE * kvw],
                           w1[:, CMP_STRIDE * kvw:], w2, jnp.swapaxes(w2, 1, 2))
    kvc = jnp.pad(kvc, ((0, 0), (0, 0), (0, 0), (0, LANES - nch), (0, 0)))
    kvc_t = jnp.pad(kvc_t, ((0, 0), (0, 0), (0, 0), (0, 0), (0, LANES - nch)))

    o_a, o_b = _attention(proj3, v_t, kvc, kvc_t, _overlap_matrix_t(seq), eye, sinks[0])

    w_r = jnp.concatenate([w_expert[0], w_group[0],
                           jnp.zeros((D_MODEL, LANES - N_EXPERTS - N_GROUPS), F32)], axis=1)
    w_r = w_r.T
    w_rh = w_r.astype(BF16)
    w_rl = (w_r - w_rh.astype(F32)).astype(BF16)
    b_r = jnp.concatenate([b_expert[0], b_group[0], jnp.zeros((LANES - N_EXPERTS - N_GROUPS,), F32)])[:, None]
    tt = _pick_tile(n, 256)
    x2, hn, route, counts = _merge(
        x2d, o_a.reshape(n, nsa_w), o_b.reshape(n, nsa_w), norm_mix_g[0][None], norm_ffn_g[0][None], w_gm,
        w_a[0].astype(BF16), w_b[0].astype(BF16), w_o[0].astype(BF16), w_rh, w_rl, b_r, tt)

    tb = 256
    nblk = -(-(2 * n + N_EXPERTS * (tb - 1)) // tb)
    cnt = counts[:N_EXPERTS, 0].astype(jnp.int32)
    padded = (cnt + tb - 1) // tb * tb
    pad_end = jnp.cumsum(padded)
    pad_start = pad_end - padded
    block_e = jnp.minimum(jnp.sum(pad_end[None, :] <= (jnp.arange(nblk) * tb)[:, None], axis=1), N_EXPERTS - 1)
    n_used = (pad_end[-1:] // tb).astype(jnp.int32)
    tab = route[:, 0:4, :].astype(jnp.int32)
    w_slot = jnp.swapaxes(route[:, 4:6, :], 1, 2).reshape(n, 2)

    buf = _dispatch(pad_start, tab, hn, jnp.zeros((nblk * tb * ROW_TILE, LANES), F32), tt)
    yb = _experts(block_e.astype(jnp.int32), n_used, buf, w_gate_e[0], w_up_e[0], w_down_e[0], tb)
    out = _combine(pad_start, tab, x2, w_slot, norm_final_g[None], yb, tt)
    return out.reshape(bsz, seq, D_MODEL)
```

```python
import functools

import numpy as np
import jax
import jax.numpy as jnp
from jax import lax
from jax.experimental import pallas as pl
from jax.experimental.pallas import tpu as pltpu

F32 = jnp.float32
BF16 = jnp.bfloat16

D_MODEL = 1024
HEAD_DIM = 64
N_HEADS = 8
N_KV = 2
GROUP = N_HEADS // N_KV
CMP_LEN = 32
CMP_STRIDE = 16
CMP_HIDDEN = 256
SLC_LEN = 64
SLC_TOPK = 8
NSA_WINDOW = 256
SWA_WINDOW = 128
N_GROUPS = 4
EPG = 8
N_EXPERTS = N_GROUPS * EPG
EXPERT_FF = 256
RMS_EPS = 1e-6
NEG_INF = -1e30
FORCE_SCORE = 1e9

LANES = 128
TQ = 128
KCHUNK = 512
SUB = 128
N_SLC_BLK = LANES // 4
FEAT_POS = HEAD_DIM
FEAT_SEL = HEAD_DIM + 6
ROWS_LO, ROWS_HI = 64, 104
N_KV_SECT = 6
COL_QB = 512
COL_KV = 1024
COL_GN = COL_KV + N_KV_SECT * LANES
PROJ_W = COL_GN + LANES
VT_W = N_KV_SECT * LANES
CMP_W = 2 * LANES
ROW_TILE = D_MODEL // LANES
VMEM_LIMIT = 56 * 1024 * 1024


def _alibi_slopes():
    n = 2 * N_HEADS
    s = 2.0 ** (-8.0 * np.arange(1, n + 1) / n)
    return [float(v) for v in s[:N_HEADS]], [float(v) for v in s[N_HEADS:]]


SLOPES_SWA, SLOPES_NSA = _alibi_slopes()


def _bf16_pieces(v):
    out, rem = [], np.float32(v)
    for _ in range(3):
        p = np.float32(np.asarray(rem, np.float32).astype(BF16).astype(np.float32))
        out.append(float(p))
        rem = np.float32(rem - p)
    return out


def _rms(x, g):
    return x * lax.rsqrt(jnp.mean(x * x, axis=-1, keepdims=True) + RMS_EPS) * g


def _dot(a, b):
    return jnp.dot(a, b, preferred_element_type=F32)


def _tiles_to_rows(ref, n, lead=()):
    return jnp.concatenate([ref[lead + (pl.ds(j, n, stride=ROW_TILE), slice(None))] for j in range(ROW_TILE)], axis=1)


def _rows_to_tiles(ref, val):
    n = val.shape[0]
    for j in range(ROW_TILE):
        ref[pl.ds(j, n, stride=ROW_TILE), :] = val[:, j * LANES:(j + 1) * LANES]


def _dot_nt(a, b):
    return lax.dot_general(a, b, (((1,), (1,)), ((), ())), preferred_element_type=F32)


def _proj_kernel(x_ref, g_ref, w_ref, fk_ref, fv_ref, eye_ref, o_ref, vt_ref, cmp_ref):
    h = _rms(x_ref[...], g_ref[...]).astype(BF16)
    res = _dot(h, w_ref[...])
    o_ref[:, :COL_KV] = res[:, :COL_KV].astype(o_ref.dtype)
    for j in range(N_KV_SECT):
        c0 = COL_KV + j * LANES
        o_ref[:, c0:c0 + LANES] = (res[:, c0:c0 + LANES] + fk_ref[...]).astype(o_ref.dtype)
    o_ref[:, COL_GN:] = res[:, COL_GN:PROJ_W].astype(o_ref.dtype)
    for j in range(N_KV_SECT):
        c0 = PROJ_W + j * LANES
        v_t = _dot_nt(eye_ref[...], (res[:, c0:c0 + LANES] + fv_ref[...]).astype(BF16))
        for k in range(v_t.shape[1] // LANES):
            vt_ref[j, 0, k] = v_t[:, k * LANES:(k + 1) * LANES].astype(vt_ref.dtype)
    cmp_ref[...] = res[:, PROJ_W + VT_W:]


def _proj(x2d, g, w, feat_k, feat_v, eye, tm):
    n = x2d.shape[0]
    seq = feat_k.shape[0]
    nper = seq // tm
    kb = tm // LANES
    return pl.pallas_call(
        _proj_kernel,
        grid=(n // tm,),
        in_specs=[pl.BlockSpec((tm, D_MODEL), lambda i: (i, 0)),
                  pl.BlockSpec((1, D_MODEL), lambda i: (0, 0)),
                  pl.BlockSpec((D_MODEL, PROJ_W + VT_W + CMP_W), lambda i: (0, 0)),
                  pl.BlockSpec((tm, LANES), lambda i: (i % nper, 0)),
                  pl.BlockSpec((tm, LANES), lambda i: (i % nper, 0)),
                  pl.BlockSpec((LANES, LANES), lambda i: (0, 0))],
        out_specs=[pl.BlockSpec((tm, PROJ_W), lambda i: (i, 0)),
                   pl.BlockSpec((N_KV_SECT, 1, kb, LANES, LANES), lambda i: (0, i // nper, i % nper, 0, 0)),
                   pl.BlockSpec((tm, CMP_W), lambda i: (i, 0))],
        out_shape=[jax.ShapeDtypeStruct((n, PROJ_W), BF16),
                   jax.ShapeDtypeStruct((N_KV_SECT, n // seq, seq // LANES, LANES, LANES), BF16),
                   jax.ShapeDtypeStruct((n, CMP_W), F32)],
        compiler_params=pltpu.CompilerParams(dimension_semantics=("parallel",),
                                             vmem_limit_bytes=VMEM_LIMIT),
        name="proj",
    )(x2d, g, w, feat_k, feat_v, eye)


def _compress_kernel(x_ref, pa_ref, pb_ref, w1a_ref, w1b_ref, w2_ref, w2t_ref, o_ref, ot_ref, *, nch):
    r = jnp.concatenate([x_ref[0, pl.ds(j, nch, stride=CMP_STRIDE), :] for j in range(CMP_STRIDE)], axis=1)
    a = _dot((r + pa_ref[0]).astype(BF16), w1a_ref[0])
    b = _dot((r + pb_ref[0]).astype(BF16), w1b_ref[0])
    hid = a + pltpu.roll(b, nch - 1, 0)
    hid = hid * jax.nn.sigmoid(hid)
    for h in range(N_KV):
        hid_h = hid[:, h * CMP_HIDDEN:(h + 1) * CMP_HIDDEN].astype(BF16)
        o_ref[0, 0, h] = _dot(hid_h, w2_ref[0]).astype(o_ref.dtype)
        ot_ref[0, 0, h] = _dot_nt(w2t_ref[0], hid_h).astype(ot_ref.dtype)


def _compress(cmp3, pos_a, pos_b, w1a, w1b, w2, w2t):
    bsz, seq, _ = cmp3.shape
    nch = seq // CMP_STRIDE
    wspec = lambda a: pl.BlockSpec((1,) + a.shape[1:], lambda b, j: (j, 0, 0))
    return pl.pallas_call(
        functools.partial(_compress_kernel, nch=nch),
        grid=(bsz, 2),
        in_specs=[pl.BlockSpec((1, seq, LANES), lambda b, j: (b, 0, j)),
                  wspec(pos_a), wspec(pos_b), wspec(w1a), wspec(w1b), wspec(w2), wspec(w2t)],
        out_specs=[pl.BlockSpec((1, 1, N_KV, nch, LANES), lambda b, j: (b, j, 0, 0, 0)),
                   pl.BlockSpec((1, 1, N_KV, LANES, nch), lambda b, j: (b, j, 0, 0, 0))],
        out_shape=[jax.ShapeDtypeStruct((bsz, 2, N_KV, nch, LANES), BF16),
                   jax.ShapeDtypeStruct((bsz, 2, N_KV, LANES, nch), BF16)],
        compiler_params=pltpu.CompilerParams(dimension_semantics=("parallel", "parallel"),
                                             vmem_limit_bytes=VMEM_LIMIT),
        name="compress",
    )(cmp3, pos_a, pos_b, w1a, w1b, w2, w2t)


def _attn_t_kernel(sinks_ref, qa_ref, qb_ref, kc_ref, vct_ref, *rest, n_cmp):
    n_br = 3
    ks = rest[:n_br * N_KV]
    vts = rest[n_br * N_KV:2 * n_br * N_KV]
    gn_ref, ovt_ref, eye_ref, oa_ref, ob_ref = rest[2 * n_br * N_KV:]
    ksect = lambda branch, h: ks[branch * N_KV + h]
    vsect = lambda branch, h: vts[branch * N_KV + h]
    qi = pl.program_id(1)
    q0 = pl.multiple_of(qi * TQ, TQ)
    lane = lax.broadcasted_iota(jnp.int32, (1, TQ), 1)
    sub = lax.broadcasted_iota(jnp.int32, (LANES, 1), 0)
    t_row = q0 + lane
    eye = eye_ref[...]
    gates = jax.nn.sigmoid(_dot_nt(eye, gn_ref[0]))
    gate = lambda hh, c: gates[3 * hh + c:3 * hh + c + 1, :]
    sub40 = sub[ROWS_LO:ROWS_HI]
    blk = sub40 - FEAT_SEL
    in_rng = (blk >= 0) & (blk < N_SLC_BLK)
    is_pos = (sub40 >= FEAT_POS) & (sub40 < FEAT_SEL)
    zeros_lo = jnp.zeros((LANES - ROWS_HI, TQ), F32)

    def q_t(ref, hh):
        both = _dot_nt(eye, ref[0, :, (hh // 2) * LANES:(hh // 2 + 1) * LANES])
        return both[(hh % 2) * HEAD_DIM:(hh % 2 + 1) * HEAD_DIM]

    def slope_col(slope):
        hi, mid, lo = _bf16_pieces(slope)
        col = jnp.zeros(sub40.shape, F32)
        for i, v in enumerate([SLC_LEN * hi, SLC_LEN * mid, SLC_LEN * lo, hi, mid, lo]):
            col = jnp.where(sub40 == FEAT_POS + i, v, col)
        return col

    def q_aug_t(q_ts, tails):
        return jnp.concatenate([jnp.concatenate([q, jnp.broadcast_to(t, (ROWS_HI - ROWS_LO, TQ)), zeros_lo], axis=0)
                                for q, t in zip(q_ts, tails)], axis=1).astype(BF16)

    def v_t(ref, first, n):
        return jnp.concatenate([ref[0, 0, first + j] for j in range(n)], axis=1)

    def masked(s, mask):
        return jnp.concatenate([jnp.where(mask, s[:, g * TQ:(g + 1) * TQ], NEG_INF) for g in range(GROUP)], axis=1)

    def store_heads(ref, outs):
        for p in range(N_HEADS // 2):
            pair = jnp.concatenate([outs[2 * p][:HEAD_DIM], outs[2 * p + 1][:HEAD_DIM]], axis=0).astype(BF16)
            ref[0, :, p * LANES:(p + 1) * LANES] = _dot_nt(eye, pair).astype(ref.dtype)

    nw = NSA_WINDOW // TQ + 1
    w_first = jnp.maximum(qi - (nw - 1), 0)
    w_start = pl.multiple_of(w_first * TQ, TQ)
    nb = (SWA_WINDOW - 1 + TQ - 1) // TQ + 1
    b_first = jnp.maximum(qi - (nb - 1), 0)
    b_start = pl.multiple_of(b_first * TQ, TQ)
    pre = []
    for h in range(N_KV):
        heads = [h * GROUP + g for g in range(GROUP)]
        q_ts = [q_t(qa_ref, hh) for hh in heads]
        scol = [slope_col(SLOPES_NSA[hh]) for hh in heads]
        s_cmp = _dot(kc_ref[0, 0, h], q_aug_t(q_ts, [jnp.zeros((1, 1), F32)] * GROUP))
        s_win = _dot(ksect(1, h)[0, pl.ds(w_start, nw * TQ), :], q_aug_t(q_ts, scol))
        q_b = q_aug_t([q_t(qb_ref, hh) for hh in heads], [slope_col(SLOPES_SWA[hh]) for hh in heads])
        s_swa = _dot(ksect(2, h)[0, pl.ds(b_start, nb * TQ), :], q_b)
        pre.append((q_ts, scol, s_cmp, s_win, s_swa))

    def nsa_local(h):
        heads = [h * GROUP + g for g in range(GROUP)]
        _, _, s_all, s_win, _ = pre[h]

        end_c = sub * CMP_STRIDE + (CMP_LEN - 1)
        cmask = (t_row >= end_c) & (sub < n_cmp)
        ps = []
        for g in range(GROUP):
            s = s_all[:, g * TQ:(g + 1) * TQ] + SLOPES_NSA[heads[g]] * end_c.astype(F32)
            s = jnp.where(cmask, s, NEG_INF)
            m = jnp.max(s, axis=0, keepdims=True)
            e = jnp.where(cmask, jnp.exp(s - m), 0.0)
            z = jnp.sum(e, axis=0, keepdims=True)
            ps.append(e / jnp.where(z > 0, z, 1.0))
        o_cmp = _dot(vct_ref[0, 0, h], jnp.concatenate(ps, axis=1).astype(BF16))

        dist = t_row - (w_start + lax.broadcasted_iota(jnp.int32, (nw * TQ, 1), 0))
        s = masked(s_win, (dist >= 0) & (dist < NSA_WINDOW))
        m = jnp.max(s, axis=0, keepdims=True)
        acc = _dot(v_t(vsect(1, h), w_first, nw), jnp.exp(s - m).astype(BF16))
        o_win = acc / acc[HEAD_DIM:HEAD_DIM + 1, :]

        psum = ps[0] + ps[1] + ps[2] + ps[3]
        p_hi = psum.astype(BF16)
        p_lo = (psum - p_hi.astype(F32)).astype(BF16)
        imp = (_dot(ovt_ref[...], p_hi) + _dot(ovt_ref[...], p_lo))[ROWS_LO:ROWS_HI]
        part = [gate(hh, 0) * o_cmp[:, g * TQ:(g + 1) * TQ] + gate(hh, 2) * o_win[:, g * TQ:(g + 1) * TQ]
                for g, hh in enumerate(heads)]
        return imp, part

    def nsa_select(h, imp):
        q_ts, scol = pre[h][:2]
        cur = lax.shift_right_logical(t_row, int(np.log2(SLC_LEN)))
        valid = in_rng & (blk * SLC_LEN <= t_row)
        forced = in_rng & ((blk == 0) | (blk == cur) | (blk == cur - 1))
        score = jnp.where(forced, FORCE_SCORE, jnp.where(valid, imp, NEG_INF))
        rank = jnp.zeros(score.shape, F32)
        for i in range(N_SLC_BLK):
            r = FEAT_SEL - ROWS_LO + i
            si = score[r:r + 1, :]
            rank = rank + jnp.where((si > score) | ((si == score) & (blk > i)), 1.0, 0.0)
        sel = in_rng & (rank < SLC_TOPK) & (score > 0.5 * NEG_INF)
        bias_diag = jnp.where(in_rng & jnp.logical_not(sel), NEG_INF, 0.0)
        bias_main = jnp.where(in_rng & jnp.logical_not(sel & (blk < 2 * qi)), NEG_INF, 0.0)

        q_d = q_aug_t(q_ts, [jnp.where(is_pos, scol[g], bias_diag) for g in range(GROUP)])
        q_m = q_aug_t(q_ts, [jnp.where(is_pos, scol[g], bias_main) for g in range(GROUP)])
        return q_m, _dot(ksect(0, h)[0, pl.ds(q0, TQ), :], q_d)

    def nsa_diag(h, q_m, s_diag):
        s = masked(s_diag, q0 + sub <= t_row)
        m0 = jnp.max(s, axis=0, keepdims=True)
        acc0 = _dot(v_t(vsect(0, h), qi, 1), jnp.exp(s - m0).astype(BF16))
        return q_m, m0, acc0

    local = [nsa_local(h) for h in range(N_KV)]

    dist = t_row - (b_start + lax.broadcasted_iota(jnp.int32, (nb * TQ, 1), 0))
    bmask = (dist >= 0) & (dist < SWA_WINDOW)
    outs_b = []
    for h in range(N_KV):
        heads = [h * GROUP + g for g in range(GROUP)]
        s = masked(pre[h][4], bmask)
        sink = jnp.concatenate([sinks_ref[hh] + SLOPES_SWA[hh] * t_row.astype(F32) for hh in heads], axis=1)
        m = jnp.maximum(jnp.max(s, axis=0, keepdims=True), sink)
        acc = _dot(v_t(vsect(2, h), b_first, nb), jnp.exp(s - m).astype(BF16))
        o_all = acc / (acc[HEAD_DIM:HEAD_DIM + 1, :] + jnp.exp(sink - m))
        outs_b += [o_all[:, g * TQ:(g + 1) * TQ] for g in range(GROUP)]
    store_heads(ob_ref, outs_b)

    selected = [nsa_select(h, local[h][0]) for h in range(N_KV)]
    fronts = [nsa_diag(h, *selected[h]) + (local[h][1],) for h in range(N_KV)]
    nblk = KCHUNK // TQ

    def slc_step(c, carry):
        state = list(carry)
        subs = [(j, h) for j in range(KCHUNK // SUB) for h in range(N_KV)]
        scores = [_dot(ksect(0, h)[0, pl.ds(pl.multiple_of(c * KCHUNK + j * SUB, SUB), SUB), :], fronts[h][0])
                  for j, h in subs]
        for (j, h), s in zip(subs, scores):
            m_i, acc = state[h]
            m_new = jnp.maximum(m_i, jnp.max(s, axis=0, keepdims=True))
            pv = _dot(v_t(vsect(0, h), c * nblk + j * (SUB // TQ), SUB // TQ), jnp.exp(s - m_new).astype(BF16))
            state[h] = (m_new, jnp.exp(m_i - m_new) * acc + pv)
        return tuple(state)

    n_main = lax.shift_right_logical(q0 + (KCHUNK - 1), int(np.log2(KCHUNK)))
    swept = lax.fori_loop(0, n_main, slc_step, tuple((f[1], f[2]) for f in fronts))
    outs_a = []
    for h in range(N_KV):
        acc = swept[h][1]
        o_slc = acc / acc[HEAD_DIM:HEAD_DIM + 1, :]
        outs_a += [fronts[h][3][g] + gate(h * GROUP + g, 1) * o_slc[:, g * TQ:(g + 1) * TQ] for g in range(GROUP)]
    store_heads(oa_ref, outs_a)


def _attention(proj3, v_t, kc, vc_t, ovt, eye, sinks):
    bsz, seq, _ = proj3.shape
    n_cmp = seq // CMP_STRIDE - CMP_LEN // CMP_STRIDE + 1
    width = N_HEADS * HEAD_DIM
    consts = [ovt, eye]
    qspec = lambda col: pl.BlockSpec((1, TQ, width), lambda b, q, s: (b, q, col))
    cspec = lambda a, j: pl.BlockSpec((1, 1) + a.shape[2:], lambda b, q, s: (b, j, 0, 0, 0))
    in_specs = [qspec(0), qspec(COL_QB // width), cspec(kc, 0), cspec(vc_t, 1)]
    in_specs += [pl.BlockSpec((1, seq, LANES), lambda b, q, s, j=j: (b, 0, COL_KV // LANES + j))
                 for j in range(N_KV_SECT)]
    in_specs += [pl.BlockSpec((1, 1) + v_t.shape[2:], lambda b, q, s, j=j: (j, b, 0, 0, 0)) for j in range(N_KV_SECT)]
    in_specs += [pl.BlockSpec((1, TQ, LANES), lambda b, q, s: (b, q, COL_GN // LANES))]
    in_specs += [pl.BlockSpec(c.shape, lambda b, q, s: (0, 0)) for c in consts]
    ospec = pl.BlockSpec((1, TQ, width), lambda b, q, s: (b, q, 0))
    return pl.pallas_call(
        functools.partial(_attn_t_kernel, n_cmp=n_cmp),
        grid_spec=pltpu.PrefetchScalarGridSpec(
            num_scalar_prefetch=1, grid=(bsz, seq // TQ), in_specs=in_specs, out_specs=[ospec, ospec]),
        out_shape=[jax.ShapeDtypeStruct((bsz, seq, width), BF16)] * 2,
        compiler_params=pltpu.CompilerParams(dimension_semantics=("parallel", "parallel"),
                                             vmem_limit_bytes=VMEM_LIMIT),
        name="attn",
    )(sinks, proj3, proj3, kc, vc_t, *([proj3] * N_KV_SECT), *([v_t] * N_KV_SECT), proj3, *consts)


def _merge_kernel(x_ref, oa_ref, ob_ref, g1_ref, g2_ref, wg_ref, wa_ref, wb_ref, wo_ref,
                  wrh_ref, wrl_ref, br_ref, x2_ref, hn_ref, route_ref, cnt_ref):
    tm = x_ref.shape[0]

    @pl.when(pl.program_id(0) == 0)
    def _():
        cnt_ref[...] = jnp.zeros_like(cnt_ref)

    halves = [slice(i * (tm // 2), (i + 1) * (tm // 2)) for i in range(2)]
    xs = [x_ref[r, :] for r in halves]
    pre = []
    for r, x in zip(halves, xs):
        h = _rms(x, g1_ref[...]).astype(BF16)
        pre.append((_dot(h, wg_ref[...]), _dot(oa_ref[r, :], wa_ref[...]), _dot(ob_ref[r, :], wb_ref[...])))
    mixes = []
    for g_pre, a, b in pre:
        gm = jax.nn.sigmoid(g_pre)
        mixin = gm[:, :D_MODEL] * a + gm[:, D_MODEL:] * b
        mixes.append(_dot(mixin.astype(BF16), wo_ref[...]))
    logit_halves = []
    for r, x, mix in zip(halves, xs, mixes):
        x2 = x + mix
        x2_ref[r, :] = x2
        hn = _rms(x2, g2_ref[...])
        for j in range(ROW_TILE):
            hn_ref[pl.ds(r.start * ROW_TILE + j, tm // 2, stride=ROW_TILE), :] = hn[:, j * LANES:(j + 1) * LANES]
        hn_b = hn.astype(BF16)
        hn_lo = (hn - hn_b.astype(F32)).astype(BF16)
        logit_halves.append(_dot_nt(wrh_ref[...], hn_b) + _dot_nt(wrh_ref[...], hn_lo) + _dot_nt(wrl_ref[...], hn_b))
    logits = jnp.concatenate(logit_halves, axis=1) + br_ref[...]
    row = lax.broadcasted_iota(jnp.int32, (LANES, 1), 0)
    rowf = row.astype(F32)
    big = float(LANES)
    top = lambda a: jnp.max(a, axis=0, keepdims=True)
    first = lambda hit: jnp.min(jnp.where(hit, rowf, big), axis=0, keepdims=True)
    is_g = (row >= N_EXPERTS) & (row < N_EXPERTS + N_GROUPS)
    gl = jnp.where(is_g, logits, NEG_INF)
    gmax = top(gl)
    grp = first(gl == gmax) - N_EXPERTS
    p_grp = 1.0 / jnp.sum(jnp.where(is_g, jnp.exp(gl - gmax), 0.0), axis=0, keepdims=True)
    in_grp = (rowf >= grp * EPG) & (rowf < grp * EPG + EPG)
    el = jnp.where(in_grp, logits, NEG_INF)
    v0 = top(el)
    i0 = first(el == v0)
    el1 = jnp.where(rowf == i0, NEG_INF, el)
    v1 = top(el1)
    i1 = first(el1 == v1)
    e1 = jnp.exp(v1 - v0)
    w0 = p_grp / (1.0 + e1)
    w1 = p_grp * e1 / (1.0 + e1)

    oh0 = jnp.where(rowf == i0, 1.0, 0.0)
    oh1 = jnp.where(rowf == i1, 1.0, 0.0)
    oh = oh0 + oh1
    r_i = lax.broadcasted_iota(jnp.int32, (tm, tm), 0)
    c_i = lax.broadcasted_iota(jnp.int32, (tm, tm), 1)
    earlier = jnp.where(r_i < c_i, 1.0, 0.0).astype(BF16)
    before = cnt_ref[...] + _dot(oh.astype(BF16), earlier)
    rank0 = jnp.sum(oh0 * before, axis=0, keepdims=True)
    rank1 = jnp.sum(oh1 * before, axis=0, keepdims=True)
    cnt_ref[...] = cnt_ref[...] + jnp.sum(oh, axis=1, keepdims=True)
    row8 = row[:8]
    route = jnp.zeros((8, tm), F32)
    for k, v in enumerate((i0, i1, rank0, rank1, w0, w1)):
        route = jnp.where(row8 == k, v, route)
    route_ref[0] = route


def _merge(x2d, oa, ob, g1, g2, wg, wa, wb, wo, wrh, wrl, br, tm):
    n = x2d.shape[0]
    width = N_HEADS * HEAD_DIM
    row = lambda w: pl.BlockSpec((tm, w), lambda i: (i, 0))
    full = lambda a: pl.BlockSpec(a.shape, lambda i: (0, 0))
    return pl.pallas_call(
        _merge_kernel,
        grid=(n // tm,),
        in_specs=[row(D_MODEL), row(width), row(width), full(g1), full(g2), full(wg), full(wa), full(wb),
                  full(wo), full(wrh), full(wrl), full(br)],
        out_specs=[row(D_MODEL), pl.BlockSpec((tm * ROW_TILE, LANES), lambda i: (i, 0)),
                   pl.BlockSpec((1, 8, tm), lambda i: (i, 0, 0)), pl.BlockSpec((LANES, 1), lambda i: (0, 0))],
        out_shape=[jax.ShapeDtypeStruct((n, D_MODEL), F32), jax.ShapeDtypeStruct((n * ROW_TILE, LANES), F32),
                   jax.ShapeDtypeStruct((n // tm, 8, tm), F32), jax.ShapeDtypeStruct((LANES, 1), F32)],
        compiler_params=pltpu.CompilerParams(dimension_semantics=("arbitrary",),
                                             vmem_limit_bytes=VMEM_LIMIT),
        name="merge",
    )(x2d, oa, ob, g1, g2, wg, wa, wb, wo, wrh, wrl, br)


def _slot_table_kernel(pos_ref, init_ref, tab_ref, sem, *, tt, n):
    step = pl.program_id(0)

    @pl.when(step == 0)
    def _():
        cp = pltpu.make_async_copy(init_ref, tab_ref, sem)
        cp.start()
        cp.wait()

    def body(i, c):
        t = step * tt + i
        tab_ref[pos_ref[0, 0, i]] = t
        tab_ref[pos_ref[0, 1, i]] = n + t
        return c

    lax.fori_loop(0, tt, body, 0, unroll=8)


def _slot_table(pos, init, tt):
    nt = pos.shape[0]
    return pl.pallas_call(
        functools.partial(_slot_table_kernel, tt=tt, n=nt * tt),
        grid=(nt,),
        in_specs=[pl.BlockSpec((1, 2, tt), lambda i: (i, 0, 0), memory_space=pltpu.SMEM),
                  pl.BlockSpec(memory_space=pl.ANY)],
        out_specs=pl.BlockSpec(memory_space=pltpu.SMEM),
        out_shape=jax.ShapeDtypeStruct(init.shape, jnp.int32),
        scratch_shapes=[pltpu.SemaphoreType.DMA(())],
        compiler_params=pltpu.CompilerParams(dimension_semantics=("arbitrary",)),
        name="slot_table",
    )(pos, init)


def _moe_kernel(be_ref, nu_ref, tab_ref, tab_next_ref, hn_ref, wg_ref, wu_ref, wd_ref, out_ref,
                xin, yout, wg_s, wu_s, wd_s, gsem, ssem, *, tb, n):
    b = pl.program_id(0)
    n_used = nu_ref[0]
    live = b < n_used
    s = lax.rem(b, 2)

    def gather(tab, slot):
        def issue(j, c):
            src = lax.rem(tab[0, 0, j], n) * ROW_TILE
            pltpu.make_async_copy(hn_ref.at[_tile_at(src)], xin.at[slot, _tile_at(j * ROW_TILE)],
                                  gsem.at[slot]).start(priority=0)
            return c
        lax.fori_loop(0, tb, issue, 0, unroll=8)

    def wait_rows(sem_ref, slot):
        pltpu.make_async_copy(hn_ref.at[pl.ds(0, tb * ROW_TILE)], xin.at[slot], sem_ref.at[slot]).wait()

    @pl.when(b == 0)
    def _():
        yout[...] = jnp.zeros_like(yout)
        fills = [pltpu.make_async_copy(yout.at[k], out_ref.at[pl.ds((2 * n + k * tb) * ROW_TILE, tb * ROW_TILE)],
                                       ssem.at[k]) for k in range(2)]
        for cp in fills:
            cp.start()
        for cp in fills:
            cp.wait()
        gather(tab_ref, 0)

    @pl.when(b + 1 < n_used)
    def _():
        gather(tab_next_ref, 1 - s)

    @pl.when(live & ((b == 0) | (be_ref[b] != be_ref[jnp.maximum(b - 1, 0)])))
    def _():
        wg_s[...] = wg_ref[0].astype(BF16)
        wu_s[...] = wu_ref[0].astype(BF16)
        wd_s[...] = wd_ref[0].astype(BF16)

    @pl.when(live & (b >= 2))
    def _():
        wait_rows(ssem, s)

    @pl.when(live)
    def _():
        wait_rows(gsem, s)
        half = tb // 2
        gu = []
        for i in range(2):
            xb = jnp.concatenate([xin[s, pl.ds(i * half * ROW_TILE + j, half, stride=ROW_TILE), :]
                                  for j in range(ROW_TILE)], axis=1).astype(BF16)
            gu.append((_dot(xb, wg_s[...]), _dot(xb, wu_s[...])))
        for i, (g, u) in enumerate(gu):
            y = _dot((g * jax.nn.sigmoid(g) * u).astype(BF16), wd_s[...])
            for j in range(ROW_TILE):
                yout[s, pl.ds(i * half * ROW_TILE + j, half, stride=ROW_TILE), :] = y[:, j * LANES:(j + 1) * LANES]

        def issue(j, c):
            dst = tab_ref[0, 0, j]
            dst = jnp.where(dst >= 2 * n, 2 * n + s * tb + j, dst)
            pltpu.make_async_copy(yout.at[s, _tile_at(j * ROW_TILE)], out_ref.at[_tile_at(dst * ROW_TILE)],
                                  ssem.at[s]).start(priority=1)
            return c
        lax.fori_loop(0, tb, issue, 0, unroll=8)

    @pl.when(live & (b == n_used - 1))
    def _():
        wait_rows(ssem, s)

        @pl.when(b >= 1)
        def _():
            wait_rows(ssem, 1 - s)


def _moe(block_e, n_used, tab, hn, wg, wu, wd, tb, n):
    nblk = tab.shape[0] // tb
    tab2 = tab.reshape(nblk, 1, tb)
    live = lambda b, be, nu: jnp.minimum(b, nu[0] - 1)
    nxt = lambda b, be, nu: jnp.minimum(b + 1, nu[0] - 1)
    wspec = lambda shape: pl.BlockSpec((1,) + shape, lambda b, be, nu: (be[live(b, be, nu)], 0, 0))
    return pl.pallas_call(
        functools.partial(_moe_kernel, tb=tb, n=n),
        grid_spec=pltpu.PrefetchScalarGridSpec(
            num_scalar_prefetch=2, grid=(nblk,),
            in_specs=[pl.BlockSpec((1, 1, tb), lambda b, be, nu: (live(b, be, nu), 0, 0), memory_space=pltpu.SMEM),
                      pl.BlockSpec((1, 1, tb), lambda b, be, nu: (nxt(b, be, nu), 0, 0), memory_space=pltpu.SMEM),
                      pl.BlockSpec(memory_space=pl.ANY),
                      wspec((D_MODEL, EXPERT_FF)), wspec((D_MODEL, EXPERT_FF)), wspec((EXPERT_FF, D_MODEL))],
            out_specs=pl.BlockSpec(memory_space=pl.ANY),
            scratch_shapes=[pltpu.VMEM((2, tb * ROW_TILE, LANES), F32), pltpu.VMEM((2, tb * ROW_TILE, LANES), F32),
                            pltpu.VMEM((D_MODEL, EXPERT_FF), BF16), pltpu.VMEM((D_MODEL, EXPERT_FF), BF16),
                            pltpu.VMEM((EXPERT_FF, D_MODEL), BF16),
                            pltpu.SemaphoreType.DMA((2,)), pltpu.SemaphoreType.DMA((2,))]),
        out_shape=jax.ShapeDtypeStruct(((2 * n + 2 * tb) * ROW_TILE, LANES), F32),
        compiler_params=pltpu.CompilerParams(dimension_semantics=("arbitrary",),
                                             vmem_limit_bytes=VMEM_LIMIT),
        name="moe",
    )(block_e, n_used, tab2, tab2, hn, wg, wu, wd)


def _final_kernel(x2_ref, w_ref, y0_ref, y1_ref, gf_ref, o_ref):
    tc = x2_ref.shape[0]
    w = w_ref[...]
    y = x2_ref[...] + (w[:, 0:1] * _tiles_to_rows(y0_ref, tc) + w[:, 1:2] * _tiles_to_rows(y1_ref, tc))
    o_ref[...] = _rms(y, gf_ref[...])


def _final(x2, w_slot, yslots, gf, tc):
    n = x2.shape[0]
    nt = n // tc
    return pl.pallas_call(
        _final_kernel,
        grid=(nt,),
        in_specs=[pl.BlockSpec((tc, D_MODEL), lambda i: (i, 0)),
                  pl.BlockSpec((tc, 2), lambda i: (i, 0)),
                  pl.BlockSpec((tc * ROW_TILE, LANES), lambda i: (i, 0)),
                  pl.BlockSpec((tc * ROW_TILE, LANES), lambda i: (nt + i, 0)),
                  pl.BlockSpec((1, D_MODEL), lambda i: (0, 0))],
        out_specs=pl.BlockSpec((tc, D_MODEL), lambda i: (i, 0)),
        out_shape=jax.ShapeDtypeStruct((n, D_MODEL), F32),
        compiler_params=pltpu.CompilerParams(dimension_semantics=("parallel",),
                                             vmem_limit_bytes=VMEM_LIMIT),
        name="final",
    )(x2, w_slot, yslots, yslots, gf)


def _tile_at(row):
    return pl.ds(pl.multiple_of(row, ROW_TILE), ROW_TILE)


def _overlap_matrix_t(seq):
    nc = seq // CMP_STRIDE - CMP_LEN // CMP_STRIDE + 1
    ns = seq // SLC_LEN
    c0 = np.arange(nc) * CMP_STRIDE
    s0 = np.arange(ns) * SLC_LEN
    ov = np.clip(np.minimum(c0[:, None] + CMP_LEN, s0[None, :] + SLC_LEN)
                 - np.maximum(c0[:, None], s0[None, :]), 0, None) / CMP_LEN
    out = np.zeros((LANES, LANES), np.float32)
    out[FEAT_SEL:FEAT_SEL + ns, :nc] = ov.T
    return jnp.asarray(out, BF16)


def _position_features(seq):
    pos = np.arange(seq)
    fk = np.zeros((seq, LANES), np.float32)
    fk[:, FEAT_POS:FEAT_POS + 3] = (pos // SLC_LEN)[:, None]
    fk[:, FEAT_POS + 3:FEAT_POS + 6] = (pos % SLC_LEN)[:, None]
    fk[pos, FEAT_SEL + pos // SLC_LEN] = 1.0
    fv = np.zeros((seq, LANES), np.float32)
    fv[:, HEAD_DIM] = 1.0
    return jnp.asarray(fk), jnp.asarray(fv)


def _pick_tile(n, pref):
    t = pref
    while n % t:
        t //= 2
    return t


def kernel(x, norm_mix_g, w_in, cmp_pos_k, cmp_w1_k, cmp_w2_k, cmp_pos_v, cmp_w1_v, cmp_w2_v, sinks, w_a, w_b,
           w_o, norm_ffn_g, w_group, b_group, w_expert, b_expert, w_gate_e, w_up_e, w_down_e, norm_final_g):
    bsz, seq, _ = x.shape
    n = bsz * seq
    assert TQ == LANES and seq % KCHUNK == 0 and seq // SLC_LEN <= N_SLC_BLK and seq // CMP_STRIDE <= LANES
    assert seq >= (NSA_WINDOW // TQ + 1) * TQ and w_in.shape[0] == 1
    x2d = x.reshape(n, D_MODEL)

    w = w_in[0]
    scale = HEAD_DIM ** -0.5
    nsa_w, kvw = N_HEADS * HEAD_DIM, N_KV * HEAD_DIM
    o_qa, o_kva, o_gn = 0, nsa_w, nsa_w + 6 * kvw
    o_qb = o_gn + 3 * N_HEADS
    o_kvb = o_qb + nsa_w
    o_gm = o_kvb + 2 * kvw
    zpad = jnp.zeros((D_MODEL, LANES - HEAD_DIM), F32)
    def sections(offsets):
        return [c for off in offsets for h in range(N_KV)
                for c in (w[:, off + h * HEAD_DIM:off + (h + 1) * HEAD_DIM], zpad)]

    w_attn = jnp.concatenate(
        [w[:, o_qa:o_qa + nsa_w] * scale, w[:, o_qb:o_qb + nsa_w] * scale]
        + sections((o_kva + 2 * kvw, o_kva + 4 * kvw, o_kvb))
        + [w[:, o_gn:o_gn + 3 * N_HEADS], jnp.zeros((D_MODEL, LANES - 3 * N_HEADS), F32)]
        + sections((o_kva + 3 * kvw, o_kva + 5 * kvw, o_kvb + kvw))
        + [w[:, o_kva:o_kva + 2 * kvw]], axis=1).astype(BF16)
    w_gm = w[:, o_gm:o_gm + 2 * D_MODEL].astype(BF16)

    tm = _pick_tile(seq, 512)
    feat_k, feat_v = _position_features(seq)
    eye = jnp.eye(LANES, dtype=BF16)
    proj, v_t, cmp_in = _proj(x2d, norm_mix_g[0][None], w_attn, feat_k, feat_v, eye, tm)
    proj3 = proj.reshape(bsz, seq, PROJ_W)

    nch = seq // CMP_STRIDE
    pos = jnp.stack([cmp_pos_k[0], cmp_pos_v[0]])
    pos = jnp.broadcast_to(pos[:, :, None, :], (2, CMP_LEN, N_KV, HEAD_DIM))
    pos_a = pos[:, :CMP_STRIDE].reshape(2, 1, CMP_STRIDE * kvw)
    pos_b = pos[:, CMP_STRIDE:].reshape(2, 1, CMP_STRIDE * kvw)
    w1 = jnp.stack([cmp_w1_k[0], cmp_w1_v[0]]).reshape(2, CMP_LEN, HEAD_DIM, CMP_HIDDEN)
    w1 = jnp.einsum('kjdc,hg->kjhdgc', w1, jnp.eye(N_KV, dtype=F32))
    w1 = w1.reshape(2, CMP_LEN * kvw, N_KV * CMP_HIDDEN).astype(BF16)
    w2 = jnp.pad(jnp.stack([cmp_w2_k[0], cmp_w2_v[0]]), ((0, 0), (0, 0), (0, LANES - HEAD_DIM))).astype(BF16)
    kvc, kvc_t = _compress(cmp_in.reshape(bsz, seq, CMP_W), pos_a, pos_b, w1[:, :CMP_STRIDE * kvw],
                           w1[:, CMP_STRIDE * kvw:], w2, jnp.swapaxes(w2, 1, 2))
    kvc = jnp.pad(kvc, ((0, 0), (0, 0), (0, 0), (0, LANES - nch), (0, 0)))
    kvc_t = jnp.pad(kvc_t, ((0, 0), (0, 0), (0, 0), (0, 0), (0, LANES - nch)))

    o_a, o_b = _attention(proj3, v_t, kvc, kvc_t, _overlap_matrix_t(seq), eye, sinks[0])

    w_r = jnp.concatenate([w_expert[0], w_group[0],
                           jnp.zeros((D_MODEL, LANES - N_EXPERTS - N_GROUPS), F32)], axis=1)
    w_r = w_r.T
    w_rh = w_r.astype(BF16)
    w_rl = (w_r - w_rh.astype(F32)).astype(BF16)
    b_r = jnp.concatenate([b_expert[0], b_group[0], jnp.zeros((LANES - N_EXPERTS - N_GROUPS,), F32)])[:, None]
    tt = _pick_tile(n, 256)
    x2, hn, route, counts = _merge(
        x2d, o_a.reshape(n, nsa_w), o_b.reshape(n, nsa_w), norm_mix_g[0][None], norm_ffn_g[0][None], w_gm,
        w_a[0].astype(BF16), w_b[0].astype(BF16), w_o[0].astype(BF16), w_rh, w_rl, b_r, tt)

    tb = 256
    nblk = -(-(2 * n + N_EXPERTS * (tb - 1)) // tb)
    cnt = counts[:N_EXPERTS, 0].astype(jnp.int32)
    padded = (cnt + tb - 1) // tb * tb
    pad_end = jnp.cumsum(padded)
    pad_start = pad_end - padded
    block_e = jnp.minimum(jnp.sum(pad_end[None, :] <= (jnp.arange(nblk) * tb)[:, None], axis=1), N_EXPERTS - 1)
    n_used = (pad_end[-1:] // tb).astype(jnp.int32)
    tab = route[:, 0:4, :].astype(jnp.int32)
    pos = pad_start[tab[:, 0:2, :]] + tab[:, 2:4, :]
    w_slot = jnp.swapaxes(route[:, 4:6, :], 1, 2).reshape(n, 2)
    slot_tab = _slot_table(pos, 2 * n + jnp.arange(nblk * tb, dtype=jnp.int32), tt)
    y_slots = _moe(block_e.astype(jnp.int32), n_used, slot_tab, hn, w_gate_e[0], w_up_e[0], w_down_e[0], tb, n)
    out = _final(x2, w_slot, y_slots, norm_final_g[None], tt)
    return out.reshape(bsz, seq, D_MODEL)
```

```python
import functools

import numpy as np
import jax
import jax.numpy as jnp
from jax import lax
from jax.experimental import pallas as pl
from jax.experimental.pallas import tpu as pltpu

F32 = jnp.float32
BF16 = jnp.bfloat16

D_MODEL = 1024
HEAD_DIM = 64
N_HEADS = 8
N_KV = 2
GROUP = N_HEADS // N_KV
CMP_LEN = 32
CMP_STRIDE = 16
CMP_HIDDEN = 256
SLC_LEN = 64
SLC_TOPK = 8
NSA_WINDOW = 256
SWA_WINDOW = 128
N_GROUPS = 4
EPG = 8
N_EXPERTS = N_GROUPS * EPG
EXPERT_FF = 256
RMS_EPS = 1e-6
NEG_INF = -1e30
FORCE_SCORE = 1e9

LANES = 128
TQ = 128
KCHUNK = 512
SUB = 128
N_SLC_BLK = LANES // 4
FEAT_POS = HEAD_DIM
FEAT_SEL = HEAD_DIM + 6
ROWS_LO, ROWS_HI = 64, 104
N_KV_SECT = 6
COL_QB = 512
COL_KV = 1024
COL_GN = COL_KV + N_KV_SECT * LANES
PROJ_W = COL_GN + LANES
VT_W = N_KV_SECT * LANES
CMP_W = 2 * LANES
ROW_TILE = D_MODEL // LANES
VMEM_LIMIT = 56 * 1024 * 1024


def _alibi_slopes():
    n = 2 * N_HEADS
    s = 2.0 ** (-8.0 * np.arange(1, n + 1) / n)
    return [float(v) for v in s[:N_HEADS]], [float(v) for v in s[N_HEADS:]]


SLOPES_SWA, SLOPES_NSA = _alibi_slopes()


def _bf16_pieces(v):
    out, rem = [], np.float32(v)
    for _ in range(3):
        p = np.float32(np.asarray(rem, np.float32).astype(BF16).astype(np.float32))
        out.append(float(p))
        rem = np.float32(rem - p)
    return out


def _rms(x, g):
    return x * lax.rsqrt(jnp.mean(x * x, axis=-1, keepdims=True) + RMS_EPS) * g


def _dot(a, b):
    return jnp.dot(a, b, preferred_element_type=F32)


def _tiles_to_rows(ref, n, lead=()):
    return jnp.concatenate([ref[lead + (pl.ds(j, n, stride=ROW_TILE), slice(None))] for j in range(ROW_TILE)], axis=1)


def _rows_to_tiles(ref, val):
    n = val.shape[0]
    for j in range(ROW_TILE):
        ref[pl.ds(j, n, stride=ROW_TILE), :] = val[:, j * LANES:(j + 1) * LANES]


def _dot_nt(a, b):
    return lax.dot_general(a, b, (((1,), (1,)), ((), ())), preferred_element_type=F32)


def _proj_kernel(x_ref, g_ref, w_ref, fk_ref, fv_ref, eye_ref, o_ref, vt_ref, cmp_ref):
    h = _rms(x_ref[...], g_ref[...]).astype(BF16)
    res = _dot(h, w_ref[...])
    o_ref[:, :COL_KV] = res[:, :COL_KV].astype(o_ref.dtype)
    for j in range(N_KV_SECT):
        c0 = COL_KV + j * LANES
        o_ref[:, c0:c0 + LANES] = (res[:, c0:c0 + LANES] + fk_ref[...]).astype(o_ref.dtype)
    o_ref[:, COL_GN:] = res[:, COL_GN:PROJ_W].astype(o_ref.dtype)
    for j in range(N_KV_SECT):
        c0 = PROJ_W + j * LANES
        v_t = _dot_nt(eye_ref[...], (res[:, c0:c0 + LANES] + fv_ref[...]).astype(BF16))
        for k in range(v_t.shape[1] // LANES):
            vt_ref[j, 0, k] = v_t[:, k * LANES:(k + 1) * LANES].astype(vt_ref.dtype)
    cmp_ref[...] = res[:, PROJ_W + VT_W:]


def _proj(x2d, g, w, feat_k, feat_v, eye, tm):
    n = x2d.shape[0]
    seq = feat_k.shape[0]
    nper = seq // tm
    kb = tm // LANES
    return pl.pallas_call(
        _proj_kernel,
        grid=(n // tm,),
        in_specs=[pl.BlockSpec((tm, D_MODEL), lambda i: (i, 0)),
                  pl.BlockSpec((1, D_MODEL), lambda i: (0, 0)),
                  pl.BlockSpec((D_MODEL, PROJ_W + VT_W + CMP_W), lambda i: (0, 0)),
                  pl.BlockSpec((tm, LANES), lambda i: (i % nper, 0)),
                  pl.BlockSpec((tm, LANES), lambda i: (i % nper, 0)),
                  pl.BlockSpec((LANES, LANES), lambda i: (0, 0))],
        out_specs=[pl.BlockSpec((tm, PROJ_W), lambda i: (i, 0)),
                   pl.BlockSpec((N_KV_SECT, 1, kb, LANES, LANES), lambda i: (0, i // nper, i % nper, 0, 0)),
                   pl.BlockSpec((tm, CMP_W), lambda i: (i, 0))],
        out_shape=[jax.ShapeDtypeStruct((n, PROJ_W), BF16),
                   jax.ShapeDtypeStruct((N_KV_SECT, n // seq, seq // LANES, LANES, LANES), BF16),
                   jax.ShapeDtypeStruct((n, CMP_W), F32)],
        compiler_params=pltpu.CompilerParams(dimension_semantics=("parallel",),
                                             vmem_limit_bytes=VMEM_LIMIT),
        name="proj",
    )(x2d, g, w, feat_k, feat_v, eye)


def _compress_kernel(x_ref, pa_ref, pb_ref, w1a_ref, w1b_ref, w2_ref, w2t_ref, o_ref, ot_ref, *, nch):
    r = jnp.concatenate([x_ref[0, pl.ds(j, nch, stride=CMP_STRIDE), :] for j in range(CMP_STRIDE)], axis=1)
    a = _dot((r + pa_ref[0]).astype(BF16), w1a_ref[0])
    b = _dot((r + pb_ref[0]).astype(BF16), w1b_ref[0])
    hid = a + pltpu.roll(b, nch - 1, 0)
    hid = hid * jax.nn.sigmoid(hid)
    for h in range(N_KV):
        hid_h = hid[:, h * CMP_HIDDEN:(h + 1) * CMP_HIDDEN].astype(BF16)
        o_ref[0, 0, h] = _dot(hid_h, w2_ref[0]).astype(o_ref.dtype)
        ot_ref[0, 0, h] = _dot_nt(w2t_ref[0], hid_h).astype(ot_ref.dtype)


def _compress(cmp3, pos_a, pos_b, w1a, w1b, w2, w2t):
    bsz, seq, _ = cmp3.shape
    nch = seq // CMP_STRIDE
    wspec = lambda a: pl.BlockSpec((1,) + a.shape[1:], lambda b, j: (j, 0, 0))
    return pl.pallas_call(
        functools.partial(_compress_kernel, nch=nch),
        grid=(bsz, 2),
        in_specs=[pl.BlockSpec((1, seq, LANES), lambda b, j: (b, 0, j)),
                  wspec(pos_a), wspec(pos_b), wspec(w1a), wspec(w1b), wspec(w2), wspec(w2t)],
        out_specs=[pl.BlockSpec((1, 1, N_KV, nch, LANES), lambda b, j: (b, j, 0, 0, 0)),
                   pl.BlockSpec((1, 1, N_KV, LANES, nch), lambda b, j: (b, j, 0, 0, 0))],
        out_shape=[jax.ShapeDtypeStruct((bsz, 2, N_KV, nch, LANES), BF16),
                   jax.ShapeDtypeStruct((bsz, 2, N_KV, LANES, nch), BF16)],
        compiler_params=pltpu.CompilerParams(dimension_semantics=("parallel", "parallel"),
                                             vmem_limit_bytes=VMEM_LIMIT),
        name="compress",
    )(cmp3, pos_a, pos_b, w1a, w1b, w2, w2t)


def _attn_t_kernel(sinks_ref, qa_ref, qb_ref, kc_ref, vct_ref, *rest, n_cmp):
    n_br = 3
    ks = rest[:n_br * N_KV]
    vts = rest[n_br * N_KV:2 * n_br * N_KV]
    gn_ref, ovt_ref, eye_ref, oa_ref, ob_ref = rest[2 * n_br * N_KV:]
    ksect = lambda branch, h: ks[branch * N_KV + h]
    vsect = lambda branch, h: vts[branch * N_KV + h]
    qi = pl.program_id(1)
    q0 = pl.multiple_of(qi * TQ, TQ)
    lane = lax.broadcasted_iota(jnp.int32, (1, TQ), 1)
    sub = lax.broadcasted_iota(jnp.int32, (LANES, 1), 0)
    t_row = q0 + lane
    eye = eye_ref[...]
    gates = jax.nn.sigmoid(_dot_nt(eye, gn_ref[0]))
    gate = lambda hh, c: gates[3 * hh + c:3 * hh + c + 1, :]
    sub40 = sub[ROWS_LO:ROWS_HI]
    blk = sub40 - FEAT_SEL
    in_rng = (blk >= 0) & (blk < N_SLC_BLK)
    is_pos = (sub40 >= FEAT_POS) & (sub40 < FEAT_SEL)
    zeros_lo = jnp.zeros((LANES - ROWS_HI, TQ), F32)

    def q_t(ref, hh):
        both = _dot_nt(eye, ref[0, :, (hh // 2) * LANES:(hh // 2 + 1) * LANES])
        return both[(hh % 2) * HEAD_DIM:(hh % 2 + 1) * HEAD_DIM]

    def slope_col(slope):
        hi, mid, lo = _bf16_pieces(slope)
        col = jnp.zeros(sub40.shape, F32)
        for i, v in enumerate([SLC_LEN * hi, SLC_LEN * mid, SLC_LEN * lo, hi, mid, lo]):
            col = jnp.where(sub40 == FEAT_POS + i, v, col)
        return col

    def q_aug_t(q_ts, tails):
        return jnp.concatenate([jnp.concatenate([q, jnp.broadcast_to(t, (ROWS_HI - ROWS_LO, TQ)), zeros_lo], axis=0)
                                for q, t in zip(q_ts, tails)], axis=1).astype(BF16)

    def v_t(ref, first, n):
        return jnp.concatenate([ref[0, 0, first + j] for j in range(n)], axis=1)

    def masked(s, mask):
        return jnp.concatenate([jnp.where(mask, s[:, g * TQ:(g + 1) * TQ], NEG_INF) for g in range(GROUP)], axis=1)

    def store_heads(ref, outs):
        for p in range(N_HEADS // 2):
            pair = jnp.concatenate([outs[2 * p][:HEAD_DIM], outs[2 * p + 1][:HEAD_DIM]], axis=0).astype(BF16)
            ref[0, :, p * LANES:(p + 1) * LANES] = _dot_nt(eye, pair).astype(ref.dtype)

    nw = NSA_WINDOW // TQ + 1
    w_first = jnp.maximum(qi - (nw - 1), 0)
    w_start = pl.multiple_of(w_first * TQ, TQ)
    nb = (SWA_WINDOW - 1 + TQ - 1) // TQ + 1
    b_first = jnp.maximum(qi - (nb - 1), 0)
    b_start = pl.multiple_of(b_first * TQ, TQ)
    pre = []
    for h in range(N_KV):
        heads = [h * GROUP + g for g in range(GROUP)]
        q_ts = [q_t(qa_ref, hh) for hh in heads]
        scol = [slope_col(SLOPES_NSA[hh]) for hh in heads]
        s_cmp = _dot(kc_ref[0, 0, h], q_aug_t(q_ts, [jnp.zeros((1, 1), F32)] * GROUP))
        s_win = _dot(ksect(1, h)[0, pl.ds(w_start, nw * TQ), :], q_aug_t(q_ts, scol))
        q_b = q_aug_t([q_t(qb_ref, hh) for hh in heads], [slope_col(SLOPES_SWA[hh]) for hh in heads])
        s_swa = _dot(ksect(2, h)[0, pl.ds(b_start, nb * TQ), :], q_b)
        pre.append((q_ts, scol, s_cmp, s_win, s_swa))

    def nsa_local(h):
        heads = [h * GROUP + g for g in range(GROUP)]
        _, _, s_all, s_win, _ = pre[h]

        end_c = sub * CMP_STRIDE + (CMP_LEN - 1)
        cmask = (t_row >= end_c) & (sub < n_cmp)
        ps = []
        for g in range(GROUP):
            s = s_all[:, g * TQ:(g + 1) * TQ] + SLOPES_NSA[heads[g]] * end_c.astype(F32)
            s = jnp.where(cmask, s, NEG_INF)
            m = jnp.max(s, axis=0, keepdims=True)
            e = jnp.where(cmask, jnp.exp(s - m), 0.0)
            z = jnp.sum(e, axis=0, keepdims=True)
            ps.append(e / jnp.where(z > 0, z, 1.0))
        o_cmp = _dot(vct_ref[0, 0, h], jnp.concatenate(ps, axis=1).astype(BF16))

        dist = t_row - (w_start + lax.broadcasted_iota(jnp.int32, (nw * TQ, 1), 0))
        s = masked(s_win, (dist >= 0) & (dist < NSA_WINDOW))
        m = jnp.max(s, axis=0, keepdims=True)
        acc = _dot(v_t(vsect(1, h), w_first, nw), jnp.exp(s - m).astype(BF16))
        o_win = acc / acc[HEAD_DIM:HEAD_DIM + 1, :]

        psum = ps[0] + ps[1] + ps[2] + ps[3]
        p_hi = psum.astype(BF16)
        p_lo = (psum - p_hi.astype(F32)).astype(BF16)
        imp = (_dot(ovt_ref[...], p_hi) + _dot(ovt_ref[...], p_lo))[ROWS_LO:ROWS_HI]
        part = [gate(hh, 0) * o_cmp[:, g * TQ:(g + 1) * TQ] + gate(hh, 2) * o_win[:, g * TQ:(g + 1) * TQ]
                for g, hh in enumerate(heads)]
        return imp, part

    def nsa_select(h, imp):
        q_ts, scol = pre[h][:2]
        cur = lax.shift_right_logical(t_row, int(np.log2(SLC_LEN)))
        valid = in_rng & (blk * SLC_LEN <= t_row)
        forced = in_rng & ((blk == 0) | (blk == cur) | (blk == cur - 1))
        score = jnp.where(forced, FORCE_SCORE, jnp.where(valid, imp, NEG_INF))
        rank = jnp.zeros(score.shape, F32)
        for i in range(N_SLC_BLK):
            r = FEAT_SEL - ROWS_LO + i
            si = score[r:r + 1, :]
            rank = rank + jnp.where((si > score) | ((si == score) & (blk > i)), 1.0, 0.0)
        sel = in_rng & (rank < SLC_TOPK) & (score > 0.5 * NEG_INF)
        bias_diag = jnp.where(in_rng & jnp.logical_not(sel), NEG_INF, 0.0)
        bias_main = jnp.where(in_rng & jnp.logical_not(sel & (blk < 2 * qi)), NEG_INF, 0.0)

        q_d = q_aug_t(q_ts, [jnp.where(is_pos, scol[g], bias_diag) for g in range(GROUP)])
        q_m = q_aug_t(q_ts, [jnp.where(is_pos, scol[g], bias_main) for g in range(GROUP)])
        return q_m, _dot(ksect(0, h)[0, pl.ds(q0, TQ), :], q_d)

    def nsa_diag(h, q_m, s_diag):
        s = masked(s_diag, q0 + sub <= t_row)
        m0 = jnp.max(s, axis=0, keepdims=True)
        acc0 = _dot(v_t(vsect(0, h), qi, 1), jnp.exp(s - m0).astype(BF16))
        return q_m, m0, acc0

    local = [nsa_local(h) for h in range(N_KV)]

    dist = t_row - (b_start + lax.broadcasted_iota(jnp.int32, (nb * TQ, 1), 0))
    bmask = (dist >= 0) & (dist < SWA_WINDOW)
    outs_b = []
    for h in range(N_KV):
        heads = [h * GROUP + g for g in range(GROUP)]
        s = masked(pre[h][4], bmask)
        sink = jnp.concatenate([sinks_ref[hh] + SLOPES_SWA[hh] * t_row.astype(F32) for hh in heads], axis=1)
        m = jnp.maximum(jnp.max(s, axis=0, keepdims=True), sink)
        acc = _dot(v_t(vsect(2, h), b_first, nb), jnp.exp(s - m).astype(BF16))
        o_all = acc / (acc[HEAD_DIM:HEAD_DIM + 1, :] + jnp.exp(sink - m))
        outs_b += [o_all[:, g * TQ:(g + 1) * TQ] for g in range(GROUP)]
    store_heads(ob_ref, outs_b)

    selected = [nsa_select(h, local[h][0]) for h in range(N_KV)]
    fronts = [nsa_diag(h, *selected[h]) + (local[h][1],) for h in range(N_KV)]
    nblk = KCHUNK // TQ

    def slc_step(c, carry):
        state = list(carry)
        subs = [(j, h) for j in range(KCHUNK // SUB) for h in range(N_KV)]
        scores = [_dot(ksect(0, h)[0, pl.ds(pl.multiple_of(c * KCHUNK + j * SUB, SUB), SUB), :], fronts[h][0])
                  for j, h in subs]
        for (j, h), s in zip(subs, scores):
            m_i, acc = state[h]
            m_new = jnp.maximum(m_i, jnp.max(s, axis=0, keepdims=True))
            pv = _dot(v_t(vsect(0, h), c * nblk + j * (SUB // TQ), SUB // TQ), jnp.exp(s - m_new).astype(BF16))
            state[h] = (m_new, jnp.exp(m_i - m_new) * acc + pv)
        return tuple(state)

    n_main = lax.shift_right_logical(q0 + (KCHUNK - 1), int(np.log2(KCHUNK)))
    swept = lax.fori_loop(0, n_main, slc_step, tuple((f[1], f[2]) for f in fronts))
    outs_a = []
    for h in range(N_KV):
        acc = swept[h][1]
        o_slc = acc / acc[HEAD_DIM:HEAD_DIM + 1, :]
        outs_a += [fronts[h][3][g] + gate(h * GROUP + g, 1) * o_slc[:, g * TQ:(g + 1) * TQ] for g in range(GROUP)]
    store_heads(oa_ref, outs_a)


def _attention(proj3, v_t, kc, vc_t, ovt, eye, sinks):
    bsz, seq, _ = proj3.shape
    n_cmp = seq // CMP_STRIDE - CMP_LEN // CMP_STRIDE + 1
    width = N_HEADS * HEAD_DIM
    consts = [ovt, eye]
    qspec = lambda col: pl.BlockSpec((1, TQ, width), lambda b, q, s: (b, q, col))
    cspec = lambda a, j: pl.BlockSpec((1, 1) + a.shape[2:], lambda b, q, s: (b, j, 0, 0, 0))
    in_specs = [qspec(0), qspec(COL_QB // width), cspec(kc, 0), cspec(vc_t, 1)]
    in_specs += [pl.BlockSpec((1, seq, LANES), lambda b, q, s, j=j: (b, 0, COL_KV // LANES + j))
                 for j in range(N_KV_SECT)]
    in_specs += [pl.BlockSpec((1, 1) + v_t.shape[2:], lambda b, q, s, j=j: (j, b, 0, 0, 0)) for j in range(N_KV_SECT)]
    in_specs += [pl.BlockSpec((1, TQ, LANES), lambda b, q, s: (b, q, COL_GN // LANES))]
    in_specs += [pl.BlockSpec(c.shape, lambda b, q, s: (0, 0)) for c in consts]
    ospec = pl.BlockSpec((1, TQ, width), lambda b, q, s: (b, q, 0))
    return pl.pallas_call(
        functools.partial(_attn_t_kernel, n_cmp=n_cmp),
        grid_spec=pltpu.PrefetchScalarGridSpec(
            num_scalar_prefetch=1, grid=(bsz, seq // TQ), in_specs=in_specs, out_specs=[ospec, ospec]),
        out_shape=[jax.ShapeDtypeStruct((bsz, seq, width), BF16)] * 2,
        compiler_params=pltpu.CompilerParams(dimension_semantics=("parallel", "parallel"),
                                             vmem_limit_bytes=VMEM_LIMIT),
        name="attn",
    )(sinks, proj3, proj3, kc, vc_t, *([proj3] * N_KV_SECT), *([v_t] * N_KV_SECT), proj3, *consts)


def _merge_kernel(x_ref, oa_ref, ob_ref, g1_ref, g2_ref, wg_ref, wa_ref, wb_ref, wo_ref,
                  wrh_ref, wrl_ref, br_ref, x2_ref, hn_ref, route_ref, cnt_ref):
    tm = x_ref.shape[0]

    @pl.when(pl.program_id(0) == 0)
    def _():
        cnt_ref[...] = jnp.zeros_like(cnt_ref)

    halves = [slice(i * (tm // 2), (i + 1) * (tm // 2)) for i in range(2)]
    xs = [x_ref[r, :] for r in halves]
    pre = []
    for r, x in zip(halves, xs):
        h = _rms(x, g1_ref[...]).astype(BF16)
        pre.append((_dot(h, wg_ref[...]), _dot(oa_ref[r, :], wa_ref[...]), _dot(ob_ref[r, :], wb_ref[...])))
    mixes = []
    for g_pre, a, b in pre:
        gm = jax.nn.sigmoid(g_pre)
        mixin = gm[:, :D_MODEL] * a + gm[:, D_MODEL:] * b
        mixes.append(_dot(mixin.astype(BF16), wo_ref[...]))
    logit_halves = []
    for r, x, mix in zip(halves, xs, mixes):
        x2 = x + mix
        x2_ref[r, :] = x2
        hn = _rms(x2, g2_ref[...])
        for j in range(ROW_TILE):
            hn_ref[pl.ds(r.start * ROW_TILE + j, tm // 2, stride=ROW_TILE), :] = hn[:, j * LANES:(j + 1) * LANES]
        hn_b = hn.astype(BF16)
        hn_lo = (hn - hn_b.astype(F32)).astype(BF16)
        logit_halves.append(_dot_nt(wrh_ref[...], hn_b) + _dot_nt(wrh_ref[...], hn_lo) + _dot_nt(wrl_ref[...], hn_b))
    logits = jnp.concatenate(logit_halves, axis=1) + br_ref[...]
    row = lax.broadcasted_iota(jnp.int32, (LANES, 1), 0)
    rowf = row.astype(F32)
    big = float(LANES)
    top = lambda a: jnp.max(a, axis=0, keepdims=True)
    first = lambda hit: jnp.min(jnp.where(hit, rowf, big), axis=0, keepdims=True)
    is_g = (row >= N_EXPERTS) & (row < N_EXPERTS + N_GROUPS)
    gl = jnp.where(is_g, logits, NEG_INF)
    gmax = top(gl)
    grp = first(gl == gmax) - N_EXPERTS
    p_grp = 1.0 / jnp.sum(jnp.where(is_g, jnp.exp(gl - gmax), 0.0), axis=0, keepdims=True)
    in_grp = (rowf >= grp * EPG) & (rowf < grp * EPG + EPG)
    el = jnp.where(in_grp, logits, NEG_INF)
    v0 = top(el)
    i0 = first(el == v0)
    el1 = jnp.where(rowf == i0, NEG_INF, el)
    v1 = top(el1)
    i1 = first(el1 == v1)
    e1 = jnp.exp(v1 - v0)
    w0 = p_grp / (1.0 + e1)
    w1 = p_grp * e1 / (1.0 + e1)

    oh0 = jnp.where(rowf == i0, 1.0, 0.0)
    oh1 = jnp.where(rowf == i1, 1.0, 0.0)
    oh = oh0 + oh1
    r_i = lax.broadcasted_iota(jnp.int32, (tm, tm), 0)
    c_i = lax.broadcasted_iota(jnp.int32, (tm, tm), 1)
    earlier = jnp.where(r_i < c_i, 1.0, 0.0).astype(BF16)
    before = cnt_ref[...] + _dot(oh.astype(BF16), earlier)
    rank0 = jnp.sum(oh0 * before, axis=0, keepdims=True)
    rank1 = jnp.sum(oh1 * before, axis=0, keepdims=True)
    cnt_ref[...] = cnt_ref[...] + jnp.sum(oh, axis=1, keepdims=True)
    row8 = row[:8]
    route = jnp.zeros((8, tm), F32)
    for k, v in enumerate((i0, i1, rank0, rank1, w0, w1)):
        route = jnp.where(row8 == k, v, route)
    route_ref[0] = route


def _merge(x2d, oa, ob, g1, g2, wg, wa, wb, wo, wrh, wrl, br, tm):
    n = x2d.shape[0]
    width = N_HEADS * HEAD_DIM
    row = lambda w: pl.BlockSpec((tm, w), lambda i: (i, 0))
    full = lambda a: pl.BlockSpec(a.shape, lambda i: (0, 0))
    return pl.pallas_call(
        _merge_kernel,
        grid=(n // tm,),
        in_specs=[row(D_MODEL), row(width), row(width), full(g1), full(g2), full(wg), full(wa), full(wb),
                  full(wo), full(wrh), full(wrl), full(br)],
        out_specs=[row(D_MODEL), pl.BlockSpec((tm * ROW_TILE, LANES), lambda i: (i, 0)),
                   pl.BlockSpec((1, 8, tm), lambda i: (i, 0, 0)), pl.BlockSpec((LANES, 1), lambda i: (0, 0))],
        out_shape=[jax.ShapeDtypeStruct((n, D_MODEL), F32), jax.ShapeDtypeStruct((n * ROW_TILE, LANES), F32),
                   jax.ShapeDtypeStruct((n // tm, 8, tm), F32), jax.ShapeDtypeStruct((LANES, 1), F32)],
        compiler_params=pltpu.CompilerParams(dimension_semantics=("arbitrary",),
                                             vmem_limit_bytes=VMEM_LIMIT),
        name="merge",
    )(x2d, oa, ob, g1, g2, wg, wa, wb, wo, wrh, wrl, br)


def _slot_table_kernel(pos_ref, init_ref, tab_ref, sem, *, tt, n):
    step = pl.program_id(0)

    @pl.when(step == 0)
    def _():
        cp = pltpu.make_async_copy(init_ref, tab_ref, sem)
        cp.start()
        cp.wait()

    def body(i, c):
        t = step * tt + i
        tab_ref[pos_ref[0, 0, i]] = t * ROW_TILE
        tab_ref[pos_ref[0, 1, i]] = (n + t) * ROW_TILE
        return c

    lax.fori_loop(0, tt, body, 0, unroll=8)


def _slot_table(pos, init, tt):
    nt = pos.shape[0]
    return pl.pallas_call(
        functools.partial(_slot_table_kernel, tt=tt, n=nt * tt),
        grid=(nt,),
        in_specs=[pl.BlockSpec((1, 2, tt), lambda i: (i, 0, 0), memory_space=pltpu.SMEM),
                  pl.BlockSpec(memory_space=pl.ANY)],
        out_specs=pl.BlockSpec(memory_space=pltpu.SMEM),
        out_shape=jax.ShapeDtypeStruct(init.shape, jnp.int32),
        scratch_shapes=[pltpu.SemaphoreType.DMA(())],
        compiler_params=pltpu.CompilerParams(dimension_semantics=("arbitrary",)),
        name="slot_table",
    )(pos, init)


def _moe_kernel(be_ref, nu_ref, tab_ref, tab_next_ref, hn_ref, wg_ref, wu_ref, wd_ref, out_ref,
                xin, yout, wg_s, wu_s, wd_s, gsem, ssem, *, tb, n):
    b = pl.program_id(0)
    n_used = nu_ref[0]
    live = b < n_used
    s = lax.rem(b, 2)

    def gather(tab, slot):
        for j in range(tb):
            t, rows = tab[0, 0, j], n * ROW_TILE
            src = jnp.bitwise_and(t, rows - 1) if rows & (rows - 1) == 0 else lax.rem(t, rows)
            pltpu.make_async_copy(hn_ref.at[_tile_at(src)], xin.at[slot, pl.ds(j * ROW_TILE, ROW_TILE)],
                                  gsem.at[slot]).start(priority=0)

    def wait_rows(sem_ref, slot):
        pltpu.make_async_copy(hn_ref.at[pl.ds(0, tb * ROW_TILE)], xin.at[slot], sem_ref.at[slot]).wait()

    @pl.when(b == 0)
    def _():
        yout[...] = jnp.zeros_like(yout)
        fills = [pltpu.make_async_copy(yout.at[k], out_ref.at[pl.ds((2 * n + k * tb) * ROW_TILE, tb * ROW_TILE)],
                                       ssem.at[k]) for k in range(2)]
        for cp in fills:
            cp.start()
        for cp in fills:
            cp.wait()
        gather(tab_ref, 0)

    @pl.when(live & ((b == 0) | (be_ref[b] != be_ref[jnp.maximum(b - 1, 0)])))
    def _():
        wg_s[...] = wg_ref[0].astype(BF16)
        wu_s[...] = wu_ref[0].astype(BF16)
        wd_s[...] = wd_ref[0].astype(BF16)

    @pl.when(live & (b >= 2))
    def _():
        wait_rows(ssem, s)

    def block(s):
        wait_rows(gsem, s)
        gather(tab_next_ref, 1 - s)
        half = tb // 2
        gu = []
        for i in range(2):
            xb = jnp.concatenate([xin[s, pl.ds(i * half * ROW_TILE + j, half, stride=ROW_TILE), :]
                                  for j in range(ROW_TILE)], axis=1).astype(BF16)
            gu.append((_dot(xb, wg_s[...]), _dot(xb, wu_s[...])))
        for i, (g, u) in enumerate(gu):
            y = _dot((g * jax.nn.sigmoid(g) * u).astype(BF16), wd_s[...])
            for j in range(ROW_TILE):
                yout[s, pl.ds(i * half * ROW_TILE + j, half, stride=ROW_TILE), :] = y[:, j * LANES:(j + 1) * LANES]

        for j in range(tb):
            pltpu.make_async_copy(yout.at[s, pl.ds(j * ROW_TILE, ROW_TILE)], out_ref.at[_tile_at(tab_ref[0, 0, j])],
                                  ssem.at[s]).start(priority=1)

    for parity in range(2):
        pl.when(live & (s == parity))(functools.partial(block, parity))

    @pl.when(live & (b == n_used - 1))
    def _():
        wait_rows(gsem, 1 - s)
        wait_rows(ssem, s)

        @pl.when(b >= 1)
        def _():
            wait_rows(ssem, 1 - s)


def _moe(block_e, n_used, tab, hn, wg, wu, wd, tb, n):
    nblk = tab.shape[0] // tb
    tab2 = tab.reshape(nblk, 1, tb)
    live = lambda b, be, nu: jnp.minimum(b, nu[0] - 1)
    nxt = lambda b, be, nu: jnp.minimum(b + 1, nu[0] - 1)
    wspec = lambda shape: pl.BlockSpec((1,) + shape, lambda b, be, nu: (be[live(b, be, nu)], 0, 0))
    return pl.pallas_call(
        functools.partial(_moe_kernel, tb=tb, n=n),
        grid_spec=pltpu.PrefetchScalarGridSpec(
            num_scalar_prefetch=2, grid=(nblk,),
            in_specs=[pl.BlockSpec((1, 1, tb), lambda b, be, nu: (live(b, be, nu), 0, 0), memory_space=pltpu.SMEM),
                      pl.BlockSpec((1, 1, tb), lambda b, be, nu: (nxt(b, be, nu), 0, 0), memory_space=pltpu.SMEM),
                      pl.BlockSpec(memory_space=pl.ANY),
                      wspec((D_MODEL, EXPERT_FF)), wspec((D_MODEL, EXPERT_FF)), wspec((EXPERT_FF, D_MODEL))],
            out_specs=pl.BlockSpec(memory_space=pl.ANY),
            scratch_shapes=[pltpu.VMEM((2, tb * ROW_TILE, LANES), F32), pltpu.VMEM((2, tb * ROW_TILE, LANES), F32),
                            pltpu.VMEM((D_MODEL, EXPERT_FF), BF16), pltpu.VMEM((D_MODEL, EXPERT_FF), BF16),
                            pltpu.VMEM((EXPERT_FF, D_MODEL), BF16),
                            pltpu.SemaphoreType.DMA((2,)), pltpu.SemaphoreType.DMA((2,))]),
        out_shape=jax.ShapeDtypeStruct(((2 * n + 2 * tb) * ROW_TILE, LANES), F32),
        compiler_params=pltpu.CompilerParams(dimension_semantics=("arbitrary",),
                                             vmem_limit_bytes=VMEM_LIMIT),
        name="moe",
    )(block_e, n_used, tab2, tab2, hn, wg, wu, wd)


def _final_kernel(x2_ref, w_ref, y0_ref, y1_ref, gf_ref, o_ref):
    tc = x2_ref.shape[0]
    w = w_ref[...]
    y = x2_ref[...] + (w[:, 0:1] * _tiles_to_rows(y0_ref, tc) + w[:, 1:2] * _tiles_to_rows(y1_ref, tc))
    o_ref[...] = _rms(y, gf_ref[...])


def _final(x2, w_slot, yslots, gf, tc):
    n = x2.shape[0]
    nt = n // tc
    return pl.pallas_call(
        _final_kernel,
        grid=(nt,),
        in_specs=[pl.BlockSpec((tc, D_MODEL), lambda i: (i, 0)),
                  pl.BlockSpec((tc, 2), lambda i: (i, 0)),
                  pl.BlockSpec((tc * ROW_TILE, LANES), lambda i: (i, 0)),
                  pl.BlockSpec((tc * ROW_TILE, LANES), lambda i: (nt + i, 0)),
                  pl.BlockSpec((1, D_MODEL), lambda i: (0, 0))],
        out_specs=pl.BlockSpec((tc, D_MODEL), lambda i: (i, 0)),
        out_shape=jax.ShapeDtypeStruct((n, D_MODEL), F32),
        compiler_params=pltpu.CompilerParams(dimension_semantics=("parallel",),
                                             vmem_limit_bytes=VMEM_LIMIT),
        name="final",
    )(x2, w_slot, yslots, yslots, gf)


def _tile_at(row):
    return pl.ds(pl.multiple_of(row, ROW_TILE), ROW_TILE)


def _overlap_matrix_t(seq):
    nc = seq // CMP_STRIDE - CMP_LEN // CMP_STRIDE + 1
    ns = seq // SLC_LEN
    c0 = np.arange(nc) * CMP_STRIDE
    s0 = np.arange(ns) * SLC_LEN
    ov = np.clip(np.minimum(c0[:, None] + CMP_LEN, s0[None, :] + SLC_LEN)
                 - np.maximum(c0[:, None], s0[None, :]), 0, None) / CMP_LEN
    out = np.zeros((LANES, LANES), np.float32)
    out[FEAT_SEL:FEAT_SEL + ns, :nc] = ov.T
    return jnp.asarray(out, BF16)


def _position_features(seq):
    pos = np.arange(seq)
    fk = np.zeros((seq, LANES), np.float32)
    fk[:, FEAT_POS:FEAT_POS + 3] = (pos // SLC_LEN)[:, None]
    fk[:, FEAT_POS + 3:FEAT_POS + 6] = (pos % SLC_LEN)[:, None]
    fk[pos, FEAT_SEL + pos // SLC_LEN] = 1.0
    fv = np.zeros((seq, LANES), np.float32)
    fv[:, HEAD_DIM] = 1.0
    return jnp.asarray(fk), jnp.asarray(fv)


def _pick_tile(n, pref):
    t = pref
    while n % t:
        t //= 2
    return t


def kernel(x, norm_mix_g, w_in, cmp_pos_k, cmp_w1_k, cmp_w2_k, cmp_pos_v, cmp_w1_v, cmp_w2_v, sinks, w_a, w_b,
           w_o, norm_ffn_g, w_group, b_group, w_expert, b_expert, w_gate_e, w_up_e, w_down_e, norm_final_g):
    bsz, seq, _ = x.shape
    n = bsz * seq
    assert TQ == LANES and seq % KCHUNK == 0 and seq // SLC_LEN <= N_SLC_BLK and seq // CMP_STRIDE <= LANES
    assert seq >= (NSA_WINDOW // TQ + 1) * TQ and w_in.shape[0] == 1
    x2d = x.reshape(n, D_MODEL)

    w = w_in[0]
    scale = HEAD_DIM ** -0.5
    nsa_w, kvw = N_HEADS * HEAD_DIM, N_KV * HEAD_DIM
    o_qa, o_kva, o_gn = 0, nsa_w, nsa_w + 6 * kvw
    o_qb = o_gn + 3 * N_HEADS
    o_kvb = o_qb + nsa_w
    o_gm = o_kvb + 2 * kvw
    zpad = jnp.zeros((D_MODEL, LANES - HEAD_DIM), F32)
    def sections(offsets):
        return [c for off in offsets for h in range(N_KV)
                for c in (w[:, off + h * HEAD_DIM:off + (h + 1) * HEAD_DIM], zpad)]

    w_attn = jnp.concatenate(
        [w[:, o_qa:o_qa + nsa_w] * scale, w[:, o_qb:o_qb + nsa_w] * scale]
        + sections((o_kva + 2 * kvw, o_kva + 4 * kvw, o_kvb))
        + [w[:, o_gn:o_gn + 3 * N_HEADS], jnp.zeros((D_MODEL, LANES - 3 * N_HEADS), F32)]
        + sections((o_kva + 3 * kvw, o_kva + 5 * kvw, o_kvb + kvw))
        + [w[:, o_kva:o_kva + 2 * kvw]], axis=1).astype(BF16)
    w_gm = w[:, o_gm:o_gm + 2 * D_MODEL].astype(BF16)

    tm = _pick_tile(seq, 512)
    feat_k, feat_v = _position_features(seq)
    eye = jnp.eye(LANES, dtype=BF16)
    proj, v_t, cmp_in = _proj(x2d, norm_mix_g[0][None], w_attn, feat_k, feat_v, eye, tm)
    proj3 = proj.reshape(bsz, seq, PROJ_W)

    nch = seq // CMP_STRIDE
    pos = jnp.stack([cmp_pos_k[0], cmp_pos_v[0]])
    pos = jnp.broadcast_to(pos[:, :, None, :], (2, CMP_LEN, N_KV, HEAD_DIM))
    pos_a = pos[:, :CMP_STRIDE].reshape(2, 1, CMP_STRIDE * kvw)
    pos_b = pos[:, CMP_STRIDE:].reshape(2, 1, CMP_STRIDE * kvw)
    w1 = jnp.stack([cmp_w1_k[0], cmp_w1_v[0]]).reshape(2, CMP_LEN, HEAD_DIM, CMP_HIDDEN)
    w1 = jnp.einsum('kjdc,hg->kjhdgc', w1, jnp.eye(N_KV, dtype=F32))
    w1 = w1.reshape(2, CMP_LEN * kvw, N_KV * CMP_HIDDEN).astype(BF16)
    w2 = jnp.pad(jnp.stack([cmp_w2_k[0], cmp_w2_v[0]]), ((0, 0), (0, 0), (0, LANES - HEAD_DIM))).astype(BF16)
    kvc, kvc_t = _compress(cmp_in.reshape(bsz, seq, CMP_W), pos_a, pos_b, w1[:, :CMP_STRIDE * kvw],
                           w1[:, CMP_STRIDE * kvw:], w2, jnp.swapaxes(w2, 1, 2))
    kvc = jnp.pad(kvc, ((0, 0), (0, 0), (0, 0), (0, LANES - nch), (0, 0)))
    kvc_t = jnp.pad(kvc_t, ((0, 0), (0, 0), (0, 0), (0, 0), (0, LANES - nch)))

    o_a, o_b = _attention(proj3, v_t, kvc, kvc_t, _overlap_matrix_t(seq), eye, sinks[0])

    w_r = jnp.concatenate([w_expert[0], w_group[0],
                           jnp.zeros((D_MODEL, LANES - N_EXPERTS - N_GROUPS), F32)], axis=1)
    w_r = w_r.T
    w_rh = w_r.astype(BF16)
    w_rl = (w_r - w_rh.astype(F32)).astype(BF16)
    b_r = jnp.concatenate([b_expert[0], b_group[0], jnp.zeros((LANES - N_EXPERTS - N_GROUPS,), F32)])[:, None]
    tt = _pick_tile(n, 256)
    x2, hn, route, counts = _merge(
        x2d, o_a.reshape(n, nsa_w), o_b.reshape(n, nsa_w), norm_mix_g[0][None], norm_ffn_g[0][None], w_gm,
        w_a[0].astype(BF16), w_b[0].astype(BF16), w_o[0].astype(BF16), w_rh, w_rl, b_r, tt)

    tb = 256
    nblk = -(-(2 * n + N_EXPERTS * (tb - 1)) // tb)
    cnt = counts[:N_EXPERTS, 0].astype(jnp.int32)
    padded = (cnt + tb - 1) // tb * tb
    pad_end = jnp.cumsum(padded)
    pad_start = pad_end - padded
    block_e = jnp.minimum(jnp.sum(pad_end[None, :] <= (jnp.arange(nblk) * tb)[:, None], axis=1), N_EXPERTS - 1)
    n_used = (pad_end[-1:] // tb).astype(jnp.int32)
    tab = route[:, 0:4, :].astype(jnp.int32)
    eids = tab[:, 0:2, :]
    start_of = sum(jnp.where(eids == e, pad_start[e], 0) for e in range(N_EXPERTS))
    pos = start_of + tab[:, 2:4, :]
    w_slot = jnp.swapaxes(route[:, 4:6, :], 1, 2).reshape(n, 2)
    spare = (2 * n + jnp.arange(nblk * tb, dtype=jnp.int32) % (2 * tb)) * ROW_TILE
    slot_tab = _slot_table(pos, spare, tt)
    y_slots = _moe(block_e.astype(jnp.int32), n_used, slot_tab, hn, w_gate_e[0], w_up_e[0], w_down_e[0], tb, n)
    out = _final(x2, w_slot, y_slots, norm_final_g[None], tt)
    return out.reshape(bsz, seq, D_MODEL)
```

```python
import functools

import numpy as np
import jax
import jax.numpy as jnp
from jax import lax
from jax.experimental import pallas as pl
from jax.experimental.pallas import tpu as pltpu

F32 = jnp.float32
BF16 = jnp.bfloat16

D_MODEL = 1024
HEAD_DIM = 64
N_HEADS = 8
N_KV = 2
GROUP = N_HEADS // N_KV
CMP_LEN = 32
CMP_STRIDE = 16
CMP_HIDDEN = 256
SLC_LEN = 64
SLC_TOPK = 8
NSA_WINDOW = 256
SWA_WINDOW = 128
N_GROUPS = 4
EPG = 8
N_EXPERTS = N_GROUPS * EPG
EXPERT_FF = 256
RMS_EPS = 1e-6
NEG_INF = -1e30
FORCE_SCORE = 1e9

LANES = 128
TQ = 128
KCHUNK = 512
SUB = 128
N_SLC_BLK = LANES // 4
FEAT_POS = HEAD_DIM
FEAT_SEL = HEAD_DIM + 6
ROWS_LO, ROWS_HI = 64, 104
N_KV_SECT = 6
COL_QB = 512
COL_KV = 1024
COL_GN = COL_KV + N_KV_SECT * LANES
PROJ_W = COL_GN + LANES
VT_W = N_KV_SECT * LANES
CMP_W = 2 * LANES
ROW_TILE = D_MODEL // LANES
VMEM_LIMIT = 56 * 1024 * 1024


LOG2E = float(np.log2(np.e))


def _alibi_slopes():
    n = 2 * N_HEADS
    s = 2.0 ** (-8.0 * np.arange(1, n + 1) / n) * LOG2E
    return [float(v) for v in s[:N_HEADS]], [float(v) for v in s[N_HEADS:]]


SLOPES_SWA, SLOPES_NSA = _alibi_slopes()


def _bf16_pieces(v):
    out, rem = [], np.float32(v)
    for _ in range(3):
        p = np.float32(np.asarray(rem, np.float32).astype(BF16).astype(np.float32))
        out.append(float(p))
        rem = np.float32(rem - p)
    return out


def _rms(x, g):
    return x * lax.rsqrt(jnp.mean(x * x, axis=-1, keepdims=True) + RMS_EPS) * g


def _dot(a, b):
    return jnp.dot(a, b, preferred_element_type=F32)


def _tiles_to_rows(ref, n, lead=()):
    return jnp.concatenate([ref[lead + (pl.ds(j, n, stride=ROW_TILE), slice(None))] for j in range(ROW_TILE)], axis=1)


def _rows_to_tiles(ref, val):
    n = val.shape[0]
    for j in range(ROW_TILE):
        ref[pl.ds(j, n, stride=ROW_TILE), :] = val[:, j * LANES:(j + 1) * LANES]


def _dot_nt(a, b):
    return lax.dot_general(a, b, (((1,), (1,)), ((), ())), preferred_element_type=F32)


def _proj_kernel(x_ref, g_ref, w_ref, fk_ref, fv_ref, eye_ref, o_ref, vt_ref, cmp_ref):
    h = _rms(x_ref[...], g_ref[...]).astype(BF16)
    res = _dot(h, w_ref[...])
    o_ref[:, :COL_KV] = res[:, :COL_KV].astype(o_ref.dtype)
    for j in range(N_KV_SECT):
        c0 = COL_KV + j * LANES
        o_ref[:, c0:c0 + LANES] = (res[:, c0:c0 + LANES] + fk_ref[...]).astype(o_ref.dtype)
    o_ref[:, COL_GN:] = res[:, COL_GN:PROJ_W].astype(o_ref.dtype)
    for j in range(N_KV_SECT):
        c0 = PROJ_W + j * LANES
        v_t = _dot_nt(eye_ref[...], (res[:, c0:c0 + LANES] + fv_ref[...]).astype(BF16))
        for k in range(v_t.shape[1] // LANES):
            vt_ref[j, 0, k] = v_t[:, k * LANES:(k + 1) * LANES].astype(vt_ref.dtype)
    cmp_ref[...] = res[:, PROJ_W + VT_W:]


def _proj(x2d, g, w, feat_k, feat_v, eye, tm):
    n = x2d.shape[0]
    seq = feat_k.shape[0]
    nper = seq // tm
    kb = tm // LANES
    return pl.pallas_call(
        _proj_kernel,
        grid=(n // tm,),
        in_specs=[pl.BlockSpec((tm, D_MODEL), lambda i: (i, 0)),
                  pl.BlockSpec((1, D_MODEL), lambda i: (0, 0)),
                  pl.BlockSpec((D_MODEL, PROJ_W + VT_W + CMP_W), lambda i: (0, 0)),
                  pl.BlockSpec((tm, LANES), lambda i: (i % nper, 0)),
                  pl.BlockSpec((tm, LANES), lambda i: (i % nper, 0)),
                  pl.BlockSpec((LANES, LANES), lambda i: (0, 0))],
        out_specs=[pl.BlockSpec((tm, PROJ_W), lambda i: (i, 0)),
                   pl.BlockSpec((N_KV_SECT, 1, kb, LANES, LANES), lambda i: (0, i // nper, i % nper, 0, 0)),
                   pl.BlockSpec((tm, CMP_W), lambda i: (i, 0))],
        out_shape=[jax.ShapeDtypeStruct((n, PROJ_W), BF16),
                   jax.ShapeDtypeStruct((N_KV_SECT, n // seq, seq // LANES, LANES, LANES), BF16),
                   jax.ShapeDtypeStruct((n, CMP_W), F32)],
        compiler_params=pltpu.CompilerParams(dimension_semantics=("parallel",),
                                             vmem_limit_bytes=VMEM_LIMIT),
        name="proj",
    )(x2d, g, w, feat_k, feat_v, eye)


def _compress_kernel(x_ref, pa_ref, pb_ref, w1a_ref, w1b_ref, w2_ref, w2t_ref, o_ref, ot_ref, *, nch):
    r = jnp.concatenate([x_ref[0, pl.ds(j, nch, stride=CMP_STRIDE), :] for j in range(CMP_STRIDE)], axis=1)
    a = _dot((r + pa_ref[0]).astype(BF16), w1a_ref[0])
    b = _dot((r + pb_ref[0]).astype(BF16), w1b_ref[0])
    hid = a + pltpu.roll(b, nch - 1, 0)
    hid = hid * jax.nn.sigmoid(hid)
    for h in range(N_KV):
        hid_h = hid[:, h * CMP_HIDDEN:(h + 1) * CMP_HIDDEN].astype(BF16)
        o_ref[0, 0, h] = _dot(hid_h, w2_ref[0]).astype(o_ref.dtype)
        ot_ref[0, 0, h] = _dot_nt(w2t_ref[0], hid_h).astype(ot_ref.dtype)


def _compress(cmp3, pos_a, pos_b, w1a, w1b, w2, w2t):
    bsz, seq, _ = cmp3.shape
    nch = seq // CMP_STRIDE
    wspec = lambda a: pl.BlockSpec((1,) + a.shape[1:], lambda b, j: (j, 0, 0))
    return pl.pallas_call(
        functools.partial(_compress_kernel, nch=nch),
        grid=(bsz, 2),
        in_specs=[pl.BlockSpec((1, seq, LANES), lambda b, j: (b, 0, j)),
                  wspec(pos_a), wspec(pos_b), wspec(w1a), wspec(w1b), wspec(w2), wspec(w2t)],
        out_specs=[pl.BlockSpec((1, 1, N_KV, nch, LANES), lambda b, j: (b, j, 0, 0, 0)),
                   pl.BlockSpec((1, 1, N_KV, LANES, nch), lambda b, j: (b, j, 0, 0, 0))],
        out_shape=[jax.ShapeDtypeStruct((bsz, 2, N_KV, nch, LANES), BF16),
                   jax.ShapeDtypeStruct((bsz, 2, N_KV, LANES, nch), BF16)],
        compiler_params=pltpu.CompilerParams(dimension_semantics=("parallel", "parallel"),
                                             vmem_limit_bytes=VMEM_LIMIT),
        name="compress",
    )(cmp3, pos_a, pos_b, w1a, w1b, w2, w2t)


def _attn_t_kernel(sinks_ref, qa_ref, qb_ref, kc_ref, vct_ref, *rest, n_cmp):
    n_br = 3
    ks = rest[:n_br * N_KV]
    vts = rest[n_br * N_KV:2 * n_br * N_KV]
    gn_ref, ovt_ref, eye_ref, oa_ref, ob_ref = rest[2 * n_br * N_KV:]
    ksect = lambda branch, h: ks[branch * N_KV + h]
    vsect = lambda branch, h: vts[branch * N_KV + h]
    qi = pl.program_id(1)
    q0 = pl.multiple_of(qi * TQ, TQ)
    lane = lax.broadcasted_iota(jnp.int32, (1, TQ), 1)
    sub = lax.broadcasted_iota(jnp.int32, (LANES, 1), 0)
    t_row = q0 + lane
    eye = eye_ref[...]
    gates = jax.nn.sigmoid(_dot_nt(eye, gn_ref[0]))
    gate = lambda hh, c: gates[3 * hh + c:3 * hh + c + 1, :]
    sub40 = sub[ROWS_LO:ROWS_HI]
    blk = sub40 - FEAT_SEL
    in_rng = (blk >= 0) & (blk < N_SLC_BLK)
    is_pos = (sub40 >= FEAT_POS) & (sub40 < FEAT_SEL)
    zeros_lo = jnp.zeros((LANES - ROWS_HI, TQ), F32)

    def q_t(ref, hh):
        both = _dot_nt(eye, ref[0, :, (hh // 2) * LANES:(hh // 2 + 1) * LANES])
        return both[(hh % 2) * HEAD_DIM:(hh % 2 + 1) * HEAD_DIM]

    def slope_col(slope):
        hi, mid, lo = _bf16_pieces(slope)
        col = jnp.zeros(sub40.shape, F32)
        for i, v in enumerate([SLC_LEN * hi, SLC_LEN * mid, SLC_LEN * lo, hi, mid, lo]):
            col = jnp.where(sub40 == FEAT_POS + i, v, col)
        return col

    def q_aug_t(q_ts, tails):
        return jnp.concatenate([jnp.concatenate([q, jnp.broadcast_to(t, (ROWS_HI - ROWS_LO, TQ)), zeros_lo], axis=0)
                                for q, t in zip(q_ts, tails)], axis=1).astype(BF16)

    def v_t(ref, first, n):
        return jnp.concatenate([ref[0, 0, first + j] for j in range(n)], axis=1)

    def masked(s, mask):
        return jnp.concatenate([jnp.where(mask, s[:, g * TQ:(g + 1) * TQ], NEG_INF) for g in range(GROUP)], axis=1)

    def store_heads(ref, outs):
        for p in range(N_HEADS // 2):
            pair = jnp.concatenate([outs[2 * p][:HEAD_DIM], outs[2 * p + 1][:HEAD_DIM]], axis=0).astype(BF16)
            ref[0, :, p * LANES:(p + 1) * LANES] = _dot_nt(eye, pair).astype(ref.dtype)

    nw = NSA_WINDOW // TQ + 1
    w_first = jnp.maximum(qi - (nw - 1), 0)
    w_start = pl.multiple_of(w_first * TQ, TQ)
    nb = (SWA_WINDOW - 1 + TQ - 1) // TQ + 1
    b_first = jnp.maximum(qi - (nb - 1), 0)
    b_start = pl.multiple_of(b_first * TQ, TQ)
    pre = []
    for h in range(N_KV):
        heads = [h * GROUP + g for g in range(GROUP)]
        q_ts = [q_t(qa_ref, hh) for hh in heads]
        scol = [slope_col(SLOPES_NSA[hh]) for hh in heads]
        s_cmp = _dot(kc_ref[0, 0, h], q_aug_t(q_ts, [jnp.zeros((1, 1), F32)] * GROUP))
        s_win = _dot(ksect(1, h)[0, pl.ds(w_start, nw * TQ), :], q_aug_t(q_ts, scol))
        q_b = q_aug_t([q_t(qb_ref, hh) for hh in heads], [slope_col(SLOPES_SWA[hh]) for hh in heads])
        s_swa = _dot(ksect(2, h)[0, pl.ds(b_start, nb * TQ), :], q_b)
        pre.append((q_ts, scol, s_cmp, s_win, s_swa))

    def nsa_local(h):
        heads = [h * GROUP + g for g in range(GROUP)]
        _, _, s_all, s_win, _ = pre[h]

        end_c = sub * CMP_STRIDE + (CMP_LEN - 1)
        cmask = (t_row >= end_c) & (sub < n_cmp)
        ps = []
        for g in range(GROUP):
            s = s_all[:, g * TQ:(g + 1) * TQ] + SLOPES_NSA[heads[g]] * end_c.astype(F32)
            s = jnp.where(cmask, s, NEG_INF)
            m = jnp.max(s, axis=0, keepdims=True)
            e = jnp.where(cmask, jnp.exp2(s - m), 0.0)
            z = jnp.sum(e, axis=0, keepdims=True)
            ps.append(e / jnp.where(z > 0, z, 1.0))
        o_cmp = _dot(vct_ref[0, 0, h], jnp.concatenate(ps, axis=1).astype(BF16))

        dist = t_row - (w_start + lax.broadcasted_iota(jnp.int32, (nw * TQ, 1), 0))
        s = masked(s_win, (dist >= 0) & (dist < NSA_WINDOW))
        m = jnp.max(s, axis=0, keepdims=True)
        acc = _dot(v_t(vsect(1, h), w_first, nw), jnp.exp2(s - m).astype(BF16))
        o_win = acc / acc[HEAD_DIM:HEAD_DIM + 1, :]

        psum = ps[0] + ps[1] + ps[2] + ps[3]
        p_hi = psum.astype(BF16)
        p_lo = (psum - p_hi.astype(F32)).astype(BF16)
        imp = (_dot(ovt_ref[...], p_hi) + _dot(ovt_ref[...], p_lo))[ROWS_LO:ROWS_HI]
        part = [gate(hh, 0) * o_cmp[:, g * TQ:(g + 1) * TQ] + gate(hh, 2) * o_win[:, g * TQ:(g + 1) * TQ]
                for g, hh in enumerate(heads)]
        return imp, part

    def nsa_select(h, imp):
        q_ts, scol = pre[h][:2]
        cur = lax.shift_right_logical(t_row, int(np.log2(SLC_LEN)))
        valid = in_rng & (blk * SLC_LEN <= t_row)
        forced = in_rng & ((blk == 0) | (blk == cur) | (blk == cur - 1))
        score = jnp.where(forced, FORCE_SCORE, jnp.where(valid, imp, NEG_INF))
        rank = jnp.zeros(score.shape, F32)
        for i in range(N_SLC_BLK):
            r = FEAT_SEL - ROWS_LO + i
            si = score[r:r + 1, :]
            rank = rank + jnp.where((si > score) | ((si == score) & (blk > i)), 1.0, 0.0)
        sel = in_rng & (rank < SLC_TOPK) & (score > 0.5 * NEG_INF)
        bias_diag = jnp.where(in_rng & jnp.logical_not(sel), NEG_INF, 0.0)
        bias_main = jnp.where(in_rng & jnp.logical_not(sel & (blk < 2 * qi)), NEG_INF, 0.0)

        q_d = q_aug_t(q_ts, [jnp.where(is_pos, scol[g], bias_diag) for g in range(GROUP)])
        q_m = q_aug_t(q_ts, [jnp.where(is_pos, scol[g], bias_main) for g in range(GROUP)])
        return q_m, _dot(ksect(0, h)[0, pl.ds(q0, TQ), :], q_d)

    def nsa_diag(h, q_m, s_diag):
        s = masked(s_diag, q0 + sub <= t_row)
        m0 = jnp.max(s, axis=0, keepdims=True)
        acc0 = _dot(v_t(vsect(0, h), qi, 1), jnp.exp2(s - m0).astype(BF16))
        return q_m, m0, acc0

    local = [nsa_local(h) for h in range(N_KV)]

    dist = t_row - (b_start + lax.broadcasted_iota(jnp.int32, (nb * TQ, 1), 0))
    bmask = (dist >= 0) & (dist < SWA_WINDOW)
    outs_b = []
    for h in range(N_KV):
        heads = [h * GROUP + g for g in range(GROUP)]
        s = masked(pre[h][4], bmask)
        sink = jnp.concatenate([sinks_ref[hh] * LOG2E + SLOPES_SWA[hh] * t_row.astype(F32) for hh in heads], axis=1)
        m = jnp.maximum(jnp.max(s, axis=0, keepdims=True), sink)
        acc = _dot(v_t(vsect(2, h), b_first, nb), jnp.exp2(s - m).astype(BF16))
        o_all = acc / (acc[HEAD_DIM:HEAD_DIM + 1, :] + jnp.exp2(sink - m))
        outs_b += [o_all[:, g * TQ:(g + 1) * TQ] for g in range(GROUP)]
    store_heads(ob_ref, outs_b)

    selected = [nsa_select(h, local[h][0]) for h in range(N_KV)]
    fronts = [nsa_diag(h, *selected[h]) + (local[h][1],) for h in range(N_KV)]
    nblk = KCHUNK // TQ

    def slc_step(c, carry):
        state = list(carry)
        subs = [(j, h) for j in range(KCHUNK // SUB) for h in range(N_KV)]
        scores = [_dot(ksect(0, h)[0, pl.ds(pl.multiple_of(c * KCHUNK + j * SUB, SUB), SUB), :], fronts[h][0])
                  for j, h in subs]
        for (j, h), s in zip(subs, scores):
            m_i, acc = state[h]
            m_new = jnp.maximum(m_i, jnp.max(s, axis=0, keepdims=True))
            pv = _dot(v_t(vsect(0, h), c * nblk + j * (SUB // TQ), SUB // TQ), jnp.exp2(s - m_new).astype(BF16))
            state[h] = (m_new, jnp.exp2(m_i - m_new) * acc + pv)
        return tuple(state)

    n_main = lax.shift_right_logical(q0 + (KCHUNK - 1), int(np.log2(KCHUNK)))
    swept = lax.fori_loop(0, n_main, slc_step, tuple((f[1], f[2]) for f in fronts))
    outs_a = []
    for h in range(N_KV):
        acc = swept[h][1]
        o_slc = acc / acc[HEAD_DIM:HEAD_DIM + 1, :]
        outs_a += [fronts[h][3][g] + gate(h * GROUP + g, 1) * o_slc[:, g * TQ:(g + 1) * TQ] for g in range(GROUP)]
    store_heads(oa_ref, outs_a)


def _attention(proj3, v_t, kc, vc_t, ovt, eye, sinks):
    bsz, seq, _ = proj3.shape
    n_cmp = seq // CMP_STRIDE - CMP_LEN // CMP_STRIDE + 1
    width = N_HEADS * HEAD_DIM
    consts = [ovt, eye]
    qspec = lambda col: pl.BlockSpec((1, TQ, width), lambda b, q, s: (b, q, col))
    cspec = lambda a, j: pl.BlockSpec((1, 1) + a.shape[2:], lambda b, q, s: (b, j, 0, 0, 0))
    in_specs = [qspec(0), qspec(COL_QB // width), cspec(kc, 0), cspec(vc_t, 1)]
    in_specs += [pl.BlockSpec((1, seq, LANES), lambda b, q, s, j=j: (b, 0, COL_KV // LANES + j))
                 for j in range(N_KV_SECT)]
    in_specs += [pl.BlockSpec((1, 1) + v_t.shape[2:], lambda b, q, s, j=j: (j, b, 0, 0, 0)) for j in range(N_KV_SECT)]
    in_specs += [pl.BlockSpec((1, TQ, LANES), lambda b, q, s: (b, q, COL_GN // LANES))]
    in_specs += [pl.BlockSpec(c.shape, lambda b, q, s: (0, 0)) for c in consts]
    ospec = pl.BlockSpec((1, TQ, width), lambda b, q, s: (b, q, 0))
    return pl.pallas_call(
        functools.partial(_attn_t_kernel, n_cmp=n_cmp),
        grid_spec=pltpu.PrefetchScalarGridSpec(
            num_scalar_prefetch=1, grid=(bsz, seq // TQ), in_specs=in_specs, out_specs=[ospec, ospec]),
        out_shape=[jax.ShapeDtypeStruct((bsz, seq, width), BF16)] * 2,
        compiler_params=pltpu.CompilerParams(dimension_semantics=("parallel", "parallel"),
                                             vmem_limit_bytes=VMEM_LIMIT),
        name="attn",
    )(sinks, proj3, proj3, kc, vc_t, *([proj3] * N_KV_SECT), *([v_t] * N_KV_SECT), proj3, *consts)


def _merge_kernel(x_ref, oa_ref, ob_ref, g1_ref, g2_ref, wg_ref, wa_ref, wb_ref, wo_ref,
                  wrh_ref, wrl_ref, br_ref, x2_ref, hn_ref, route_ref, cnt_ref):
    tm = x_ref.shape[0]

    @pl.when(pl.program_id(0) == 0)
    def _():
        cnt_ref[...] = jnp.zeros_like(cnt_ref)

    halves = [slice(i * (tm // 2), (i + 1) * (tm // 2)) for i in range(2)]
    xs = [x_ref[r, :] for r in halves]
    pre = []
    for r, x in zip(halves, xs):
        h = _rms(x, g1_ref[...]).astype(BF16)
        pre.append((_dot(h, wg_ref[...]), _dot(oa_ref[r, :], wa_ref[...]), _dot(ob_ref[r, :], wb_ref[...])))
    mixes = []
    for g_pre, a, b in pre:
        gm = jax.nn.sigmoid(g_pre)
        mixin = gm[:, :D_MODEL] * a + gm[:, D_MODEL:] * b
        mixes.append(_dot(mixin.astype(BF16), wo_ref[...]))
    logit_halves = []
    for r, x, mix in zip(halves, xs, mixes):
        x2 = x + mix
        x2_ref[r, :] = x2
        hn = _rms(x2, g2_ref[...])
        for j in range(ROW_TILE):
            hn_ref[pl.ds(r.start * ROW_TILE + j, tm // 2, stride=ROW_TILE), :] = hn[:, j * LANES:(j + 1) * LANES]
        hn_b = hn.astype(BF16)
        hn_lo = (hn - hn_b.astype(F32)).astype(BF16)
        logit_halves.append(_dot_nt(wrh_ref[...], hn_b) + _dot_nt(wrh_ref[...], hn_lo) + _dot_nt(wrl_ref[...], hn_b))
    logits = jnp.concatenate(logit_halves, axis=1) + br_ref[...]
    row = lax.broadcasted_iota(jnp.int32, (LANES, 1), 0)
    rowf = row.astype(F32)
    big = float(LANES)
    top = lambda a: jnp.max(a, axis=0, keepdims=True)
    first = lambda hit: jnp.min(jnp.where(hit, rowf, big), axis=0, keepdims=True)
    is_g = (row >= N_EXPERTS) & (row < N_EXPERTS + N_GROUPS)
    gl = jnp.where(is_g, logits, NEG_INF)
    gmax = top(gl)
    grp = first(gl == gmax) - N_EXPERTS
    p_grp = 1.0 / jnp.sum(jnp.where(is_g, jnp.exp(gl - gmax), 0.0), axis=0, keepdims=True)
    in_grp = (rowf >= grp * EPG) & (rowf < grp * EPG + EPG)
    el = jnp.where(in_grp, logits, NEG_INF)
    v0 = top(el)
    i0 = first(el == v0)
    el1 = jnp.where(rowf == i0, NEG_INF, el)
    v1 = top(el1)
    i1 = first(el1 == v1)
    e1 = jnp.exp(v1 - v0)
    w0 = p_grp / (1.0 + e1)
    w1 = p_grp * e1 / (1.0 + e1)

    oh0 = jnp.where(rowf == i0, 1.0, 0.0)
    oh1 = jnp.where(rowf == i1, 1.0, 0.0)
    oh = oh0 + oh1
    r_i = lax.broadcasted_iota(jnp.int32, (tm, tm), 0)
    c_i = lax.broadcasted_iota(jnp.int32, (tm, tm), 1)
    earlier = jnp.where(r_i < c_i, 1.0, 0.0).astype(BF16)
    before = cnt_ref[...] + _dot(oh.astype(BF16), earlier)
    rank0 = jnp.sum(oh0 * before, axis=0, keepdims=True)
    rank1 = jnp.sum(oh1 * before, axis=0, keepdims=True)
    cnt_ref[...] = cnt_ref[...] + jnp.sum(oh, axis=1, keepdims=True)
    row8 = row[:8]
    route = jnp.zeros((8, tm), F32)
    for k, v in enumerate((i0, i1, rank0, rank1, w0, w1)):
        route = jnp.where(row8 == k, v, route)
    route_ref[0] = route


def _merge(x2d, oa, ob, g1, g2, wg, wa, wb, wo, wrh, wrl, br, tm):
    n = x2d.shape[0]
    width = N_HEADS * HEAD_DIM
    row = lambda w: pl.BlockSpec((tm, w), lambda i: (i, 0))
    full = lambda a: pl.BlockSpec(a.shape, lambda i: (0, 0))
    return pl.pallas_call(
        _merge_kernel,
        grid=(n // tm,),
        in_specs=[row(D_MODEL), row(width), row(width), full(g1), full(g2), full(wg), full(wa), full(wb),
                  full(wo), full(wrh), full(wrl), full(br)],
        out_specs=[row(D_MODEL), pl.BlockSpec((tm * ROW_TILE, LANES), lambda i: (i, 0)),
                   pl.BlockSpec((1, 8, tm), lambda i: (i, 0, 0)), pl.BlockSpec((LANES, 1), lambda i: (0, 0))],
        out_shape=[jax.ShapeDtypeStruct((n, D_MODEL), F32), jax.ShapeDtypeStruct((n * ROW_TILE, LANES), F32),
                   jax.ShapeDtypeStruct((n // tm, 8, tm), F32), jax.ShapeDtypeStruct((LANES, 1), F32)],
        compiler_params=pltpu.CompilerParams(dimension_semantics=("arbitrary",),
                                             vmem_limit_bytes=VMEM_LIMIT),
        name="merge",
    )(x2d, oa, ob, g1, g2, wg, wa, wb, wo, wrh, wrl, br)


def _slot_table_kernel(pos_ref, init_ref, tab_ref, sem, *, tt, n):
    step = pl.program_id(0)

    @pl.when(step == 0)
    def _():
        cp = pltpu.make_async_copy(init_ref, tab_ref, sem)
        cp.start()
        cp.wait()

    def body(i, c):
        t = step * tt + i
        tab_ref[pos_ref[0, 0, i]] = t * ROW_TILE
        tab_ref[pos_ref[0, 1, i]] = (n + t) * ROW_TILE
        return c

    lax.fori_loop(0, tt, body, 0, unroll=8)


def _slot_table(pos, init, tt):
    nt = pos.shape[0]
    return pl.pallas_call(
        functools.partial(_slot_table_kernel, tt=tt, n=nt * tt),
        grid=(nt,),
        in_specs=[pl.BlockSpec((1, 2, tt), lambda i: (i, 0, 0), memory_space=pltpu.SMEM),
                  pl.BlockSpec(memory_space=pl.ANY)],
        out_specs=pl.BlockSpec(memory_space=pltpu.SMEM),
        out_shape=jax.ShapeDtypeStruct(init.shape, jnp.int32),
        scratch_shapes=[pltpu.SemaphoreType.DMA(())],
        compiler_params=pltpu.CompilerParams(dimension_semantics=("arbitrary",)),
        name="slot_table",
    )(pos, init)


def _moe_kernel(be_ref, nu_ref, tab_ref, tab_next_ref, hn_ref, wg_ref, wu_ref, wd_ref, out_ref,
                xin, yout, wg_s, wu_s, wd_s, gsem, ssem, *, tb, n):
    b = pl.program_id(0)
    n_used = nu_ref[0]
    live = b < n_used
    s = lax.rem(b, 2)

    def gather(tab, slot):
        for j in range(tb):
            t, rows = tab[0, 0, j], n * ROW_TILE
            src = jnp.bitwise_and(t, rows - 1) if rows & (rows - 1) == 0 else lax.rem(t, rows)
            pltpu.make_async_copy(hn_ref.at[_tile_at(src)], xin.at[slot, pl.ds(j * ROW_TILE, ROW_TILE)],
                                  gsem.at[slot]).start(priority=0)

    def wait_rows(sem_ref, slot):
        pltpu.make_async_copy(hn_ref.at[pl.ds(0, tb * ROW_TILE)], xin.at[slot], sem_ref.at[slot]).wait()

    @pl.when(b == 0)
    def _():
        yout[...] = jnp.zeros_like(yout)
        fills = [pltpu.make_async_copy(yout.at[k], out_ref.at[pl.ds((2 * n + k * tb) * ROW_TILE, tb * ROW_TILE)],
                                       ssem.at[k]) for k in range(2)]
        for cp in fills:
            cp.start()
        for cp in fills:
            cp.wait()
        gather(tab_ref, 0)

    @pl.when(live & ((b == 0) | (be_ref[b] != be_ref[jnp.maximum(b - 1, 0)])))
    def _():
        wg_s[...] = wg_ref[0].astype(BF16)
        wu_s[...] = wu_ref[0].astype(BF16)
        wd_s[...] = wd_ref[0].astype(BF16)

    @pl.when(live & (b >= 2))
    def _():
        wait_rows(ssem, s)

    def block(s):
        wait_rows(gsem, s)
        gather(tab_next_ref, 1 - s)
        half = tb // 2
        gu = []
        for i in range(2):
            xb = jnp.concatenate([xin[s, pl.ds(i * half * ROW_TILE + j, half, stride=ROW_TILE), :]
                                  for j in range(ROW_TILE)], axis=1).astype(BF16)
            gu.append((_dot(xb, wg_s[...]), _dot(xb, wu_s[...])))
        for i, (g, u) in enumerate(gu):
            y = _dot((g * jax.nn.sigmoid(g) * u).astype(BF16), wd_s[...])
            for j in range(ROW_TILE):
                yout[s, pl.ds(i * half * ROW_TILE + j, half, stride=ROW_TILE), :] = y[:, j * LANES:(j + 1) * LANES]

        for j in range(tb):
            pltpu.make_async_copy(yout.at[s, pl.ds(j * ROW_TILE, ROW_TILE)], out_ref.at[_tile_at(tab_ref[0, 0, j])],
                                  ssem.at[s]).start(priority=1)

    for parity in range(2):
        pl.when(live & (s == parity))(functools.partial(block, parity))

    @pl.when(live & (b == n_used - 1))
    def _():
        wait_rows(gsem, 1 - s)
        wait_rows(ssem, s)

        @pl.when(b >= 1)
        def _():
            wait_rows(ssem, 1 - s)


def _moe(block_e, n_used, tab, hn, wg, wu, wd, tb, n):
    nblk = tab.shape[0] // tb
    tab2 = tab.reshape(nblk, 1, tb)
    live = lambda b, be, nu: jnp.minimum(b, nu[0] - 1)
    nxt = lambda b, be, nu: jnp.minimum(b + 1, nu[0] - 1)
    wspec = lambda shape: pl.BlockSpec((1,) + shape, lambda b, be, nu: (be[live(b, be, nu)], 0, 0))
    return pl.pallas_call(
        functools.partial(_moe_kernel, tb=tb, n=n),
        grid_spec=pltpu.PrefetchScalarGridSpec(
            num_scalar_prefetch=2, grid=(nblk,),
            in_specs=[pl.BlockSpec((1, 1, tb), lambda b, be, nu: (live(b, be, nu), 0, 0), memory_space=pltpu.SMEM),
                      pl.BlockSpec((1, 1, tb), lambda b, be, nu: (nxt(b, be, nu), 0, 0), memory_space=pltpu.SMEM),
                      pl.BlockSpec(memory_space=pl.ANY),
                      wspec((D_MODEL, EXPERT_FF)), wspec((D_MODEL, EXPERT_FF)), wspec((EXPERT_FF, D_MODEL))],
            out_specs=pl.BlockSpec(memory_space=pl.ANY),
            scratch_shapes=[pltpu.VMEM((2, tb * ROW_TILE, LANES), F32), pltpu.VMEM((2, tb * ROW_TILE, LANES), F32),
                            pltpu.VMEM((D_MODEL, EXPERT_FF), BF16), pltpu.VMEM((D_MODEL, EXPERT_FF), BF16),
                            pltpu.VMEM((EXPERT_FF, D_MODEL), BF16),
                            pltpu.SemaphoreType.DMA((2,)), pltpu.SemaphoreType.DMA((2,))]),
        out_shape=jax.ShapeDtypeStruct(((2 * n + 2 * tb) * ROW_TILE, LANES), F32),
        compiler_params=pltpu.CompilerParams(dimension_semantics=("arbitrary",),
                                             vmem_limit_bytes=VMEM_LIMIT),
        name="moe",
    )(block_e, n_used, tab2, tab2, hn, wg, wu, wd)


def _final_kernel(x2_ref, w_ref, y0_ref, y1_ref, gf_ref, o_ref):
    tc = x2_ref.shape[0]
    w = w_ref[...]
    y = x2_ref[...] + (w[:, 0:1] * _tiles_to_rows(y0_ref, tc) + w[:, 1:2] * _tiles_to_rows(y1_ref, tc))
    o_ref[...] = _rms(y, gf_ref[...])


def _final(x2, w_slot, yslots, gf, tc):
    n = x2.shape[0]
    nt = n // tc
    return pl.pallas_call(
        _final_kernel,
        grid=(nt,),
        in_specs=[pl.BlockSpec((tc, D_MODEL), lambda i: (i, 0)),
                  pl.BlockSpec((tc, 2), lambda i: (i, 0)),
                  pl.BlockSpec((tc * ROW_TILE, LANES), lambda i: (i, 0)),
                  pl.BlockSpec((tc * ROW_TILE, LANES), lambda i: (nt + i, 0)),
                  pl.BlockSpec((1, D_MODEL), lambda i: (0, 0))],
        out_specs=pl.BlockSpec((tc, D_MODEL), lambda i: (i, 0)),
        out_shape=jax.ShapeDtypeStruct((n, D_MODEL), F32),
        compiler_params=pltpu.CompilerParams(dimension_semantics=("parallel",),
                                             vmem_limit_bytes=VMEM_LIMIT),
        name="final",
    )(x2, w_slot, yslots, yslots, gf)


def _tile_at(row):
    return pl.ds(pl.multiple_of(row, ROW_TILE), ROW_TILE)


def _overlap_matrix_t(seq):
    nc = seq // CMP_STRIDE - CMP_LEN // CMP_STRIDE + 1
    ns = seq // SLC_LEN
    c0 = np.arange(nc) * CMP_STRIDE
    s0 = np.arange(ns) * SLC_LEN
    ov = np.clip(np.minimum(c0[:, None] + CMP_LEN, s0[None, :] + SLC_LEN)
                 - np.maximum(c0[:, None], s0[None, :]), 0, None) / CMP_LEN
    out = np.zeros((LANES, LANES), np.float32)
    out[FEAT_SEL:FEAT_SEL + ns, :nc] = ov.T
    return jnp.asarray(out, BF16)


def _position_features(seq):
    pos = np.arange(seq)
    fk = np.zeros((seq, LANES), np.float32)
    fk[:, FEAT_POS:FEAT_POS + 3] = (pos // SLC_LEN)[:, None]
    fk[:, FEAT_POS + 3:FEAT_POS + 6] = (pos % SLC_LEN)[:, None]
    fk[pos, FEAT_SEL + pos // SLC_LEN] = 1.0
    fv = np.zeros((seq, LANES), np.float32)
    fv[:, HEAD_DIM] = 1.0
    return jnp.asarray(fk), jnp.asarray(fv)


def _pick_tile(n, pref):
    t = pref
    while n % t:
        t //= 2
    return t


def kernel(x, norm_mix_g, w_in, cmp_pos_k, cmp_w1_k, cmp_w2_k, cmp_pos_v, cmp_w1_v, cmp_w2_v, sinks, w_a, w_b,
           w_o, norm_ffn_g, w_group, b_group, w_expert, b_expert, w_gate_e, w_up_e, w_down_e, norm_final_g):
    bsz, seq, _ = x.shape
    n = bsz * seq
    assert TQ == LANES and seq % KCHUNK == 0 and seq // SLC_LEN <= N_SLC_BLK and seq // CMP_STRIDE <= LANES
    assert seq >= (NSA_WINDOW // TQ + 1) * TQ and w_in.shape[0] == 1
    x2d = x.reshape(n, D_MODEL)

    w = w_in[0]
    scale = HEAD_DIM ** -0.5 * LOG2E
    nsa_w, kvw = N_HEADS * HEAD_DIM, N_KV * HEAD_DIM
    o_qa, o_kva, o_gn = 0, nsa_w, nsa_w + 6 * kvw
    o_qb = o_gn + 3 * N_HEADS
    o_kvb = o_qb + nsa_w
    o_gm = o_kvb + 2 * kvw
    zpad = jnp.zeros((D_MODEL, LANES - HEAD_DIM), F32)
    def sections(offsets):
        return [c for off in offsets for h in range(N_KV)
                for c in (w[:, off + h * HEAD_DIM:off + (h + 1) * HEAD_DIM], zpad)]

    w_attn = jnp.concatenate(
        [w[:, o_qa:o_qa + nsa_w] * scale, w[:, o_qb:o_qb + nsa_w] * scale]
        + sections((o_kva + 2 * kvw, o_kva + 4 * kvw, o_kvb))
        + [w[:, o_gn:o_gn + 3 * N_HEADS], jnp.zeros((D_MODEL, LANES - 3 * N_HEADS), F32)]
        + sections((o_kva + 3 * kvw, o_kva + 5 * kvw, o_kvb + kvw))
        + [w[:, o_kva:o_kva + 2 * kvw]], axis=1).astype(BF16)
    w_gm = w[:, o_gm:o_gm + 2 * D_MODEL].astype(BF16)

    tm = _pick_tile(seq, 512)
    feat_k, feat_v = _position_features(seq)
    eye = jnp.eye(LANES, dtype=BF16)
    proj, v_t, cmp_in = _proj(x2d, norm_mix_g[0][None], w_attn, feat_k, feat_v, eye, tm)
    proj3 = proj.reshape(bsz, seq, PROJ_W)

    nch = seq // CMP_STRIDE
    pos = jnp.stack([cmp_pos_k[0], cmp_pos_v[0]])
    pos = jnp.broadcast_to(pos[:, :, None, :], (2, CMP_LEN, N_KV, HEAD_DIM))
    pos_a = pos[:, :CMP_STRIDE].reshape(2, 1, CMP_STRIDE * kvw)
    pos_b = pos[:, CMP_STRIDE:].reshape(2, 1, CMP_STRIDE * kvw)
    w1 = jnp.stack([cmp_w1_k[0], cmp_w1_v[0]]).reshape(2, CMP_LEN, HEAD_DIM, CMP_HIDDEN)
    zero = jnp.zeros_like(w1)
    w1 = jnp.stack([jnp.concatenate([w1, zero], axis=-1), jnp.concatenate([zero, w1], axis=-1)], axis=2)
    w1 = w1.reshape(2, CMP_LEN * kvw, N_KV * CMP_HIDDEN).astype(BF16)
    w2 = jnp.pad(jnp.stack([cmp_w2_k[0], cmp_w2_v[0]]), ((0, 0), (0, 0), (0, LANES - HEAD_DIM))).astype(BF16)
    kvc, kvc_t = _compress(cmp_in.reshape(bsz, seq, CMP_W), pos_a, pos_b, w1[:, :CMP_STRIDE * kvw],
                           w1[:, CMP_STRIDE * kvw:], w2, jnp.swapaxes(w2, 1, 2))
    kvc = jnp.pad(kvc, ((0, 0), (0, 0), (0, 0), (0, LANES - nch), (0, 0)))
    kvc_t = jnp.pad(kvc_t, ((0, 0), (0, 0), (0, 0), (0, 0), (0, LANES - nch)))

    o_a, o_b = _attention(proj3, v_t, kvc, kvc_t, _overlap_matrix_t(seq), eye, sinks[0])

    w_r = jnp.concatenate([w_expert[0], w_group[0],
                           jnp.zeros((D_MODEL, LANES - N_EXPERTS - N_GROUPS), F32)], axis=1)
    w_r = w_r.T
    w_rh = w_r.astype(BF16)
    w_rl = (w_r - w_rh.astype(F32)).astype(BF16)
    b_r = jnp.concatenate([b_expert[0], b_group[0], jnp.zeros((LANES - N_EXPERTS - N_GROUPS,), F32)])[:, None]
    tt = _pick_tile(n, 512)
    x2, hn, route, counts = _merge(
        x2d, o_a.reshape(n, nsa_w), o_b.reshape(n, nsa_w), norm_mix_g[0][None], norm_ffn_g[0][None], w_gm,
        w_a[0].astype(BF16), w_b[0].astype(BF16), w_o[0].astype(BF16), w_rh, w_rl, b_r, tt)

    tb = 256
    nblk = -(-(2 * n + N_EXPERTS * (tb - 1)) // tb)
    cnt = counts[:N_EXPERTS, 0].astype(jnp.int32)
    padded = (cnt + tb - 1) // tb * tb
    pad_end = jnp.cumsum(padded)
    pad_start = pad_end - padded
    block_e = jnp.minimum(jnp.sum(pad_end[None, :] <= (jnp.arange(nblk) * tb)[:, None], axis=1), N_EXPERTS - 1)
    n_used = (pad_end[-1:] // tb).astype(jnp.int32)
    tab = route[:, 0:4, :].astype(jnp.int32)
    eids = tab[:, 0:2, :]
    start_of = sum(jnp.where(eids == e, pad_start[e], 0) for e in range(N_EXPERTS))
    pos = start_of + tab[:, 2:4, :]
    w_slot = jnp.swapaxes(route[:, 4:6, :], 1, 2).reshape(n, 2)
    spare = (2 * n + jnp.arange(nblk * tb, dtype=jnp.int32) % (2 * tb)) * ROW_TILE
    slot_tab = _slot_table(pos, spare, tt)
    y_slots = _moe(block_e.astype(jnp.int32), n_used, slot_tab, hn, w_gate_e[0], w_up_e[0], w_down_e[0], tb, n)
    out = _final(x2, w_slot, y_slots, norm_final_g[None], tt)
    return out.reshape(bsz, seq, D_MODEL)
```

```python
import functools

import numpy as np
import jax
import jax.numpy as jnp
from jax import lax
from jax.experimental import pallas as pl
from jax.experimental.pallas import tpu as pltpu

F32 = jnp.float32
BF16 = jnp.bfloat16

D_MODEL = 1024
HEAD_DIM = 64
N_HEADS = 8
N_KV = 2
GROUP = N_HEADS // N_KV
CMP_LEN = 32
CMP_STRIDE = 16
CMP_HIDDEN = 256
SLC_LEN = 64
SLC_TOPK = 8
NSA_WINDOW = 256
SWA_WINDOW = 128
N_GROUPS = 4
EPG = 8
N_EXPERTS = N_GROUPS * EPG
EXPERT_FF = 256
RMS_EPS = 1e-6
NEG_INF = -1e30
FORCE_SCORE = 1e9

LANES = 128
TQ = 128
KCHUNK = 512
SUB = 128
N_SLC_BLK = LANES // 4
FEAT_POS = HEAD_DIM
FEAT_SEL = HEAD_DIM + 6
ROWS_LO, ROWS_HI = 64, 104
N_KV_SECT = 6
COL_QB = 512
COL_KV = 1024
COL_GN = COL_KV + N_KV_SECT * LANES
PROJ_W = COL_GN + LANES
CMP_W = 2 * LANES
ROW_TILE = D_MODEL // LANES
VMEM_LIMIT = 56 * 1024 * 1024


LOG2E = float(np.log2(np.e))


def _alibi_slopes():
    n = 2 * N_HEADS
    s = 2.0 ** (-8.0 * np.arange(1, n + 1) / n) * LOG2E
    return [float(v) for v in s[:N_HEADS]], [float(v) for v in s[N_HEADS:]]


SLOPES_SWA, SLOPES_NSA = _alibi_slopes()


def _bf16_pieces(v):
    out, rem = [], np.float32(v)
    for _ in range(3):
        p = np.float32(np.asarray(rem, np.float32).astype(BF16).astype(np.float32))
        out.append(float(p))
        rem = np.float32(rem - p)
    return out


def _rms(x, g):
    return x * lax.rsqrt(jnp.mean(x * x, axis=-1, keepdims=True) + RMS_EPS) * g


def _dot(a, b):
    return jnp.dot(a, b, preferred_element_type=F32)


def _tiles_to_rows(ref, n, lead=()):
    return jnp.concatenate([ref[lead + (pl.ds(j, n, stride=ROW_TILE), slice(None))] for j in range(ROW_TILE)], axis=1)


def _rows_to_tiles(ref, val):
    n = val.shape[0]
    for j in range(ROW_TILE):
        ref[pl.ds(j, n, stride=ROW_TILE), :] = val[:, j * LANES:(j + 1) * LANES]


def _dot_nt(a, b):
    return lax.dot_general(a, b, (((1,), (1,)), ((), ())), preferred_element_type=F32)


def _proj_kernel(x_ref, g_ref, w_ref, fk_ref, eye_ref, swap_ref, o_ref, vt_ref, cmp_ref):
    tm = x_ref.shape[0]
    n_br = N_KV_SECT // N_KV
    col_gn = COL_KV + n_br * LANES
    col_v = col_gn + LANES
    h = _rms(x_ref[...], g_ref[...]).astype(BF16)
    res = _dot(h, w_ref[...])
    o_ref[:, :COL_KV] = res[:, :COL_KV].astype(o_ref.dtype)
    o_ref[:, COL_GN:] = res[:, col_gn:col_v].astype(o_ref.dtype)
    low = lax.broadcasted_iota(jnp.int32, (1, LANES), 1) < HEAD_DIM
    ones_row = jnp.where(lax.broadcasted_iota(jnp.int32, (LANES - HEAD_DIM, tm), 0) == 0, 1.0, 0.0)
    for br in range(n_br):
        k_pair = res[:, COL_KV + br * LANES:COL_KV + (br + 1) * LANES]
        heads = (k_pair, _dot(k_pair.astype(BF16), swap_ref[...]))
        for hd in range(N_KV):
            c0 = COL_KV + (br * N_KV + hd) * LANES
            o_ref[:, c0:c0 + LANES] = jnp.where(low, heads[hd], fk_ref[...]).astype(o_ref.dtype)
        v_pair_t = _dot_nt(eye_ref[...], res[:, col_v + br * LANES:col_v + (br + 1) * LANES].astype(BF16))
        for hd in range(N_KV):
            v_t = jnp.concatenate([v_pair_t[hd * HEAD_DIM:(hd + 1) * HEAD_DIM], ones_row], axis=0)
            for k in range(tm // LANES):
                vt_ref[br * N_KV + hd, 0, k] = v_t[:, k * LANES:(k + 1) * LANES].astype(vt_ref.dtype)
    cmp_ref[...] = res[:, col_v + n_br * LANES:]


def _proj(x2d, g, w, feat_k, eye, swap, tm):
    n = x2d.shape[0]
    seq = feat_k.shape[0]
    nper = seq // tm
    kb = tm // LANES
    return pl.pallas_call(
        _proj_kernel,
        grid=(n // tm,),
        in_specs=[pl.BlockSpec((tm, D_MODEL), lambda i: (i, 0)),
                  pl.BlockSpec((1, D_MODEL), lambda i: (0, 0)),
                  pl.BlockSpec(w.shape, lambda i: (0, 0)),
                  pl.BlockSpec((tm, LANES), lambda i: (i % nper, 0)),
                  pl.BlockSpec((LANES, LANES), lambda i: (0, 0)),
                  pl.BlockSpec((LANES, LANES), lambda i: (0, 0))],
        out_specs=[pl.BlockSpec((tm, PROJ_W), lambda i: (i, 0)),
                   pl.BlockSpec((N_KV_SECT, 1, kb, LANES, LANES), lambda i: (0, i // nper, i % nper, 0, 0)),
                   pl.BlockSpec((tm, CMP_W), lambda i: (i, 0))],
        out_shape=[jax.ShapeDtypeStruct((n, PROJ_W), BF16),
                   jax.ShapeDtypeStruct((N_KV_SECT, n // seq, seq // LANES, LANES, LANES), BF16),
                   jax.ShapeDtypeStruct((n, CMP_W), F32)],
        compiler_params=pltpu.CompilerParams(dimension_semantics=("parallel",),
                                             vmem_limit_bytes=VMEM_LIMIT),
        name="proj",
    )(x2d, g, w, feat_k, eye, swap)


def _compress_kernel(x_ref, pa_ref, pb_ref, w1a_ref, w1b_ref, w2_ref, w2t_ref, o_ref, ot_ref, *, nch):
    r = jnp.concatenate([x_ref[0, pl.ds(j, nch, stride=CMP_STRIDE), :] for j in range(CMP_STRIDE)], axis=1)
    a = _dot((r + pa_ref[0]).astype(BF16), w1a_ref[0])
    b = _dot((r + pb_ref[0]).astype(BF16), w1b_ref[0])
    hid = a + pltpu.roll(b, nch - 1, 0)
    hid = hid * jax.nn.sigmoid(hid)
    for h in range(N_KV):
        hid_h = hid[:, h * CMP_HIDDEN:(h + 1) * CMP_HIDDEN].astype(BF16)
        o_ref[0, 0, h] = _dot(hid_h, w2_ref[0]).astype(o_ref.dtype)
        ot_ref[0, 0, h] = _dot_nt(w2t_ref[0], hid_h).astype(ot_ref.dtype)


def _compress(cmp3, pos_a, pos_b, w1a, w1b, w2, w2t):
    bsz, seq, _ = cmp3.shape
    nch = seq // CMP_STRIDE
    wspec = lambda a: pl.BlockSpec((1,) + a.shape[1:], lambda b, j: (j, 0, 0))
    return pl.pallas_call(
        functools.partial(_compress_kernel, nch=nch),
        grid=(bsz, 2),
        in_specs=[pl.BlockSpec((1, seq, LANES), lambda b, j: (b, 0, j)),
                  wspec(pos_a), wspec(pos_b), wspec(w1a), wspec(w1b), wspec(w2), wspec(w2t)],
        out_specs=[pl.BlockSpec((1, 1, N_KV, nch, LANES), lambda b, j: (b, j, 0, 0, 0)),
                   pl.BlockSpec((1, 1, N_KV, LANES, nch), lambda b, j: (b, j, 0, 0, 0))],
        out_shape=[jax.ShapeDtypeStruct((bsz, 2, N_KV, nch, LANES), BF16),
                   jax.ShapeDtypeStruct((bsz, 2, N_KV, LANES, nch), BF16)],
        compiler_params=pltpu.CompilerParams(dimension_semantics=("parallel", "parallel"),
                                             vmem_limit_bytes=VMEM_LIMIT),
        name="compress",
    )(cmp3, pos_a, pos_b, w1a, w1b, w2, w2t)


def _attn_t_kernel(sinks_ref, qa_ref, qb_ref, kc_ref, vct_ref, *rest, n_cmp):
    n_br = 3
    ks = rest[:n_br * N_KV]
    vts = rest[n_br * N_KV:2 * n_br * N_KV]
    gn_ref, ovt_ref, eye_ref, oa_ref, ob_ref = rest[2 * n_br * N_KV:]
    ksect = lambda branch, h: ks[branch * N_KV + h]
    vsect = lambda branch, h: vts[branch * N_KV + h]
    qi = pl.program_id(1)
    q0 = pl.multiple_of(qi * TQ, TQ)
    lane = lax.broadcasted_iota(jnp.int32, (1, TQ), 1)
    sub = lax.broadcasted_iota(jnp.int32, (LANES, 1), 0)
    t_row = q0 + lane
    eye = eye_ref[...]
    gates = jax.nn.sigmoid(_dot_nt(eye, gn_ref[0]))
    gate = lambda hh, c: gates[3 * hh + c:3 * hh + c + 1, :]
    sub40 = sub[ROWS_LO:ROWS_HI]
    blk = sub40 - FEAT_SEL
    in_rng = (blk >= 0) & (blk < N_SLC_BLK)
    is_pos = (sub40 >= FEAT_POS) & (sub40 < FEAT_SEL)
    zeros_lo = jnp.zeros((LANES - ROWS_HI, TQ), F32)

    def q_t(ref, hh):
        both = _dot_nt(eye, ref[0, :, (hh // 2) * LANES:(hh // 2 + 1) * LANES])
        return both[(hh % 2) * HEAD_DIM:(hh % 2 + 1) * HEAD_DIM]

    def slope_col(slope):
        hi, mid, lo = _bf16_pieces(slope)
        col = jnp.zeros(sub40.shape, F32)
        for i, v in enumerate([SLC_LEN * hi, SLC_LEN * mid, SLC_LEN * lo, hi, mid, lo]):
            col = jnp.where(sub40 == FEAT_POS + i, v, col)
        return col

    def q_aug_t(q_ts, tails):
        return jnp.concatenate([jnp.concatenate([q, jnp.broadcast_to(t, (ROWS_HI - ROWS_LO, TQ)), zeros_lo], axis=0)
                                for q, t in zip(q_ts, tails)], axis=1).astype(BF16)

    def v_t(ref, first, n):
        return jnp.concatenate([ref[0, 0, first + j] for j in range(n)], axis=1)

    def masked(s, mask):
        return jnp.concatenate([jnp.where(mask, s[:, g * TQ:(g + 1) * TQ], NEG_INF) for g in range(GROUP)], axis=1)

    def store_heads(ref, outs):
        for p in range(N_HEADS // 2):
            pair = jnp.concatenate([outs[2 * p][:HEAD_DIM], outs[2 * p + 1][:HEAD_DIM]], axis=0).astype(BF16)
            ref[0, :, p * LANES:(p + 1) * LANES] = _dot_nt(eye, pair).astype(ref.dtype)

    nw = NSA_WINDOW // TQ + 1
    w_first = jnp.maximum(qi - (nw - 1), 0)
    w_start = pl.multiple_of(w_first * TQ, TQ)
    nb = (SWA_WINDOW - 1 + TQ - 1) // TQ + 1
    b_first = jnp.maximum(qi - (nb - 1), 0)
    b_start = pl.multiple_of(b_first * TQ, TQ)
    pre = []
    for h in range(N_KV):
        heads = [h * GROUP + g for g in range(GROUP)]
        q_ts = [q_t(qa_ref, hh) for hh in heads]
        scol = [slope_col(SLOPES_NSA[hh]) for hh in heads]
        s_cmp = _dot(kc_ref[0, 0, h], q_aug_t(q_ts, [jnp.zeros((1, 1), F32)] * GROUP))
        s_win = _dot(ksect(1, h)[0, pl.ds(w_start, nw * TQ), :], q_aug_t(q_ts, scol))
        q_b = q_aug_t([q_t(qb_ref, hh) for hh in heads], [slope_col(SLOPES_SWA[hh]) for hh in heads])
        s_swa = _dot(ksect(2, h)[0, pl.ds(b_start, nb * TQ), :], q_b)
        pre.append((q_ts, scol, s_cmp, s_win, s_swa))

    def nsa_local(h):
        heads = [h * GROUP + g for g in range(GROUP)]
        _, _, s_all, s_win, _ = pre[h]

        end_c = sub * CMP_STRIDE + (CMP_LEN - 1)
        cmask = (t_row >= end_c) & (sub < n_cmp)
        ps = []
        for g in range(GROUP):
            s = s_all[:, g * TQ:(g + 1) * TQ] + SLOPES_NSA[heads[g]] * end_c.astype(F32)
            s = jnp.where(cmask, s, NEG_INF)
            m = jnp.max(s, axis=0, keepdims=True)
            e = jnp.where(cmask, jnp.exp2(s - m), 0.0)
            z = jnp.sum(e, axis=0, keepdims=True)
            ps.append(e / jnp.where(z > 0, z, 1.0))
        o_cmp = _dot(vct_ref[0, 0, h], jnp.concatenate(ps, axis=1).astype(BF16))

        dist = t_row - (w_start + lax.broadcasted_iota(jnp.int32, (nw * TQ, 1), 0))
        s = masked(s_win, (dist >= 0) & (dist < NSA_WINDOW))
        m = jnp.max(s, axis=0, keepdims=True)
        acc = _dot(v_t(vsect(1, h), w_first, nw), jnp.exp2(s - m).astype(BF16))
        o_win = acc / acc[HEAD_DIM:HEAD_DIM + 1, :]

        psum = ps[0] + ps[1] + ps[2] + ps[3]
        p_hi = psum.astype(BF16)
        p_lo = (psum - p_hi.astype(F32)).astype(BF16)
        imp = (_dot(ovt_ref[...], p_hi) + _dot(ovt_ref[...], p_lo))[ROWS_LO:ROWS_HI]
        part = [gate(hh, 0) * o_cmp[:, g * TQ:(g + 1) * TQ] + gate(hh, 2) * o_win[:, g * TQ:(g + 1) * TQ]
                for g, hh in enumerate(heads)]
        return imp, part

    def nsa_select(h, imp):
        q_ts, scol = pre[h][:2]
        cur = lax.shift_right_logical(t_row, int(np.log2(SLC_LEN)))
        valid = in_rng & (blk * SLC_LEN <= t_row)
        forced = in_rng & ((blk == 0) | (blk == cur) | (blk == cur - 1))
        score = jnp.where(forced, FORCE_SCORE, jnp.where(valid, imp, NEG_INF))
        rank = jnp.zeros(score.shape, F32)
        for i in range(N_SLC_BLK):
            r = FEAT_SEL - ROWS_LO + i
            si = score[r:r + 1, :]
            rank = rank + jnp.where((si > score) | ((si == score) & (blk > i)), 1.0, 0.0)
        sel = in_rng & (rank < SLC_TOPK) & (score > 0.5 * NEG_INF)
        bias_diag = jnp.where(in_rng & jnp.logical_not(sel), NEG_INF, 0.0)
        bias_main = jnp.where(in_rng & jnp.logical_not(sel & (blk < 2 * qi)), NEG_INF, 0.0)

        q_d = q_aug_t(q_ts, [jnp.where(is_pos, scol[g], bias_diag) for g in range(GROUP)])
        q_m = q_aug_t(q_ts, [jnp.where(is_pos, scol[g], bias_main) for g in range(GROUP)])
        return q_m, _dot(ksect(0, h)[0, pl.ds(q0, TQ), :], q_d)

    def nsa_diag(h, q_m, s_diag):
        s = masked(s_diag, q0 + sub <= t_row)
        m0 = jnp.max(s, axis=0, keepdims=True)
        acc0 = _dot(v_t(vsect(0, h), qi, 1), jnp.exp2(s - m0).astype(BF16))
        return q_m, m0, acc0

    local = [nsa_local(h) for h in range(N_KV)]

    dist = t_row - (b_start + lax.broadcasted_iota(jnp.int32, (nb * TQ, 1), 0))
    bmask = (dist >= 0) & (dist < SWA_WINDOW)
    outs_b = []
    for h in range(N_KV):
        heads = [h * GROUP + g for g in range(GROUP)]
        s = masked(pre[h][4], bmask)
        sink = jnp.concatenate([sinks_ref[hh] * LOG2E + SLOPES_SWA[hh] * t_row.astype(F32) for hh in heads], axis=1)
        m = jnp.maximum(jnp.max(s, axis=0, keepdims=True), sink)
        acc = _dot(v_t(vsect(2, h), b_first, nb), jnp.exp2(s - m).astype(BF16))
        o_all = acc / (acc[HEAD_DIM:HEAD_DIM + 1, :] + jnp.exp2(sink - m))
        outs_b += [o_all[:, g * TQ:(g + 1) * TQ] for g in range(GROUP)]
    store_heads(ob_ref, outs_b)

    selected = [nsa_select(h, local[h][0]) for h in range(N_KV)]
    fronts = [nsa_diag(h, *selected[h]) + (local[h][1],) for h in range(N_KV)]
    nblk = KCHUNK // TQ

    def slc_step(c, carry):
        state = list(carry)
        subs = [(j, h) for j in range(KCHUNK // SUB) for h in range(N_KV)]
        scores = [_dot(ksect(0, h)[0, pl.ds(pl.multiple_of(c * KCHUNK + j * SUB, SUB), SUB), :], fronts[h][0])
                  for j, h in subs]
        for (j, h), s in zip(subs, scores):
            m_i, acc = state[h]
            m_new = jnp.maximum(m_i, jnp.max(s, axis=0, keepdims=True))
            pv = _dot(v_t(vsect(0, h), c * nblk + j * (SUB // TQ), SUB // TQ), jnp.exp2(s - m_new).astype(BF16))
            state[h] = (m_new, jnp.exp2(m_i - m_new) * acc + pv)
        return tuple(state)

    n_main = lax.shift_right_logical(q0 + (KCHUNK - 1), int(np.log2(KCHUNK)))
    swept = lax.fori_loop(0, n_main, slc_step, tuple((f[1], f[2]) for f in fronts))
    outs_a = []
    for h in range(N_KV):
        acc = swept[h][1]
        o_slc = acc / acc[HEAD_DIM:HEAD_DIM + 1, :]
        outs_a += [fronts[h][3][g] + gate(h * GROUP + g, 1) * o_slc[:, g * TQ:(g + 1) * TQ] for g in range(GROUP)]
    store_heads(oa_ref, outs_a)


def _attention(proj3, v_t, kc, vc_t, ovt, eye, sinks):
    bsz, seq, _ = proj3.shape
    n_cmp = seq // CMP_STRIDE - CMP_LEN // CMP_STRIDE + 1
    width = N_HEADS * HEAD_DIM
    consts = [ovt, eye]
    qspec = lambda col: pl.BlockSpec((1, TQ, width), lambda b, q, s: (b, q, col))
    cspec = lambda a, j: pl.BlockSpec((1, 1) + a.shape[2:], lambda b, q, s: (b, j, 0, 0, 0))
    in_specs = [qspec(0), qspec(COL_QB // width), cspec(kc, 0), cspec(vc_t, 1)]
    in_specs += [pl.BlockSpec((1, seq, LANES), lambda b, q, s, j=j: (b, 0, COL_KV // LANES + j))
                 for j in range(N_KV_SECT)]
    in_specs += [pl.BlockSpec((1, 1) + v_t.shape[2:], lambda b, q, s, j=j: (j, b, 0, 0, 0)) for j in range(N_KV_SECT)]
    in_specs += [pl.BlockSpec((1, TQ, LANES), lambda b, q, s: (b, q, COL_GN // LANES))]
    in_specs += [pl.BlockSpec(c.shape, lambda b, q, s: (0, 0)) for c in consts]
    ospec = pl.BlockSpec((1, TQ, width), lambda b, q, s: (b, q, 0))
    return pl.pallas_call(
        functools.partial(_attn_t_kernel, n_cmp=n_cmp),
        grid_spec=pltpu.PrefetchScalarGridSpec(
            num_scalar_prefetch=1, grid=(bsz, seq // TQ), in_specs=in_specs, out_specs=[ospec, ospec]),
        out_shape=[jax.ShapeDtypeStruct((bsz, seq, width), BF16)] * 2,
        compiler_params=pltpu.CompilerParams(dimension_semantics=("parallel", "parallel"),
                                             vmem_limit_bytes=VMEM_LIMIT),
        name="attn",
    )(sinks, proj3, proj3, kc, vc_t, *([proj3] * N_KV_SECT), *([v_t] * N_KV_SECT), proj3, *consts)


def _merge_kernel(x_ref, oa_ref, ob_ref, g1_ref, g2_ref, wg_ref, wa_ref, wb_ref, wo_ref,
                  wrh_ref, wrl_ref, br_ref, x2_ref, hn_ref, route_ref, cnt_ref):
    tm = x_ref.shape[0]

    @pl.when(pl.program_id(0) == 0)
    def _():
        cnt_ref[...] = jnp.zeros_like(cnt_ref)

    halves = [slice(i * (tm // 2), (i + 1) * (tm // 2)) for i in range(2)]
    xs = [x_ref[r, :] for r in halves]
    pre = []
    for r, x in zip(halves, xs):
        h = _rms(x, g1_ref[...]).astype(BF16)
        pre.append((_dot(h, wg_ref[...]), _dot(oa_ref[r, :], wa_ref[...]), _dot(ob_ref[r, :], wb_ref[...])))
    mixes = []
    for g_pre, a, b in pre:
        gm = jax.nn.sigmoid(g_pre)
        mixin = gm[:, :D_MODEL] * a + gm[:, D_MODEL:] * b
        mixes.append(_dot(mixin.astype(BF16), wo_ref[...]))
    logit_halves = []
    for r, x, mix in zip(halves, xs, mixes):
        x2 = x + mix
        x2_ref[r, :] = x2
        hn = _rms(x2, g2_ref[...])
        for j in range(ROW_TILE):
            hn_ref[pl.ds(r.start * ROW_TILE + j, tm // 2, stride=ROW_TILE), :] = hn[:, j * LANES:(j + 1) * LANES]
        hn_b = hn.astype(BF16)
        hn_lo = (hn - hn_b.astype(F32)).astype(BF16)
        logit_halves.append(_dot_nt(wrh_ref[...], hn_b) + _dot_nt(wrh_ref[...], hn_lo) + _dot_nt(wrl_ref[...], hn_b))
    bias = br_ref[...]
    logits = jnp.concatenate(logit_halves, axis=1) + jnp.concatenate([bias] * (tm // LANES), axis=1)
    row = lax.broadcasted_iota(jnp.int32, (LANES, 1), 0)
    rowf = row.astype(F32)
    big = float(LANES)
    top = lambda a: jnp.max(a, axis=0, keepdims=True)
    first = lambda hit: jnp.min(jnp.where(hit, rowf, big), axis=0, keepdims=True)
    is_g = (row >= N_EXPERTS) & (row < N_EXPERTS + N_GROUPS)
    gl = jnp.where(is_g, logits, NEG_INF)
    gmax = top(gl)
    grp = first(gl == gmax) - N_EXPERTS
    p_grp = 1.0 / jnp.sum(jnp.where(is_g, jnp.exp(gl - gmax), 0.0), axis=0, keepdims=True)
    in_grp = (rowf >= grp * EPG) & (rowf < grp * EPG + EPG)
    el = jnp.where(in_grp, logits, NEG_INF)
    v0 = top(el)
    i0 = first(el == v0)
    el1 = jnp.where(rowf == i0, NEG_INF, el)
    v1 = top(el1)
    i1 = first(el1 == v1)
    e1 = jnp.exp(v1 - v0)
    w0 = p_grp / (1.0 + e1)
    w1 = p_grp * e1 / (1.0 + e1)

    oh0 = jnp.where(rowf == i0, 1.0, 0.0)
    oh1 = jnp.where(rowf == i1, 1.0, 0.0)
    oh = oh0 + oh1
    r_i = lax.broadcasted_iota(jnp.int32, (tm, tm), 0)
    c_i = lax.broadcasted_iota(jnp.int32, (tm, tm), 1)
    earlier = jnp.where(r_i < c_i, 1.0, 0.0).astype(BF16)
    before = cnt_ref[...] + _dot(oh.astype(BF16), earlier)
    rank0 = jnp.sum(oh0 * before, axis=0, keepdims=True)
    rank1 = jnp.sum(oh1 * before, axis=0, keepdims=True)
    cnt_ref[...] = cnt_ref[...] + jnp.sum(oh, axis=1, keepdims=True)
    row8 = row[:8]
    route = jnp.zeros((8, tm), F32)
    for k, v in enumerate((i0, i1, rank0, rank1, w0, w1)):
        route = jnp.where(row8 == k, v, route)
    route_ref[0] = route


def _merge(x2d, oa, ob, g1, g2, wg, wa, wb, wo, wrh, wrl, br, tm):
    n = x2d.shape[0]
    width = N_HEADS * HEAD_DIM
    row = lambda w: pl.BlockSpec((tm, w), lambda i: (i, 0))
    full = lambda a: pl.BlockSpec(a.shape, lambda i: (0, 0))
    return pl.pallas_call(
        _merge_kernel,
        grid=(n // tm,),
        in_specs=[row(D_MODEL), row(width), row(width), full(g1), full(g2), full(wg), full(wa), full(wb),
                  full(wo), full(wrh), full(wrl), full(br)],
        out_specs=[row(D_MODEL), pl.BlockSpec((tm * ROW_TILE, LANES), lambda i: (i, 0)),
                   pl.BlockSpec((1, 8, tm), lambda i: (i, 0, 0)), pl.BlockSpec((LANES, 1), lambda i: (0, 0))],
        out_shape=[jax.ShapeDtypeStruct((n, D_MODEL), F32), jax.ShapeDtypeStruct((n * ROW_TILE, LANES), F32),
                   jax.ShapeDtypeStruct((n // tm, 8, tm), F32), jax.ShapeDtypeStruct((LANES, 1), F32)],
        compiler_params=pltpu.CompilerParams(dimension_semantics=("arbitrary",),
                                             vmem_limit_bytes=VMEM_LIMIT),
        name="merge",
    )(x2d, oa, ob, g1, g2, wg, wa, wb, wo, wrh, wrl, br)


def _slot_table_kernel(pos_ref, init_ref, tab_ref, sem, *, tt, n):
    step = pl.program_id(0)

    @pl.when(step == 0)
    def _():
        cp = pltpu.make_async_copy(init_ref, tab_ref, sem)
        cp.start()
        cp.wait()

    def chunk(c, row):
        base = pl.multiple_of(c * LANES, LANES)
        for k in range(LANES):
            tab_ref[pos_ref[0, 0, base + k]] = row + k * ROW_TILE
            tab_ref[pos_ref[0, 1, base + k]] = row + (n + k) * ROW_TILE
        return row + LANES * ROW_TILE

    lax.fori_loop(0, tt // LANES, chunk, step * (tt * ROW_TILE))


def _slot_table(pos, init, tt):
    nt = pos.shape[0]
    return pl.pallas_call(
        functools.partial(_slot_table_kernel, tt=tt, n=nt * tt),
        grid=(nt,),
        in_specs=[pl.BlockSpec((1, 2, tt), lambda i: (i, 0, 0), memory_space=pltpu.SMEM),
                  pl.BlockSpec(memory_space=pl.ANY)],
        out_specs=pl.BlockSpec(memory_space=pltpu.SMEM),
        out_shape=jax.ShapeDtypeStruct(init.shape, jnp.int32),
        scratch_shapes=[pltpu.SemaphoreType.DMA(())],
        compiler_params=pltpu.CompilerParams(dimension_semantics=("arbitrary",)),
        name="slot_table",
    )(pos, init)


def _moe_kernel(be_ref, nu_ref, tab_ref, tab_next_ref, hn_ref, wg_ref, wu_ref, wd_ref, out_ref,
                xin, yout, wg_s, wu_s, wd_s, gsem, ssem, *, tb, n):
    b = pl.program_id(0)
    n_used = nu_ref[0]
    live = b < n_used
    s = lax.rem(b, 2)

    def gather(tab, slot):
        for j in range(tb):
            t, rows = tab[0, 0, j], n * ROW_TILE
            src = jnp.bitwise_and(t, rows - 1) if rows & (rows - 1) == 0 else lax.rem(t, rows)
            pltpu.make_async_copy(hn_ref.at[_tile_at(src)], xin.at[slot, pl.ds(j * ROW_TILE, ROW_TILE)],
                                  gsem.at[slot]).start(priority=0)

    def wait_rows(sem_ref, slot):
        pltpu.make_async_copy(hn_ref.at[pl.ds(0, tb * ROW_TILE)], xin.at[slot], sem_ref.at[slot]).wait()

    @pl.when(b == 0)
    def _():
        yout[...] = jnp.zeros_like(yout)
        fills = [pltpu.make_async_copy(yout.at[k], out_ref.at[pl.ds((2 * n + k * tb) * ROW_TILE, tb * ROW_TILE)],
                                       ssem.at[k]) for k in range(2)]
        for cp in fills:
            cp.start()
        for cp in fills:
            cp.wait()
        gather(tab_ref, 0)

    @pl.when(live & ((b == 0) | (be_ref[b] != be_ref[jnp.maximum(b - 1, 0)])))
    def _():
        wg_s[...] = wg_ref[0].astype(BF16)
        wu_s[...] = wu_ref[0].astype(BF16)
        wd_s[...] = wd_ref[0].astype(BF16)

    @pl.when(live & (b >= 2))
    def _():
        wait_rows(ssem, s)

    def block(s):
        wait_rows(gsem, s)
        gather(tab_next_ref, 1 - s)
        half = tb // 2
        gu = []
        for i in range(2):
            xb = jnp.concatenate([xin[s, pl.ds(i * half * ROW_TILE + j, half, stride=ROW_TILE), :]
                                  for j in range(ROW_TILE)], axis=1).astype(BF16)
            gu.append((_dot(xb, wg_s[...]), _dot(xb, wu_s[...])))
        for i, (g, u) in enumerate(gu):
            y = _dot((g * jax.nn.sigmoid(g) * u).astype(BF16), wd_s[...])
            for j in range(ROW_TILE):
                yout[s, pl.ds(i * half * ROW_TILE + j, half, stride=ROW_TILE), :] = y[:, j * LANES:(j + 1) * LANES]

        for j in range(tb):
            pltpu.make_async_copy(yout.at[s, pl.ds(j * ROW_TILE, ROW_TILE)], out_ref.at[_tile_at(tab_ref[0, 0, j])],
                                  ssem.at[s]).start(priority=1)

    for parity in range(2):
        pl.when(live & (s == parity))(functools.partial(block, parity))

    @pl.when(live & (b == n_used - 1))
    def _():
        wait_rows(gsem, 1 - s)
        wait_rows(ssem, s)

        @pl.when(b >= 1)
        def _():
            wait_rows(ssem, 1 - s)


def _moe(block_e, n_used, tab, hn, wg, wu, wd, tb, n):
    nblk = tab.shape[0] // tb
    tab2 = tab.reshape(nblk, 1, tb)
    live = lambda b, be, nu: jnp.minimum(b, nu[0] - 1)
    nxt = lambda b, be, nu: jnp.minimum(b + 1, nu[0] - 1)
    wspec = lambda shape: pl.BlockSpec((1,) + shape, lambda b, be, nu: (be[live(b, be, nu)], 0, 0))
    return pl.pallas_call(
        functools.partial(_moe_kernel, tb=tb, n=n),
        grid_spec=pltpu.PrefetchScalarGridSpec(
            num_scalar_prefetch=2, grid=(nblk,),
            in_specs=[pl.BlockSpec((1, 1, tb), lambda b, be, nu: (live(b, be, nu), 0, 0), memory_space=pltpu.SMEM),
                      pl.BlockSpec((1, 1, tb), lambda b, be, nu: (nxt(b, be, nu), 0, 0), memory_space=pltpu.SMEM),
                      pl.BlockSpec(memory_space=pl.ANY),
                      wspec((D_MODEL, EXPERT_FF)), wspec((D_MODEL, EXPERT_FF)), wspec((EXPERT_FF, D_MODEL))],
            out_specs=pl.BlockSpec(memory_space=pl.ANY),
            scratch_shapes=[pltpu.VMEM((2, tb * ROW_TILE, LANES), F32), pltpu.VMEM((2, tb * ROW_TILE, LANES), F32),
                            pltpu.VMEM((D_MODEL, EXPERT_FF), BF16), pltpu.VMEM((D_MODEL, EXPERT_FF), BF16),
                            pltpu.VMEM((EXPERT_FF, D_MODEL), BF16),
                            pltpu.SemaphoreType.DMA((2,)), pltpu.SemaphoreType.DMA((2,))]),
        out_shape=jax.ShapeDtypeStruct(((2 * n + 2 * tb) * ROW_TILE, LANES), F32),
        compiler_params=pltpu.CompilerParams(dimension_semantics=("arbitrary",),
                                             vmem_limit_bytes=VMEM_LIMIT),
        name="moe",
    )(block_e, n_used, tab2, tab2, hn, wg, wu, wd)


def _final_kernel(x2_ref, w_ref, y0_ref, y1_ref, gf_ref, o_ref):
    tc = x2_ref.shape[0]
    w = w_ref[...]
    y = x2_ref[...] + (w[:, 0:1] * _tiles_to_rows(y0_ref, tc) + w[:, 1:2] * _tiles_to_rows(y1_ref, tc))
    o_ref[...] = _rms(y, gf_ref[...])


def _final(x2, w_slot, yslots, gf, tc):
    n = x2.shape[0]
    nt = n // tc
    return pl.pallas_call(
        _final_kernel,
        grid=(nt,),
        in_specs=[pl.BlockSpec((tc, D_MODEL), lambda i: (i, 0)),
                  pl.BlockSpec((tc, 2), lambda i: (i, 0)),
                  pl.BlockSpec((tc * ROW_TILE, LANES), lambda i: (i, 0)),
                  pl.BlockSpec((tc * ROW_TILE, LANES), lambda i: (nt + i, 0)),
                  pl.BlockSpec((1, D_MODEL), lambda i: (0, 0))],
        out_specs=pl.BlockSpec((tc, D_MODEL), lambda i: (i, 0)),
        out_shape=jax.ShapeDtypeStruct((n, D_MODEL), F32),
        compiler_params=pltpu.CompilerParams(dimension_semantics=("parallel",),
                                             vmem_limit_bytes=VMEM_LIMIT),
        name="final",
    )(x2, w_slot, yslots, yslots, gf)


def _tile_at(row):
    return pl.ds(pl.multiple_of(row, ROW_TILE), ROW_TILE)


def _overlap_matrix_t(seq):
    nc = seq // CMP_STRIDE - CMP_LEN // CMP_STRIDE + 1
    ns = seq // SLC_LEN
    c0 = np.arange(nc) * CMP_STRIDE
    s0 = np.arange(ns) * SLC_LEN
    ov = np.clip(np.minimum(c0[:, None] + CMP_LEN, s0[None, :] + SLC_LEN)
                 - np.maximum(c0[:, None], s0[None, :]), 0, None) / CMP_LEN
    out = np.zeros((LANES, LANES), np.float32)
    out[FEAT_SEL:FEAT_SEL + ns, :nc] = ov.T
    return jnp.asarray(out, BF16)


def _position_features(seq):
    pos = np.arange(seq)
    fk = np.zeros((seq, LANES), np.float32)
    fk[:, FEAT_POS:FEAT_POS + 3] = (pos // SLC_LEN)[:, None]
    fk[:, FEAT_POS + 3:FEAT_POS + 6] = (pos % SLC_LEN)[:, None]
    fk[pos, FEAT_SEL + pos // SLC_LEN] = 1.0
    return jnp.asarray(fk)


def _pick_tile(n, pref):
    t = pref
    while n % t:
        t //= 2
    return t


def kernel(x, norm_mix_g, w_in, cmp_pos_k, cmp_w1_k, cmp_w2_k, cmp_pos_v, cmp_w1_v, cmp_w2_v, sinks, w_a, w_b,
           w_o, norm_ffn_g, w_group, b_group, w_expert, b_expert, w_gate_e, w_up_e, w_down_e, norm_final_g):
    bsz, seq, _ = x.shape
    n = bsz * seq
    assert TQ == LANES and seq % KCHUNK == 0 and seq // SLC_LEN <= N_SLC_BLK and seq // CMP_STRIDE <= LANES
    assert seq >= (NSA_WINDOW // TQ + 1) * TQ and w_in.shape[0] == 1
    x2d = x.reshape(n, D_MODEL)

    w = w_in[0]
    scale = HEAD_DIM ** -0.5 * LOG2E
    nsa_w, kvw = N_HEADS * HEAD_DIM, N_KV * HEAD_DIM
    o_qa, o_kva, o_gn = 0, nsa_w, nsa_w + 6 * kvw
    o_qb = o_gn + 3 * N_HEADS
    o_kvb = o_qb + nsa_w
    o_gm = o_kvb + 2 * kvw
    pair = lambda off: w[:, off:off + kvw]
    w_attn = jnp.concatenate(
        [w[:, o_qa:o_qa + nsa_w] * scale, w[:, o_qb:o_qb + nsa_w] * scale,
         pair(o_kva + 2 * kvw), pair(o_kva + 4 * kvw), pair(o_kvb),
         w[:, o_gn:o_gn + 3 * N_HEADS], jnp.zeros((D_MODEL, LANES - 3 * N_HEADS), F32),
         pair(o_kva + 3 * kvw), pair(o_kva + 5 * kvw), pair(o_kvb + kvw),
         w[:, o_kva:o_kva + 2 * kvw]], axis=1).astype(BF16)
    w_gm = w[:, o_gm:o_gm + 2 * D_MODEL].astype(BF16)

    tm = _pick_tile(seq, 512)
    feat_k = _position_features(seq)
    eye = jnp.eye(LANES, dtype=BF16)
    swap = jnp.roll(eye, HEAD_DIM, axis=1)
    proj, v_t, cmp_in = _proj(x2d, norm_mix_g[0][None], w_attn, feat_k, eye, swap, tm)
    proj3 = proj.reshape(bsz, seq, PROJ_W)

    nch = seq // CMP_STRIDE
    pos = jnp.stack([cmp_pos_k[0], cmp_pos_v[0]])
    pos = jnp.broadcast_to(pos[:, :, None, :], (2, CMP_LEN, N_KV, HEAD_DIM))
    pos_a = pos[:, :CMP_STRIDE].reshape(2, 1, CMP_STRIDE * kvw)
    pos_b = pos[:, CMP_STRIDE:].reshape(2, 1, CMP_STRIDE * kvw)
    w1 = jnp.stack([cmp_w1_k[0], cmp_w1_v[0]]).reshape(2, CMP_LEN, HEAD_DIM, CMP_HIDDEN)
    zero = jnp.zeros_like(w1)
    w1 = jnp.stack([jnp.concatenate([w1, zero], axis=-1), jnp.concatenate([zero, w1], axis=-1)], axis=2)
    w1 = w1.reshape(2, CMP_LEN * kvw, N_KV * CMP_HIDDEN).astype(BF16)
    w2 = jnp.pad(jnp.stack([cmp_w2_k[0], cmp_w2_v[0]]), ((0, 0), (0, 0), (0, LANES - HEAD_DIM))).astype(BF16)
    kvc, kvc_t = _compress(cmp_in.reshape(bsz, seq, CMP_W), pos_a, pos_b, w1[:, :CMP_STRIDE * kvw],
                           w1[:, CMP_STRIDE * kvw:], w2, jnp.swapaxes(w2, 1, 2))
    kvc = jnp.pad(kvc, ((0, 0), (0, 0), (0, 0), (0, LANES - nch), (0, 0)))
    kvc_t = jnp.pad(kvc_t, ((0, 0), (0, 0), (0, 0), (0, 0), (0, LANES - nch)))

    o_a, o_b = _attention(proj3, v_t, kvc, kvc_t, _overlap_matrix_t(seq), eye, sinks[0])

    w_r = jnp.concatenate([w_expert[0], w_group[0],
                           jnp.zeros((D_MODEL, LANES - N_EXPERTS - N_GROUPS), F32)], axis=1)
    w_r = w_r.T
    w_rh = w_r.astype(BF16)
    w_rl = (w_r - w_rh.astype(F32)).astype(BF16)
    b_r = jnp.concatenate([b_expert[0], b_group[0], jnp.zeros((LANES - N_EXPERTS - N_GROUPS,), F32)])
    b_r = b_r[:, None] * jnp.ones((1, LANES), F32)
    tt = _pick_tile(n, 512)
    x2, hn, route, counts = _merge(
        x2d, o_a.reshape(n, nsa_w), o_b.reshape(n, nsa_w), norm_mix_g[0][None], norm_ffn_g[0][None], w_gm,
        w_a[0].astype(BF16), w_b[0].astype(BF16), w_o[0].astype(BF16), w_rh, w_rl, b_r, tt)

    tb = 256
    nblk = -(-(2 * n + N_EXPERTS * (tb - 1)) // tb)
    cnt = counts[:N_EXPERTS, 0].astype(jnp.int32)
    padded = (cnt + tb - 1) // tb * tb
    pad_end = jnp.cumsum(padded)
    pad_start = pad_end - padded
    block_e = jnp.minimum(jnp.sum(pad_end[None, :] <= (jnp.arange(nblk) * tb)[:, None], axis=1), N_EXPERTS - 1)
    n_used = (pad_end[-1:] // tb).astype(jnp.int32)
    tab = route[:, 0:4, :].astype(jnp.int32)
    eids = tab[:, 0:2, :]
    start_of = sum(jnp.where(eids == e, pad_start[e], 0) for e in range(N_EXPERTS))
    pos = start_of + tab[:, 2:4, :]
    w_slot = jnp.swapaxes(route[:, 4:6, :], 1, 2).reshape(n, 2)
    spare = (2 * n + jnp.arange(nblk * tb, dtype=jnp.int32) % (2 * tb)) * ROW_TILE
    slot_tab = _slot_table(pos, spare, tt)
    y_slots = _moe(block_e.astype(jnp.int32), n_used, slot_tab, hn, w_gate_e[0], w_up_e[0], w_down_e[0], tb, n)
    out = _final(x2, w_slot, y_slots, norm_final_g[None], tt)
    return out.reshape(bsz, seq, D_MODEL)
```

```python
import functools

import numpy as np
import jax
import jax.numpy as jnp
from jax import lax
from jax.experimental import pallas as pl
from jax.experimental.pallas import tpu as pltpu

F32 = jnp.float32
BF16 = jnp.bfloat16

D_MODEL = 1024
HEAD_DIM = 64
N_HEADS = 8
N_KV = 2
GROUP = N_HEADS // N_KV
CMP_LEN = 32
CMP_STRIDE = 16
CMP_HIDDEN = 256
SLC_LEN = 64
SLC_TOPK = 8
NSA_WINDOW = 256
SWA_WINDOW = 128
N_GROUPS = 4
EPG = 8
N_EXPERTS = N_GROUPS * EPG
EXPERT_FF = 256
RMS_EPS = 1e-6
NEG_INF = -1e30
FORCE_SCORE = 1e9

LANES = 128
TQ = 128
TILES_PER_STEP = 4
KCHUNK = 512
SUB = 128
N_SLC_BLK = LANES // 4
FEAT_POS = HEAD_DIM
FEAT_SEL = HEAD_DIM + 6
ROWS_LO, ROWS_HI = 64, 104
N_KV_SECT = 6
COL_QB = 512
COL_KV = 1024
COL_GN = COL_KV + N_KV_SECT * LANES
PROJ_W = COL_GN + LANES
CMP_W = 2 * LANES
ROW_TILE = D_MODEL // LANES
VMEM_LIMIT = 56 * 1024 * 1024


LOG2E = float(np.log2(np.e))


def _alibi_slopes():
    n = 2 * N_HEADS
    s = 2.0 ** (-8.0 * np.arange(1, n + 1) / n) * LOG2E
    return [float(v) for v in s[:N_HEADS]], [float(v) for v in s[N_HEADS:]]


SLOPES_SWA, SLOPES_NSA = _alibi_slopes()


def _bf16_pieces(v):
    out, rem = [], np.float32(v)
    for _ in range(3):
        p = np.float32(np.asarray(rem, np.float32).astype(BF16).astype(np.float32))
        out.append(float(p))
        rem = np.float32(rem - p)
    return out


def _rms(x, g):
    return x * lax.rsqrt(jnp.mean(x * x, axis=-1, keepdims=True) + RMS_EPS) * g


def _dot(a, b):
    return jnp.dot(a, b, preferred_element_type=F32)


def _tiles_to_rows(ref, n, lead=()):
    return jnp.concatenate([ref[lead + (pl.ds(j, n, stride=ROW_TILE), slice(None))] for j in range(ROW_TILE)], axis=1)


def _rows_to_tiles(ref, val):
    n = val.shape[0]
    for j in range(ROW_TILE):
        ref[pl.ds(j, n, stride=ROW_TILE), :] = val[:, j * LANES:(j + 1) * LANES]


def _dot_nt(a, b):
    return lax.dot_general(a, b, (((1,), (1,)), ((), ())), preferred_element_type=F32)


def _proj_kernel(x_ref, g_ref, w_ref, fk_ref, eye_ref, swap_ref, o_ref, vt_ref, cmp_ref):
    tm = x_ref.shape[0]
    n_br = N_KV_SECT // N_KV
    col_gn = COL_KV + n_br * LANES
    col_v = col_gn + LANES
    h = _rms(x_ref[...], g_ref[...]).astype(BF16)
    res = _dot(h, w_ref[...])
    o_ref[:, :COL_KV] = res[:, :COL_KV].astype(o_ref.dtype)
    o_ref[:, COL_GN:] = res[:, col_gn:col_v].astype(o_ref.dtype)
    low = lax.broadcasted_iota(jnp.int32, (1, LANES), 1) < HEAD_DIM
    ones_row = jnp.where(lax.broadcasted_iota(jnp.int32, (LANES - HEAD_DIM, tm), 0) == 0, 1.0, 0.0)
    for br in range(n_br):
        k_pair = res[:, COL_KV + br * LANES:COL_KV + (br + 1) * LANES]
        heads = (k_pair, _dot(k_pair.astype(BF16), swap_ref[...]))
        for hd in range(N_KV):
            c0 = COL_KV + (br * N_KV + hd) * LANES
            o_ref[:, c0:c0 + LANES] = jnp.where(low, heads[hd], fk_ref[...]).astype(o_ref.dtype)
        v_pair_t = _dot_nt(eye_ref[...], res[:, col_v + br * LANES:col_v + (br + 1) * LANES].astype(BF16))
        for hd in range(N_KV):
            v_t = jnp.concatenate([v_pair_t[hd * HEAD_DIM:(hd + 1) * HEAD_DIM], ones_row], axis=0)
            for k in range(tm // LANES):
                vt_ref[br * N_KV + hd, 0, k] = v_t[:, k * LANES:(k + 1) * LANES].astype(vt_ref.dtype)
    cmp_ref[...] = res[:, col_v + n_br * LANES:]


def _proj(x2d, g, w, feat_k, eye, swap, tm):
    n = x2d.shape[0]
    seq = feat_k.shape[0]
    nper = seq // tm
    kb = tm // LANES
    return pl.pallas_call(
        _proj_kernel,
        grid=(n // tm,),
        in_specs=[pl.BlockSpec((tm, D_MODEL), lambda i: (i, 0)),
                  pl.BlockSpec((1, D_MODEL), lambda i: (0, 0)),
                  pl.BlockSpec(w.shape, lambda i: (0, 0)),
                  pl.BlockSpec((tm, LANES), lambda i: (i % nper, 0)),
                  pl.BlockSpec((LANES, LANES), lambda i: (0, 0)),
                  pl.BlockSpec((LANES, LANES), lambda i: (0, 0))],
        out_specs=[pl.BlockSpec((tm, PROJ_W), lambda i: (i, 0)),
                   pl.BlockSpec((N_KV_SECT, 1, kb, LANES, LANES), lambda i: (0, i // nper, i % nper, 0, 0)),
                   pl.BlockSpec((tm, CMP_W), lambda i: (i, 0))],
        out_shape=[jax.ShapeDtypeStruct((n, PROJ_W), BF16),
                   jax.ShapeDtypeStruct((N_KV_SECT, n // seq, seq // LANES, LANES, LANES), BF16),
                   jax.ShapeDtypeStruct((n, CMP_W), F32)],
        compiler_params=pltpu.CompilerParams(dimension_semantics=("parallel",),
                                             vmem_limit_bytes=VMEM_LIMIT),
        name="proj",
    )(x2d, g, w, feat_k, eye, swap)


def _compress_kernel(x_ref, pa_ref, pb_ref, w1a_ref, w1b_ref, w2_ref, w2t_ref, o_ref, ot_ref, *, nch):
    r = jnp.concatenate([x_ref[0, pl.ds(j, nch, stride=CMP_STRIDE), :] for j in range(CMP_STRIDE)], axis=1)
    a = _dot((r + pa_ref[0]).astype(BF16), w1a_ref[0])
    b = _dot((r + pb_ref[0]).astype(BF16), w1b_ref[0])
    hid = a + pltpu.roll(b, nch - 1, 0)
    hid = hid * jax.nn.sigmoid(hid)
    for h in range(N_KV):
        hid_h = hid[:, h * CMP_HIDDEN:(h + 1) * CMP_HIDDEN].astype(BF16)
        o_ref[0, 0, h] = _dot(hid_h, w2_ref[0]).astype(o_ref.dtype)
        ot_ref[0, 0, h] = _dot_nt(w2t_ref[0], hid_h).astype(ot_ref.dtype)


def _compress(cmp3, pos_a, pos_b, w1a, w1b, w2, w2t):
    bsz, seq, _ = cmp3.shape
    nch = seq // CMP_STRIDE
    wspec = lambda a: pl.BlockSpec((1,) + a.shape[1:], lambda b, j: (j, 0, 0))
    return pl.pallas_call(
        functools.partial(_compress_kernel, nch=nch),
        grid=(bsz, 2),
        in_specs=[pl.BlockSpec((1, seq, LANES), lambda b, j: (b, 0, j)),
                  wspec(pos_a), wspec(pos_b), wspec(w1a), wspec(w1b), wspec(w2), wspec(w2t)],
        out_specs=[pl.BlockSpec((1, 1, N_KV, nch, LANES), lambda b, j: (b, j, 0, 0, 0)),
                   pl.BlockSpec((1, 1, N_KV, LANES, nch), lambda b, j: (b, j, 0, 0, 0))],
        out_shape=[jax.ShapeDtypeStruct((bsz, 2, N_KV, nch, LANES), BF16),
                   jax.ShapeDtypeStruct((bsz, 2, N_KV, LANES, nch), BF16)],
        compiler_params=pltpu.CompilerParams(dimension_semantics=("parallel", "parallel"),
                                             vmem_limit_bytes=VMEM_LIMIT),
        name="compress",
    )(cmp3, pos_a, pos_b, w1a, w1b, w2, w2t)


def _attn_t_kernel(*refs, n_cmp):
    tiles = [_attn_tile(u, *refs, n_cmp=n_cmp) for u in range(TILES_PER_STEP)]
    while all([next(t) is PHASED for t in tiles]):
        pass
    for t in tiles:
        for _ in t:
            pass


PHASED, SWEEP = "phase done", "ready for the sweep"


def _attn_tile(u, sinks_ref, qa_ref, qb_ref, kc_ref, vct_ref, *rest, n_cmp):
    rows = slice(u * TQ, (u + 1) * TQ)
    n_br = 3
    ks = rest[:n_br * N_KV]
    vts = rest[n_br * N_KV:2 * n_br * N_KV]
    gn_ref, ovt_ref, eye_ref, oa_ref, ob_ref = rest[2 * n_br * N_KV:]
    ksect = lambda branch, h: ks[branch * N_KV + h]
    vsect = lambda branch, h: vts[branch * N_KV + h]
    qi = pl.program_id(1) * TILES_PER_STEP + u
    q0 = pl.multiple_of(qi * TQ, TQ)
    lane = lax.broadcasted_iota(jnp.int32, (1, TQ), 1)
    sub = lax.broadcasted_iota(jnp.int32, (LANES, 1), 0)
    t_row = q0 + lane
    eye = eye_ref[...]
    gates = jax.nn.sigmoid(_dot_nt(eye, gn_ref[0, rows, :]))
    gate = lambda hh, c: gates[3 * hh + c:3 * hh + c + 1, :]
    sub40 = sub[ROWS_LO:ROWS_HI]
    blk = sub40 - FEAT_SEL
    in_rng = (blk >= 0) & (blk < N_SLC_BLK)
    is_pos = (sub40 >= FEAT_POS) & (sub40 < FEAT_SEL)
    zeros_lo = jnp.zeros((LANES - ROWS_HI, TQ), F32)

    def q_t(ref, hh):
        both = _dot_nt(eye, ref[0, rows, (hh // 2) * LANES:(hh // 2 + 1) * LANES])
        return both[(hh % 2) * HEAD_DIM:(hh % 2 + 1) * HEAD_DIM]

    def slope_col(slope):
        hi, mid, lo = _bf16_pieces(slope)
        col = jnp.zeros(sub40.shape, F32)
        for i, v in enumerate([SLC_LEN * hi, SLC_LEN * mid, SLC_LEN * lo, hi, mid, lo]):
            col = jnp.where(sub40 == FEAT_POS + i, v, col)
        return col

    def q_aug_t(q_ts, tails):
        return jnp.concatenate([jnp.concatenate([q, jnp.broadcast_to(t, (ROWS_HI - ROWS_LO, TQ)), zeros_lo], axis=0)
                                for q, t in zip(q_ts, tails)], axis=1).astype(BF16)

    def v_t(ref, first, n):
        return jnp.concatenate([ref[0, 0, first + j] for j in range(n)], axis=1)

    def masked(s, mask):
        return jnp.concatenate([jnp.where(mask, s[:, g * TQ:(g + 1) * TQ], NEG_INF) for g in range(GROUP)], axis=1)

    def store_heads(ref, outs):
        for p in range(N_HEADS // 2):
            pair = jnp.concatenate([outs[2 * p][:HEAD_DIM], outs[2 * p + 1][:HEAD_DIM]], axis=0).astype(BF16)
            ref[0, rows, p * LANES:(p + 1) * LANES] = _dot_nt(eye, pair).astype(ref.dtype)

    nw = NSA_WINDOW // TQ + 1
    w_first = jnp.maximum(qi - (nw - 1), 0)
    w_start = pl.multiple_of(w_first * TQ, TQ)
    nb = (SWA_WINDOW - 1 + TQ - 1) // TQ + 1
    b_first = jnp.maximum(qi - (nb - 1), 0)
    b_start = pl.multiple_of(b_first * TQ, TQ)
    pre = []
    for h in range(N_KV):
        heads = [h * GROUP + g for g in range(GROUP)]
        q_ts = [q_t(qa_ref, hh) for hh in heads]
        scol = [slope_col(SLOPES_NSA[hh]) for hh in heads]
        s_cmp = _dot(kc_ref[0, 0, h], q_aug_t(q_ts, [jnp.zeros((1, 1), F32)] * GROUP))
        s_win = _dot(ksect(1, h)[0, pl.ds(w_start, nw * TQ), :], q_aug_t(q_ts, scol))
        q_b = q_aug_t([q_t(qb_ref, hh) for hh in heads], [slope_col(SLOPES_SWA[hh]) for hh in heads])
        s_swa = _dot(ksect(2, h)[0, pl.ds(b_start, nb * TQ), :], q_b)
        pre.append((q_ts, scol, s_cmp, s_win, s_swa))

    yield PHASED
    def nsa_local(h):
        heads = [h * GROUP + g for g in range(GROUP)]
        _, _, s_all, s_win, _ = pre[h]

        end_c = sub * CMP_STRIDE + (CMP_LEN - 1)
        cmask = (t_row >= end_c) & (sub < n_cmp)
        ps = []
        for g in range(GROUP):
            s = s_all[:, g * TQ:(g + 1) * TQ] + SLOPES_NSA[heads[g]] * end_c.astype(F32)
            s = jnp.where(cmask, s, NEG_INF)
            m = jnp.max(s, axis=0, keepdims=True)
            e = jnp.where(cmask, jnp.exp2(s - m), 0.0)
            z = jnp.sum(e, axis=0, keepdims=True)
            ps.append(e / jnp.where(z > 0, z, 1.0))
        o_cmp = _dot(vct_ref[0, 0, h], jnp.concatenate(ps, axis=1).astype(BF16))

        dist = t_row - (w_start + lax.broadcasted_iota(jnp.int32, (nw * TQ, 1), 0))
        s = masked(s_win, (dist >= 0) & (dist < NSA_WINDOW))
        m = jnp.max(s, axis=0, keepdims=True)
        acc = _dot(v_t(vsect(1, h), w_first, nw), jnp.exp2(s - m).astype(BF16))
        o_win = acc / acc[HEAD_DIM:HEAD_DIM + 1, :]

        psum = ps[0] + ps[1] + ps[2] + ps[3]
        p_hi = psum.astype(BF16)
        p_lo = (psum - p_hi.astype(F32)).astype(BF16)
        imp = (_dot(ovt_ref[...], p_hi) + _dot(ovt_ref[...], p_lo))[ROWS_LO:ROWS_HI]
        part = [gate(hh, 0) * o_cmp[:, g * TQ:(g + 1) * TQ] + gate(hh, 2) * o_win[:, g * TQ:(g + 1) * TQ]
                for g, hh in enumerate(heads)]
        return imp, part

    def nsa_select(h, imp):
        q_ts, scol = pre[h][:2]
        cur = lax.shift_right_logical(t_row, int(np.log2(SLC_LEN)))
        valid = in_rng & (blk * SLC_LEN <= t_row)
        forced = in_rng & ((blk == 0) | (blk == cur) | (blk == cur - 1))
        score = jnp.where(forced, FORCE_SCORE, jnp.where(valid, imp, NEG_INF))
        rank = jnp.zeros(score.shape, F32)
        for i in range(N_SLC_BLK):
            r = FEAT_SEL - ROWS_LO + i
            si = score[r:r + 1, :]
            rank = rank + jnp.where((si > score) | ((si == score) & (blk > i)), 1.0, 0.0)
        sel = in_rng & (rank < SLC_TOPK) & (score > 0.5 * NEG_INF)
        bias_diag = jnp.where(in_rng & jnp.logical_not(sel), NEG_INF, 0.0)
        bias_main = jnp.where(in_rng & jnp.logical_not(sel & (blk < 2 * qi)), NEG_INF, 0.0)

        q_d = q_aug_t(q_ts, [jnp.where(is_pos, scol[g], bias_diag) for g in range(GROUP)])
        q_m = q_aug_t(q_ts, [jnp.where(is_pos, scol[g], bias_main) for g in range(GROUP)])
        return q_m, _dot(ksect(0, h)[0, pl.ds(q0, TQ), :], q_d)

    def nsa_diag(h, q_m, s_diag):
        s = masked(s_diag, q0 + sub <= t_row)
        m0 = jnp.max(s, axis=0, keepdims=True)
        acc0 = _dot(v_t(vsect(0, h), qi, 1), jnp.exp2(s - m0).astype(BF16))
        return q_m, m0, acc0

    local = [nsa_local(h) for h in range(N_KV)]
    yield PHASED

    dist = t_row - (b_start + lax.broadcasted_iota(jnp.int32, (nb * TQ, 1), 0))
    bmask = (dist >= 0) & (dist < SWA_WINDOW)
    outs_b = []
    for h in range(N_KV):
        heads = [h * GROUP + g for g in range(GROUP)]
        s = masked(pre[h][4], bmask)
        sink = jnp.concatenate([sinks_ref[hh] * LOG2E + SLOPES_SWA[hh] * t_row.astype(F32) for hh in heads], axis=1)
        m = jnp.maximum(jnp.max(s, axis=0, keepdims=True), sink)
        acc = _dot(v_t(vsect(2, h), b_first, nb), jnp.exp2(s - m).astype(BF16))
        o_all = acc / (acc[HEAD_DIM:HEAD_DIM + 1, :] + jnp.exp2(sink - m))
        outs_b += [o_all[:, g * TQ:(g + 1) * TQ] for g in range(GROUP)]
    store_heads(ob_ref, outs_b)

    yield PHASED
    selected = [nsa_select(h, local[h][0]) for h in range(N_KV)]
    yield PHASED
    fronts = [nsa_diag(h, *selected[h]) + (local[h][1],) for h in range(N_KV)]
    nblk = KCHUNK // TQ

    def slc_step(c, carry):
        state = list(carry)
        subs = [(j, h) for j in range(KCHUNK // SUB) for h in range(N_KV)]
        scores = [_dot(ksect(0, h)[0, pl.ds(pl.multiple_of(c * KCHUNK + j * SUB, SUB), SUB), :], fronts[h][0])
                  for j, h in subs]
        for (j, h), s in zip(subs, scores):
            m_i, acc = state[h]
            m_new = jnp.maximum(m_i, jnp.max(s, axis=0, keepdims=True))
            pv = _dot(v_t(vsect(0, h), c * nblk + j * (SUB // TQ), SUB // TQ), jnp.exp2(s - m_new).astype(BF16))
            state[h] = (m_new, jnp.exp2(m_i - m_new) * acc + pv)
        return tuple(state)

    yield SWEEP
    n_main = lax.shift_right_logical(q0 + (KCHUNK - 1), int(np.log2(KCHUNK)))
    swept = lax.fori_loop(0, n_main, slc_step, tuple((f[1], f[2]) for f in fronts))
    outs_a = []
    for h in range(N_KV):
        acc = swept[h][1]
        o_slc = acc / acc[HEAD_DIM:HEAD_DIM + 1, :]
        outs_a += [fronts[h][3][g] + gate(h * GROUP + g, 1) * o_slc[:, g * TQ:(g + 1) * TQ] for g in range(GROUP)]
    store_heads(oa_ref, outs_a)


def _attention(proj3, v_t, kc, vc_t, ovt, eye, sinks):
    bsz, seq, _ = proj3.shape
    n_cmp = seq // CMP_STRIDE - CMP_LEN // CMP_STRIDE + 1
    width = N_HEADS * HEAD_DIM
    consts = [ovt, eye]
    tq = TILES_PER_STEP * TQ
    qspec = lambda col: pl.BlockSpec((1, tq, width), lambda b, q, s: (b, q, col))
    cspec = lambda a, j: pl.BlockSpec((1, 1) + a.shape[2:], lambda b, q, s: (b, j, 0, 0, 0))
    in_specs = [qspec(0), qspec(COL_QB // width), cspec(kc, 0), cspec(vc_t, 1)]
    in_specs += [pl.BlockSpec((1, seq, LANES), lambda b, q, s, j=j: (b, 0, COL_KV // LANES + j))
                 for j in range(N_KV_SECT)]
    in_specs += [pl.BlockSpec((1, 1) + v_t.shape[2:], lambda b, q, s, j=j: (j, b, 0, 0, 0)) for j in range(N_KV_SECT)]
    in_specs += [pl.BlockSpec((1, tq, LANES), lambda b, q, s: (b, q, COL_GN // LANES))]
    in_specs += [pl.BlockSpec(c.shape, lambda b, q, s: (0, 0)) for c in consts]
    ospec = pl.BlockSpec((1, tq, width), lambda b, q, s: (b, q, 0))
    return pl.pallas_call(
        functools.partial(_attn_t_kernel, n_cmp=n_cmp),
        grid_spec=pltpu.PrefetchScalarGridSpec(
            num_scalar_prefetch=1, grid=(bsz, seq // tq), in_specs=in_specs, out_specs=[ospec, ospec]),
        out_shape=[jax.ShapeDtypeStruct((bsz, seq, width), BF16)] * 2,
        compiler_params=pltpu.CompilerParams(dimension_semantics=("parallel", "parallel"),
                                             vmem_limit_bytes=VMEM_LIMIT),
        name="attn",
    )(sinks, proj3, proj3, kc, vc_t, *([proj3] * N_KV_SECT), *([v_t] * N_KV_SECT), proj3, *consts)


def _merge_kernel(x_ref, oa_ref, ob_ref, g1_ref, g2_ref, wg_ref, wa_ref, wb_ref, wo_ref,
                  wrh_ref, wrl_ref, br_ref, x2_ref, hn_ref, route_ref, cnt_ref):
    tm = x_ref.shape[0]

    @pl.when(pl.program_id(0) == 0)
    def _():
        cnt_ref[...] = jnp.zeros_like(cnt_ref)

    halves = [slice(i * (tm // 2), (i + 1) * (tm // 2)) for i in range(2)]
    xs = [x_ref[r, :] for r in halves]
    pre = []
    for r, x in zip(halves, xs):
        h = _rms(x, g1_ref[...]).astype(BF16)
        pre.append((_dot(h, wg_ref[...]), _dot(oa_ref[r, :], wa_ref[...]), _dot(ob_ref[r, :], wb_ref[...])))
    mixes = []
    for g_pre, a, b in pre:
        gm = jax.nn.sigmoid(g_pre)
        mixin = gm[:, :D_MODEL] * a + gm[:, D_MODEL:] * b
        mixes.append(_dot(mixin.astype(BF16), wo_ref[...]))
    logit_halves = []
    for r, x, mix in zip(halves, xs, mixes):
        x2 = x + mix
        x2_ref[r, :] = x2
        hn = _rms(x2, g2_ref[...])
        for j in range(ROW_TILE):
            hn_ref[pl.ds(r.start * ROW_TILE + j, tm // 2, stride=ROW_TILE), :] = hn[:, j * LANES:(j + 1) * LANES]
        hn_b = hn.astype(BF16)
        hn_lo = (hn - hn_b.astype(F32)).astype(BF16)
        logit_halves.append(_dot_nt(wrh_ref[...], hn_b) + _dot_nt(wrh_ref[...], hn_lo) + _dot_nt(wrl_ref[...], hn_b))
    bias = br_ref[...]
    logits = jnp.concatenate(logit_halves, axis=1) + jnp.concatenate([bias] * (tm // LANES), axis=1)
    row = lax.broadcasted_iota(jnp.int32, (LANES, 1), 0)
    rowf = row.astype(F32)
    big = float(LANES)
    top = lambda a: jnp.max(a, axis=0, keepdims=True)
    first = lambda hit: jnp.min(jnp.where(hit, rowf, big), axis=0, keepdims=True)
    is_g = (row >= N_EXPERTS) & (row < N_EXPERTS + N_GROUPS)
    gl = jnp.where(is_g, logits, NEG_INF)
    gmax = top(gl)
    grp = first(gl == gmax) - N_EXPERTS
    p_grp = 1.0 / jnp.sum(jnp.where(is_g, jnp.exp(gl - gmax), 0.0), axis=0, keepdims=True)
    in_grp = (rowf >= grp * EPG) & (rowf < grp * EPG + EPG)
    el = jnp.where(in_grp, logits, NEG_INF)
    v0 = top(el)
    i0 = first(el == v0)
    el1 = jnp.where(rowf == i0, NEG_INF, el)
    v1 = top(el1)
    i1 = first(el1 == v1)
    e1 = jnp.exp(v1 - v0)
    w0 = p_grp / (1.0 + e1)
    w1 = p_grp * e1 / (1.0 + e1)

    oh0 = jnp.where(rowf == i0, 1.0, 0.0)
    oh1 = jnp.where(rowf == i1, 1.0, 0.0)
    oh = oh0 + oh1
    r_i = lax.broadcasted_iota(jnp.int32, (tm, tm), 0)
    c_i = lax.broadcasted_iota(jnp.int32, (tm, tm), 1)
    earlier = jnp.where(r_i < c_i, 1.0, 0.0).astype(BF16)
    before = cnt_ref[...] + _dot(oh.astype(BF16), earlier)
    rank0 = jnp.sum(oh0 * before, axis=0, keepdims=True)
    rank1 = jnp.sum(oh1 * before, axis=0, keepdims=True)
    cnt_ref[...] = cnt_ref[...] + jnp.sum(oh, axis=1, keepdims=True)
    row8 = row[:8]
    route = jnp.zeros((8, tm), F32)
    for k, v in enumerate((i0, i1, rank0, rank1, w0, w1)):
        route = jnp.where(row8 == k, v, route)
    route_ref[0] = route


def _merge(x2d, oa, ob, g1, g2, wg, wa, wb, wo, wrh, wrl, br, tm):
    n = x2d.shape[0]
    width = N_HEADS * HEAD_DIM
    row = lambda w: pl.BlockSpec((tm, w), lambda i: (i, 0))
    full = lambda a: pl.BlockSpec(a.shape, lambda i: (0, 0))
    return pl.pallas_call(
        _merge_kernel,
        grid=(n // tm,),
        in_specs=[row(D_MODEL), row(width), row(width), full(g1), full(g2), full(wg), full(wa), full(wb),
                  full(wo), full(wrh), full(wrl), full(br)],
        out_specs=[row(D_MODEL), pl.BlockSpec((tm * ROW_TILE, LANES), lambda i: (i, 0)),
                   pl.BlockSpec((1, 8, tm), lambda i: (i, 0, 0)), pl.BlockSpec((LANES, 1), lambda i: (0, 0))],
        out_shape=[jax.ShapeDtypeStruct((n, D_MODEL), F32), jax.ShapeDtypeStruct((n * ROW_TILE, LANES), F32),
                   jax.ShapeDtypeStruct((n // tm, 8, tm), F32), jax.ShapeDtypeStruct((LANES, 1), F32)],
        compiler_params=pltpu.CompilerParams(dimension_semantics=("arbitrary",),
                                             vmem_limit_bytes=VMEM_LIMIT),
        name="merge",
    )(x2d, oa, ob, g1, g2, wg, wa, wb, wo, wrh, wrl, br)


def _slot_table_kernel(pos_ref, init_ref, tab_ref, sem, *, tt, n):
    step = pl.program_id(0)

    @pl.when(step == 0)
    def _():
        cp = pltpu.make_async_copy(init_ref, tab_ref, sem)
        cp.start()
        cp.wait()

    def chunk(c, row):
        base = pl.multiple_of(c * LANES, LANES)
        for k in range(LANES):
            tab_ref[pos_ref[0, 0, base + k]] = row + k * ROW_TILE
            tab_ref[pos_ref[0, 1, base + k]] = row + (n + k) * ROW_TILE
        return row + LANES * ROW_TILE

    lax.fori_loop(0, tt // LANES, chunk, step * (tt * ROW_TILE))


def _slot_table(pos, init, tt):
    nt = pos.shape[0]
    return pl.pallas_call(
        functools.partial(_slot_table_kernel, tt=tt, n=nt * tt),
        grid=(nt,),
        in_specs=[pl.BlockSpec((1, 2, tt), lambda i: (i, 0, 0), memory_space=pltpu.SMEM),
                  pl.BlockSpec(memory_space=pl.ANY)],
        out_specs=pl.BlockSpec(memory_space=pltpu.SMEM),
        out_shape=jax.ShapeDtypeStruct(init.shape, jnp.int32),
        scratch_shapes=[pltpu.SemaphoreType.DMA(())],
        compiler_params=pltpu.CompilerParams(dimension_semantics=("arbitrary",)),
        name="slot_table",
    )(pos, init)


def _moe_kernel(be_ref, nu_ref, tab_ref, tab_next_ref, hn_ref, wg_ref, wu_ref, wd_ref, out_ref,
                xin, yout, wg_s, wu_s, wd_s, gsem, ssem, *, tb, n):
    b = pl.program_id(0)
    n_used = nu_ref[0]
    live = b < n_used
    s = lax.rem(b, 2)

    def gather(tab, slot):
        for j in range(tb):
            t, rows = tab[0, 0, j], n * ROW_TILE
            src = jnp.bitwise_and(t, rows - 1) if rows & (rows - 1) == 0 else lax.rem(t, rows)
            pltpu.make_async_copy(hn_ref.at[_tile_at(src)], xin.at[slot, pl.ds(j * ROW_TILE, ROW_TILE)],
                                  gsem.at[slot]).start(priority=0)

    def wait_rows(sem_ref, slot):
        pltpu.make_async_copy(hn_ref.at[pl.ds(0, tb * ROW_TILE)], xin.at[slot], sem_ref.at[slot]).wait()

    @pl.when(b == 0)
    def _():
        yout[...] = jnp.zeros_like(yout)
        fills = [pltpu.make_async_copy(yout.at[k], out_ref.at[pl.ds((2 * n + k * tb) * ROW_TILE, tb * ROW_TILE)],
                                       ssem.at[k]) for k in range(2)]
        for cp in fills:
            cp.start()
        for cp in fills:
            cp.wait()
        gather(tab_ref, 0)

    @pl.when(live & ((b == 0) | (be_ref[b] != be_ref[jnp.maximum(b - 1, 0)])))
    def _():
        wg_s[...] = wg_ref[0].astype(BF16)
        wu_s[...] = wu_ref[0].astype(BF16)
        wd_s[...] = wd_ref[0].astype(BF16)

    @pl.when(live & (b >= 2))
    def _():
        wait_rows(ssem, s)

    def block(s):
        wait_rows(gsem, s)
        gather(tab_next_ref, 1 - s)
        half = tb // 2
        gu = []
        for i in range(2):
            xb = jnp.concatenate([xin[s, pl.ds(i * half * ROW_TILE + j, half, stride=ROW_TILE), :]
                                  for j in range(ROW_TILE)], axis=1).astype(BF16)
            gu.append((_dot(xb, wg_s[...]), _dot(xb, wu_s[...])))
        for i, (g, u) in enumerate(gu):
            y = _dot((g * jax.nn.sigmoid(g) * u).astype(BF16), wd_s[...])
            for j in range(ROW_TILE):
                yout[s, pl.ds(i * half * ROW_TILE + j, half, stride=ROW_TILE), :] = y[:, j * LANES:(j + 1) * LANES]

        for j in range(tb):
            pltpu.make_async_copy(yout.at[s, pl.ds(j * ROW_TILE, ROW_TILE)], out_ref.at[_tile_at(tab_ref[0, 0, j])],
                                  ssem.at[s]).start(priority=1)

    for parity in range(2):
        pl.when(live & (s == parity))(functools.partial(block, parity))

    @pl.when(live & (b == n_used - 1))
    def _():
        wait_rows(gsem, 1 - s)
        wait_rows(ssem, s)

        @pl.when(b >= 1)
        def _():
            wait_rows(ssem, 1 - s)


def _moe(block_e, n_used, tab, hn, wg, wu, wd, tb, n):
    nblk = tab.shape[0] // tb
    tab2 = tab.reshape(nblk, 1, tb)
    live = lambda b, be, nu: jnp.minimum(b, nu[0] - 1)
    nxt = lambda b, be, nu: jnp.minimum(b + 1, nu[0] - 1)
    wspec = lambda shape: pl.BlockSpec((1,) + shape, lambda b, be, nu: (be[live(b, be, nu)], 0, 0))
    return pl.pallas_call(
        functools.partial(_moe_kernel, tb=tb, n=n),
        grid_spec=pltpu.PrefetchScalarGridSpec(
            num_scalar_prefetch=2, grid=(nblk,),
            in_specs=[pl.BlockSpec((1, 1, tb), lambda b, be, nu: (live(b, be, nu), 0, 0), memory_space=pltpu.SMEM),
                      pl.BlockSpec((1, 1, tb), lambda b, be, nu: (nxt(b, be, nu), 0, 0), memory_space=pltpu.SMEM),
                      pl.BlockSpec(memory_space=pl.ANY),
                      wspec((D_MODEL, EXPERT_FF)), wspec((D_MODEL, EXPERT_FF)), wspec((EXPERT_FF, D_MODEL))],
            out_specs=pl.BlockSpec(memory_space=pl.ANY),
            scratch_shapes=[pltpu.VMEM((2, tb * ROW_TILE, LANES), F32), pltpu.VMEM((2, tb * ROW_TILE, LANES), F32),
                            pltpu.VMEM((D_MODEL, EXPERT_FF), BF16), pltpu.VMEM((D_MODEL, EXPERT_FF), BF16),
                            pltpu.VMEM((EXPERT_FF, D_MODEL), BF16),
                            pltpu.SemaphoreType.DMA((2,)), pltpu.SemaphoreType.DMA((2,))]),
        out_shape=jax.ShapeDtypeStruct(((2 * n + 2 * tb) * ROW_TILE, LANES), F32),
        compiler_params=pltpu.CompilerParams(dimension_semantics=("arbitrary",),
                                             vmem_limit_bytes=VMEM_LIMIT),
        name="moe",
    )(block_e, n_used, tab2, tab2, hn, wg, wu, wd)


def _final_kernel(x2_ref, w_ref, y0_ref, y1_ref, gf_ref, o_ref):
    tc = x2_ref.shape[0]
    w = w_ref[...]
    y = x2_ref[...] + (w[:, 0:1] * _tiles_to_rows(y0_ref, tc) + w[:, 1:2] * _tiles_to_rows(y1_ref, tc))
    o_ref[...] = _rms(y, gf_ref[...])


def _final(x2, w_slot, yslots, gf, tc):
    n = x2.shape[0]
    nt = n // tc
    return pl.pallas_call(
        _final_kernel,
        grid=(nt,),
        in_specs=[pl.BlockSpec((tc, D_MODEL), lambda i: (i, 0)),
                  pl.BlockSpec((tc, 2), lambda i: (i, 0)),
                  pl.BlockSpec((tc * ROW_TILE, LANES), lambda i: (i, 0)),
                  pl.BlockSpec((tc * ROW_TILE, LANES), lambda i: (nt + i, 0)),
                  pl.BlockSpec((1, D_MODEL), lambda i: (0, 0))],
        out_specs=pl.BlockSpec((tc, D_MODEL), lambda i: (i, 0)),
        out_shape=jax.ShapeDtypeStruct((n, D_MODEL), F32),
        compiler_params=pltpu.CompilerParams(dimension_semantics=("parallel",),
                                             vmem_limit_bytes=VMEM_LIMIT),
        name="final",
    )(x2, w_slot, yslots, yslots, gf)


def _tile_at(row):
    return pl.ds(pl.multiple_of(row, ROW_TILE), ROW_TILE)


def _overlap_matrix_t(seq):
    nc = seq // CMP_STRIDE - CMP_LEN // CMP_STRIDE + 1
    ns = seq // SLC_LEN
    c0 = np.arange(nc) * CMP_STRIDE
    s0 = np.arange(ns) * SLC_LEN
    ov = np.clip(np.minimum(c0[:, None] + CMP_LEN, s0[None, :] + SLC_LEN)
                 - np.maximum(c0[:, None], s0[None, :]), 0, None) / CMP_LEN
    out = np.zeros((LANES, LANES), np.float32)
    out[FEAT_SEL:FEAT_SEL + ns, :nc] = ov.T
    return jnp.asarray(out, BF16)


def _position_features(seq):
    pos = np.arange(seq)
    fk = np.zeros((seq, LANES), np.float32)
    fk[:, FEAT_POS:FEAT_POS + 3] = (pos // SLC_LEN)[:, None]
    fk[:, FEAT_POS + 3:FEAT_POS + 6] = (pos % SLC_LEN)[:, None]
    fk[pos, FEAT_SEL + pos // SLC_LEN] = 1.0
    return jnp.asarray(fk)


def _pick_tile(n, pref):
    t = pref
    while n % t:
        t //= 2
    return t


def kernel(x, norm_mix_g, w_in, cmp_pos_k, cmp_w1_k, cmp_w2_k, cmp_pos_v, cmp_w1_v, cmp_w2_v, sinks, w_a, w_b,
           w_o, norm_ffn_g, w_group, b_group, w_expert, b_expert, w_gate_e, w_up_e, w_down_e, norm_final_g):
    bsz, seq, _ = x.shape
    n = bsz * seq
    assert TQ == LANES and seq % KCHUNK == 0 and seq // SLC_LEN <= N_SLC_BLK and seq // CMP_STRIDE <= LANES
    assert seq >= (NSA_WINDOW // TQ + 1) * TQ and w_in.shape[0] == 1
    x2d = x.reshape(n, D_MODEL)

    w = w_in[0]
    scale = HEAD_DIM ** -0.5 * LOG2E
    nsa_w, kvw = N_HEADS * HEAD_DIM, N_KV * HEAD_DIM
    o_qa, o_kva, o_gn = 0, nsa_w, nsa_w + 6 * kvw
    o_qb = o_gn + 3 * N_HEADS
    o_kvb = o_qb + nsa_w
    o_gm = o_kvb + 2 * kvw
    pair = lambda off: w[:, off:off + kvw]
    w_attn = jnp.concatenate(
        [w[:, o_qa:o_qa + nsa_w] * scale, w[:, o_qb:o_qb + nsa_w] * scale,
         pair(o_kva + 2 * kvw), pair(o_kva + 4 * kvw), pair(o_kvb),
         w[:, o_gn:o_gn + 3 * N_HEADS], jnp.zeros((D_MODEL, LANES - 3 * N_HEADS), F32),
         pair(o_kva + 3 * kvw), pair(o_kva + 5 * kvw), pair(o_kvb + kvw),
         w[:, o_kva:o_kva + 2 * kvw]], axis=1).astype(BF16)
    w_gm = w[:, o_gm:o_gm + 2 * D_MODEL].astype(BF16)

    tm = _pick_tile(seq, 512)
    feat_k = _position_features(seq)
    eye = jnp.eye(LANES, dtype=BF16)
    swap = jnp.roll(eye, HEAD_DIM, axis=1)
    proj, v_t, cmp_in = _proj(x2d, norm_mix_g[0][None], w_attn, feat_k, eye, swap, tm)
    proj3 = proj.reshape(bsz, seq, PROJ_W)

    nch = seq // CMP_STRIDE
    pos = jnp.stack([cmp_pos_k[0], cmp_pos_v[0]])
    pos = jnp.broadcast_to(pos[:, :, None, :], (2, CMP_LEN, N_KV, HEAD_DIM))
    pos_a = pos[:, :CMP_STRIDE].reshape(2, 1, CMP_STRIDE * kvw)
    pos_b = pos[:, CMP_STRIDE:].reshape(2, 1, CMP_STRIDE * kvw)
    w1 = jnp.stack([cmp_w1_k[0], cmp_w1_v[0]]).reshape(2, CMP_LEN, HEAD_DIM, CMP_HIDDEN)
    zero = jnp.zeros_like(w1)
    w1 = jnp.stack([jnp.concatenate([w1, zero], axis=-1), jnp.concatenate([zero, w1], axis=-1)], axis=2)
    w1 = w1.reshape(2, CMP_LEN * kvw, N_KV * CMP_HIDDEN).astype(BF16)
    w2 = jnp.pad(jnp.stack([cmp_w2_k[0], cmp_w2_v[0]]), ((0, 0), (0, 0), (0, LANES - HEAD_DIM))).astype(BF16)
    kvc, kvc_t = _compress(cmp_in.reshape(bsz, seq, CMP_W), pos_a, pos_b, w1[:, :CMP_STRIDE * kvw],
                           w1[:, CMP_STRIDE * kvw:], w2, jnp.swapaxes(w2, 1, 2))
    kvc = jnp.pad(kvc, ((0, 0), (0, 0), (0, 0), (0, LANES - nch), (0, 0)))
    kvc_t = jnp.pad(kvc_t, ((0, 0), (0, 0), (0, 0), (0, 0), (0, LANES - nch)))

    o_a, o_b = _attention(proj3, v_t, kvc, kvc_t, _overlap_matrix_t(seq), eye, sinks[0])

    w_r = jnp.concatenate([w_expert[0], w_group[0],
                           jnp.zeros((D_MODEL, LANES - N_EXPERTS - N_GROUPS), F32)], axis=1)
    w_r = w_r.T
    w_rh = w_r.astype(BF16)
    w_rl = (w_r - w_rh.astype(F32)).astype(BF16)
    b_r = jnp.concatenate([b_expert[0], b_group[0], jnp.zeros((LANES - N_EXPERTS - N_GROUPS,), F32)])
    b_r = b_r[:, None] * jnp.ones((1, LANES), F32)
    tt = _pick_tile(n, 512)
    x2, hn, route, counts = _merge(
        x2d, o_a.reshape(n, nsa_w), o_b.reshape(n, nsa_w), norm_mix_g[0][None], norm_ffn_g[0][None], w_gm,
        w_a[0].astype(BF16), w_b[0].astype(BF16), w_o[0].astype(BF16), w_rh, w_rl, b_r, tt)

    tb = 256
    nblk = -(-(2 * n + N_EXPERTS * (tb - 1)) // tb)
    cnt = counts[:N_EXPERTS, 0].astype(jnp.int32)
    padded = (cnt + tb - 1) // tb * tb
    pad_end = jnp.cumsum(padded)
    pad_start = pad_end - padded
    block_e = jnp.minimum(jnp.sum(pad_end[None, :] <= (jnp.arange(nblk) * tb)[:, None], axis=1), N_EXPERTS - 1)
    n_used = (pad_end[-1:] // tb).astype(jnp.int32)
    tab = route[:, 0:4, :].astype(jnp.int32)
    eids = tab[:, 0:2, :]
    start_of = sum(jnp.where(eids == e, pad_start[e], 0) for e in range(N_EXPERTS))
    pos = start_of + tab[:, 2:4, :]
    w_slot = jnp.swapaxes(route[:, 4:6, :], 1, 2).reshape(n, 2)
    spare = (2 * n + jnp.arange(nblk * tb, dtype=jnp.int32) % (2 * tb)) * ROW_TILE
    slot_tab = _slot_table(pos, spare, tt)
    y_slots = _moe(block_e.astype(jnp.int32), n_used, slot_tab, hn, w_gate_e[0], w_up_e[0], w_down_e[0], tb, n)
    out = _final(x2, w_slot, y_slots, norm_final_g[None], tt)
    return out.reshape(bsz, seq, D_MODEL)
```

```python
import functools

import numpy as np
import jax
import jax.numpy as jnp
from jax import lax
from jax.experimental import pallas as pl
from jax.experimental.pallas import tpu as pltpu

F32 = jnp.float32
BF16 = jnp.bfloat16

D_MODEL = 1024
HEAD_DIM = 64
N_HEADS = 8
N_KV = 2
GROUP = N_HEADS // N_KV
CMP_LEN = 32
CMP_STRIDE = 16
CMP_HIDDEN = 256
SLC_LEN = 64
SLC_TOPK = 8
NSA_WINDOW = 256
SWA_WINDOW = 128
N_GROUPS = 4
EPG = 8
N_EXPERTS = N_GROUPS * EPG
EXPERT_FF = 256
RMS_EPS = 1e-6
NEG_INF = -1e30
FORCE_SCORE = 1e9

LANES = 128
TQ = 128
TILES_PER_STEP = 4
KCHUNK = 512
SUB = 128
N_SLC_BLK = LANES // 4
FEAT_POS = HEAD_DIM
FEAT_SEL = HEAD_DIM + 6
ROWS_LO, ROWS_HI = 64, 104
N_KV_SECT = 6
COL_QB = 512
COL_KV = 1024
COL_GN = COL_KV + N_KV_SECT * LANES
PROJ_W = COL_GN + LANES
CMP_W = 2 * LANES
ROW_TILE = D_MODEL // LANES
VMEM_LIMIT = 56 * 1024 * 1024


LOG2E = float(np.log2(np.e))


def _alibi_slopes():
    n = 2 * N_HEADS
    s = 2.0 ** (-8.0 * np.arange(1, n + 1) / n) * LOG2E
    return [float(v) for v in s[:N_HEADS]], [float(v) for v in s[N_HEADS:]]


SLOPES_SWA, SLOPES_NSA = _alibi_slopes()


def _bf16_pieces(v):
    out, rem = [], np.float32(v)
    for _ in range(3):
        p = np.float32(np.asarray(rem, np.float32).astype(BF16).astype(np.float32))
        out.append(float(p))
        rem = np.float32(rem - p)
    return out


def _rms(x, g):
    return x * lax.rsqrt(jnp.mean(x * x, axis=-1, keepdims=True) + RMS_EPS) * g


def _dot(a, b):
    return jnp.dot(a, b, preferred_element_type=F32)


def _tiles_to_rows(ref, n, lead=()):
    return jnp.concatenate([ref[lead + (pl.ds(j, n, stride=ROW_TILE), slice(None))] for j in range(ROW_TILE)], axis=1)


def _rows_to_tiles(ref, val):
    n = val.shape[0]
    for j in range(ROW_TILE):
        ref[pl.ds(j, n, stride=ROW_TILE), :] = val[:, j * LANES:(j + 1) * LANES]


def _dot_nt(a, b):
    return lax.dot_general(a, b, (((1,), (1,)), ((), ())), preferred_element_type=F32)


def _proj_kernel(x_ref, g_ref, w_ref, fk_ref, eye_ref, swap_ref, o_ref, vt_ref, cmp_ref):
    tm = x_ref.shape[0]
    n_br = N_KV_SECT // N_KV
    col_gn = COL_KV + n_br * LANES
    col_v = col_gn + LANES
    h = _rms(x_ref[...], g_ref[...]).astype(BF16)
    res = _dot(h, w_ref[...])
    o_ref[:, :COL_KV] = res[:, :COL_KV].astype(o_ref.dtype)
    o_ref[:, COL_GN:] = res[:, col_gn:col_v].astype(o_ref.dtype)
    low = lax.broadcasted_iota(jnp.int32, (1, LANES), 1) < HEAD_DIM
    ones_row = jnp.where(lax.broadcasted_iota(jnp.int32, (LANES - HEAD_DIM, tm), 0) == 0, 1.0, 0.0)
    for br in range(n_br):
        k_pair = res[:, COL_KV + br * LANES:COL_KV + (br + 1) * LANES]
        heads = (k_pair, _dot(k_pair.astype(BF16), swap_ref[...]))
        for hd in range(N_KV):
            c0 = COL_KV + (br * N_KV + hd) * LANES
            o_ref[:, c0:c0 + LANES] = jnp.where(low, heads[hd], fk_ref[...]).astype(o_ref.dtype)
        v_pair_t = _dot_nt(eye_ref[...], res[:, col_v + br * LANES:col_v + (br + 1) * LANES].astype(BF16))
        for hd in range(N_KV):
            v_t = jnp.concatenate([v_pair_t[hd * HEAD_DIM:(hd + 1) * HEAD_DIM], ones_row], axis=0)
            for k in range(tm // LANES):
                vt_ref[br * N_KV + hd, 0, k] = v_t[:, k * LANES:(k + 1) * LANES].astype(vt_ref.dtype)
    cmp_ref[...] = res[:, col_v + n_br * LANES:]


def _proj(x2d, g, w, feat_k, eye, swap, tm):
    n = x2d.shape[0]
    seq = feat_k.shape[0]
    nper = seq // tm
    kb = tm // LANES
    return pl.pallas_call(
        _proj_kernel,
        grid=(n // tm,),
        in_specs=[pl.BlockSpec((tm, D_MODEL), lambda i: (i, 0)),
                  pl.BlockSpec((1, D_MODEL), lambda i: (0, 0)),
                  pl.BlockSpec(w.shape, lambda i: (0, 0)),
                  pl.BlockSpec((tm, LANES), lambda i: (i % nper, 0)),
                  pl.BlockSpec((LANES, LANES), lambda i: (0, 0)),
                  pl.BlockSpec((LANES, LANES), lambda i: (0, 0))],
        out_specs=[pl.BlockSpec((tm, PROJ_W), lambda i: (i, 0)),
                   pl.BlockSpec((N_KV_SECT, 1, kb, LANES, LANES), lambda i: (0, i // nper, i % nper, 0, 0)),
                   pl.BlockSpec((tm, CMP_W), lambda i: (i, 0))],
        out_shape=[jax.ShapeDtypeStruct((n, PROJ_W), BF16),
                   jax.ShapeDtypeStruct((N_KV_SECT, n // seq, seq // LANES, LANES, LANES), BF16),
                   jax.ShapeDtypeStruct((n, CMP_W), F32)],
        compiler_params=pltpu.CompilerParams(dimension_semantics=("parallel",),
                                             vmem_limit_bytes=VMEM_LIMIT),
        name="proj",
    )(x2d, g, w, feat_k, eye, swap)


def _compress_kernel(x_ref, pa_ref, pb_ref, w1a_ref, w1b_ref, w2_ref, w2t_ref, o_ref, ot_ref, *, nch):
    r = jnp.concatenate([x_ref[0, pl.ds(j, nch, stride=CMP_STRIDE), :] for j in range(CMP_STRIDE)], axis=1)
    a = _dot((r + pa_ref[0]).astype(BF16), w1a_ref[0])
    b = _dot((r + pb_ref[0]).astype(BF16), w1b_ref[0])
    hid = a + pltpu.roll(b, nch - 1, 0)
    hid = hid * jax.nn.sigmoid(hid)
    for h in range(N_KV):
        hid_h = hid[:, h * CMP_HIDDEN:(h + 1) * CMP_HIDDEN].astype(BF16)
        o_ref[0, 0, h] = _dot(hid_h, w2_ref[0]).astype(o_ref.dtype)
        ot_ref[0, 0, h] = _dot_nt(w2t_ref[0], hid_h).astype(ot_ref.dtype)


def _compress(cmp3, pos_a, pos_b, w1a, w1b, w2, w2t):
    bsz, seq, _ = cmp3.shape
    nch = seq // CMP_STRIDE
    wspec = lambda a: pl.BlockSpec((1,) + a.shape[1:], lambda b, j: (j, 0, 0))
    return pl.pallas_call(
        functools.partial(_compress_kernel, nch=nch),
        grid=(bsz, 2),
        in_specs=[pl.BlockSpec((1, seq, LANES), lambda b, j: (b, 0, j)),
                  wspec(pos_a), wspec(pos_b), wspec(w1a), wspec(w1b), wspec(w2), wspec(w2t)],
        out_specs=[pl.BlockSpec((1, 1, N_KV, nch, LANES), lambda b, j: (b, j, 0, 0, 0)),
                   pl.BlockSpec((1, 1, N_KV, LANES, nch), lambda b, j: (b, j, 0, 0, 0))],
        out_shape=[jax.ShapeDtypeStruct((bsz, 2, N_KV, nch, LANES), BF16),
                   jax.ShapeDtypeStruct((bsz, 2, N_KV, LANES, nch), BF16)],
        compiler_params=pltpu.CompilerParams(dimension_semantics=("parallel", "parallel"),
                                             vmem_limit_bytes=VMEM_LIMIT),
        name="compress",
    )(cmp3, pos_a, pos_b, w1a, w1b, w2, w2t)


def _attn_t_kernel(*refs, n_cmp):
    tiles = [_attn_tile(u, *refs, n_cmp=n_cmp) for u in range(TILES_PER_STEP)]
    while all([next(t) is PHASED for t in tiles]):
        pass
    for t in tiles:
        for _ in t:
            pass


PHASED, SWEEP = "phase done", "ready for the sweep"


def _attn_tile(u, sinks_ref, qa_ref, qb_ref, kc_ref, vct_ref, *rest, n_cmp):
    rows = slice(u * TQ, (u + 1) * TQ)
    n_br = 3
    ks = rest[:n_br * N_KV]
    vts = rest[n_br * N_KV:2 * n_br * N_KV]
    gn_ref, ovt_ref, eye_ref, oa_ref, ob_ref = rest[2 * n_br * N_KV:]
    ksect = lambda branch, h: ks[branch * N_KV + h]
    vsect = lambda branch, h: vts[branch * N_KV + h]
    qi = pl.program_id(1) * TILES_PER_STEP + u
    q0 = pl.multiple_of(qi * TQ, TQ)
    lane = lax.broadcasted_iota(jnp.int32, (1, TQ), 1)
    sub = lax.broadcasted_iota(jnp.int32, (LANES, 1), 0)
    t_row = q0 + lane
    eye = eye_ref[...]
    gates = jax.nn.sigmoid(_dot_nt(eye, gn_ref[0, rows, :]))
    gate = lambda hh, c: gates[3 * hh + c:3 * hh + c + 1, :]
    sub40 = sub[ROWS_LO:ROWS_HI]
    blk = sub40 - FEAT_SEL
    in_rng = (blk >= 0) & (blk < N_SLC_BLK)
    is_pos = (sub40 >= FEAT_POS) & (sub40 < FEAT_SEL)
    zeros_lo = jnp.zeros((LANES - ROWS_HI, TQ), F32)

    def q_t(ref, hh):
        both = _dot_nt(eye, ref[0, rows, (hh // 2) * LANES:(hh // 2 + 1) * LANES])
        return both[(hh % 2) * HEAD_DIM:(hh % 2 + 1) * HEAD_DIM]

    def slope_col(slope):
        hi, mid, lo = _bf16_pieces(slope)
        col = jnp.zeros(sub40.shape, F32)
        for i, v in enumerate([SLC_LEN * hi, SLC_LEN * mid, SLC_LEN * lo, hi, mid, lo]):
            col = jnp.where(sub40 == FEAT_POS + i, v, col)
        return col

    def q_aug_t(q_ts, tails):
        return jnp.concatenate([jnp.concatenate([q, jnp.broadcast_to(t, (ROWS_HI - ROWS_LO, TQ)), zeros_lo], axis=0)
                                for q, t in zip(q_ts, tails)], axis=1).astype(BF16)

    def v_t(ref, first, n):
        return jnp.concatenate([ref[0, 0, first + j] for j in range(n)], axis=1)

    def masked(s, mask):
        return jnp.concatenate([jnp.where(mask, s[:, g * TQ:(g + 1) * TQ], NEG_INF) for g in range(GROUP)], axis=1)

    def store_heads(ref, outs):
        for p in range(N_HEADS // 2):
            pair = jnp.concatenate([outs[2 * p][:HEAD_DIM], outs[2 * p + 1][:HEAD_DIM]], axis=0).astype(BF16)
            ref[0, rows, p * LANES:(p + 1) * LANES] = _dot_nt(eye, pair).astype(ref.dtype)

    nw = NSA_WINDOW // TQ + 1
    w_first = jnp.maximum(qi - (nw - 1), 0)
    w_start = pl.multiple_of(w_first * TQ, TQ)
    nb = (SWA_WINDOW - 1 + TQ - 1) // TQ + 1
    b_first = jnp.maximum(qi - (nb - 1), 0)
    b_start = pl.multiple_of(b_first * TQ, TQ)
    pre = []
    for h in range(N_KV):
        heads = [h * GROUP + g for g in range(GROUP)]
        q_ts = [q_t(qa_ref, hh) for hh in heads]
        scol = [slope_col(SLOPES_NSA[hh]) for hh in heads]
        s_cmp = _dot(kc_ref[0, 0, h], q_aug_t(q_ts, [jnp.zeros((1, 1), F32)] * GROUP))
        s_win = _dot(ksect(1, h)[0, pl.ds(w_start, nw * TQ), :], q_aug_t(q_ts, scol))
        q_b = q_aug_t([q_t(qb_ref, hh) for hh in heads], [slope_col(SLOPES_SWA[hh]) for hh in heads])
        s_swa = _dot(ksect(2, h)[0, pl.ds(b_start, nb * TQ), :], q_b)
        pre.append((q_ts, scol, s_cmp, s_win, s_swa))

    yield PHASED
    def nsa_local(h):
        heads = [h * GROUP + g for g in range(GROUP)]
        _, _, s_all, s_win, _ = pre[h]

        end_c = sub * CMP_STRIDE + (CMP_LEN - 1)
        cmask = (t_row >= end_c) & (sub < n_cmp)
        ps = []
        for g in range(GROUP):
            s = s_all[:, g * TQ:(g + 1) * TQ] + SLOPES_NSA[heads[g]] * end_c.astype(F32)
            s = jnp.where(cmask, s, NEG_INF)
            m = jnp.max(s, axis=0, keepdims=True)
            e = jnp.where(cmask, jnp.exp2(s - m), 0.0)
            z = jnp.sum(e, axis=0, keepdims=True)
            ps.append(e / jnp.where(z > 0, z, 1.0))
        o_cmp = _dot(vct_ref[0, 0, h], jnp.concatenate(ps, axis=1).astype(BF16))

        dist = t_row - (w_start + lax.broadcasted_iota(jnp.int32, (nw * TQ, 1), 0))
        s = masked(s_win, (dist >= 0) & (dist < NSA_WINDOW))
        m = jnp.max(s, axis=0, keepdims=True)
        acc = _dot(v_t(vsect(1, h), w_first, nw), jnp.exp2(s - m).astype(BF16))
        o_win = acc / acc[HEAD_DIM:HEAD_DIM + 1, :]

        psum = ps[0] + ps[1] + ps[2] + ps[3]
        p_hi = psum.astype(BF16)
        p_lo = (psum - p_hi.astype(F32)).astype(BF16)
        imp = (_dot(ovt_ref[...], p_hi) + _dot(ovt_ref[...], p_lo))[ROWS_LO:ROWS_HI]
        part = [gate(hh, 0) * o_cmp[:, g * TQ:(g + 1) * TQ] + gate(hh, 2) * o_win[:, g * TQ:(g + 1) * TQ]
                for g, hh in enumerate(heads)]
        return imp, part

    def nsa_select(h, imp):
        q_ts, scol = pre[h][:2]
        cur = lax.shift_right_logical(t_row, int(np.log2(SLC_LEN)))
        valid = in_rng & (blk * SLC_LEN <= t_row)
        forced = in_rng & ((blk == 0) | (blk == cur) | (blk == cur - 1))
        score = jnp.where(forced, FORCE_SCORE, jnp.where(valid, imp, NEG_INF))
        rank = jnp.zeros(score.shape, F32)
        for i in range(N_SLC_BLK):
            r = FEAT_SEL - ROWS_LO + i
            si = score[r:r + 1, :]
            rank = rank + jnp.where((si > score) | ((si == score) & (blk > i)), 1.0, 0.0)
        sel = in_rng & (rank < SLC_TOPK) & (score > 0.5 * NEG_INF)
        bias_diag = jnp.where(in_rng & jnp.logical_not(sel), NEG_INF, 0.0)
        bias_main = jnp.where(in_rng & jnp.logical_not(sel & (blk < 2 * qi)), NEG_INF, 0.0)

        q_d = q_aug_t(q_ts, [jnp.where(is_pos, scol[g], bias_diag) for g in range(GROUP)])
        q_m = q_aug_t(q_ts, [jnp.where(is_pos, scol[g], bias_main) for g in range(GROUP)])
        return q_m, _dot(ksect(0, h)[0, pl.ds(q0, TQ), :], q_d)

    def nsa_diag(h, q_m, s_diag):
        s = masked(s_diag, q0 + sub <= t_row)
        m0 = jnp.max(s, axis=0, keepdims=True)
        acc0 = _dot(v_t(vsect(0, h), qi, 1), jnp.exp2(s - m0).astype(BF16))
        return q_m, m0, acc0

    local = [nsa_local(h) for h in range(N_KV)]
    yield PHASED

    dist = t_row - (b_start + lax.broadcasted_iota(jnp.int32, (nb * TQ, 1), 0))
    bmask = (dist >= 0) & (dist < SWA_WINDOW)
    outs_b = []
    for h in range(N_KV):
        heads = [h * GROUP + g for g in range(GROUP)]
        s = masked(pre[h][4], bmask)
        sink = jnp.concatenate([sinks_ref[hh] * LOG2E + SLOPES_SWA[hh] * t_row.astype(F32) for hh in heads], axis=1)
        m = jnp.maximum(jnp.max(s, axis=0, keepdims=True), sink)
        acc = _dot(v_t(vsect(2, h), b_first, nb), jnp.exp2(s - m).astype(BF16))
        o_all = acc / (acc[HEAD_DIM:HEAD_DIM + 1, :] + jnp.exp2(sink - m))
        outs_b += [o_all[:, g * TQ:(g + 1) * TQ] for g in range(GROUP)]
    store_heads(ob_ref, outs_b)

    yield PHASED
    selected = [nsa_select(h, local[h][0]) for h in range(N_KV)]
    yield PHASED
    fronts = [nsa_diag(h, *selected[h]) + (local[h][1],) for h in range(N_KV)]
    nblk = KCHUNK // TQ

    def absorb(carry, c, pieces):
        state = list(carry)
        subs = [(j, h) for j in pieces for h in range(N_KV)]
        scores = [_dot(ksect(0, h)[0, pl.ds(pl.multiple_of(c * KCHUNK + j * SUB, SUB), SUB), :], fronts[h][0])
                  for j, h in subs]
        for (j, h), s in zip(subs, scores):
            m_i, acc = state[h]
            m_new = jnp.maximum(m_i, jnp.max(s, axis=0, keepdims=True))
            pv = _dot(v_t(vsect(0, h), c * nblk + j * (SUB // TQ), SUB // TQ), jnp.exp2(s - m_new).astype(BF16))
            state[h] = (m_new, jnp.exp2(m_i - m_new) * acc + pv)
        return tuple(state)

    yield SWEEP
    step_chunk = pl.program_id(1)
    swept = lax.fori_loop(0, step_chunk, lambda c, st: absorb(st, c, range(KCHUNK // SUB)),
                          tuple((f[1], f[2]) for f in fronts))
    if u:
        swept = absorb(swept, step_chunk, range(u * TQ // SUB))
    outs_a = []
    for h in range(N_KV):
        acc = swept[h][1]
        o_slc = acc / acc[HEAD_DIM:HEAD_DIM + 1, :]
        outs_a += [fronts[h][3][g] + gate(h * GROUP + g, 1) * o_slc[:, g * TQ:(g + 1) * TQ] for g in range(GROUP)]
    store_heads(oa_ref, outs_a)


def _attention(proj3, v_t, kc, vc_t, ovt, eye, sinks):
    bsz, seq, _ = proj3.shape
    n_cmp = seq // CMP_STRIDE - CMP_LEN // CMP_STRIDE + 1
    width = N_HEADS * HEAD_DIM
    consts = [ovt, eye]
    tq = TILES_PER_STEP * TQ
    qspec = lambda col: pl.BlockSpec((1, tq, width), lambda b, q, s: (b, q, col))
    cspec = lambda a, j: pl.BlockSpec((1, 1) + a.shape[2:], lambda b, q, s: (b, j, 0, 0, 0))
    in_specs = [qspec(0), qspec(COL_QB // width), cspec(kc, 0), cspec(vc_t, 1)]
    in_specs += [pl.BlockSpec((1, seq, LANES), lambda b, q, s, j=j: (b, 0, COL_KV // LANES + j))
                 for j in range(N_KV_SECT)]
    in_specs += [pl.BlockSpec((1, 1) + v_t.shape[2:], lambda b, q, s, j=j: (j, b, 0, 0, 0)) for j in range(N_KV_SECT)]
    in_specs += [pl.BlockSpec((1, tq, LANES), lambda b, q, s: (b, q, COL_GN // LANES))]
    in_specs += [pl.BlockSpec(c.shape, lambda b, q, s: (0, 0)) for c in consts]
    ospec = pl.BlockSpec((1, tq, width), lambda b, q, s: (b, q, 0))
    return pl.pallas_call(
        functools.partial(_attn_t_kernel, n_cmp=n_cmp),
        grid_spec=pltpu.PrefetchScalarGridSpec(
            num_scalar_prefetch=1, grid=(bsz, seq // tq), in_specs=in_specs, out_specs=[ospec, ospec]),
        out_shape=[jax.ShapeDtypeStruct((bsz, seq, width), BF16)] * 2,
        compiler_params=pltpu.CompilerParams(dimension_semantics=("parallel", "parallel"),
                                             vmem_limit_bytes=VMEM_LIMIT),
        name="attn",
    )(sinks, proj3, proj3, kc, vc_t, *([proj3] * N_KV_SECT), *([v_t] * N_KV_SECT), proj3, *consts)


def _merge_kernel(x_ref, oa_ref, ob_ref, g1_ref, g2_ref, wg_ref, wa_ref, wb_ref, wo_ref,
                  wrh_ref, wrl_ref, br_ref, x2_ref, hn_ref, route_ref, cnt_ref):
    tm = x_ref.shape[0]

    @pl.when(pl.program_id(0) == 0)
    def _():
        cnt_ref[...] = jnp.zeros_like(cnt_ref)

    halves = [slice(i * (tm // 2), (i + 1) * (tm // 2)) for i in range(2)]
    xs = [x_ref[r, :] for r in halves]
    pre = []
    for r, x in zip(halves, xs):
        h = _rms(x, g1_ref[...]).astype(BF16)
        pre.append((_dot(h, wg_ref[...]), _dot(oa_ref[r, :], wa_ref[...]), _dot(ob_ref[r, :], wb_ref[...])))
    mixes = []
    for g_pre, a, b in pre:
        gm = jax.nn.sigmoid(g_pre)
        mixin = gm[:, :D_MODEL] * a + gm[:, D_MODEL:] * b
        mixes.append(_dot(mixin.astype(BF16), wo_ref[...]))
    logit_halves = []
    for r, x, mix in zip(halves, xs, mixes):
        x2 = x + mix
        x2_ref[r, :] = x2
        hn = _rms(x2, g2_ref[...])
        for j in range(ROW_TILE):
            hn_ref[pl.ds(r.start * ROW_TILE + j, tm // 2, stride=ROW_TILE), :] = hn[:, j * LANES:(j + 1) * LANES]
        hn_b = hn.astype(BF16)
        hn_lo = (hn - hn_b.astype(F32)).astype(BF16)
        logit_halves.append(_dot_nt(wrh_ref[...], hn_b) + _dot_nt(wrh_ref[...], hn_lo) + _dot_nt(wrl_ref[...], hn_b))
    bias = br_ref[...]
    logits = jnp.concatenate(logit_halves, axis=1) + jnp.concatenate([bias] * (tm // LANES), axis=1)
    row = lax.broadcasted_iota(jnp.int32, (LANES, 1), 0)
    rowf = row.astype(F32)
    big = float(LANES)
    top = lambda a: jnp.max(a, axis=0, keepdims=True)
    first = lambda hit: jnp.min(jnp.where(hit, rowf, big), axis=0, keepdims=True)
    is_g = (row >= N_EXPERTS) & (row < N_EXPERTS + N_GROUPS)
    gl = jnp.where(is_g, logits, NEG_INF)
    gmax = top(gl)
    grp = first(gl == gmax) - N_EXPERTS
    p_grp = 1.0 / jnp.sum(jnp.where(is_g, jnp.exp(gl - gmax), 0.0), axis=0, keepdims=True)
    in_grp = (rowf >= grp * EPG) & (rowf < grp * EPG + EPG)
    el = jnp.where(in_grp, logits, NEG_INF)
    v0 = top(el)
    i0 = first(el == v0)
    el1 = jnp.where(rowf == i0, NEG_INF, el)
    v1 = top(el1)
    i1 = first(el1 == v1)
    e1 = jnp.exp(v1 - v0)
    w0 = p_grp / (1.0 + e1)
    w1 = p_grp * e1 / (1.0 + e1)

    oh0 = jnp.where(rowf == i0, 1.0, 0.0)
    oh1 = jnp.where(rowf == i1, 1.0, 0.0)
    oh = oh0 + oh1
    r_i = lax.broadcasted_iota(jnp.int32, (tm, tm), 0)
    c_i = lax.broadcasted_iota(jnp.int32, (tm, tm), 1)
    earlier = jnp.where(r_i < c_i, 1.0, 0.0).astype(BF16)
    before = cnt_ref[...] + _dot(oh.astype(BF16), earlier)
    rank0 = jnp.sum(oh0 * before, axis=0, keepdims=True)
    rank1 = jnp.sum(oh1 * before, axis=0, keepdims=True)
    cnt_ref[...] = cnt_ref[...] + jnp.sum(oh, axis=1, keepdims=True)
    row8 = row[:8]
    route = jnp.zeros((8, tm), F32)
    for k, v in enumerate((i0, i1, rank0, rank1, w0, w1)):
        route = jnp.where(row8 == k, v, route)
    route_ref[0] = route


def _merge(x2d, oa, ob, g1, g2, wg, wa, wb, wo, wrh, wrl, br, tm):
    n = x2d.shape[0]
    width = N_HEADS * HEAD_DIM
    row = lambda w: pl.BlockSpec((tm, w), lambda i: (i, 0))
    full = lambda a: pl.BlockSpec(a.shape, lambda i: (0, 0))
    return pl.pallas_call(
        _merge_kernel,
        grid=(n // tm,),
        in_specs=[row(D_MODEL), row(width), row(width), full(g1), full(g2), full(wg), full(wa), full(wb),
                  full(wo), full(wrh), full(wrl), full(br)],
        out_specs=[row(D_MODEL), pl.BlockSpec((tm * ROW_TILE, LANES), lambda i: (i, 0)),
                   pl.BlockSpec((1, 8, tm), lambda i: (i, 0, 0)), pl.BlockSpec((LANES, 1), lambda i: (0, 0))],
        out_shape=[jax.ShapeDtypeStruct((n, D_MODEL), F32), jax.ShapeDtypeStruct((n * ROW_TILE, LANES), F32),
                   jax.ShapeDtypeStruct((n // tm, 8, tm), F32), jax.ShapeDtypeStruct((LANES, 1), F32)],
        compiler_params=pltpu.CompilerParams(dimension_semantics=("arbitrary",),
                                             vmem_limit_bytes=VMEM_LIMIT),
        name="merge",
    )(x2d, oa, ob, g1, g2, wg, wa, wb, wo, wrh, wrl, br)


def _slot_table_kernel(pos_ref, init_ref, tab_ref, sem, *, tt, n):
    step = pl.program_id(0)

    @pl.when(step == 0)
    def _():
        cp = pltpu.make_async_copy(init_ref, tab_ref, sem)
        cp.start()
        cp.wait()

    def chunk(c, row):
        base = pl.multiple_of(c * LANES, LANES)
        for k in range(LANES):
            tab_ref[pos_ref[0, 0, base + k]] = row + k * ROW_TILE
            tab_ref[pos_ref[0, 1, base + k]] = row + (n + k) * ROW_TILE
        return row + LANES * ROW_TILE

    lax.fori_loop(0, tt // LANES, chunk, step * (tt * ROW_TILE))


def _slot_table(pos, init, tt):
    nt = pos.shape[0]
    return pl.pallas_call(
        functools.partial(_slot_table_kernel, tt=tt, n=nt * tt),
        grid=(nt,),
        in_specs=[pl.BlockSpec((1, 2, tt), lambda i: (i, 0, 0), memory_space=pltpu.SMEM),
                  pl.BlockSpec(memory_space=pl.ANY)],
        out_specs=pl.BlockSpec(memory_space=pltpu.SMEM),
        out_shape=jax.ShapeDtypeStruct(init.shape, jnp.int32),
        scratch_shapes=[pltpu.SemaphoreType.DMA(())],
        compiler_params=pltpu.CompilerParams(dimension_semantics=("arbitrary",)),
        name="slot_table",
    )(pos, init)


def _moe_kernel(be_ref, nu_ref, tab_ref, tab_next_ref, hn_ref, wg_ref, wu_ref, wd_ref, out_ref,
                xin, yout, wg_s, wu_s, wd_s, gsem, ssem, *, tb, n):
    b = pl.program_id(0)
    n_used = nu_ref[0]
    live = b < n_used
    s = lax.rem(b, 2)

    def gather(tab, slot):
        for j in range(tb):
            t, rows = tab[0, 0, j], n * ROW_TILE
            src = jnp.bitwise_and(t, rows - 1) if rows & (rows - 1) == 0 else lax.rem(t, rows)
            pltpu.make_async_copy(hn_ref.at[_tile_at(src)], xin.at[slot, pl.ds(j * ROW_TILE, ROW_TILE)],
                                  gsem.at[slot]).start(priority=0)

    def wait_rows(sem_ref, slot):
        pltpu.make_async_copy(hn_ref.at[pl.ds(0, tb * ROW_TILE)], xin.at[slot], sem_ref.at[slot]).wait()

    @pl.when(b == 0)
    def _():
        yout[...] = jnp.zeros_like(yout)
        fills = [pltpu.make_async_copy(yout.at[k], out_ref.at[pl.ds((2 * n + k * tb) * ROW_TILE, tb * ROW_TILE)],
                                       ssem.at[k]) for k in range(2)]
        for cp in fills:
            cp.start()
        for cp in fills:
            cp.wait()
        gather(tab_ref, 0)

    @pl.when(live & ((b == 0) | (be_ref[b] != be_ref[jnp.maximum(b - 1, 0)])))
    def _():
        wg_s[...] = wg_ref[0].astype(BF16)
        wu_s[...] = wu_ref[0].astype(BF16)
        wd_s[...] = wd_ref[0].astype(BF16)

    @pl.when(live & (b >= 2))
    def _():
        wait_rows(ssem, s)

    def block(s):
        wait_rows(gsem, s)
        gather(tab_next_ref, 1 - s)
        half = tb // 2
        gu = []
        for i in range(2):
            xb = jnp.concatenate([xin[s, pl.ds(i * half * ROW_TILE + j, half, stride=ROW_TILE), :]
                                  for j in range(ROW_TILE)], axis=1).astype(BF16)
            gu.append((_dot(xb, wg_s[...]), _dot(xb, wu_s[...])))
        for i, (g, u) in enumerate(gu):
            y = _dot((g * jax.nn.sigmoid(g) * u).astype(BF16), wd_s[...])
            for j in range(ROW_TILE):
                yout[s, pl.ds(i * half * ROW_TILE + j, half, stride=ROW_TILE), :] = y[:, j * LANES:(j + 1) * LANES]

        for j in range(tb):
            pltpu.make_async_copy(yout.at[s, pl.ds(j * ROW_TILE, ROW_TILE)], out_ref.at[_tile_at(tab_ref[0, 0, j])],
                                  ssem.at[s]).start(priority=1)

    for parity in range(2):
        pl.when(live & (s == parity))(functools.partial(block, parity))

    @pl.when(live & (b == n_used - 1))
    def _():
        wait_rows(gsem, 1 - s)
        wait_rows(ssem, s)

        @pl.when(b >= 1)
        def _():
            wait_rows(ssem, 1 - s)


def _moe(block_e, n_used, tab, hn, wg, wu, wd, tb, n):
    nblk = tab.shape[0] // tb
    tab2 = tab.reshape(nblk, 1, tb)
    live = lambda b, be, nu: jnp.minimum(b, nu[0] - 1)
    nxt = lambda b, be, nu: jnp.minimum(b + 1, nu[0] - 1)
    wspec = lambda shape: pl.BlockSpec((1,) + shape, lambda b, be, nu: (be[live(b, be, nu)], 0, 0))
    return pl.pallas_call(
        functools.partial(_moe_kernel, tb=tb, n=n),
        grid_spec=pltpu.PrefetchScalarGridSpec(
            num_scalar_prefetch=2, grid=(nblk,),
            in_specs=[pl.BlockSpec((1, 1, tb), lambda b, be, nu: (live(b, be, nu), 0, 0), memory_space=pltpu.SMEM),
                      pl.BlockSpec((1, 1, tb), lambda b, be, nu: (nxt(b, be, nu), 0, 0), memory_space=pltpu.SMEM),
                      pl.BlockSpec(memory_space=pl.ANY),
                      wspec((D_MODEL, EXPERT_FF)), wspec((D_MODEL, EXPERT_FF)), wspec((EXPERT_FF, D_MODEL))],
            out_specs=pl.BlockSpec(memory_space=pl.ANY),
            scratch_shapes=[pltpu.VMEM((2, tb * ROW_TILE, LANES), F32), pltpu.VMEM((2, tb * ROW_TILE, LANES), F32),
                            pltpu.VMEM((D_MODEL, EXPERT_FF), BF16), pltpu.VMEM((D_MODEL, EXPERT_FF), BF16),
                            pltpu.VMEM((EXPERT_FF, D_MODEL), BF16),
                            pltpu.SemaphoreType.DMA((2,)), pltpu.SemaphoreType.DMA((2,))]),
        out_shape=jax.ShapeDtypeStruct(((2 * n + 2 * tb) * ROW_TILE, LANES), F32),
        compiler_params=pltpu.CompilerParams(dimension_semantics=("arbitrary",),
                                             vmem_limit_bytes=VMEM_LIMIT),
        name="moe",
    )(block_e, n_used, tab2, tab2, hn, wg, wu, wd)


def _final_kernel(x2_ref, w_ref, y0_ref, y1_ref, gf_ref, o_ref):
    tc = x2_ref.shape[0]
    w = w_ref[...]
    y = x2_ref[...] + (w[:, 0:1] * _tiles_to_rows(y0_ref, tc) + w[:, 1:2] * _tiles_to_rows(y1_ref, tc))
    o_ref[...] = _rms(y, gf_ref[...])


def _final(x2, w_slot, yslots, gf, tc):
    n = x2.shape[0]
    nt = n // tc
    return pl.pallas_call(
        _final_kernel,
        grid=(nt,),
        in_specs=[pl.BlockSpec((tc, D_MODEL), lambda i: (i, 0)),
                  pl.BlockSpec((tc, 2), lambda i: (i, 0)),
                  pl.BlockSpec((tc * ROW_TILE, LANES), lambda i: (i, 0)),
                  pl.BlockSpec((tc * ROW_TILE, LANES), lambda i: (nt + i, 0)),
                  pl.BlockSpec((1, D_MODEL), lambda i: (0, 0))],
        out_specs=pl.BlockSpec((tc, D_MODEL), lambda i: (i, 0)),
        out_shape=jax.ShapeDtypeStruct((n, D_MODEL), F32),
        compiler_params=pltpu.CompilerParams(dimension_semantics=("parallel",),
                                             vmem_limit_bytes=VMEM_LIMIT),
        name="final",
    )(x2, w_slot, yslots, yslots, gf)


def _tile_at(row):
    return pl.ds(pl.multiple_of(row, ROW_TILE), ROW_TILE)


def _overlap_matrix_t(seq):
    nc = seq // CMP_STRIDE - CMP_LEN // CMP_STRIDE + 1
    ns = seq // SLC_LEN
    c0 = np.arange(nc) * CMP_STRIDE
    s0 = np.arange(ns) * SLC_LEN
    ov = np.clip(np.minimum(c0[:, None] + CMP_LEN, s0[None, :] + SLC_LEN)
                 - np.maximum(c0[:, None], s0[None, :]), 0, None) / CMP_LEN
    out = np.zeros((LANES, LANES), np.float32)
    out[FEAT_SEL:FEAT_SEL + ns, :nc] = ov.T
    return jnp.asarray(out, BF16)


def _position_features(seq):
    pos = np.arange(seq)
    fk = np.zeros((seq, LANES), np.float32)
    fk[:, FEAT_POS:FEAT_POS + 3] = (pos // SLC_LEN)[:, None]
    fk[:, FEAT_POS + 3:FEAT_POS + 6] = (pos % SLC_LEN)[:, None]
    fk[pos, FEAT_SEL + pos // SLC_LEN] = 1.0
    return jnp.asarray(fk)


def _pick_tile(n, pref):
    t = pref
    while n % t:
        t //= 2
    return t


def kernel(x, norm_mix_g, w_in, cmp_pos_k, cmp_w1_k, cmp_w2_k, cmp_pos_v, cmp_w1_v, cmp_w2_v, sinks, w_a, w_b,
           w_o, norm_ffn_g, w_group, b_group, w_expert, b_expert, w_gate_e, w_up_e, w_down_e, norm_final_g):
    bsz, seq, _ = x.shape
    n = bsz * seq
    assert TQ == LANES and seq % KCHUNK == 0 and seq // SLC_LEN <= N_SLC_BLK and seq // CMP_STRIDE <= LANES
    assert TILES_PER_STEP * TQ == KCHUNK and TQ % SUB == 0
    assert seq >= (NSA_WINDOW // TQ + 1) * TQ and w_in.shape[0] == 1
    x2d = x.reshape(n, D_MODEL)

    w = w_in[0]
    scale = HEAD_DIM ** -0.5 * LOG2E
    nsa_w, kvw = N_HEADS * HEAD_DIM, N_KV * HEAD_DIM
    o_qa, o_kva, o_gn = 0, nsa_w, nsa_w + 6 * kvw
    o_qb = o_gn + 3 * N_HEADS
    o_kvb = o_qb + nsa_w
    o_gm = o_kvb + 2 * kvw
    pair = lambda off: w[:, off:off + kvw]
    w_attn = jnp.concatenate(
        [w[:, o_qa:o_qa + nsa_w] * scale, w[:, o_qb:o_qb + nsa_w] * scale,
         pair(o_kva + 2 * kvw), pair(o_kva + 4 * kvw), pair(o_kvb),
         w[:, o_gn:o_gn + 3 * N_HEADS], jnp.zeros((D_MODEL, LANES - 3 * N_HEADS), F32),
         pair(o_kva + 3 * kvw), pair(o_kva + 5 * kvw), pair(o_kvb + kvw),
         w[:, o_kva:o_kva + 2 * kvw]], axis=1).astype(BF16)
    w_gm = w[:, o_gm:o_gm + 2 * D_MODEL].astype(BF16)

    tm = _pick_tile(seq, 512)
    feat_k = _position_features(seq)
    eye = jnp.eye(LANES, dtype=BF16)
    swap = jnp.roll(eye, HEAD_DIM, axis=1)
    proj, v_t, cmp_in = _proj(x2d, norm_mix_g[0][None], w_attn, feat_k, eye, swap, tm)
    proj3 = proj.reshape(bsz, seq, PROJ_W)

    nch = seq // CMP_STRIDE
    pos = jnp.stack([cmp_pos_k[0], cmp_pos_v[0]])
    pos = jnp.broadcast_to(pos[:, :, None, :], (2, CMP_LEN, N_KV, HEAD_DIM))
    pos_a = pos[:, :CMP_STRIDE].reshape(2, 1, CMP_STRIDE * kvw)
    pos_b = pos[:, CMP_STRIDE:].reshape(2, 1, CMP_STRIDE * kvw)
    w1 = jnp.stack([cmp_w1_k[0], cmp_w1_v[0]]).reshape(2, CMP_LEN, HEAD_DIM, CMP_HIDDEN)
    zero = jnp.zeros_like(w1)
    w1 = jnp.stack([jnp.concatenate([w1, zero], axis=-1), jnp.concatenate([zero, w1], axis=-1)], axis=2)
    w1 = w1.reshape(2, CMP_LEN * kvw, N_KV * CMP_HIDDEN).astype(BF16)
    w2 = jnp.pad(jnp.stack([cmp_w2_k[0], cmp_w2_v[0]]), ((0, 0), (0, 0), (0, LANES - HEAD_DIM))).astype(BF16)
    kvc, kvc_t = _compress(cmp_in.reshape(bsz, seq, CMP_W), pos_a, pos_b, w1[:, :CMP_STRIDE * kvw],
                           w1[:, CMP_STRIDE * kvw:], w2, jnp.swapaxes(w2, 1, 2))
    kvc = jnp.pad(kvc, ((0, 0), (0, 0), (0, 0), (0, LANES - nch), (0, 0)))
    kvc_t = jnp.pad(kvc_t, ((0, 0), (0, 0), (0, 0), (0, 0), (0, LANES - nch)))

    o_a, o_b = _attention(proj3, v_t, kvc, kvc_t, _overlap_matrix_t(seq), eye, sinks[0])

    w_r = jnp.concatenate([w_expert[0], w_group[0],
                           jnp.zeros((D_MODEL, LANES - N_EXPERTS - N_GROUPS), F32)], axis=1)
    w_r = w_r.T
    w_rh = w_r.astype(BF16)
    w_rl = (w_r - w_rh.astype(F32)).astype(BF16)
    b_r = jnp.concatenate([b_expert[0], b_group[0], jnp.zeros((LANES - N_EXPERTS - N_GROUPS,), F32)])
    b_r = b_r[:, None] * jnp.ones((1, LANES), F32)
    tt = _pick_tile(n, 512)
    x2, hn, route, counts = _merge(
        x2d, o_a.reshape(n, nsa_w), o_b.reshape(n, nsa_w), norm_mix_g[0][None], norm_ffn_g[0][None], w_gm,
        w_a[0].astype(BF16), w_b[0].astype(BF16), w_o[0].astype(BF16), w_rh, w_rl, b_r, tt)

    tb = 256
    nblk = -(-(2 * n + N_EXPERTS * (tb - 1)) // tb)
    cnt = counts[:N_EXPERTS, 0].astype(jnp.int32)
    padded = (cnt + tb - 1) // tb * tb
    pad_end = jnp.cumsum(padded)
    pad_start = pad_end - padded
    block_e = jnp.minimum(jnp.sum(pad_end[None, :] <= (jnp.arange(nblk) * tb)[:, None], axis=1), N_EXPERTS - 1)
    n_used = (pad_end[-1:] // tb).astype(jnp.int32)
    tab = route[:, 0:4, :].astype(jnp.int32)
    eids = tab[:, 0:2, :]
    start_of = sum(jnp.where(eids == e, pad_start[e], 0) for e in range(N_EXPERTS))
    pos = start_of + tab[:, 2:4, :]
    w_slot = jnp.swapaxes(route[:, 4:6, :], 1, 2).reshape(n, 2)
    spare = (2 * n + jnp.arange(nblk * tb, dtype=jnp.int32) % (2 * tb)) * ROW_TILE
    slot_tab = _slot_table(pos, spare, tt)
    y_slots = _moe(block_e.astype(jnp.int32), n_used, slot_tab, hn, w_gate_e[0], w_up_e[0], w_down_e[0], tb, n)
    out = _final(x2, w_slot, y_slots, norm_final_g[None], tt)
    return out.reshape(bsz, seq, D_MODEL)
```

```python
import functools

import numpy as np
import jax
import jax.numpy as jnp
from jax import lax
from jax.experimental import pallas as pl
from jax.experimental.pallas import tpu as pltpu

F32 = jnp.float32
BF16 = jnp.bfloat16

D_MODEL = 1024
HEAD_DIM = 64
N_HEADS = 8
N_KV = 2
GROUP = N_HEADS // N_KV
CMP_LEN = 32
CMP_STRIDE = 16
CMP_HIDDEN = 256
SLC_LEN = 64
SLC_TOPK = 8
NSA_WINDOW = 256
SWA_WINDOW = 128
N_GROUPS = 4
EPG = 8
N_EXPERTS = N_GROUPS * EPG
EXPERT_FF = 256
RMS_EPS = 1e-6
NEG_INF = -1e30
FORCE_SCORE = 1e9

LANES = 128
TQ = 128
TILES_PER_STEP = 4
KCHUNK = 512
SUB = 128
N_SLC_BLK = LANES // 4
FEAT_POS = HEAD_DIM
FEAT_SEL = HEAD_DIM + 6
ROWS_LO, ROWS_HI = 64, 104
N_KV_SECT = 6
COL_QB = 512
COL_KV = 1024
COL_GN = COL_KV + N_KV_SECT * LANES
PROJ_W = COL_GN + LANES
CMP_W = 2 * LANES
ROW_TILE = D_MODEL // LANES
VMEM_LIMIT = 56 * 1024 * 1024


LOG2E = float(np.log2(np.e))


def _alibi_slopes():
    n = 2 * N_HEADS
    s = 2.0 ** (-8.0 * np.arange(1, n + 1) / n) * LOG2E
    return [float(v) for v in s[:N_HEADS]], [float(v) for v in s[N_HEADS:]]


SLOPES_SWA, SLOPES_NSA = _alibi_slopes()


def _bf16_pieces(v):
    out, rem = [], np.float32(v)
    for _ in range(3):
        p = np.float32(np.asarray(rem, np.float32).astype(BF16).astype(np.float32))
        out.append(float(p))
        rem = np.float32(rem - p)
    return out


def _rms(x, g):
    return x * lax.rsqrt(jnp.mean(x * x, axis=-1, keepdims=True) + RMS_EPS) * g


def _dot(a, b):
    return jnp.dot(a, b, preferred_element_type=F32)


def _tiles_to_rows(ref, n, lead=()):
    return jnp.concatenate([ref[lead + (pl.ds(j, n, stride=ROW_TILE), slice(None))] for j in range(ROW_TILE)], axis=1)


def _rows_to_tiles(ref, val):
    n = val.shape[0]
    for j in range(ROW_TILE):
        ref[pl.ds(j, n, stride=ROW_TILE), :] = val[:, j * LANES:(j + 1) * LANES]


def _dot_nt(a, b):
    return lax.dot_general(a, b, (((1,), (1,)), ((), ())), preferred_element_type=F32)


def _proj_kernel(x_ref, g_ref, w_ref, fk_ref, eye_ref, swap_ref, o_ref, vt_ref, cmp_ref):
    tm = x_ref.shape[0]
    n_br = N_KV_SECT // N_KV
    col_gn = COL_KV + n_br * LANES
    col_v = col_gn + LANES
    h = _rms(x_ref[...], g_ref[...]).astype(BF16)
    res = _dot(h, w_ref[...])
    o_ref[:, :COL_KV] = res[:, :COL_KV].astype(o_ref.dtype)
    o_ref[:, COL_GN:] = res[:, col_gn:col_v].astype(o_ref.dtype)
    low = lax.broadcasted_iota(jnp.int32, (1, LANES), 1) < HEAD_DIM
    ones_row = jnp.where(lax.broadcasted_iota(jnp.int32, (LANES - HEAD_DIM, tm), 0) == 0, 1.0, 0.0)
    for br in range(n_br):
        k_pair = res[:, COL_KV + br * LANES:COL_KV + (br + 1) * LANES]
        heads = (k_pair, _dot(k_pair.astype(BF16), swap_ref[...]))
        for hd in range(N_KV):
            c0 = COL_KV + (br * N_KV + hd) * LANES
            o_ref[:, c0:c0 + LANES] = jnp.where(low, heads[hd], fk_ref[...]).astype(o_ref.dtype)
        v_pair_t = _dot_nt(eye_ref[...], res[:, col_v + br * LANES:col_v + (br + 1) * LANES].astype(BF16))
        for hd in range(N_KV):
            v_t = jnp.concatenate([v_pair_t[hd * HEAD_DIM:(hd + 1) * HEAD_DIM], ones_row], axis=0)
            for k in range(tm // LANES):
                vt_ref[br * N_KV + hd, 0, k] = v_t[:, k * LANES:(k + 1) * LANES].astype(vt_ref.dtype)
    cmp_ref[...] = res[:, col_v + n_br * LANES:]


def _proj(x2d, g, w, feat_k, eye, swap, tm):
    n = x2d.shape[0]
    seq = feat_k.shape[0]
    nper = seq // tm
    kb = tm // LANES
    return pl.pallas_call(
        _proj_kernel,
        grid=(n // tm,),
        in_specs=[pl.BlockSpec((tm, D_MODEL), lambda i: (i, 0)),
                  pl.BlockSpec((1, D_MODEL), lambda i: (0, 0)),
                  pl.BlockSpec(w.shape, lambda i: (0, 0)),
                  pl.BlockSpec((tm, LANES), lambda i: (i % nper, 0)),
                  pl.BlockSpec((LANES, LANES), lambda i: (0, 0)),
                  pl.BlockSpec((LANES, LANES), lambda i: (0, 0))],
        out_specs=[pl.BlockSpec((tm, PROJ_W), lambda i: (i, 0)),
                   pl.BlockSpec((N_KV_SECT, 1, kb, LANES, LANES), lambda i: (0, i // nper, i % nper, 0, 0)),
                   pl.BlockSpec((tm, CMP_W), lambda i: (i, 0))],
        out_shape=[jax.ShapeDtypeStruct((n, PROJ_W), BF16),
                   jax.ShapeDtypeStruct((N_KV_SECT, n // seq, seq // LANES, LANES, LANES), BF16),
                   jax.ShapeDtypeStruct((n, CMP_W), F32)],
        compiler_params=pltpu.CompilerParams(dimension_semantics=("parallel",),
                                             vmem_limit_bytes=VMEM_LIMIT),
        name="proj",
    )(x2d, g, w, feat_k, eye, swap)


def _compress_kernel(x_ref, pa_ref, pb_ref, w1a_ref, w1b_ref, w2_ref, w2t_ref, o_ref, ot_ref, *, nch):
    r = jnp.concatenate([x_ref[0, pl.ds(j, nch, stride=CMP_STRIDE), :] for j in range(CMP_STRIDE)], axis=1)
    a = _dot((r + pa_ref[0]).astype(BF16), w1a_ref[0])
    b = _dot((r + pb_ref[0]).astype(BF16), w1b_ref[0])
    hid = a + pltpu.roll(b, nch - 1, 0)
    hid = hid * jax.nn.sigmoid(hid)
    for h in range(N_KV):
        hid_h = hid[:, h * CMP_HIDDEN:(h + 1) * CMP_HIDDEN].astype(BF16)
        o_ref[0, 0, h] = _dot(hid_h, w2_ref[0]).astype(o_ref.dtype)
        ot_ref[0, 0, h] = _dot_nt(w2t_ref[0], hid_h).astype(ot_ref.dtype)


def _compress(cmp3, pos_a, pos_b, w1a, w1b, w2, w2t):
    bsz, seq, _ = cmp3.shape
    nch = seq // CMP_STRIDE
    wspec = lambda a: pl.BlockSpec((1,) + a.shape[1:], lambda b, j: (j, 0, 0))
    return pl.pallas_call(
        functools.partial(_compress_kernel, nch=nch),
        grid=(bsz, 2),
        in_specs=[pl.BlockSpec((1, seq, LANES), lambda b, j: (b, 0, j)),
                  wspec(pos_a), wspec(pos_b), wspec(w1a), wspec(w1b), wspec(w2), wspec(w2t)],
        out_specs=[pl.BlockSpec((1, 1, N_KV, nch, LANES), lambda b, j: (b, j, 0, 0, 0)),
                   pl.BlockSpec((1, 1, N_KV, LANES, nch), lambda b, j: (b, j, 0, 0, 0))],
        out_shape=[jax.ShapeDtypeStruct((bsz, 2, N_KV, nch, LANES), BF16),
                   jax.ShapeDtypeStruct((bsz, 2, N_KV, LANES, nch), BF16)],
        compiler_params=pltpu.CompilerParams(dimension_semantics=("parallel", "parallel"),
                                             vmem_limit_bytes=VMEM_LIMIT),
        name="compress",
    )(cmp3, pos_a, pos_b, w1a, w1b, w2, w2t)


def _attn_t_kernel(*refs, n_cmp):
    tiles = [_attn_tile(u, *refs, n_cmp=n_cmp) for u in range(TILES_PER_STEP)]
    while all([next(t) is PHASED for t in tiles]):
        pass
    for t in tiles:
        for _ in t:
            pass


PHASED, SWEEP = "phase done", "ready for the sweep"


def _attn_tile(u, sinks_ref, qa_ref, qb_ref, kc_ref, vct_ref, *rest, n_cmp):
    rows = slice(u * TQ, (u + 1) * TQ)
    n_br = 3
    ks = rest[:n_br * N_KV]
    vts = rest[n_br * N_KV:2 * n_br * N_KV]
    gn_ref, ovt_ref, eye_ref, oa_ref, ob_ref = rest[2 * n_br * N_KV:]
    ksect = lambda branch, h: ks[branch * N_KV + h]
    vsect = lambda branch, h: vts[branch * N_KV + h]
    qi = pl.program_id(1) * TILES_PER_STEP + u
    q0 = pl.multiple_of(qi * TQ, TQ)
    lane = lax.broadcasted_iota(jnp.int32, (1, TQ), 1)
    sub = lax.broadcasted_iota(jnp.int32, (LANES, 1), 0)
    t_row = q0 + lane
    eye = eye_ref[...]
    gates = jax.nn.sigmoid(_dot_nt(eye, gn_ref[0, rows, :]))
    gate = lambda hh, c: gates[3 * hh + c:3 * hh + c + 1, :]
    sub40 = sub[ROWS_LO:ROWS_HI]
    blk = sub40 - FEAT_SEL
    in_rng = (blk >= 0) & (blk < N_SLC_BLK)
    is_pos = (sub40 >= FEAT_POS) & (sub40 < FEAT_SEL)
    zeros_lo = jnp.zeros((LANES - ROWS_HI, TQ), F32)

    def q_t(ref, hh):
        both = _dot_nt(eye, ref[0, rows, (hh // 2) * LANES:(hh // 2 + 1) * LANES])
        return both[(hh % 2) * HEAD_DIM:(hh % 2 + 1) * HEAD_DIM]

    def slope_col(slope):
        hi, mid, lo = _bf16_pieces(slope)
        col = jnp.zeros(sub40.shape, F32)
        for i, v in enumerate([SLC_LEN * hi, SLC_LEN * mid, SLC_LEN * lo, hi, mid, lo]):
            col = jnp.where(sub40 == FEAT_POS + i, v, col)
        return col

    def q_aug_t(q_ts, tails):
        return jnp.concatenate([jnp.concatenate([q, jnp.broadcast_to(t, (ROWS_HI - ROWS_LO, TQ)), zeros_lo], axis=0)
                                for q, t in zip(q_ts, tails)], axis=1).astype(BF16)

    def v_t(ref, first, n):
        return jnp.concatenate([ref[0, 0, first + j] for j in range(n)], axis=1)

    def masked(s, mask):
        return jnp.concatenate([jnp.where(mask, s[:, g * TQ:(g + 1) * TQ], NEG_INF) for g in range(GROUP)], axis=1)

    def store_heads(ref, outs):
        for p in range(N_HEADS // 2):
            pair = jnp.concatenate([outs[2 * p][:HEAD_DIM], outs[2 * p + 1][:HEAD_DIM]], axis=0).astype(BF16)
            ref[0, rows, p * LANES:(p + 1) * LANES] = _dot_nt(eye, pair).astype(ref.dtype)

    nw = NSA_WINDOW // TQ + 1
    w_first = jnp.maximum(qi - (nw - 1), 0)
    w_start = pl.multiple_of(w_first * TQ, TQ)
    nb = (SWA_WINDOW - 1 + TQ - 1) // TQ + 1
    b_first = jnp.maximum(qi - (nb - 1), 0)
    b_start = pl.multiple_of(b_first * TQ, TQ)
    pre = []
    for h in range(N_KV):
        heads = [h * GROUP + g for g in range(GROUP)]
        q_ts = [q_t(qa_ref, hh) for hh in heads]
        scol = [slope_col(SLOPES_NSA[hh]) for hh in heads]
        s_cmp = _dot(kc_ref[0, 0, h], q_aug_t(q_ts, [jnp.zeros((1, 1), F32)] * GROUP))
        s_win = _dot(ksect(1, h)[0, pl.ds(w_start, nw * TQ), :], q_aug_t(q_ts, scol))
        q_b = q_aug_t([q_t(qb_ref, hh) for hh in heads], [slope_col(SLOPES_SWA[hh]) for hh in heads])
        s_swa = _dot(ksect(2, h)[0, pl.ds(b_start, nb * TQ), :], q_b)
        pre.append((q_ts, scol, s_cmp, s_win, s_swa))

    yield PHASED
    def nsa_local(h):
        heads = [h * GROUP + g for g in range(GROUP)]
        _, _, s_all, s_win, _ = pre[h]

        end_c = sub * CMP_STRIDE + (CMP_LEN - 1)
        cmask = (t_row >= end_c) & (sub < n_cmp)
        ps = []
        for g in range(GROUP):
            s = s_all[:, g * TQ:(g + 1) * TQ] + SLOPES_NSA[heads[g]] * end_c.astype(F32)
            s = jnp.where(cmask, s, NEG_INF)
            m = jnp.max(s, axis=0, keepdims=True)
            e = jnp.where(cmask, jnp.exp2(s - m), 0.0)
            z = jnp.sum(e, axis=0, keepdims=True)
            ps.append(e / jnp.where(z > 0, z, 1.0))
        o_cmp = _dot(vct_ref[0, 0, h], jnp.concatenate(ps, axis=1).astype(BF16))

        dist = t_row - (w_start + lax.broadcasted_iota(jnp.int32, (nw * TQ, 1), 0))
        s = masked(s_win, (dist >= 0) & (dist < NSA_WINDOW))
        m = jnp.max(s, axis=0, keepdims=True)
        acc = _dot(v_t(vsect(1, h), w_first, nw), jnp.exp2(s - m).astype(BF16))
        o_win = acc / acc[HEAD_DIM:HEAD_DIM + 1, :]

        psum = ps[0] + ps[1] + ps[2] + ps[3]
        p_hi = psum.astype(BF16)
        p_lo = (psum - p_hi.astype(F32)).astype(BF16)
        imp = (_dot(ovt_ref[...], p_hi) + _dot(ovt_ref[...], p_lo))[ROWS_LO:ROWS_HI]
        part = [gate(hh, 0) * o_cmp[:, g * TQ:(g + 1) * TQ] + gate(hh, 2) * o_win[:, g * TQ:(g + 1) * TQ]
                for g, hh in enumerate(heads)]
        return imp, part

    def nsa_select(h, imp):
        q_ts, scol = pre[h][:2]
        cur = lax.shift_right_logical(t_row, int(np.log2(SLC_LEN)))
        valid = in_rng & (blk * SLC_LEN <= t_row)
        forced = in_rng & ((blk == 0) | (blk == cur) | (blk == cur - 1))
        score = jnp.where(forced, FORCE_SCORE, jnp.where(valid, imp, NEG_INF))
        rank = jnp.zeros(score.shape, F32)
        for i in range(N_SLC_BLK):
            r = FEAT_SEL - ROWS_LO + i
            si = score[r:r + 1, :]
            rank = rank + jnp.where((si > score) | ((si == score) & (blk > i)), 1.0, 0.0)
        sel = in_rng & (rank < SLC_TOPK) & (score > 0.5 * NEG_INF)
        bias_diag = jnp.where(in_rng & jnp.logical_not(sel), NEG_INF, 0.0)
        bias_main = jnp.where(in_rng & jnp.logical_not(sel & (blk < 2 * qi)), NEG_INF, 0.0)

        q_d = q_aug_t(q_ts, [jnp.where(is_pos, scol[g], bias_diag) for g in range(GROUP)])
        q_m = q_aug_t(q_ts, [jnp.where(is_pos, scol[g], bias_main) for g in range(GROUP)])
        return q_m, _dot(ksect(0, h)[0, pl.ds(q0, TQ), :], q_d)

    def nsa_diag(h, q_m, s_diag):
        s = masked(s_diag, q0 + sub <= t_row)
        m0 = jnp.max(s, axis=0, keepdims=True)
        acc0 = _dot(v_t(vsect(0, h), qi, 1), jnp.exp2(s - m0).astype(BF16))
        return q_m, m0, acc0

    local = [nsa_local(h) for h in range(N_KV)]
    yield PHASED

    dist = t_row - (b_start + lax.broadcasted_iota(jnp.int32, (nb * TQ, 1), 0))
    bmask = (dist >= 0) & (dist < SWA_WINDOW)
    outs_b = []
    for h in range(N_KV):
        heads = [h * GROUP + g for g in range(GROUP)]
        s = masked(pre[h][4], bmask)
        sink = jnp.concatenate([sinks_ref[hh] * LOG2E + SLOPES_SWA[hh] * t_row.astype(F32) for hh in heads], axis=1)
        m = jnp.maximum(jnp.max(s, axis=0, keepdims=True), sink)
        acc = _dot(v_t(vsect(2, h), b_first, nb), jnp.exp2(s - m).astype(BF16))
        o_all = acc / (acc[HEAD_DIM:HEAD_DIM + 1, :] + jnp.exp2(sink - m))
        outs_b += [o_all[:, g * TQ:(g + 1) * TQ] for g in range(GROUP)]
    store_heads(ob_ref, outs_b)

    yield PHASED
    selected = [nsa_select(h, local[h][0]) for h in range(N_KV)]
    yield PHASED
    fronts = [nsa_diag(h, *selected[h]) + (local[h][1],) for h in range(N_KV)]
    nblk = KCHUNK // TQ

    def absorb(carry, c, pieces):
        state = list(carry)
        subs = [(j, h) for j in pieces for h in range(N_KV)]
        scores = [_dot(ksect(0, h)[0, pl.ds(pl.multiple_of(c * KCHUNK + j * SUB, SUB), SUB), :], fronts[h][0])
                  for j, h in subs]
        for (j, h), s in zip(subs, scores):
            m_i, acc = state[h]
            m_new = jnp.maximum(m_i, jnp.max(s, axis=0, keepdims=True))
            pv = _dot(v_t(vsect(0, h), c * nblk + j * (SUB // TQ), SUB // TQ), jnp.exp2(s - m_new).astype(BF16))
            state[h] = (m_new, jnp.exp2(m_i - m_new) * acc + pv)
        return tuple(state)

    yield SWEEP
    step_chunk = pl.program_id(1)
    swept = lax.fori_loop(0, step_chunk, lambda c, st: absorb(st, c, range(KCHUNK // SUB)),
                          tuple((f[1], f[2]) for f in fronts))
    if u:
        swept = absorb(swept, step_chunk, range(u * TQ // SUB))
    outs_a = []
    for h in range(N_KV):
        acc = swept[h][1]
        o_slc = acc / acc[HEAD_DIM:HEAD_DIM + 1, :]
        outs_a += [fronts[h][3][g] + gate(h * GROUP + g, 1) * o_slc[:, g * TQ:(g + 1) * TQ] for g in range(GROUP)]
    store_heads(oa_ref, outs_a)


def _attention(proj3, v_t, kc, vc_t, ovt, eye, sinks):
    bsz, seq, _ = proj3.shape
    n_cmp = seq // CMP_STRIDE - CMP_LEN // CMP_STRIDE + 1
    width = N_HEADS * HEAD_DIM
    consts = [ovt, eye]
    tq = TILES_PER_STEP * TQ
    qspec = lambda col: pl.BlockSpec((1, tq, width), lambda b, q, s: (b, q, col))
    cspec = lambda a, j: pl.BlockSpec((1, 1) + a.shape[2:], lambda b, q, s: (b, j, 0, 0, 0))
    in_specs = [qspec(0), qspec(COL_QB // width), cspec(kc, 0), cspec(vc_t, 1)]
    in_specs += [pl.BlockSpec((1, seq, LANES), lambda b, q, s, j=j: (b, 0, COL_KV // LANES + j))
                 for j in range(N_KV_SECT)]
    in_specs += [pl.BlockSpec((1, 1) + v_t.shape[2:], lambda b, q, s, j=j: (j, b, 0, 0, 0)) for j in range(N_KV_SECT)]
    in_specs += [pl.BlockSpec((1, tq, LANES), lambda b, q, s: (b, q, COL_GN // LANES))]
    in_specs += [pl.BlockSpec(c.shape, lambda b, q, s: (0, 0)) for c in consts]
    ospec = pl.BlockSpec((1, tq, width), lambda b, q, s: (b, q, 0))
    return pl.pallas_call(
        functools.partial(_attn_t_kernel, n_cmp=n_cmp),
        grid_spec=pltpu.PrefetchScalarGridSpec(
            num_scalar_prefetch=1, grid=(bsz, seq // tq), in_specs=in_specs, out_specs=[ospec, ospec]),
        out_shape=[jax.ShapeDtypeStruct((bsz, seq, width), BF16)] * 2,
        compiler_params=pltpu.CompilerParams(dimension_semantics=("parallel", "parallel"),
                                             vmem_limit_bytes=VMEM_LIMIT),
        name="attn",
    )(sinks, proj3, proj3, kc, vc_t, *([proj3] * N_KV_SECT), *([v_t] * N_KV_SECT), proj3, *consts)


def _merge_kernel(x_ref, oa_ref, ob_ref, g1_ref, g2_ref, wg_ref, wa_ref, wb_ref, wo_ref,
                  wrh_ref, wrl_ref, br_ref, x2_ref, hn_ref, route_ref, cnt_ref):
    tm = x_ref.shape[0]

    @pl.when(pl.program_id(0) == 0)
    def _():
        cnt_ref[...] = jnp.zeros_like(cnt_ref)

    halves = [slice(i * (tm // 2), (i + 1) * (tm // 2)) for i in range(2)]
    xs = [x_ref[r, :] for r in halves]
    pre = []
    for r, x in zip(halves, xs):
        h = _rms(x, g1_ref[...]).astype(BF16)
        pre.append((_dot(h, wg_ref[...]), _dot(oa_ref[r, :], wa_ref[...]), _dot(ob_ref[r, :], wb_ref[...])))
    mixes = []
    for g_pre, a, b in pre:
        gm = jax.nn.sigmoid(g_pre)
        mixin = gm[:, :D_MODEL] * a + gm[:, D_MODEL:] * b
        mixes.append(_dot(mixin.astype(BF16), wo_ref[...]))
    logit_halves = []
    for r, x, mix in zip(halves, xs, mixes):
        x2 = x + mix
        x2_ref[r, :] = x2
        hn = _rms(x2, g2_ref[...])
        for j in range(ROW_TILE):
            hn_ref[pl.ds(r.start * ROW_TILE + j, tm // 2, stride=ROW_TILE), :] = hn[:, j * LANES:(j + 1) * LANES]
        hn_b = hn.astype(BF16)
        hn_lo = (hn - hn_b.astype(F32)).astype(BF16)
        logit_halves.append(_dot_nt(wrh_ref[...], hn_b) + _dot_nt(wrh_ref[...], hn_lo) + _dot_nt(wrl_ref[...], hn_b))
    bias = br_ref[...]
    logits = jnp.concatenate(logit_halves, axis=1) + jnp.concatenate([bias] * (tm // LANES), axis=1)
    row = lax.broadcasted_iota(jnp.int32, (LANES, 1), 0)
    rowf = row.astype(F32)
    big = float(LANES)
    top = lambda a: jnp.max(a, axis=0, keepdims=True)
    first = lambda hit: jnp.min(jnp.where(hit, rowf, big), axis=0, keepdims=True)
    is_g = (row >= N_EXPERTS) & (row < N_EXPERTS + N_GROUPS)
    gl = jnp.where(is_g, logits, NEG_INF)
    gmax = top(gl)
    grp = first(gl == gmax) - N_EXPERTS
    p_grp = 1.0 / jnp.sum(jnp.where(is_g, jnp.exp(gl - gmax), 0.0), axis=0, keepdims=True)
    in_grp = (rowf >= grp * EPG) & (rowf < grp * EPG + EPG)
    el = jnp.where(in_grp, logits, NEG_INF)
    v0 = top(el)
    i0 = first(el == v0)
    el1 = jnp.where(rowf == i0, NEG_INF, el)
    v1 = top(el1)
    i1 = first(el1 == v1)
    e1 = jnp.exp(v1 - v0)
    w0 = p_grp / (1.0 + e1)
    w1 = p_grp * e1 / (1.0 + e1)

    oh0 = jnp.where(rowf == i0, 1.0, 0.0)
    oh1 = jnp.where(rowf == i1, 1.0, 0.0)
    oh = oh0 + oh1
    r_i = lax.broadcasted_iota(jnp.int32, (tm, tm), 0)
    c_i = lax.broadcasted_iota(jnp.int32, (tm, tm), 1)
    earlier = jnp.where(r_i < c_i, 1.0, 0.0).astype(BF16)
    before = cnt_ref[...] + _dot(oh.astype(BF16), earlier)
    rank0 = jnp.sum(oh0 * before, axis=0, keepdims=True)
    rank1 = jnp.sum(oh1 * before, axis=0, keepdims=True)
    cnt_ref[...] = cnt_ref[...] + jnp.sum(oh, axis=1, keepdims=True)
    row8 = row[:8]
    route = jnp.zeros((8, tm), F32)
    for k, v in enumerate((i0, i1, rank0, rank1, w0, w1)):
        route = jnp.where(row8 == k, v, route)
    route_ref[0] = route


def _merge(x2d, oa, ob, g1, g2, wg, wa, wb, wo, wrh, wrl, br, tm):
    n = x2d.shape[0]
    width = N_HEADS * HEAD_DIM
    row = lambda w: pl.BlockSpec((tm, w), lambda i: (i, 0))
    full = lambda a: pl.BlockSpec(a.shape, lambda i: (0, 0))
    return pl.pallas_call(
        _merge_kernel,
        grid=(n // tm,),
        in_specs=[row(D_MODEL), row(width), row(width), full(g1), full(g2), full(wg), full(wa), full(wb),
                  full(wo), full(wrh), full(wrl), full(br)],
        out_specs=[row(D_MODEL), pl.BlockSpec((tm * ROW_TILE, LANES), lambda i: (i, 0)),
                   pl.BlockSpec((1, 8, tm), lambda i: (i, 0, 0)), pl.BlockSpec((LANES, 1), lambda i: (0, 0))],
        out_shape=[jax.ShapeDtypeStruct((n, D_MODEL), F32), jax.ShapeDtypeStruct((n * ROW_TILE, LANES), F32),
                   jax.ShapeDtypeStruct((n // tm, 8, tm), F32), jax.ShapeDtypeStruct((LANES, 1), F32)],
        compiler_params=pltpu.CompilerParams(dimension_semantics=("arbitrary",),
                                             vmem_limit_bytes=VMEM_LIMIT),
        name="merge",
    )(x2d, oa, ob, g1, g2, wg, wa, wb, wo, wrh, wrl, br)


def _slot_table_kernel(pos_ref, init_ref, tab_ref, sem, *, tt, n):
    step = pl.program_id(0)

    @pl.when(step == 0)
    def _():
        cp = pltpu.make_async_copy(init_ref, tab_ref, sem)
        cp.start()
        cp.wait()

    def chunk(c, row):
        base = pl.multiple_of(c * LANES, LANES)
        for k in range(LANES):
            tab_ref[pos_ref[0, 0, base + k]] = row + k * ROW_TILE
            tab_ref[pos_ref[0, 1, base + k]] = row + (n + k) * ROW_TILE
        return row + LANES * ROW_TILE

    lax.fori_loop(0, tt // LANES, chunk, step * (tt * ROW_TILE))


def _slot_table(pos, init, tt):
    nt = pos.shape[0]
    return pl.pallas_call(
        functools.partial(_slot_table_kernel, tt=tt, n=nt * tt),
        grid=(nt,),
        in_specs=[pl.BlockSpec((1, 2, tt), lambda i: (i, 0, 0), memory_space=pltpu.SMEM),
                  pl.BlockSpec(memory_space=pl.ANY)],
        out_specs=pl.BlockSpec(memory_space=pltpu.SMEM),
        out_shape=jax.ShapeDtypeStruct(init.shape, jnp.int32),
        scratch_shapes=[pltpu.SemaphoreType.DMA(())],
        compiler_params=pltpu.CompilerParams(dimension_semantics=("arbitrary",)),
        name="slot_table",
    )(pos, init)


def _moe_kernel(be_ref, nu_ref, tab_ref, tab_next_ref, hn_ref, wg_ref, wu_ref, wd_ref, out_ref,
                xin, yout, wg_s, wu_s, wd_s, gsem, ssem, *, tb, n):
    b = pl.program_id(0)
    n_used = nu_ref[0]
    live = b < n_used
    s = lax.rem(b, 2)

    def gather(tab, slot):
        for j in range(tb):
            t, rows = tab[0, 0, j], n * ROW_TILE
            src = jnp.bitwise_and(t, rows - 1) if rows & (rows - 1) == 0 else lax.rem(t, rows)
            pltpu.make_async_copy(hn_ref.at[_tile_at(src)], xin.at[slot, pl.ds(j * ROW_TILE, ROW_TILE)],
                                  gsem.at[slot]).start(priority=0)

    def wait_rows(sem_ref, slot):
        pltpu.make_async_copy(hn_ref.at[pl.ds(0, tb * ROW_TILE)], xin.at[slot], sem_ref.at[slot]).wait()

    @pl.when(b == 0)
    def _():
        yout[...] = jnp.zeros_like(yout)
        fills = [pltpu.make_async_copy(yout.at[k], out_ref.at[pl.ds((2 * n + k * tb) * ROW_TILE, tb * ROW_TILE)],
                                       ssem.at[k]) for k in range(2)]
        for cp in fills:
            cp.start()
        for cp in fills:
            cp.wait()
        gather(tab_ref, 0)

    @pl.when(live & ((b == 0) | (be_ref[b] != be_ref[jnp.maximum(b - 1, 0)])))
    def _():
        wg_s[...] = wg_ref[0].astype(BF16)
        wu_s[...] = wu_ref[0].astype(BF16)
        wd_s[...] = wd_ref[0].astype(BF16)

    @pl.when(live & (b >= 2))
    def _():
        wait_rows(ssem, s)

    def block(s):
        wait_rows(gsem, s)
        gather(tab_next_ref, 1 - s)
        half = tb // 2
        gu = []
        for i in range(2):
            xb = jnp.concatenate([xin[s, pl.ds(i * half * ROW_TILE + j, half, stride=ROW_TILE), :]
                                  for j in range(ROW_TILE)], axis=1).astype(BF16)
            gu.append((_dot(xb, wg_s[...]), _dot(xb, wu_s[...])))
        for i, (g, u) in enumerate(gu):
            y = _dot((g * jax.nn.sigmoid(g) * u).astype(BF16), wd_s[...])
            for j in range(ROW_TILE):
                yout[s, pl.ds(i * half * ROW_TILE + j, half, stride=ROW_TILE), :] = y[:, j * LANES:(j + 1) * LANES]

        for j in range(tb):
            pltpu.make_async_copy(yout.at[s, pl.ds(j * ROW_TILE, ROW_TILE)], out_ref.at[_tile_at(tab_ref[0, 0, j])],
                                  ssem.at[s]).start(priority=1)

    for parity in range(2):
        pl.when(live & (s == parity))(functools.partial(block, parity))

    @pl.when(live & (b == n_used - 1))
    def _():
        wait_rows(gsem, 1 - s)
        wait_rows(ssem, s)

        @pl.when(b >= 1)
        def _():
            wait_rows(ssem, 1 - s)


def _moe(block_e, n_used, tab, hn, wg, wu, wd, tb, n):
    nblk = tab.shape[0] // tb
    tab2 = tab.reshape(nblk, 1, tb)
    live = lambda b, be, nu: jnp.minimum(b, nu[0] - 1)
    nxt = lambda b, be, nu: jnp.minimum(b + 1, nu[0] - 1)
    wspec = lambda shape: pl.BlockSpec((1,) + shape, lambda b, be, nu: (be[live(b, be, nu)], 0, 0))
    return pl.pallas_call(
        functools.partial(_moe_kernel, tb=tb, n=n),
        grid_spec=pltpu.PrefetchScalarGridSpec(
            num_scalar_prefetch=2, grid=(nblk,),
            in_specs=[pl.BlockSpec((1, 1, tb), lambda b, be, nu: (live(b, be, nu), 0, 0), memory_space=pltpu.SMEM),
                      pl.BlockSpec((1, 1, tb), lambda b, be, nu: (nxt(b, be, nu), 0, 0), memory_space=pltpu.SMEM),
                      pl.BlockSpec(memory_space=pl.ANY),
                      wspec((D_MODEL, EXPERT_FF)), wspec((D_MODEL, EXPERT_FF)), wspec((EXPERT_FF, D_MODEL))],
            out_specs=pl.BlockSpec(memory_space=pl.ANY),
            scratch_shapes=[pltpu.VMEM((2, tb * ROW_TILE, LANES), F32), pltpu.VMEM((2, tb * ROW_TILE, LANES), F32),
                            pltpu.VMEM((D_MODEL, EXPERT_FF), BF16), pltpu.VMEM((D_MODEL, EXPERT_FF), BF16),
                            pltpu.VMEM((EXPERT_FF, D_MODEL), BF16),
                            pltpu.SemaphoreType.DMA((2,)), pltpu.SemaphoreType.DMA((2,))]),
        out_shape=jax.ShapeDtypeStruct(((2 * n + 2 * tb) * ROW_TILE, LANES), F32),
        compiler_params=pltpu.CompilerParams(dimension_semantics=("arbitrary",),
                                             vmem_limit_bytes=VMEM_LIMIT),
        name="moe",
    )(block_e, n_used, tab2, tab2, hn, wg, wu, wd)


def _final_kernel(x2_ref, w_ref, y0_ref, y1_ref, gf_ref, o_ref):
    tc = x2_ref.shape[0]
    w = w_ref[...]
    y = x2_ref[...] + (w[:, 0:1] * _tiles_to_rows(y0_ref, tc) + w[:, 1:2] * _tiles_to_rows(y1_ref, tc))
    o_ref[...] = _rms(y, gf_ref[...])


def _final(x2, w_slot, yslots, gf, tc):
    n = x2.shape[0]
    nt = n // tc
    return pl.pallas_call(
        _final_kernel,
        grid=(nt,),
        in_specs=[pl.BlockSpec((tc, D_MODEL), lambda i: (i, 0)),
                  pl.BlockSpec((tc, 2), lambda i: (i, 0)),
                  pl.BlockSpec((tc * ROW_TILE, LANES), lambda i: (i, 0)),
                  pl.BlockSpec((tc * ROW_TILE, LANES), lambda i: (nt + i, 0)),
                  pl.BlockSpec((1, D_MODEL), lambda i: (0, 0))],
        out_specs=pl.BlockSpec((tc, D_MODEL), lambda i: (i, 0)),
        out_shape=jax.ShapeDtypeStruct((n, D_MODEL), F32),
        compiler_params=pltpu.CompilerParams(dimension_semantics=("parallel",),
                                             vmem_limit_bytes=VMEM_LIMIT),
        name="final",
    )(x2, w_slot, yslots, yslots, gf)


def _tile_at(row):
    return pl.ds(pl.multiple_of(row, ROW_TILE), ROW_TILE)


def _overlap_matrix_t(seq):
    nc = seq // CMP_STRIDE - CMP_LEN // CMP_STRIDE + 1
    ns = seq // SLC_LEN
    c0 = np.arange(nc) * CMP_STRIDE
    s0 = np.arange(ns) * SLC_LEN
    ov = np.clip(np.minimum(c0[:, None] + CMP_LEN, s0[None, :] + SLC_LEN)
                 - np.maximum(c0[:, None], s0[None, :]), 0, None) / CMP_LEN
    out = np.zeros((LANES, LANES), np.float32)
    out[FEAT_SEL:FEAT_SEL + ns, :nc] = ov.T
    return jnp.asarray(out, BF16)


def _position_features(seq):
    pos = np.arange(seq)
    fk = np.zeros((seq, LANES), np.float32)
    fk[:, FEAT_POS:FEAT_POS + 3] = (pos // SLC_LEN)[:, None]
    fk[:, FEAT_POS + 3:FEAT_POS + 6] = (pos % SLC_LEN)[:, None]
    fk[pos, FEAT_SEL + pos // SLC_LEN] = 1.0
    return jnp.asarray(fk)


def _pick_tile(n, pref):
    t = pref
    while n % t:
        t //= 2
    return t


def kernel(x, norm_mix_g, w_in, cmp_pos_k, cmp_w1_k, cmp_w2_k, cmp_pos_v, cmp_w1_v, cmp_w2_v, sinks, w_a, w_b,
           w_o, norm_ffn_g, w_group, b_group, w_expert, b_expert, w_gate_e, w_up_e, w_down_e, norm_final_g):
    bsz, seq, _ = x.shape
    n = bsz * seq
    assert TQ == LANES and seq % KCHUNK == 0 and seq // SLC_LEN <= N_SLC_BLK and seq // CMP_STRIDE <= LANES
    assert TILES_PER_STEP * TQ == KCHUNK and TQ % SUB == 0
    assert seq >= (NSA_WINDOW // TQ + 1) * TQ and w_in.shape[0] == 1
    x2d = x.reshape(n, D_MODEL)

    w = w_in[0]
    scale = HEAD_DIM ** -0.5 * LOG2E
    nsa_w, kvw = N_HEADS * HEAD_DIM, N_KV * HEAD_DIM
    o_qa, o_kva, o_gn = 0, nsa_w, nsa_w + 6 * kvw
    o_qb = o_gn + 3 * N_HEADS
    o_kvb = o_qb + nsa_w
    o_gm = o_kvb + 2 * kvw
    pair = lambda off: w[:, off:off + kvw]
    w_attn = jnp.concatenate(
        [w[:, o_qa:o_qa + nsa_w] * scale, w[:, o_qb:o_qb + nsa_w] * scale,
         pair(o_kva + 2 * kvw), pair(o_kva + 4 * kvw), pair(o_kvb),
         w[:, o_gn:o_gn + 3 * N_HEADS], jnp.zeros((D_MODEL, LANES - 3 * N_HEADS), F32),
         pair(o_kva + 3 * kvw), pair(o_kva + 5 * kvw), pair(o_kvb + kvw),
         w[:, o_kva:o_kva + 2 * kvw]], axis=1).astype(BF16)
    w_gm = w[:, o_gm:o_gm + 2 * D_MODEL].astype(BF16)

    tm = _pick_tile(seq, 512)
    feat_k = _position_features(seq)
    eye = jnp.eye(LANES, dtype=BF16)
    swap = jnp.roll(eye, HEAD_DIM, axis=1)
    proj, v_t, cmp_in = _proj(x2d, norm_mix_g[0][None], w_attn, feat_k, eye, swap, tm)
    proj3 = proj.reshape(bsz, seq, PROJ_W)

    nch = seq // CMP_STRIDE
    pos = jnp.stack([cmp_pos_k[0], cmp_pos_v[0]])
    pos = jnp.broadcast_to(pos[:, :, None, :], (2, CMP_LEN, N_KV, HEAD_DIM))
    pos_a = pos[:, :CMP_STRIDE].reshape(2, 1, CMP_STRIDE * kvw)
    pos_b = pos[:, CMP_STRIDE:].reshape(2, 1, CMP_STRIDE * kvw)
    w1 = jnp.stack([cmp_w1_k[0], cmp_w1_v[0]]).reshape(2, CMP_LEN, HEAD_DIM, CMP_HIDDEN)
    zero = jnp.zeros_like(w1)
    w1 = jnp.stack([jnp.concatenate([w1, zero], axis=-1), jnp.concatenate([zero, w1], axis=-1)], axis=2)
    w1 = w1.reshape(2, CMP_LEN * kvw, N_KV * CMP_HIDDEN).astype(BF16)
    w2 = jnp.pad(jnp.stack([cmp_w2_k[0], cmp_w2_v[0]]), ((0, 0), (0, 0), (0, LANES - HEAD_DIM))).astype(BF16)
    kvc, kvc_t = _compress(cmp_in.reshape(bsz, seq, CMP_W), pos_a, pos_b, w1[:, :CMP_STRIDE * kvw],
                           w1[:, CMP_STRIDE * kvw:], w2, jnp.swapaxes(w2, 1, 2))
    kvc = jnp.pad(kvc, ((0, 0), (0, 0), (0, 0), (0, LANES - nch), (0, 0)))
    kvc_t = jnp.pad(kvc_t, ((0, 0), (0, 0), (0, 0), (0, 0), (0, LANES - nch)))

    o_a, o_b = _attention(proj3, v_t, kvc, kvc_t, _overlap_matrix_t(seq), eye, sinks[0])

    w_r = jnp.concatenate([w_expert[0], w_group[0],
                           jnp.zeros((D_MODEL, LANES - N_EXPERTS - N_GROUPS), F32)], axis=1)
    w_r = w_r.T
    w_rh = w_r.astype(BF16)
    w_rl = (w_r - w_rh.astype(F32)).astype(BF16)
    b_r = jnp.concatenate([b_expert[0], b_group[0], jnp.zeros((LANES - N_EXPERTS - N_GROUPS,), F32)])
    b_r = b_r[:, None] * jnp.ones((1, LANES), F32)
    tt = _pick_tile(n, 512)
    x2, hn, route, counts = _merge(
        x2d, o_a.reshape(n, nsa_w), o_b.reshape(n, nsa_w), norm_mix_g[0][None], norm_ffn_g[0][None], w_gm,
        w_a[0].astype(BF16), w_b[0].astype(BF16), w_o[0].astype(BF16), w_rh, w_rl, b_r, tt)

    tb = 128
    nblk = -(-(2 * n + N_EXPERTS * (tb - 1)) // tb)
    cnt = counts[:N_EXPERTS, 0].astype(jnp.int32)
    padded = (cnt + tb - 1) // tb * tb
    pad_end = jnp.cumsum(padded)
    pad_start = pad_end - padded
    block_e = jnp.minimum(jnp.sum(pad_end[None, :] <= (jnp.arange(nblk) * tb)[:, None], axis=1), N_EXPERTS - 1)
    n_used = (pad_end[-1:] // tb).astype(jnp.int32)
    nt = n // tt
    eids = route[:, 0:2, :].reshape(2 * nt, tt).astype(jnp.int32)
    ranks = route[:, 2:4, :].reshape(2 * nt, tt).astype(jnp.int32)
    start_of = sum(jnp.where(eids == e, pad_start[e], 0) for e in range(N_EXPERTS))
    pos = (start_of + ranks).reshape(nt, 2, tt)
    w_slot = jnp.swapaxes(route[:, 4:6, :], 1, 2).reshape(n, 2)
    spare = (2 * n + jnp.arange(nblk * tb, dtype=jnp.int32) % (2 * tb)) * ROW_TILE
    slot_tab = _slot_table(pos, spare, tt)
    y_slots = _moe(block_e.astype(jnp.int32), n_used, slot_tab, hn, w_gate_e[0], w_up_e[0], w_down_e[0], tb, n)
    out = _final(x2, w_slot, y_slots, norm_final_g[None], tt)
    return out.reshape(bsz, seq, D_MODEL)
```

```python
import functools

import numpy as np
import jax
import jax.numpy as jnp
from jax import lax
from jax.experimental import pallas as pl
from jax.experimental.pallas import tpu as pltpu

F32 = jnp.float32
BF16 = jnp.bfloat16

D_MODEL = 1024
HEAD_DIM = 64
N_HEADS = 8
N_KV = 2
GROUP = N_HEADS // N_KV
CMP_LEN = 32
CMP_STRIDE = 16
CMP_HIDDEN = 256
SLC_LEN = 64
SLC_TOPK = 8
NSA_WINDOW = 256
SWA_WINDOW = 128
N_GROUPS = 4
EPG = 8
N_EXPERTS = N_GROUPS * EPG
EXPERT_FF = 256
RMS_EPS = 1e-6
NEG_INF = -1e30
FORCE_SCORE = 1e9

LANES = 128
TQ = 128
TILES_PER_STEP = 4
KCHUNK = 512
SUB = 128
N_SLC_BLK = LANES // 4
FEAT_POS = HEAD_DIM
FEAT_SEL = HEAD_DIM + 6
ROWS_LO, ROWS_HI = 64, 104
N_KV_SECT = 6
COL_QB = 512
COL_KV = 1024
COL_GN = COL_KV + N_KV_SECT * LANES
PROJ_W = COL_GN + LANES
CMP_W = 2 * LANES
ROW_TILE = D_MODEL // LANES
VMEM_LIMIT = 56 * 1024 * 1024


LOG2E = float(np.log2(np.e))


def _alibi_slopes():
    n = 2 * N_HEADS
    s = 2.0 ** (-8.0 * np.arange(1, n + 1) / n) * LOG2E
    return [float(v) for v in s[:N_HEADS]], [float(v) for v in s[N_HEADS:]]


SLOPES_SWA, SLOPES_NSA = _alibi_slopes()


def _bf16_pieces(v):
    out, rem = [], np.float32(v)
    for _ in range(3):
        p = np.float32(np.asarray(rem, np.float32).astype(BF16).astype(np.float32))
        out.append(float(p))
        rem = np.float32(rem - p)
    return out


def _rms(x, g):
    return x * lax.rsqrt(jnp.mean(x * x, axis=-1, keepdims=True) + RMS_EPS) * g


def _dot(a, b):
    return jnp.dot(a, b, preferred_element_type=F32)


def _tiles_to_rows(ref, n, lead=()):
    return jnp.concatenate([ref[lead + (pl.ds(j, n, stride=ROW_TILE), slice(None))] for j in range(ROW_TILE)], axis=1)


def _rows_to_tiles(ref, val):
    n = val.shape[0]
    for j in range(ROW_TILE):
        ref[pl.ds(j, n, stride=ROW_TILE), :] = val[:, j * LANES:(j + 1) * LANES]


def _dot_nt(a, b):
    return lax.dot_general(a, b, (((1,), (1,)), ((), ())), preferred_element_type=F32)


def _proj_kernel(x_ref, g_ref, w_ref, fk_ref, eye_ref, swap_ref, o_ref, vt_ref, cmp_ref):
    tm = x_ref.shape[0]
    n_br = N_KV_SECT // N_KV
    col_gn = COL_KV + n_br * LANES
    col_v = col_gn + LANES
    h = _rms(x_ref[...], g_ref[...]).astype(BF16)
    res = _dot(h, w_ref[...])
    o_ref[:, :COL_KV] = res[:, :COL_KV].astype(o_ref.dtype)
    o_ref[:, COL_GN:] = res[:, col_gn:col_v].astype(o_ref.dtype)
    low = lax.broadcasted_iota(jnp.int32, (1, LANES), 1) < HEAD_DIM
    ones_row = jnp.where(lax.broadcasted_iota(jnp.int32, (LANES - HEAD_DIM, tm), 0) == 0, 1.0, 0.0)
    for br in range(n_br):
        k_pair = res[:, COL_KV + br * LANES:COL_KV + (br + 1) * LANES]
        heads = (k_pair, _dot(k_pair.astype(BF16), swap_ref[...]))
        for hd in range(N_KV):
            c0 = COL_KV + (br * N_KV + hd) * LANES
            o_ref[:, c0:c0 + LANES] = jnp.where(low, heads[hd], fk_ref[...]).astype(o_ref.dtype)
        v_pair_t = _dot_nt(eye_ref[...], res[:, col_v + br * LANES:col_v + (br + 1) * LANES].astype(BF16))
        for hd in range(N_KV):
            v_t = jnp.concatenate([v_pair_t[hd * HEAD_DIM:(hd + 1) * HEAD_DIM], ones_row], axis=0)
            for k in range(tm // LANES):
                vt_ref[br * N_KV + hd, 0, k] = v_t[:, k * LANES:(k + 1) * LANES].astype(vt_ref.dtype)
    cmp_ref[...] = res[:, col_v + n_br * LANES:]


def _proj(x2d, g, w, feat_k, eye, swap, tm):
    n = x2d.shape[0]
    seq = feat_k.shape[0]
    nper = seq // tm
    kb = tm // LANES
    return pl.pallas_call(
        _proj_kernel,
        grid=(n // tm,),
        in_specs=[pl.BlockSpec((tm, D_MODEL), lambda i: (i, 0)),
                  pl.BlockSpec((1, D_MODEL), lambda i: (0, 0)),
                  pl.BlockSpec(w.shape, lambda i: (0, 0)),
                  pl.BlockSpec((tm, LANES), lambda i: (i % nper, 0)),
                  pl.BlockSpec((LANES, LANES), lambda i: (0, 0)),
                  pl.BlockSpec((LANES, LANES), lambda i: (0, 0))],
        out_specs=[pl.BlockSpec((tm, PROJ_W), lambda i: (i, 0)),
                   pl.BlockSpec((N_KV_SECT, 1, kb, LANES, LANES), lambda i: (0, i // nper, i % nper, 0, 0)),
                   pl.BlockSpec((tm, CMP_W), lambda i: (i, 0))],
        out_shape=[jax.ShapeDtypeStruct((n, PROJ_W), BF16),
                   jax.ShapeDtypeStruct((N_KV_SECT, n // seq, seq // LANES, LANES, LANES), BF16),
                   jax.ShapeDtypeStruct((n, CMP_W), F32)],
        compiler_params=pltpu.CompilerParams(dimension_semantics=("parallel",),
                                             vmem_limit_bytes=VMEM_LIMIT),
        name="proj",
    )(x2d, g, w, feat_k, eye, swap)


def _compress_kernel(x_ref, pa_ref, pb_ref, w1a_ref, w1b_ref, w2_ref, w2t_ref, o_ref, ot_ref, *, nch):
    r = jnp.concatenate([x_ref[0, pl.ds(j, nch, stride=CMP_STRIDE), :] for j in range(CMP_STRIDE)], axis=1)
    a = _dot((r + pa_ref[0]).astype(BF16), w1a_ref[0])
    b = _dot((r + pb_ref[0]).astype(BF16), w1b_ref[0])
    hid = a + pltpu.roll(b, nch - 1, 0)
    hid = hid * jax.nn.sigmoid(hid)
    for h in range(N_KV):
        hid_h = hid[:, h * CMP_HIDDEN:(h + 1) * CMP_HIDDEN].astype(BF16)
        o_ref[0, 0, h] = _dot(hid_h, w2_ref[0]).astype(o_ref.dtype)
        ot_ref[0, 0, h] = _dot_nt(w2t_ref[0], hid_h).astype(ot_ref.dtype)


def _compress(cmp3, pos_a, pos_b, w1a, w1b, w2, w2t):
    bsz, seq, _ = cmp3.shape
    nch = seq // CMP_STRIDE
    wspec = lambda a: pl.BlockSpec((1,) + a.shape[1:], lambda b, j: (j, 0, 0))
    return pl.pallas_call(
        functools.partial(_compress_kernel, nch=nch),
        grid=(bsz, 2),
        in_specs=[pl.BlockSpec((1, seq, LANES), lambda b, j: (b, 0, j)),
                  wspec(pos_a), wspec(pos_b), wspec(w1a), wspec(w1b), wspec(w2), wspec(w2t)],
        out_specs=[pl.BlockSpec((1, 1, N_KV, nch, LANES), lambda b, j: (b, j, 0, 0, 0)),
                   pl.BlockSpec((1, 1, N_KV, LANES, nch), lambda b, j: (b, j, 0, 0, 0))],
        out_shape=[jax.ShapeDtypeStruct((bsz, 2, N_KV, nch, LANES), BF16),
                   jax.ShapeDtypeStruct((bsz, 2, N_KV, LANES, nch), BF16)],
        compiler_params=pltpu.CompilerParams(dimension_semantics=("parallel", "parallel"),
                                             vmem_limit_bytes=VMEM_LIMIT),
        name="compress",
    )(cmp3, pos_a, pos_b, w1a, w1b, w2, w2t)


def _attn_t_kernel(*refs, n_cmp):
    tiles = [_attn_tile(u, *refs, n_cmp=n_cmp) for u in range(TILES_PER_STEP)]
    while all([next(t) is PHASED for t in tiles]):
        pass
    for t in tiles:
        for _ in t:
            pass


PHASED, SWEEP = "phase done", "ready for the sweep"


def _attn_tile(u, sinks_ref, qa_ref, qb_ref, kc_ref, vct_ref, *rest, n_cmp):
    rows = slice(u * TQ, (u + 1) * TQ)
    n_br = 3
    ks = rest[:n_br * N_KV]
    vts = rest[n_br * N_KV:2 * n_br * N_KV]
    gn_ref, ovt_ref, eye_ref, oa_ref, ob_ref = rest[2 * n_br * N_KV:]
    ksect = lambda branch, h: ks[branch * N_KV + h]
    vsect = lambda branch, h: vts[branch * N_KV + h]
    qi = pl.program_id(1) * TILES_PER_STEP + u
    q0 = pl.multiple_of(qi * TQ, TQ)
    lane = lax.broadcasted_iota(jnp.int32, (1, TQ), 1)
    sub = lax.broadcasted_iota(jnp.int32, (LANES, 1), 0)
    t_row = q0 + lane
    eye = eye_ref[...]
    gates = jax.nn.sigmoid(_dot_nt(eye, gn_ref[0, rows, :]))
    gate = lambda hh, c: gates[3 * hh + c:3 * hh + c + 1, :]
    sub40 = sub[ROWS_LO:ROWS_HI]
    blk = sub40 - FEAT_SEL
    in_rng = (blk >= 0) & (blk < N_SLC_BLK)
    is_pos = (sub40 >= FEAT_POS) & (sub40 < FEAT_SEL)
    zeros_lo = jnp.zeros((LANES - ROWS_HI, TQ), F32)

    def q_t(ref, hh):
        both = _dot_nt(eye, ref[0, rows, (hh // 2) * LANES:(hh // 2 + 1) * LANES])
        return both[(hh % 2) * HEAD_DIM:(hh % 2 + 1) * HEAD_DIM]

    def slope_col(slope):
        hi, mid, lo = _bf16_pieces(slope)
        col = jnp.zeros(sub40.shape, F32)
        for i, v in enumerate([SLC_LEN * hi, SLC_LEN * mid, SLC_LEN * lo, hi, mid, lo]):
            col = jnp.where(sub40 == FEAT_POS + i, v, col)
        return col

    def q_aug_t(q_ts, tails):
        return jnp.concatenate([jnp.concatenate([q, jnp.broadcast_to(t, (ROWS_HI - ROWS_LO, TQ)), zeros_lo], axis=0)
                                for q, t in zip(q_ts, tails)], axis=1).astype(BF16)

    def v_t(ref, first, n):
        return jnp.concatenate([ref[0, 0, first + j] for j in range(n)], axis=1)

    def masked(s, mask):
        return jnp.concatenate([jnp.where(mask, s[:, g * TQ:(g + 1) * TQ], NEG_INF) for g in range(GROUP)], axis=1)

    def store_heads(ref, outs):
        for p in range(N_HEADS // 2):
            pair = jnp.concatenate([outs[2 * p][:HEAD_DIM], outs[2 * p + 1][:HEAD_DIM]], axis=0).astype(BF16)
            ref[0, rows, p * LANES:(p + 1) * LANES] = _dot_nt(eye, pair).astype(ref.dtype)

    nw = NSA_WINDOW // TQ + 1
    w_first = jnp.maximum(qi - (nw - 1), 0)
    w_start = pl.multiple_of(w_first * TQ, TQ)
    nb = (SWA_WINDOW - 1 + TQ - 1) // TQ + 1
    b_first = jnp.maximum(qi - (nb - 1), 0)
    b_start = pl.multiple_of(b_first * TQ, TQ)
    pre = []
    for h in range(N_KV):
        heads = [h * GROUP + g for g in range(GROUP)]
        q_ts = [q_t(qa_ref, hh) for hh in heads]
        scol = [slope_col(SLOPES_NSA[hh]) for hh in heads]
        s_cmp = _dot(kc_ref[0, 0, h], q_aug_t(q_ts, [jnp.zeros((1, 1), F32)] * GROUP))
        s_win = _dot(ksect(1, h)[0, pl.ds(w_start, nw * TQ), :], q_aug_t(q_ts, scol))
        q_b = q_aug_t([q_t(qb_ref, hh) for hh in heads], [slope_col(SLOPES_SWA[hh]) for hh in heads])
        s_swa = _dot(ksect(2, h)[0, pl.ds(b_start, nb * TQ), :], q_b)
        pre.append((q_ts, scol, s_cmp, s_win, s_swa))

    yield PHASED
    def nsa_local(h):
        heads = [h * GROUP + g for g in range(GROUP)]
        _, _, s_all, s_win, _ = pre[h]

        end_c = sub * CMP_STRIDE + (CMP_LEN - 1)
        cmask = (t_row >= end_c) & (sub < n_cmp)
        ps = []
        for g in range(GROUP):
            s = s_all[:, g * TQ:(g + 1) * TQ] + SLOPES_NSA[heads[g]] * end_c.astype(F32)
            s = jnp.where(cmask, s, NEG_INF)
            m = jnp.max(s, axis=0, keepdims=True)
            e = jnp.where(cmask, jnp.exp2(s - m), 0.0)
            z = jnp.sum(e, axis=0, keepdims=True)
            ps.append(e / jnp.where(z > 0, z, 1.0))
        o_cmp = _dot(vct_ref[0, 0, h], jnp.concatenate(ps, axis=1).astype(BF16))

        dist = t_row - (w_start + lax.broadcasted_iota(jnp.int32, (nw * TQ, 1), 0))
        s = masked(s_win, (dist >= 0) & (dist < NSA_WINDOW))
        m = jnp.max(s, axis=0, keepdims=True)
        acc = _dot(v_t(vsect(1, h), w_first, nw), jnp.exp2(s - m).astype(BF16))
        o_win = acc / acc[HEAD_DIM:HEAD_DIM + 1, :]

        psum = ps[0] + ps[1] + ps[2] + ps[3]
        p_hi = psum.astype(BF16)
        p_lo = (psum - p_hi.astype(F32)).astype(BF16)
        imp = (_dot(ovt_ref[...], p_hi) + _dot(ovt_ref[...], p_lo))[ROWS_LO:ROWS_HI]
        part = [gate(hh, 0) * o_cmp[:, g * TQ:(g + 1) * TQ] + gate(hh, 2) * o_win[:, g * TQ:(g + 1) * TQ]
                for g, hh in enumerate(heads)]
        return imp, part

    def nsa_select(h, imp):
        q_ts, scol = pre[h][:2]
        cur = lax.shift_right_logical(t_row, int(np.log2(SLC_LEN)))
        valid = in_rng & (blk * SLC_LEN <= t_row)
        forced = in_rng & ((blk == 0) | (blk == cur) | (blk == cur - 1))
        score = jnp.where(forced, FORCE_SCORE, jnp.where(valid, imp, NEG_INF))
        rank = jnp.zeros(score.shape, F32)
        for i in range(N_SLC_BLK):
            r = FEAT_SEL - ROWS_LO + i
            si = score[r:r + 1, :]
            rank = rank + jnp.where((si > score) | ((si == score) & (blk > i)), 1.0, 0.0)
        sel = in_rng & (rank < SLC_TOPK) & (score > 0.5 * NEG_INF)
        bias_diag = jnp.where(in_rng & jnp.logical_not(sel), NEG_INF, 0.0)
        bias_main = jnp.where(in_rng & jnp.logical_not(sel & (blk < 2 * qi)), NEG_INF, 0.0)

        q_d = q_aug_t(q_ts, [jnp.where(is_pos, scol[g], bias_diag) for g in range(GROUP)])
        q_m = q_aug_t(q_ts, [jnp.where(is_pos, scol[g], bias_main) for g in range(GROUP)])
        return q_m, _dot(ksect(0, h)[0, pl.ds(q0, TQ), :], q_d)

    def nsa_diag(h, q_m, s_diag):
        s = masked(s_diag, q0 + sub <= t_row)
        m0 = jnp.max(s, axis=0, keepdims=True)
        acc0 = _dot(v_t(vsect(0, h), qi, 1), jnp.exp2(s - m0).astype(BF16))
        return q_m, m0, acc0

    local = [nsa_local(h) for h in range(N_KV)]
    yield PHASED

    dist = t_row - (b_start + lax.broadcasted_iota(jnp.int32, (nb * TQ, 1), 0))
    bmask = (dist >= 0) & (dist < SWA_WINDOW)
    outs_b = []
    for h in range(N_KV):
        heads = [h * GROUP + g for g in range(GROUP)]
        s = masked(pre[h][4], bmask)
        sink = jnp.concatenate([sinks_ref[hh] * LOG2E + SLOPES_SWA[hh] * t_row.astype(F32) for hh in heads], axis=1)
        m = jnp.maximum(jnp.max(s, axis=0, keepdims=True), sink)
        acc = _dot(v_t(vsect(2, h), b_first, nb), jnp.exp2(s - m).astype(BF16))
        o_all = acc / (acc[HEAD_DIM:HEAD_DIM + 1, :] + jnp.exp2(sink - m))
        outs_b += [o_all[:, g * TQ:(g + 1) * TQ] for g in range(GROUP)]
    store_heads(ob_ref, outs_b)

    yield PHASED
    selected = [nsa_select(h, local[h][0]) for h in range(N_KV)]
    yield PHASED
    fronts = [nsa_diag(h, *selected[h]) + (local[h][1],) for h in range(N_KV)]
    nblk = KCHUNK // TQ

    def absorb(carry, c, pieces):
        state = list(carry)
        subs = [(j, h) for j in pieces for h in range(N_KV)]
        scores = [_dot(ksect(0, h)[0, pl.ds(pl.multiple_of(c * KCHUNK + j * SUB, SUB), SUB), :], fronts[h][0])
                  for j, h in subs]
        for (j, h), s in zip(subs, scores):
            m_i, acc = state[h]
            m_new = jnp.maximum(m_i, jnp.max(s, axis=0, keepdims=True))
            pv = _dot(v_t(vsect(0, h), c * nblk + j * (SUB // TQ), SUB // TQ), jnp.exp2(s - m_new).astype(BF16))
            state[h] = (m_new, jnp.exp2(m_i - m_new) * acc + pv)
        return tuple(state)

    yield SWEEP
    step_chunk = pl.program_id(1)
    swept = lax.fori_loop(0, step_chunk, lambda c, st: absorb(st, c, range(KCHUNK // SUB)),
                          tuple((f[1], f[2]) for f in fronts))
    if u:
        swept = absorb(swept, step_chunk, range(u * TQ // SUB))
    outs_a = []
    for h in range(N_KV):
        acc = swept[h][1]
        o_slc = acc / acc[HEAD_DIM:HEAD_DIM + 1, :]
        outs_a += [fronts[h][3][g] + gate(h * GROUP + g, 1) * o_slc[:, g * TQ:(g + 1) * TQ] for g in range(GROUP)]
    store_heads(oa_ref, outs_a)


def _attention(proj3, v_t, kc, vc_t, ovt, eye, sinks):
    bsz, seq, _ = proj3.shape
    n_cmp = seq // CMP_STRIDE - CMP_LEN // CMP_STRIDE + 1
    width = N_HEADS * HEAD_DIM
    consts = [ovt, eye]
    tq = TILES_PER_STEP * TQ
    qspec = lambda col: pl.BlockSpec((1, tq, width), lambda b, q, s: (b, q, col))
    cspec = lambda a, j: pl.BlockSpec((1, 1) + a.shape[2:], lambda b, q, s: (b, j, 0, 0, 0))
    in_specs = [qspec(0), qspec(COL_QB // width), cspec(kc, 0), cspec(vc_t, 1)]
    in_specs += [pl.BlockSpec((1, seq, LANES), lambda b, q, s, j=j: (b, 0, COL_KV // LANES + j))
                 for j in range(N_KV_SECT)]
    in_specs += [pl.BlockSpec((1, 1) + v_t.shape[2:], lambda b, q, s, j=j: (j, b, 0, 0, 0)) for j in range(N_KV_SECT)]
    in_specs += [pl.BlockSpec((1, tq, LANES), lambda b, q, s: (b, q, COL_GN // LANES))]
    in_specs += [pl.BlockSpec(c.shape, lambda b, q, s: (0, 0)) for c in consts]
    ospec = pl.BlockSpec((1, tq, width), lambda b, q, s: (b, q, 0))
    return pl.pallas_call(
        functools.partial(_attn_t_kernel, n_cmp=n_cmp),
        grid_spec=pltpu.PrefetchScalarGridSpec(
            num_scalar_prefetch=1, grid=(bsz, seq // tq), in_specs=in_specs, out_specs=[ospec, ospec]),
        out_shape=[jax.ShapeDtypeStruct((bsz, seq, width), BF16)] * 2,
        compiler_params=pltpu.CompilerParams(dimension_semantics=("parallel", "parallel"),
                                             vmem_limit_bytes=VMEM_LIMIT),
        name="attn",
    )(sinks, proj3, proj3, kc, vc_t, *([proj3] * N_KV_SECT), *([v_t] * N_KV_SECT), proj3, *consts)


def _merge_kernel(x_ref, oa_ref, ob_ref, g1_ref, g2_ref, wg_ref, wa_ref, wb_ref, wo_ref,
                  wrh_ref, wrl_ref, br_ref, x2_ref, hn_ref, route_ref, cnt_ref):
    tm = x_ref.shape[0]

    @pl.when(pl.program_id(0) == 0)
    def _():
        cnt_ref[...] = jnp.zeros_like(cnt_ref)

    halves = [slice(i * (tm // 2), (i + 1) * (tm // 2)) for i in range(2)]
    xs = [x_ref[r, :] for r in halves]
    pre = []
    for r, x in zip(halves, xs):
        h = _rms(x, g1_ref[...]).astype(BF16)
        pre.append((_dot(h, wg_ref[...]), _dot(oa_ref[r, :], wa_ref[...]), _dot(ob_ref[r, :], wb_ref[...])))
    mixes = []
    for g_pre, a, b in pre:
        gm = jax.nn.sigmoid(g_pre)
        mixin = gm[:, :D_MODEL] * a + gm[:, D_MODEL:] * b
        mixes.append(_dot(mixin.astype(BF16), wo_ref[...]))
    logit_halves = []
    for r, x, mix in zip(halves, xs, mixes):
        x2 = x + mix
        x2_ref[r, :] = x2
        hn = _rms(x2, g2_ref[...])
        for j in range(ROW_TILE):
            hn_ref[pl.ds(r.start * ROW_TILE + j, tm // 2, stride=ROW_TILE), :] = hn[:, j * LANES:(j + 1) * LANES]
        hn_b = hn.astype(BF16)
        hn_lo = (hn - hn_b.astype(F32)).astype(BF16)
        logit_halves.append(_dot_nt(wrh_ref[...], hn_b) + _dot_nt(wrh_ref[...], hn_lo) + _dot_nt(wrl_ref[...], hn_b))
    bias = br_ref[...]
    logits = jnp.concatenate(logit_halves, axis=1) + jnp.concatenate([bias] * (tm // LANES), axis=1)
    row = lax.broadcasted_iota(jnp.int32, (LANES, 1), 0)
    rowf = row.astype(F32)
    big = float(LANES)
    top = lambda a: jnp.max(a, axis=0, keepdims=True)
    first = lambda hit: jnp.min(jnp.where(hit, rowf, big), axis=0, keepdims=True)
    is_g = (row >= N_EXPERTS) & (row < N_EXPERTS + N_GROUPS)
    gl = jnp.where(is_g, logits, NEG_INF)
    gmax = top(gl)
    grp = first(gl == gmax) - N_EXPERTS
    p_grp = 1.0 / jnp.sum(jnp.where(is_g, jnp.exp(gl - gmax), 0.0), axis=0, keepdims=True)
    in_grp = (rowf >= grp * EPG) & (rowf < grp * EPG + EPG)
    el = jnp.where(in_grp, logits, NEG_INF)
    v0 = top(el)
    i0 = first(el == v0)
    el1 = jnp.where(rowf == i0, NEG_INF, el)
    v1 = top(el1)
    i1 = first(el1 == v1)
    e1 = jnp.exp(v1 - v0)
    w0 = p_grp / (1.0 + e1)
    w1 = p_grp * e1 / (1.0 + e1)

    oh0 = jnp.where(rowf == i0, 1.0, 0.0)
    oh1 = jnp.where(rowf == i1, 1.0, 0.0)
    oh = oh0 + oh1
    r_i = lax.broadcasted_iota(jnp.int32, (tm, tm), 0)
    c_i = lax.broadcasted_iota(jnp.int32, (tm, tm), 1)
    earlier = jnp.where(r_i < c_i, 1.0, 0.0).astype(BF16)
    before = cnt_ref[...] + _dot(oh.astype(BF16), earlier)
    rank0 = jnp.sum(oh0 * before, axis=0, keepdims=True)
    rank1 = jnp.sum(oh1 * before, axis=0, keepdims=True)
    cnt_ref[...] = cnt_ref[...] + jnp.sum(oh, axis=1, keepdims=True)
    row8 = row[:8]
    route = jnp.zeros((8, tm), F32)
    for k, v in enumerate((i0, i1, rank0, rank1, w0, w1)):
        route = jnp.where(row8 == k, v, route)
    route_ref[0] = route


def _merge(x2d, oa, ob, g1, g2, wg, wa, wb, wo, wrh, wrl, br, tm):
    n = x2d.shape[0]
    width = N_HEADS * HEAD_DIM
    row = lambda w: pl.BlockSpec((tm, w), lambda i: (i, 0))
    full = lambda a: pl.BlockSpec(a.shape, lambda i: (0, 0), pipeline_mode=pl.Buffered(1))
    return pl.pallas_call(
        _merge_kernel,
        grid=(n // tm,),
        in_specs=[row(D_MODEL), row(width), row(width), full(g1), full(g2), full(wg), full(wa), full(wb),
                  full(wo), full(wrh), full(wrl), full(br)],
        out_specs=[row(D_MODEL), pl.BlockSpec((tm * ROW_TILE, LANES), lambda i: (i, 0)),
                   pl.BlockSpec((1, 8, tm), lambda i: (i, 0, 0)), pl.BlockSpec((LANES, 1), lambda i: (0, 0))],
        out_shape=[jax.ShapeDtypeStruct((n, D_MODEL), F32), jax.ShapeDtypeStruct((n * ROW_TILE, LANES), F32),
                   jax.ShapeDtypeStruct((n // tm, 8, tm), F32), jax.ShapeDtypeStruct((LANES, 1), F32)],
        compiler_params=pltpu.CompilerParams(dimension_semantics=("arbitrary",),
                                             vmem_limit_bytes=VMEM_LIMIT),
        name="merge",
    )(x2d, oa, ob, g1, g2, wg, wa, wb, wo, wrh, wrl, br)


def _slot_table_kernel(pos_ref, init_ref, tab_ref, sem, *, tt, n):
    step = pl.program_id(0)

    @pl.when(step == 0)
    def _():
        cp = pltpu.make_async_copy(init_ref, tab_ref, sem)
        cp.start()
        cp.wait()

    def chunk(c, row):
        base = pl.multiple_of(c * LANES, LANES)
        for k in range(LANES):
            tab_ref[pos_ref[0, 0, base + k]] = row + k * ROW_TILE
            tab_ref[pos_ref[0, 1, base + k]] = row + (n + k) * ROW_TILE
        return row + LANES * ROW_TILE

    lax.fori_loop(0, tt // LANES, chunk, step * (tt * ROW_TILE))


def _slot_table(pos, init, tt):
    nt = pos.shape[0]
    return pl.pallas_call(
        functools.partial(_slot_table_kernel, tt=tt, n=nt * tt),
        grid=(nt,),
        in_specs=[pl.BlockSpec((1, 2, tt), lambda i: (i, 0, 0), memory_space=pltpu.SMEM),
                  pl.BlockSpec(memory_space=pl.ANY)],
        out_specs=pl.BlockSpec(memory_space=pltpu.SMEM),
        out_shape=jax.ShapeDtypeStruct(init.shape, jnp.int32),
        scratch_shapes=[pltpu.SemaphoreType.DMA(())],
        compiler_params=pltpu.CompilerParams(dimension_semantics=("arbitrary",)),
        name="slot_table",
    )(pos, init)


def _moe_kernel(be_ref, nu_ref, tab_ref, tab_next_ref, hn_ref, wg_ref, wu_ref, wd_ref, out_ref,
                xin, yout, wg_s, wu_s, wd_s, gsem, ssem, *, tb, n):
    b = pl.program_id(0)
    n_used = nu_ref[0]
    live = b < n_used
    s = lax.rem(b, 2)

    def gather(tab, slot):
        for j in range(tb):
            t, rows = tab[0, 0, j], n * ROW_TILE
            src = jnp.bitwise_and(t, rows - 1) if rows & (rows - 1) == 0 else lax.rem(t, rows)
            pltpu.make_async_copy(hn_ref.at[_tile_at(src)], xin.at[slot, pl.ds(j * ROW_TILE, ROW_TILE)],
                                  gsem.at[slot]).start(priority=j % 2)

    def wait_rows(sem_ref, slot):
        pltpu.make_async_copy(hn_ref.at[pl.ds(0, tb * ROW_TILE)], xin.at[slot], sem_ref.at[slot]).wait()

    @pl.when(b == 0)
    def _():
        yout[...] = jnp.zeros_like(yout)
        fills = [pltpu.make_async_copy(yout.at[k], out_ref.at[pl.ds((2 * n + k * tb) * ROW_TILE, tb * ROW_TILE)],
                                       ssem.at[k]) for k in range(2)]
        for cp in fills:
            cp.start()
        for cp in fills:
            cp.wait()
        gather(tab_ref, 0)

    @pl.when(live & ((b == 0) | (be_ref[b] != be_ref[jnp.maximum(b - 1, 0)])))
    def _():
        wg_s[...] = wg_ref[0].astype(BF16)
        wu_s[...] = wu_ref[0].astype(BF16)
        wd_s[...] = wd_ref[0].astype(BF16)

    @pl.when(live & (b >= 2))
    def _():
        wait_rows(ssem, s)

    def block(s):
        wait_rows(gsem, s)
        gather(tab_next_ref, 1 - s)
        half = tb // 2
        gu = []
        for i in range(2):
            xb = jnp.concatenate([xin[s, pl.ds(i * half * ROW_TILE + j, half, stride=ROW_TILE), :]
                                  for j in range(ROW_TILE)], axis=1).astype(BF16)
            gu.append((_dot(xb, wg_s[...]), _dot(xb, wu_s[...])))
        for i, (g, u) in enumerate(gu):
            y = _dot((g * jax.nn.sigmoid(g) * u).astype(BF16), wd_s[...])
            for j in range(ROW_TILE):
                yout[s, pl.ds(i * half * ROW_TILE + j, half, stride=ROW_TILE), :] = y[:, j * LANES:(j + 1) * LANES]

        for j in range(tb):
            pltpu.make_async_copy(yout.at[s, pl.ds(j * ROW_TILE, ROW_TILE)], out_ref.at[_tile_at(tab_ref[0, 0, j])],
                                  ssem.at[s]).start(priority=j % 2)

    for parity in range(2):
        pl.when(live & (s == parity))(functools.partial(block, parity))

    @pl.when(live & (b == n_used - 1))
    def _():
        wait_rows(gsem, 1 - s)
        wait_rows(ssem, s)

        @pl.when(b >= 1)
        def _():
            wait_rows(ssem, 1 - s)


def _moe(block_e, n_used, tab, hn, wg, wu, wd, tb, n):
    nblk = tab.shape[0] // tb
    tab2 = tab.reshape(nblk, 1, tb)
    live = lambda b, be, nu: jnp.minimum(b, nu[0] - 1)
    nxt = lambda b, be, nu: jnp.minimum(b + 1, nu[0] - 1)
    wspec = lambda shape: pl.BlockSpec((1,) + shape, lambda b, be, nu: (be[live(b, be, nu)], 0, 0))
    return pl.pallas_call(
        functools.partial(_moe_kernel, tb=tb, n=n),
        grid_spec=pltpu.PrefetchScalarGridSpec(
            num_scalar_prefetch=2, grid=(nblk,),
            in_specs=[pl.BlockSpec((1, 1, tb), lambda b, be, nu: (live(b, be, nu), 0, 0), memory_space=pltpu.SMEM),
                      pl.BlockSpec((1, 1, tb), lambda b, be, nu: (nxt(b, be, nu), 0, 0), memory_space=pltpu.SMEM),
                      pl.BlockSpec(memory_space=pl.ANY),
                      wspec((D_MODEL, EXPERT_FF)), wspec((D_MODEL, EXPERT_FF)), wspec((EXPERT_FF, D_MODEL))],
            out_specs=pl.BlockSpec(memory_space=pl.ANY),
            scratch_shapes=[pltpu.VMEM((2, tb * ROW_TILE, LANES), F32), pltpu.VMEM((2, tb * ROW_TILE, LANES), F32),
                            pltpu.VMEM((D_MODEL, EXPERT_FF), BF16), pltpu.VMEM((D_MODEL, EXPERT_FF), BF16),
                            pltpu.VMEM((EXPERT_FF, D_MODEL), BF16),
                            pltpu.SemaphoreType.DMA((2,)), pltpu.SemaphoreType.DMA((2,))]),
        out_shape=jax.ShapeDtypeStruct(((2 * n + 2 * tb) * ROW_TILE, LANES), F32),
        compiler_params=pltpu.CompilerParams(dimension_semantics=("arbitrary",),
                                             vmem_limit_bytes=VMEM_LIMIT),
        name="moe",
    )(block_e, n_used, tab2, tab2, hn, wg, wu, wd)


def _final_kernel(x2_ref, w_ref, y0_ref, y1_ref, gf_ref, o_ref):
    tc = x2_ref.shape[0]
    w = w_ref[...]
    y = x2_ref[...] + (w[:, 0:1] * _tiles_to_rows(y0_ref, tc) + w[:, 1:2] * _tiles_to_rows(y1_ref, tc))
    o_ref[...] = _rms(y, gf_ref[...])


def _final(x2, w_slot, yslots, gf, tc):
    n = x2.shape[0]
    nt = n // tc
    return pl.pallas_call(
        _final_kernel,
        grid=(nt,),
        in_specs=[pl.BlockSpec((tc, D_MODEL), lambda i: (i, 0)),
                  pl.BlockSpec((tc, 2), lambda i: (i, 0)),
                  pl.BlockSpec((tc * ROW_TILE, LANES), lambda i: (i, 0)),
                  pl.BlockSpec((tc * ROW_TILE, LANES), lambda i: (nt + i, 0)),
                  pl.BlockSpec((1, D_MODEL), lambda i: (0, 0))],
        out_specs=pl.BlockSpec((tc, D_MODEL), lambda i: (i, 0)),
        out_shape=jax.ShapeDtypeStruct((n, D_MODEL), F32),
        compiler_params=pltpu.CompilerParams(dimension_semantics=("parallel",),
                                             vmem_limit_bytes=VMEM_LIMIT),
        name="final",
    )(x2, w_slot, yslots, yslots, gf)


def _tile_at(row):
    return pl.ds(pl.multiple_of(row, ROW_TILE), ROW_TILE)


def _overlap_matrix_t(seq):
    nc = seq // CMP_STRIDE - CMP_LEN // CMP_STRIDE + 1
    ns = seq // SLC_LEN
    c0 = np.arange(nc) * CMP_STRIDE
    s0 = np.arange(ns) * SLC_LEN
    ov = np.clip(np.minimum(c0[:, None] + CMP_LEN, s0[None, :] + SLC_LEN)
                 - np.maximum(c0[:, None], s0[None, :]), 0, None) / CMP_LEN
    out = np.zeros((LANES, LANES), np.float32)
    out[FEAT_SEL:FEAT_SEL + ns, :nc] = ov.T
    return jnp.asarray(out, BF16)


def _position_features(seq):
    pos = np.arange(seq)
    fk = np.zeros((seq, LANES), np.float32)
    fk[:, FEAT_POS:FEAT_POS + 3] = (pos // SLC_LEN)[:, None]
    fk[:, FEAT_POS + 3:FEAT_POS + 6] = (pos % SLC_LEN)[:, None]
    fk[pos, FEAT_SEL + pos // SLC_LEN] = 1.0
    return jnp.asarray(fk)


def _pick_tile(n, pref):
    t = pref
    while n % t:
        t //= 2
    return t


def kernel(x, norm_mix_g, w_in, cmp_pos_k, cmp_w1_k, cmp_w2_k, cmp_pos_v, cmp_w1_v, cmp_w2_v, sinks, w_a, w_b,
           w_o, norm_ffn_g, w_group, b_group, w_expert, b_expert, w_gate_e, w_up_e, w_down_e, norm_final_g):
    bsz, seq, _ = x.shape
    n = bsz * seq
    assert TQ == LANES and seq % KCHUNK == 0 and seq // SLC_LEN <= N_SLC_BLK and seq // CMP_STRIDE <= LANES
    assert TILES_PER_STEP * TQ == KCHUNK and TQ % SUB == 0
    assert seq >= (NSA_WINDOW // TQ + 1) * TQ and w_in.shape[0] == 1
    x2d = x.reshape(n, D_MODEL)

    w = w_in[0]
    scale = HEAD_DIM ** -0.5 * LOG2E
    nsa_w, kvw = N_HEADS * HEAD_DIM, N_KV * HEAD_DIM
    o_qa, o_kva, o_gn = 0, nsa_w, nsa_w + 6 * kvw
    o_qb = o_gn + 3 * N_HEADS
    o_kvb = o_qb + nsa_w
    o_gm = o_kvb + 2 * kvw
    pair = lambda off: w[:, off:off + kvw]
    w_attn = jnp.concatenate(
        [w[:, o_qa:o_qa + nsa_w] * scale, w[:, o_qb:o_qb + nsa_w] * scale,
         pair(o_kva + 2 * kvw), pair(o_kva + 4 * kvw), pair(o_kvb),
         w[:, o_gn:o_gn + 3 * N_HEADS], jnp.zeros((D_MODEL, LANES - 3 * N_HEADS), F32),
         pair(o_kva + 3 * kvw), pair(o_kva + 5 * kvw), pair(o_kvb + kvw),
         w[:, o_kva:o_kva + 2 * kvw]], axis=1).astype(BF16)
    w_gm = w[:, o_gm:o_gm + 2 * D_MODEL].astype(BF16)

    tm = _pick_tile(seq, 512)
    feat_k = _position_features(seq)
    eye = jnp.eye(LANES, dtype=BF16)
    swap = jnp.roll(eye, HEAD_DIM, axis=1)
    proj, v_t, cmp_in = _proj(x2d, norm_mix_g[0][None], w_attn, feat_k, eye, swap, tm)
    proj3 = proj.reshape(bsz, seq, PROJ_W)

    nch = seq // CMP_STRIDE
    pos = jnp.stack([cmp_pos_k[0], cmp_pos_v[0]])
    pos = jnp.broadcast_to(pos[:, :, None, :], (2, CMP_LEN, N_KV, HEAD_DIM))
    pos_a = pos[:, :CMP_STRIDE].reshape(2, 1, CMP_STRIDE * kvw)
    pos_b = pos[:, CMP_STRIDE:].reshape(2, 1, CMP_STRIDE * kvw)
    w1 = jnp.stack([cmp_w1_k[0], cmp_w1_v[0]]).reshape(2, CMP_LEN, HEAD_DIM, CMP_HIDDEN)
    zero = jnp.zeros_like(w1)
    w1 = jnp.stack([jnp.concatenate([w1, zero], axis=-1), jnp.concatenate([zero, w1], axis=-1)], axis=2)
    w1 = w1.reshape(2, CMP_LEN * kvw, N_KV * CMP_HIDDEN).astype(BF16)
    w2 = jnp.pad(jnp.stack([cmp_w2_k[0], cmp_w2_v[0]]), ((0, 0), (0, 0), (0, LANES - HEAD_DIM))).astype(BF16)
    kvc, kvc_t = _compress(cmp_in.reshape(bsz, seq, CMP_W), pos_a, pos_b, w1[:, :CMP_STRIDE * kvw],
                           w1[:, CMP_STRIDE * kvw:], w2, jnp.swapaxes(w2, 1, 2))
    kvc = jnp.pad(kvc, ((0, 0), (0, 0), (0, 0), (0, LANES - nch), (0, 0)))
    kvc_t = jnp.pad(kvc_t, ((0, 0), (0, 0), (0, 0), (0, 0), (0, LANES - nch)))

    o_a, o_b = _attention(proj3, v_t, kvc, kvc_t, _overlap_matrix_t(seq), eye, sinks[0])

    w_r = jnp.concatenate([w_expert[0], w_group[0],
                           jnp.zeros((D_MODEL, LANES - N_EXPERTS - N_GROUPS), F32)], axis=1)
    w_r = w_r.T
    w_rh = w_r.astype(BF16)
    w_rl = (w_r - w_rh.astype(F32)).astype(BF16)
    b_r = jnp.concatenate([b_expert[0], b_group[0], jnp.zeros((LANES - N_EXPERTS - N_GROUPS,), F32)])
    b_r = b_r[:, None] * jnp.ones((1, LANES), F32)
    tt = _pick_tile(n, 1024)
    x2, hn, route, counts = _merge(
        x2d, o_a.reshape(n, nsa_w), o_b.reshape(n, nsa_w), norm_mix_g[0][None], norm_ffn_g[0][None], w_gm,
        w_a[0].astype(BF16), w_b[0].astype(BF16), w_o[0].astype(BF16), w_rh, w_rl, b_r, tt)

    tb = 256
    nblk = -(-(2 * n + N_EXPERTS * (tb - 1)) // tb)
    cnt = counts[:N_EXPERTS, 0].astype(jnp.int32)
    padded = (cnt + tb - 1) // tb * tb
    pad_end = jnp.cumsum(padded)
    pad_start = pad_end - padded
    block_e = jnp.minimum(jnp.sum(pad_end[None, :] <= (jnp.arange(nblk) * tb)[:, None], axis=1), N_EXPERTS - 1)
    n_used = (pad_end[-1:] // tb).astype(jnp.int32)
    tab = route[:, 0:4, :].astype(jnp.int32)
    eids = tab[:, 0:2, :]
    start_of = sum(jnp.where(eids == e, pad_start[e], 0) for e in range(N_EXPERTS))
    pos = start_of + tab[:, 2:4, :]
    w_slot = jnp.swapaxes(route[:, 4:6, :], 1, 2).reshape(n, 2)
    spare = (2 * n + jnp.arange(nblk * tb, dtype=jnp.int32) % (2 * tb)) * ROW_TILE
    slot_tab = _slot_table(pos, spare, tt)
    y_slots = _moe(block_e.astype(jnp.int32), n_used, slot_tab, hn, w_gate_e[0], w_up_e[0], w_down_e[0], tb, n)
    out = _final(x2, w_slot, y_slots, norm_final_g[None], tt)
    return out.reshape(bsz, seq, D_MODEL)
```

```python
import functools

import numpy as np
import jax
import jax.numpy as jnp
from jax import lax
from jax.experimental import pallas as pl
from jax.experimental.pallas import tpu as pltpu

F32 = jnp.float32
BF16 = jnp.bfloat16

D_MODEL = 1024
HEAD_DIM = 64
N_HEADS = 8
N_KV = 2
GROUP = N_HEADS // N_KV
CMP_LEN = 32
CMP_STRIDE = 16
CMP_HIDDEN = 256
SLC_LEN = 64
SLC_TOPK = 8
NSA_WINDOW = 256
SWA_WINDOW = 128
N_GROUPS = 4
EPG = 8
N_EXPERTS = N_GROUPS * EPG
EXPERT_FF = 256
RMS_EPS = 1e-6
NEG_INF = -1e30
FORCE_SCORE = 1e9

LANES = 128
TQ = 128
TILES_PER_STEP = 4
KCHUNK = 512
SUB = 128
N_SLC_BLK = LANES // 4
FEAT_POS = HEAD_DIM
FEAT_SEL = HEAD_DIM + 6
ROWS_LO, ROWS_HI = 64, 104
N_KV_SECT = 6
COL_QB = 512
COL_KV = 1024
COL_GN = COL_KV + N_KV_SECT * LANES
PROJ_W = COL_GN + LANES
CMP_W = 2 * LANES
ROW_TILE = D_MODEL // LANES
MOE_SLOTS = 3
VMEM_LIMIT = 56 * 1024 * 1024


LOG2E = float(np.log2(np.e))


def _alibi_slopes():
    n = 2 * N_HEADS
    s = 2.0 ** (-8.0 * np.arange(1, n + 1) / n) * LOG2E
    return [float(v) for v in s[:N_HEADS]], [float(v) for v in s[N_HEADS:]]


SLOPES_SWA, SLOPES_NSA = _alibi_slopes()


def _bf16_pieces(v):
    out, rem = [], np.float32(v)
    for _ in range(3):
        p = np.float32(np.asarray(rem, np.float32).astype(BF16).astype(np.float32))
        out.append(float(p))
        rem = np.float32(rem - p)
    return out


def _rms(x, g):
    return x * lax.rsqrt(jnp.mean(x * x, axis=-1, keepdims=True) + RMS_EPS) * g


def _dot(a, b):
    return jnp.dot(a, b, preferred_element_type=F32)


def _tiles_to_rows(ref, n, lead=()):
    return jnp.concatenate([ref[lead + (pl.ds(j, n, stride=ROW_TILE), slice(None))] for j in range(ROW_TILE)], axis=1)


def _rows_to_tiles(ref, val):
    n = val.shape[0]
    for j in range(ROW_TILE):
        ref[pl.ds(j, n, stride=ROW_TILE), :] = val[:, j * LANES:(j + 1) * LANES]


def _dot_nt(a, b):
    return lax.dot_general(a, b, (((1,), (1,)), ((), ())), preferred_element_type=F32)


def _proj_kernel(x_ref, g_ref, w_ref, fk_ref, eye_ref, swap_ref, o_ref, vt_ref, cmp_ref):
    tm = x_ref.shape[0]
    n_br = N_KV_SECT // N_KV
    col_gn = COL_KV + n_br * LANES
    col_v = col_gn + LANES
    h = _rms(x_ref[...], g_ref[...]).astype(BF16)
    res = _dot(h, w_ref[...])
    o_ref[:, :COL_KV] = res[:, :COL_KV].astype(o_ref.dtype)
    o_ref[:, COL_GN:] = res[:, col_gn:col_v].astype(o_ref.dtype)
    low = lax.broadcasted_iota(jnp.int32, (1, LANES), 1) < HEAD_DIM
    ones_row = jnp.where(lax.broadcasted_iota(jnp.int32, (LANES - HEAD_DIM, tm), 0) == 0, 1.0, 0.0)
    for br in range(n_br):
        k_pair = res[:, COL_KV + br * LANES:COL_KV + (br + 1) * LANES]
        heads = (k_pair, _dot(k_pair.astype(BF16), swap_ref[...]))
        for hd in range(N_KV):
            c0 = COL_KV + (br * N_KV + hd) * LANES
            o_ref[:, c0:c0 + LANES] = jnp.where(low, heads[hd], fk_ref[...]).astype(o_ref.dtype)
        v_pair_t = _dot_nt(eye_ref[...], res[:, col_v + br * LANES:col_v + (br + 1) * LANES].astype(BF16))
        for hd in range(N_KV):
            v_t = jnp.concatenate([v_pair_t[hd * HEAD_DIM:(hd + 1) * HEAD_DIM], ones_row], axis=0)
            for k in range(tm // LANES):
                vt_ref[br * N_KV + hd, 0, k] = v_t[:, k * LANES:(k + 1) * LANES].astype(vt_ref.dtype)
    cmp_ref[...] = res[:, col_v + n_br * LANES:]


def _proj(x2d, g, w, feat_k, eye, swap, tm):
    n = x2d.shape[0]
    seq = feat_k.shape[0]
    nper = seq // tm
    kb = tm // LANES
    return pl.pallas_call(
        _proj_kernel,
        grid=(n // tm,),
        in_specs=[pl.BlockSpec((tm, D_MODEL), lambda i: (i, 0)),
                  pl.BlockSpec((1, D_MODEL), lambda i: (0, 0)),
                  pl.BlockSpec(w.shape, lambda i: (0, 0)),
                  pl.BlockSpec((tm, LANES), lambda i: (i % nper, 0)),
                  pl.BlockSpec((LANES, LANES), lambda i: (0, 0)),
                  pl.BlockSpec((LANES, LANES), lambda i: (0, 0))],
        out_specs=[pl.BlockSpec((tm, PROJ_W), lambda i: (i, 0)),
                   pl.BlockSpec((N_KV_SECT, 1, kb, LANES, LANES), lambda i: (0, i // nper, i % nper, 0, 0)),
                   pl.BlockSpec((tm, CMP_W), lambda i: (i, 0))],
        out_shape=[jax.ShapeDtypeStruct((n, PROJ_W), BF16),
                   jax.ShapeDtypeStruct((N_KV_SECT, n // seq, seq // LANES, LANES, LANES), BF16),
                   jax.ShapeDtypeStruct((n, CMP_W), F32)],
        compiler_params=pltpu.CompilerParams(dimension_semantics=("parallel",),
                                             vmem_limit_bytes=VMEM_LIMIT),
        name="proj",
    )(x2d, g, w, feat_k, eye, swap)


def _compress_kernel(x_ref, pa_ref, pb_ref, w1a_ref, w1b_ref, w2_ref, w2t_ref, o_ref, ot_ref, *, nch):
    r = jnp.concatenate([x_ref[0, pl.ds(j, nch, stride=CMP_STRIDE), :] for j in range(CMP_STRIDE)], axis=1)
    a = _dot((r + pa_ref[0]).astype(BF16), w1a_ref[0])
    b = _dot((r + pb_ref[0]).astype(BF16), w1b_ref[0])
    hid = a + pltpu.roll(b, nch - 1, 0)
    hid = hid * jax.nn.sigmoid(hid)
    for h in range(N_KV):
        hid_h = hid[:, h * CMP_HIDDEN:(h + 1) * CMP_HIDDEN].astype(BF16)
        o_ref[0, 0, h] = _dot(hid_h, w2_ref[0]).astype(o_ref.dtype)
        ot_ref[0, 0, h] = _dot_nt(w2t_ref[0], hid_h).astype(ot_ref.dtype)


def _compress(cmp3, pos_a, pos_b, w1a, w1b, w2, w2t):
    bsz, seq, _ = cmp3.shape
    nch = seq // CMP_STRIDE
    wspec = lambda a: pl.BlockSpec((1,) + a.shape[1:], lambda b, j: (j, 0, 0))
    return pl.pallas_call(
        functools.partial(_compress_kernel, nch=nch),
        grid=(bsz, 2),
        in_specs=[pl.BlockSpec((1, seq, LANES), lambda b, j: (b, 0, j)),
                  wspec(pos_a), wspec(pos_b), wspec(w1a), wspec(w1b), wspec(w2), wspec(w2t)],
        out_specs=[pl.BlockSpec((1, 1, N_KV, nch, LANES), lambda b, j: (b, j, 0, 0, 0)),
                   pl.BlockSpec((1, 1, N_KV, LANES, nch), lambda b, j: (b, j, 0, 0, 0))],
        out_shape=[jax.ShapeDtypeStruct((bsz, 2, N_KV, nch, LANES), BF16),
                   jax.ShapeDtypeStruct((bsz, 2, N_KV, LANES, nch), BF16)],
        compiler_params=pltpu.CompilerParams(dimension_semantics=("parallel", "parallel"),
                                             vmem_limit_bytes=VMEM_LIMIT),
        name="compress",
    )(cmp3, pos_a, pos_b, w1a, w1b, w2, w2t)


def _attn_t_kernel(*refs, n_cmp):
    tiles = [_attn_tile(u, *refs, n_cmp=n_cmp) for u in range(TILES_PER_STEP)]
    while all([next(t) is PHASED for t in tiles]):
        pass
    for t in tiles:
        for _ in t:
            pass


PHASED, SWEEP = "phase done", "ready for the sweep"


def _attn_tile(u, sinks_ref, qa_ref, qb_ref, kc_ref, vct_ref, *rest, n_cmp):
    rows = slice(u * TQ, (u + 1) * TQ)
    n_br = 3
    ks = rest[:n_br * N_KV]
    vts = rest[n_br * N_KV:2 * n_br * N_KV]
    gn_ref, ovt_ref, eye_ref, oa_ref, ob_ref = rest[2 * n_br * N_KV:]
    ksect = lambda branch, h: ks[branch * N_KV + h]
    vsect = lambda branch, h: vts[branch * N_KV + h]
    qi = pl.program_id(1) * TILES_PER_STEP + u
    q0 = pl.multiple_of(qi * TQ, TQ)
    lane = lax.broadcasted_iota(jnp.int32, (1, TQ), 1)
    sub = lax.broadcasted_iota(jnp.int32, (LANES, 1), 0)
    t_row = q0 + lane
    eye = eye_ref[...]
    gates = jax.nn.sigmoid(_dot_nt(eye, gn_ref[0, rows, :]))
    gate = lambda hh, c: gates[3 * hh + c:3 * hh + c + 1, :]
    sub40 = sub[ROWS_LO:ROWS_HI]
    blk = sub40 - FEAT_SEL
    in_rng = (blk >= 0) & (blk < N_SLC_BLK)
    is_pos = (sub40 >= FEAT_POS) & (sub40 < FEAT_SEL)
    zeros_lo = jnp.zeros((LANES - ROWS_HI, TQ), F32)

    def q_t(ref, hh):
        both = _dot_nt(eye, ref[0, rows, (hh // 2) * LANES:(hh // 2 + 1) * LANES])
        return both[(hh % 2) * HEAD_DIM:(hh % 2 + 1) * HEAD_DIM]

    def slope_col(slope):
        hi, mid, lo = _bf16_pieces(slope)
        col = jnp.zeros(sub40.shape, F32)
        for i, v in enumerate([SLC_LEN * hi, SLC_LEN * mid, SLC_LEN * lo, hi, mid, lo]):
            col = jnp.where(sub40 == FEAT_POS + i, v, col)
        return col

    def q_aug_t(q_ts, tails):
        return jnp.concatenate([jnp.concatenate([q, jnp.broadcast_to(t, (ROWS_HI - ROWS_LO, TQ)), zeros_lo], axis=0)
                                for q, t in zip(q_ts, tails)], axis=1).astype(BF16)

    def v_t(ref, first, n):
        return jnp.concatenate([ref[0, 0, first + j] for j in range(n)], axis=1)

    def masked(s, mask):
        return jnp.concatenate([jnp.where(mask, s[:, g * TQ:(g + 1) * TQ], NEG_INF) for g in range(GROUP)], axis=1)

    def store_heads(ref, outs):
        for p in range(N_HEADS // 2):
            pair = jnp.concatenate([outs[2 * p][:HEAD_DIM], outs[2 * p + 1][:HEAD_DIM]], axis=0).astype(BF16)
            ref[0, rows, p * LANES:(p + 1) * LANES] = _dot_nt(eye, pair).astype(ref.dtype)

    nw = NSA_WINDOW // TQ + 1
    w_first = jnp.maximum(qi - (nw - 1), 0)
    w_start = pl.multiple_of(w_first * TQ, TQ)
    nb = (SWA_WINDOW - 1 + TQ - 1) // TQ + 1
    b_first = jnp.maximum(qi - (nb - 1), 0)
    b_start = pl.multiple_of(b_first * TQ, TQ)
    pre = []
    for h in range(N_KV):
        heads = [h * GROUP + g for g in range(GROUP)]
        q_ts = [q_t(qa_ref, hh) for hh in heads]
        scol = [slope_col(SLOPES_NSA[hh]) for hh in heads]
        s_cmp = _dot(kc_ref[0, 0, h], q_aug_t(q_ts, [jnp.zeros((1, 1), F32)] * GROUP))
        s_win = _dot(ksect(1, h)[0, pl.ds(w_start, nw * TQ), :], q_aug_t(q_ts, scol))
        q_b = q_aug_t([q_t(qb_ref, hh) for hh in heads], [slope_col(SLOPES_SWA[hh]) for hh in heads])
        s_swa = _dot(ksect(2, h)[0, pl.ds(b_start, nb * TQ), :], q_b)
        pre.append((q_ts, scol, s_cmp, s_win, s_swa))

    yield PHASED
    def nsa_local(h):
        heads = [h * GROUP + g for g in range(GROUP)]
        _, _, s_all, s_win, _ = pre[h]

        end_c = sub * CMP_STRIDE + (CMP_LEN - 1)
        cmask = (t_row >= end_c) & (sub < n_cmp)
        ps = []
        for g in range(GROUP):
            s = s_all[:, g * TQ:(g + 1) * TQ] + SLOPES_NSA[heads[g]] * end_c.astype(F32)
            s = jnp.where(cmask, s, NEG_INF)
            m = jnp.max(s, axis=0, keepdims=True)
            e = jnp.where(cmask, jnp.exp2(s - m), 0.0)
            z = jnp.sum(e, axis=0, keepdims=True)
            ps.append(e / jnp.where(z > 0, z, 1.0))
        o_cmp = _dot(vct_ref[0, 0, h], jnp.concatenate(ps, axis=1).astype(BF16))

        dist = t_row - (w_start + lax.broadcasted_iota(jnp.int32, (nw * TQ, 1), 0))
        s = masked(s_win, (dist >= 0) & (dist < NSA_WINDOW))
        m = jnp.max(s, axis=0, keepdims=True)
        acc = _dot(v_t(vsect(1, h), w_first, nw), jnp.exp2(s - m).astype(BF16))
        o_win = acc / acc[HEAD_DIM:HEAD_DIM + 1, :]

        psum = ps[0] + ps[1] + ps[2] + ps[3]
        p_hi = psum.astype(BF16)
        p_lo = (psum - p_hi.astype(F32)).astype(BF16)
        imp = (_dot(ovt_ref[...], p_hi) + _dot(ovt_ref[...], p_lo))[ROWS_LO:ROWS_HI]
        part = [gate(hh, 0) * o_cmp[:, g * TQ:(g + 1) * TQ] + gate(hh, 2) * o_win[:, g * TQ:(g + 1) * TQ]
                for g, hh in enumerate(heads)]
        return imp, part

    def nsa_select(h, imp):
        q_ts, scol = pre[h][:2]
        cur = lax.shift_right_logical(t_row, int(np.log2(SLC_LEN)))
        valid = in_rng & (blk * SLC_LEN <= t_row)
        forced = in_rng & ((blk == 0) | (blk == cur) | (blk == cur - 1))
        score = jnp.where(forced, FORCE_SCORE, jnp.where(valid, imp, NEG_INF))
        rank = jnp.zeros(score.shape, F32)
        for i in range(N_SLC_BLK):
            r = FEAT_SEL - ROWS_LO + i
            si = score[r:r + 1, :]
            rank = rank + jnp.where((si > score) | ((si == score) & (blk > i)), 1.0, 0.0)
        sel = in_rng & (rank < SLC_TOPK) & (score > 0.5 * NEG_INF)
        bias_diag = jnp.where(in_rng & jnp.logical_not(sel), NEG_INF, 0.0)
        bias_main = jnp.where(in_rng & jnp.logical_not(sel & (blk < 2 * qi)), NEG_INF, 0.0)

        q_d = q_aug_t(q_ts, [jnp.where(is_pos, scol[g], bias_diag) for g in range(GROUP)])
        q_m = q_aug_t(q_ts, [jnp.where(is_pos, scol[g], bias_main) for g in range(GROUP)])
        return q_m, _dot(ksect(0, h)[0, pl.ds(q0, TQ), :], q_d)

    def nsa_diag(h, q_m, s_diag):
        s = masked(s_diag, q0 + sub <= t_row)
        m0 = jnp.max(s, axis=0, keepdims=True)
        acc0 = _dot(v_t(vsect(0, h), qi, 1), jnp.exp2(s - m0).astype(BF16))
        return q_m, m0, acc0

    local = [nsa_local(h) for h in range(N_KV)]
    yield PHASED

    dist = t_row - (b_start + lax.broadcasted_iota(jnp.int32, (nb * TQ, 1), 0))
    bmask = (dist >= 0) & (dist < SWA_WINDOW)
    outs_b = []
    for h in range(N_KV):
        heads = [h * GROUP + g for g in range(GROUP)]
        s = masked(pre[h][4], bmask)
        sink = jnp.concatenate([sinks_ref[hh] * LOG2E + SLOPES_SWA[hh] * t_row.astype(F32) for hh in heads], axis=1)
        m = jnp.maximum(jnp.max(s, axis=0, keepdims=True), sink)
        acc = _dot(v_t(vsect(2, h), b_first, nb), jnp.exp2(s - m).astype(BF16))
        o_all = acc / (acc[HEAD_DIM:HEAD_DIM + 1, :] + jnp.exp2(sink - m))
        outs_b += [o_all[:, g * TQ:(g + 1) * TQ] for g in range(GROUP)]
    store_heads(ob_ref, outs_b)

    yield PHASED
    selected = [nsa_select(h, local[h][0]) for h in range(N_KV)]
    yield PHASED
    fronts = [nsa_diag(h, *selected[h]) + (local[h][1],) for h in range(N_KV)]
    nblk = KCHUNK // TQ

    def absorb(carry, c, pieces):
        state = list(carry)
        subs = [(j, h) for j in pieces for h in range(N_KV)]
        scores = [_dot(ksect(0, h)[0, pl.ds(pl.multiple_of(c * KCHUNK + j * SUB, SUB), SUB), :], fronts[h][0])
                  for j, h in subs]
        for (j, h), s in zip(subs, scores):
            m_i, acc = state[h]
            m_new = jnp.maximum(m_i, jnp.max(s, axis=0, keepdims=True))
            pv = _dot(v_t(vsect(0, h), c * nblk + j * (SUB // TQ), SUB // TQ), jnp.exp2(s - m_new).astype(BF16))
            state[h] = (m_new, jnp.exp2(m_i - m_new) * acc + pv)
        return tuple(state)

    yield SWEEP
    step_chunk = pl.program_id(1)
    swept = lax.fori_loop(0, step_chunk, lambda c, st: absorb(st, c, range(KCHUNK // SUB)),
                          tuple((f[1], f[2]) for f in fronts))
    if u:
        swept = absorb(swept, step_chunk, range(u * TQ // SUB))
    outs_a = []
    for h in range(N_KV):
        acc = swept[h][1]
        o_slc = acc / acc[HEAD_DIM:HEAD_DIM + 1, :]
        outs_a += [fronts[h][3][g] + gate(h * GROUP + g, 1) * o_slc[:, g * TQ:(g + 1) * TQ] for g in range(GROUP)]
    store_heads(oa_ref, outs_a)


def _attention(proj3, v_t, kc, vc_t, ovt, eye, sinks):
    bsz, seq, _ = proj3.shape
    n_cmp = seq // CMP_STRIDE - CMP_LEN // CMP_STRIDE + 1
    width = N_HEADS * HEAD_DIM
    consts = [ovt, eye]
    tq = TILES_PER_STEP * TQ
    qspec = lambda col: pl.BlockSpec((1, tq, width), lambda b, q, s: (b, q, col))
    cspec = lambda a, j: pl.BlockSpec((1, 1) + a.shape[2:], lambda b, q, s: (b, j, 0, 0, 0))
    in_specs = [qspec(0), qspec(COL_QB // width), cspec(kc, 0), cspec(vc_t, 1)]
    in_specs += [pl.BlockSpec((1, seq, LANES), lambda b, q, s, j=j: (b, 0, COL_KV // LANES + j))
                 for j in range(N_KV_SECT)]
    in_specs += [pl.BlockSpec((1, 1) + v_t.shape[2:], lambda b, q, s, j=j: (j, b, 0, 0, 0)) for j in range(N_KV_SECT)]
    in_specs += [pl.BlockSpec((1, tq, LANES), lambda b, q, s: (b, q, COL_GN // LANES))]
    in_specs += [pl.BlockSpec(c.shape, lambda b, q, s: (0, 0)) for c in consts]
    ospec = pl.BlockSpec((1, tq, width), lambda b, q, s: (b, q, 0))
    return pl.pallas_call(
        functools.partial(_attn_t_kernel, n_cmp=n_cmp),
        grid_spec=pltpu.PrefetchScalarGridSpec(
            num_scalar_prefetch=1, grid=(bsz, seq // tq), in_specs=in_specs, out_specs=[ospec, ospec]),
        out_shape=[jax.ShapeDtypeStruct((bsz, seq, width), BF16)] * 2,
        compiler_params=pltpu.CompilerParams(dimension_semantics=("parallel", "parallel"),
                                             vmem_limit_bytes=VMEM_LIMIT),
        name="attn",
    )(sinks, proj3, proj3, kc, vc_t, *([proj3] * N_KV_SECT), *([v_t] * N_KV_SECT), proj3, *consts)


def _merge_kernel(x_ref, oa_ref, ob_ref, g1_ref, g2_ref, wg_ref, wa_ref, wb_ref, wo_ref,
                  wrh_ref, wrl_ref, br_ref, x2_ref, hn_ref, route_ref, cnt_ref):
    tm = x_ref.shape[0]

    @pl.when(pl.program_id(0) == 0)
    def _():
        cnt_ref[...] = jnp.zeros_like(cnt_ref)

    halves = [slice(i * (tm // 2), (i + 1) * (tm // 2)) for i in range(2)]
    xs = [x_ref[r, :] for r in halves]
    pre = []
    for r, x in zip(halves, xs):
        h = _rms(x, g1_ref[...]).astype(BF16)
        pre.append((_dot(h, wg_ref[...]), _dot(oa_ref[r, :], wa_ref[...]), _dot(ob_ref[r, :], wb_ref[...])))
    mixes = []
    for g_pre, a, b in pre:
        gm = jax.nn.sigmoid(g_pre)
        mixin = gm[:, :D_MODEL] * a + gm[:, D_MODEL:] * b
        mixes.append(_dot(mixin.astype(BF16), wo_ref[...]))
    logit_halves = []
    for r, x, mix in zip(halves, xs, mixes):
        x2 = x + mix
        x2_ref[r, :] = x2
        hn = _rms(x2, g2_ref[...])
        for j in range(ROW_TILE):
            hn_ref[pl.ds(r.start * ROW_TILE + j, tm // 2, stride=ROW_TILE), :] = hn[:, j * LANES:(j + 1) * LANES]
        hn_b = hn.astype(BF16)
        hn_lo = (hn - hn_b.astype(F32)).astype(BF16)
        logit_halves.append(_dot_nt(wrh_ref[...], hn_b) + _dot_nt(wrh_ref[...], hn_lo) + _dot_nt(wrl_ref[...], hn_b))
    bias = br_ref[...]
    logits = jnp.concatenate(logit_halves, axis=1) + jnp.concatenate([bias] * (tm // LANES), axis=1)
    row = lax.broadcasted_iota(jnp.int32, (LANES, 1), 0)
    rowf = row.astype(F32)
    big = float(LANES)
    top = lambda a: jnp.max(a, axis=0, keepdims=True)
    first = lambda hit: jnp.min(jnp.where(hit, rowf, big), axis=0, keepdims=True)
    is_g = (row >= N_EXPERTS) & (row < N_EXPERTS + N_GROUPS)
    gl = jnp.where(is_g, logits, NEG_INF)
    gmax = top(gl)
    grp = first(gl == gmax) - N_EXPERTS
    p_grp = 1.0 / jnp.sum(jnp.where(is_g, jnp.exp(gl - gmax), 0.0), axis=0, keepdims=True)
    in_grp = (rowf >= grp * EPG) & (rowf < grp * EPG + EPG)
    el = jnp.where(in_grp, logits, NEG_INF)
    v0 = top(el)
    i0 = first(el == v0)
    el1 = jnp.where(rowf == i0, NEG_INF, el)
    v1 = top(el1)
    i1 = first(el1 == v1)
    e1 = jnp.exp(v1 - v0)
    w0 = p_grp / (1.0 + e1)
    w1 = p_grp * e1 / (1.0 + e1)

    oh0 = jnp.where(rowf == i0, 1.0, 0.0)
    oh1 = jnp.where(rowf == i1, 1.0, 0.0)
    oh = oh0 + oh1
    r_i = lax.broadcasted_iota(jnp.int32, (tm, tm), 0)
    c_i = lax.broadcasted_iota(jnp.int32, (tm, tm), 1)
    earlier = jnp.where(r_i < c_i, 1.0, 0.0).astype(BF16)
    before = cnt_ref[...] + _dot(oh.astype(BF16), earlier)
    rank0 = jnp.sum(oh0 * before, axis=0, keepdims=True)
    rank1 = jnp.sum(oh1 * before, axis=0, keepdims=True)
    cnt_ref[...] = cnt_ref[...] + jnp.sum(oh, axis=1, keepdims=True)
    row8 = row[:8]
    route = jnp.zeros((8, tm), F32)
    for k, v in enumerate((i0, i1, rank0, rank1, w0, w1)):
        route = jnp.where(row8 == k, v, route)
    route_ref[0] = route


def _merge(x2d, oa, ob, g1, g2, wg, wa, wb, wo, wrh, wrl, br, tm):
    n = x2d.shape[0]
    width = N_HEADS * HEAD_DIM
    row = lambda w: pl.BlockSpec((tm, w), lambda i: (i, 0))
    full = lambda a: pl.BlockSpec(a.shape, lambda i: (0, 0), pipeline_mode=pl.Buffered(1))
    return pl.pallas_call(
        _merge_kernel,
        grid=(n // tm,),
        in_specs=[row(D_MODEL), row(width), row(width), full(g1), full(g2), full(wg), full(wa), full(wb),
                  full(wo), full(wrh), full(wrl), full(br)],
        out_specs=[row(D_MODEL), pl.BlockSpec((tm * ROW_TILE, LANES), lambda i: (i, 0)),
                   pl.BlockSpec((1, 8, tm), lambda i: (i, 0, 0)), pl.BlockSpec((LANES, 1), lambda i: (0, 0))],
        out_shape=[jax.ShapeDtypeStruct((n, D_MODEL), F32), jax.ShapeDtypeStruct((n * ROW_TILE, LANES), F32),
                   jax.ShapeDtypeStruct((n // tm, 8, tm), F32), jax.ShapeDtypeStruct((LANES, 1), F32)],
        compiler_params=pltpu.CompilerParams(dimension_semantics=("arbitrary",),
                                             vmem_limit_bytes=VMEM_LIMIT),
        name="merge",
    )(x2d, oa, ob, g1, g2, wg, wa, wb, wo, wrh, wrl, br)


def _slot_table_kernel(pos_ref, init_ref, tab_ref, sem, *, tt, n):
    step = pl.program_id(0)

    @pl.when(step == 0)
    def _():
        cp = pltpu.make_async_copy(init_ref, tab_ref, sem)
        cp.start()
        cp.wait()

    def chunk(c, row):
        base = pl.multiple_of(c * LANES, LANES)
        for k in range(LANES):
            tab_ref[pos_ref[0, 0, base + k]] = row + k * ROW_TILE
            tab_ref[pos_ref[0, 1, base + k]] = row + (n + k) * ROW_TILE
        return row + LANES * ROW_TILE

    lax.fori_loop(0, tt // LANES, chunk, step * (tt * ROW_TILE))


def _slot_table(pos, init, tt):
    nt = pos.shape[0]
    return pl.pallas_call(
        functools.partial(_slot_table_kernel, tt=tt, n=nt * tt),
        grid=(nt,),
        in_specs=[pl.BlockSpec((1, 2, tt), lambda i: (i, 0, 0), memory_space=pltpu.SMEM),
                  pl.BlockSpec(memory_space=pl.ANY)],
        out_specs=pl.BlockSpec(memory_space=pltpu.SMEM),
        out_shape=jax.ShapeDtypeStruct(init.shape, jnp.int32),
        scratch_shapes=[pltpu.SemaphoreType.DMA(())],
        compiler_params=pltpu.CompilerParams(dimension_semantics=("arbitrary",)),
        name="slot_table",
    )(pos, init)


def _moe_kernel(be_ref, nu_ref, tab_ref, tab_next_ref, tab_ahead_ref, hn_ref, wg_ref, wu_ref, wd_ref, out_ref,
                xin, yout, wg_s, wu_s, wd_s, gsem, ssem, *, tb, n):
    b = pl.program_id(0)
    n_used = nu_ref[0]
    live = b < n_used
    s = lax.rem(b, MOE_SLOTS)

    def gather(tab, slot):
        for j in range(tb):
            t, rows = tab[0, 0, j], n * ROW_TILE
            src = jnp.bitwise_and(t, rows - 1) if rows & (rows - 1) == 0 else lax.rem(t, rows)
            pltpu.make_async_copy(hn_ref.at[_tile_at(src)], xin.at[slot, pl.ds(j * ROW_TILE, ROW_TILE)],
                                  gsem.at[slot]).start(priority=j % 2)

    def wait_rows(sem_ref, slot):
        pltpu.make_async_copy(hn_ref.at[pl.ds(0, tb * ROW_TILE)], xin.at[slot], sem_ref.at[slot]).wait()

    @pl.when(b == 0)
    def _():
        yout[...] = jnp.zeros_like(yout)
        fills = [pltpu.make_async_copy(yout.at[k], out_ref.at[pl.ds((2 * n + k * tb) * ROW_TILE, tb * ROW_TILE)],
                                       ssem.at[k]) for k in range(MOE_SLOTS)]
        for cp in fills:
            cp.start()
        for cp in fills:
            cp.wait()
        gather(tab_ref, 0)
        gather(tab_next_ref, 1)

    @pl.when(live & ((b == 0) | (be_ref[b] != be_ref[jnp.maximum(b - 1, 0)])))
    def _():
        wg_s[...] = wg_ref[0].astype(BF16)
        wu_s[...] = wu_ref[0].astype(BF16)
        wd_s[...] = wd_ref[0].astype(BF16)

    @pl.when(live & (b >= MOE_SLOTS))
    def _():
        wait_rows(ssem, s)

    def block(s):
        wait_rows(gsem, s)
        gather(tab_ahead_ref, (s + 2) % MOE_SLOTS)
        half = tb // 2
        gu = []
        for i in range(2):
            xb = jnp.concatenate([xin[s, pl.ds(i * half * ROW_TILE + j, half, stride=ROW_TILE), :]
                                  for j in range(ROW_TILE)], axis=1).astype(BF16)
            gu.append((_dot(xb, wg_s[...]), _dot(xb, wu_s[...])))
        for i, (g, u) in enumerate(gu):
            y = _dot((g * jax.nn.sigmoid(g) * u).astype(BF16), wd_s[...])
            for j in range(ROW_TILE):
                yout[s, pl.ds(i * half * ROW_TILE + j, half, stride=ROW_TILE), :] = y[:, j * LANES:(j + 1) * LANES]

        for j in range(tb):
            pltpu.make_async_copy(yout.at[s, pl.ds(j * ROW_TILE, ROW_TILE)], out_ref.at[_tile_at(tab_ref[0, 0, j])],
                                  ssem.at[s]).start(priority=j % 2)

    for slot in range(MOE_SLOTS):
        pl.when(live & (s == slot))(functools.partial(block, slot))

    @pl.when(live & (b == n_used - 1))
    def _():
        for k in (1, 2):
            wait_rows(gsem, lax.rem(s + k, MOE_SLOTS))
        wait_rows(ssem, s)
        for k in (1, 2):

            @pl.when(b >= k)
            def _():
                wait_rows(ssem, lax.rem(s + MOE_SLOTS - k, MOE_SLOTS))


def _moe(block_e, n_used, tab, hn, wg, wu, wd, tb, n):
    nblk = tab.shape[0] // tb
    tab2 = tab.reshape(nblk, 1, tb)
    live = lambda b, be, nu: jnp.minimum(b, nu[0] - 1)
    ahead = lambda k: (lambda b, be, nu: (jnp.minimum(b + k, nu[0] - 1), 0, 0))
    wspec = lambda shape: pl.BlockSpec((1,) + shape, lambda b, be, nu: (be[live(b, be, nu)], 0, 0))
    return pl.pallas_call(
        functools.partial(_moe_kernel, tb=tb, n=n),
        grid_spec=pltpu.PrefetchScalarGridSpec(
            num_scalar_prefetch=2, grid=(nblk,),
            in_specs=[pl.BlockSpec((1, 1, tb), ahead(0), memory_space=pltpu.SMEM),
                      pl.BlockSpec((1, 1, tb), ahead(1), memory_space=pltpu.SMEM),
                      pl.BlockSpec((1, 1, tb), ahead(2), memory_space=pltpu.SMEM),
                      pl.BlockSpec(memory_space=pl.ANY),
                      wspec((D_MODEL, EXPERT_FF)), wspec((D_MODEL, EXPERT_FF)), wspec((EXPERT_FF, D_MODEL))],
            out_specs=pl.BlockSpec(memory_space=pl.ANY),
            scratch_shapes=[pltpu.VMEM((MOE_SLOTS, tb * ROW_TILE, LANES), F32),
                            pltpu.VMEM((MOE_SLOTS, tb * ROW_TILE, LANES), F32),
                            pltpu.VMEM((D_MODEL, EXPERT_FF), BF16), pltpu.VMEM((D_MODEL, EXPERT_FF), BF16),
                            pltpu.VMEM((EXPERT_FF, D_MODEL), BF16),
                            pltpu.SemaphoreType.DMA((MOE_SLOTS,)), pltpu.SemaphoreType.DMA((MOE_SLOTS,))]),
        out_shape=jax.ShapeDtypeStruct(((2 * n + MOE_SLOTS * tb) * ROW_TILE, LANES), F32),
        compiler_params=pltpu.CompilerParams(dimension_semantics=("arbitrary",),
                                             vmem_limit_bytes=VMEM_LIMIT),
        name="moe",
    )(block_e, n_used, tab2, tab2, tab2, hn, wg, wu, wd)


def _final_kernel(x2_ref, w_ref, y0_ref, y1_ref, gf_ref, o_ref):
    tc = x2_ref.shape[0]
    w = w_ref[...]
    y = x2_ref[...] + (w[:, 0:1] * _tiles_to_rows(y0_ref, tc) + w[:, 1:2] * _tiles_to_rows(y1_ref, tc))
    o_ref[...] = _rms(y, gf_ref[...])


def _final(x2, w_slot, yslots, gf, tc):
    n = x2.shape[0]
    nt = n // tc
    return pl.pallas_call(
        _final_kernel,
        grid=(nt,),
        in_specs=[pl.BlockSpec((tc, D_MODEL), lambda i: (i, 0)),
                  pl.BlockSpec((tc, 2), lambda i: (i, 0)),
                  pl.BlockSpec((tc * ROW_TILE, LANES), lambda i: (i, 0)),
                  pl.BlockSpec((tc * ROW_TILE, LANES), lambda i: (nt + i, 0)),
                  pl.BlockSpec((1, D_MODEL), lambda i: (0, 0))],
        out_specs=pl.BlockSpec((tc, D_MODEL), lambda i: (i, 0)),
        out_shape=jax.ShapeDtypeStruct((n, D_MODEL), F32),
        compiler_params=pltpu.CompilerParams(dimension_semantics=("parallel",),
                                             vmem_limit_bytes=VMEM_LIMIT),
        name="final",
    )(x2, w_slot, yslots, yslots, gf)


def _tile_at(row):
    return pl.ds(pl.multiple_of(row, ROW_TILE), ROW_TILE)


def _overlap_matrix_t(seq):
    nc = seq // CMP_STRIDE - CMP_LEN // CMP_STRIDE + 1
    ns = seq // SLC_LEN
    c0 = np.arange(nc) * CMP_STRIDE
    s0 = np.arange(ns) * SLC_LEN
    ov = np.clip(np.minimum(c0[:, None] + CMP_LEN, s0[None, :] + SLC_LEN)
                 - np.maximum(c0[:, None], s0[None, :]), 0, None) / CMP_LEN
    out = np.zeros((LANES, LANES), np.float32)
    out[FEAT_SEL:FEAT_SEL + ns, :nc] = ov.T
    return jnp.asarray(out, BF16)


def _position_features(seq):
    pos = np.arange(seq)
    fk = np.zeros((seq, LANES), np.float32)
    fk[:, FEAT_POS:FEAT_POS + 3] = (pos // SLC_LEN)[:, None]
    fk[:, FEAT_POS + 3:FEAT_POS + 6] = (pos % SLC_LEN)[:, None]
    fk[pos, FEAT_SEL + pos // SLC_LEN] = 1.0
    return jnp.asarray(fk)


def _pick_tile(n, pref):
    t = pref
    while n % t:
        t //= 2
    return t


def kernel(x, norm_mix_g, w_in, cmp_pos_k, cmp_w1_k, cmp_w2_k, cmp_pos_v, cmp_w1_v, cmp_w2_v, sinks, w_a, w_b,
           w_o, norm_ffn_g, w_group, b_group, w_expert, b_expert, w_gate_e, w_up_e, w_down_e, norm_final_g):
    bsz, seq, _ = x.shape
    n = bsz * seq
    assert TQ == LANES and seq % KCHUNK == 0 and seq // SLC_LEN <= N_SLC_BLK and seq // CMP_STRIDE <= LANES
    assert TILES_PER_STEP * TQ == KCHUNK and TQ % SUB == 0
    assert seq >= (NSA_WINDOW // TQ + 1) * TQ and w_in.shape[0] == 1
    x2d = x.reshape(n, D_MODEL)

    w = w_in[0]
    scale = HEAD_DIM ** -0.5 * LOG2E
    nsa_w, kvw = N_HEADS * HEAD_DIM, N_KV * HEAD_DIM
    o_qa, o_kva, o_gn = 0, nsa_w, nsa_w + 6 * kvw
    o_qb = o_gn + 3 * N_HEADS
    o_kvb = o_qb + nsa_w
    o_gm = o_kvb + 2 * kvw
    pair = lambda off: w[:, off:off + kvw]
    w_attn = jnp.concatenate(
        [w[:, o_qa:o_qa + nsa_w] * scale, w[:, o_qb:o_qb + nsa_w] * scale,
         pair(o_kva + 2 * kvw), pair(o_kva + 4 * kvw), pair(o_kvb),
         w[:, o_gn:o_gn + 3 * N_HEADS], jnp.zeros((D_MODEL, LANES - 3 * N_HEADS), F32),
         pair(o_kva + 3 * kvw), pair(o_kva + 5 * kvw), pair(o_kvb + kvw),
         w[:, o_kva:o_kva + 2 * kvw]], axis=1).astype(BF16)
    w_gm = w[:, o_gm:o_gm + 2 * D_MODEL].astype(BF16)

    tm = _pick_tile(seq, 512)
    feat_k = _position_features(seq)
    eye = jnp.eye(LANES, dtype=BF16)
    swap = jnp.roll(eye, HEAD_DIM, axis=1)
    proj, v_t, cmp_in = _proj(x2d, norm_mix_g[0][None], w_attn, feat_k, eye, swap, tm)
    proj3 = proj.reshape(bsz, seq, PROJ_W)

    nch = seq // CMP_STRIDE
    pos = jnp.stack([cmp_pos_k[0], cmp_pos_v[0]])
    pos = jnp.broadcast_to(pos[:, :, None, :], (2, CMP_LEN, N_KV, HEAD_DIM))
    pos_a = pos[:, :CMP_STRIDE].reshape(2, 1, CMP_STRIDE * kvw)
    pos_b = pos[:, CMP_STRIDE:].reshape(2, 1, CMP_STRIDE * kvw)
    w1 = jnp.stack([cmp_w1_k[0], cmp_w1_v[0]]).reshape(2, CMP_LEN, HEAD_DIM, CMP_HIDDEN)
    zero = jnp.zeros_like(w1)
    w1 = jnp.stack([jnp.concatenate([w1, zero], axis=-1), jnp.concatenate([zero, w1], axis=-1)], axis=2)
    w1 = w1.reshape(2, CMP_LEN * kvw, N_KV * CMP_HIDDEN).astype(BF16)
    w2 = jnp.pad(jnp.stack([cmp_w2_k[0], cmp_w2_v[0]]), ((0, 0), (0, 0), (0, LANES - HEAD_DIM))).astype(BF16)
    kvc, kvc_t = _compress(cmp_in.reshape(bsz, seq, CMP_W), pos_a, pos_b, w1[:, :CMP_STRIDE * kvw],
                           w1[:, CMP_STRIDE * kvw:], w2, jnp.swapaxes(w2, 1, 2))
    kvc = jnp.pad(kvc, ((0, 0), (0, 0), (0, 0), (0, LANES - nch), (0, 0)))
    kvc_t = jnp.pad(kvc_t, ((0, 0), (0, 0), (0, 0), (0, 0), (0, LANES - nch)))

    o_a, o_b = _attention(proj3, v_t, kvc, kvc_t, _overlap_matrix_t(seq), eye, sinks[0])

    w_r = jnp.concatenate([w_expert[0], w_group[0],
                           jnp.zeros((D_MODEL, LANES - N_EXPERTS - N_GROUPS), F32)], axis=1)
    w_r = w_r.T
    w_rh = w_r.astype(BF16)
    w_rl = (w_r - w_rh.astype(F32)).astype(BF16)
    b_r = jnp.concatenate([b_expert[0], b_group[0], jnp.zeros((LANES - N_EXPERTS - N_GROUPS,), F32)])
    b_r = b_r[:, None] * jnp.ones((1, LANES), F32)
    tt = _pick_tile(n, 1024)
    x2, hn, route, counts = _merge(
        x2d, o_a.reshape(n, nsa_w), o_b.reshape(n, nsa_w), norm_mix_g[0][None], norm_ffn_g[0][None], w_gm,
        w_a[0].astype(BF16), w_b[0].astype(BF16), w_o[0].astype(BF16), w_rh, w_rl, b_r, tt)

    tb = 256
    nblk = -(-(2 * n + N_EXPERTS * (tb - 1)) // tb)
    cnt = counts[:N_EXPERTS, 0].astype(jnp.int32)
    padded = (cnt + tb - 1) // tb * tb
    pad_end = jnp.cumsum(padded)
    pad_start = pad_end - padded
    block_e = jnp.minimum(jnp.sum(pad_end[None, :] <= (jnp.arange(nblk) * tb)[:, None], axis=1), N_EXPERTS - 1)
    n_used = (pad_end[-1:] // tb).astype(jnp.int32)
    tab = route[:, 0:4, :].astype(jnp.int32)
    eids = tab[:, 0:2, :]
    start_of = sum(jnp.where(eids == e, pad_start[e], 0) for e in range(N_EXPERTS))
    pos = start_of + tab[:, 2:4, :]
    w_slot = jnp.swapaxes(route[:, 4:6, :], 1, 2).reshape(n, 2)
    spare = (2 * n + jnp.arange(nblk * tb, dtype=jnp.int32) % (MOE_SLOTS * tb)) * ROW_TILE
    slot_tab = _slot_table(pos, spare, tt)
    y_slots = _moe(block_e.astype(jnp.int32), n_used, slot_tab, hn, w_gate_e[0], w_up_e[0], w_down_e[0], tb, n)
    out = _final(x2, w_slot, y_slots, norm_final_g[None], tt)
    return out.reshape(bsz, seq, D_MODEL)
```

```python
import functools

import numpy as np
import jax
import jax.numpy as jnp
from jax import lax
from jax.experimental import pallas as pl
from jax.experimental.pallas import tpu as pltpu

F32 = jnp.float32
BF16 = jnp.bfloat16

D_MODEL = 1024
HEAD_DIM = 64
N_HEADS = 8
N_KV = 2
GROUP = N_HEADS // N_KV
CMP_LEN = 32
CMP_STRIDE = 16
CMP_HIDDEN = 256
SLC_LEN = 64
SLC_TOPK = 8
NSA_WINDOW = 256
SWA_WINDOW = 128
N_GROUPS = 4
EPG = 8
N_EXPERTS = N_GROUPS * EPG
EXPERT_FF = 256
RMS_EPS = 1e-6
NEG_INF = -1e30
FORCE_SCORE = 1e9

LANES = 128
TQ = 128
TILES_PER_STEP = 4
KCHUNK = 512
SUB = 128
N_SLC_BLK = LANES // 4
FEAT_POS = HEAD_DIM
FEAT_SEL = HEAD_DIM + 6
ROWS_LO, ROWS_HI = 64, 104
N_KV_SECT = 6
COL_QB = 512
COL_KV = 1024
COL_GN = COL_KV + N_KV_SECT * LANES
PROJ_W = COL_GN + LANES
CMP_W = 2 * LANES
ROW_TILE = D_MODEL // LANES
MOE_SLOTS = 3
VMEM_LIMIT = 56 * 1024 * 1024


LOG2E = float(np.log2(np.e))


def _alibi_slopes():
    n = 2 * N_HEADS
    s = 2.0 ** (-8.0 * np.arange(1, n + 1) / n) * LOG2E
    return [float(v) for v in s[:N_HEADS]], [float(v) for v in s[N_HEADS:]]


SLOPES_SWA, SLOPES_NSA = _alibi_slopes()


def _bf16_pieces(v):
    out, rem = [], np.float32(v)
    for _ in range(3):
        p = np.float32(np.asarray(rem, np.float32).astype(BF16).astype(np.float32))
        out.append(float(p))
        rem = np.float32(rem - p)
    return out


def _rms(x, g):
    return x * lax.rsqrt(jnp.mean(x * x, axis=-1, keepdims=True) + RMS_EPS) * g


def _dot(a, b):
    return jnp.dot(a, b, preferred_element_type=F32)


def _tiles_to_rows(ref, n, lead=()):
    return jnp.concatenate([ref[lead + (pl.ds(j, n, stride=ROW_TILE), slice(None))] for j in range(ROW_TILE)], axis=1)


def _rows_to_tiles(ref, val):
    n = val.shape[0]
    for j in range(ROW_TILE):
        ref[pl.ds(j, n, stride=ROW_TILE), :] = val[:, j * LANES:(j + 1) * LANES]


def _dot_nt(a, b):
    return lax.dot_general(a, b, (((1,), (1,)), ((), ())), preferred_element_type=F32)


def _proj_kernel(x_ref, g_ref, w_ref, fk_ref, eye_ref, swap_ref, o_ref, vt_ref, cmp_ref):
    tm = x_ref.shape[0]
    n_br = N_KV_SECT // N_KV
    col_gn = COL_KV + n_br * LANES
    col_v = col_gn + LANES
    h = _rms(x_ref[...], g_ref[...]).astype(BF16)
    res = _dot(h, w_ref[...])
    o_ref[:, :COL_KV] = res[:, :COL_KV].astype(o_ref.dtype)
    o_ref[:, COL_GN:] = res[:, col_gn:col_v].astype(o_ref.dtype)
    low = lax.broadcasted_iota(jnp.int32, (1, LANES), 1) < HEAD_DIM
    ones_row = jnp.where(lax.broadcasted_iota(jnp.int32, (LANES - HEAD_DIM, tm), 0) == 0, 1.0, 0.0)
    for br in range(n_br):
        k_pair = res[:, COL_KV + br * LANES:COL_KV + (br + 1) * LANES]
        heads = (k_pair, _dot(k_pair.astype(BF16), swap_ref[...]))
        for hd in range(N_KV):
            c0 = COL_KV + (br * N_KV + hd) * LANES
            o_ref[:, c0:c0 + LANES] = jnp.where(low, heads[hd], fk_ref[...]).astype(o_ref.dtype)
        v_pair_t = _dot_nt(eye_ref[...], res[:, col_v + br * LANES:col_v + (br + 1) * LANES].astype(BF16))
        for hd in range(N_KV):
            v_t = jnp.concatenate([v_pair_t[hd * HEAD_DIM:(hd + 1) * HEAD_DIM], ones_row], axis=0)
            for k in range(tm // LANES):
                vt_ref[br * N_KV + hd, 0, k] = v_t[:, k * LANES:(k + 1) * LANES].astype(vt_ref.dtype)
    cmp_ref[...] = res[:, col_v + n_br * LANES:]


def _proj(x2d, g, w, feat_k, eye, swap, tm):
    n = x2d.shape[0]
    seq = feat_k.shape[0]
    nper = seq // tm
    kb = tm // LANES
    return pl.pallas_call(
        _proj_kernel,
        grid=(n // tm,),
        in_specs=[pl.BlockSpec((tm, D_MODEL), lambda i: (i, 0)),
                  pl.BlockSpec((1, D_MODEL), lambda i: (0, 0)),
                  pl.BlockSpec(w.shape, lambda i: (0, 0)),
                  pl.BlockSpec((tm, LANES), lambda i: (i % nper, 0)),
                  pl.BlockSpec((LANES, LANES), lambda i: (0, 0)),
                  pl.BlockSpec((LANES, LANES), lambda i: (0, 0))],
        out_specs=[pl.BlockSpec((tm, PROJ_W), lambda i: (i, 0)),
                   pl.BlockSpec((N_KV_SECT, 1, kb, LANES, LANES), lambda i: (0, i // nper, i % nper, 0, 0)),
                   pl.BlockSpec((tm, CMP_W), lambda i: (i, 0))],
        out_shape=[jax.ShapeDtypeStruct((n, PROJ_W), BF16),
                   jax.ShapeDtypeStruct((N_KV_SECT, n // seq, seq // LANES, LANES, LANES), BF16),
                   jax.ShapeDtypeStruct((n, CMP_W), F32)],
        compiler_params=pltpu.CompilerParams(dimension_semantics=("parallel",),
                                             vmem_limit_bytes=VMEM_LIMIT),
        name="proj",
    )(x2d, g, w, feat_k, eye, swap)


def _compress_kernel(x_ref, pa_ref, pb_ref, w1a_ref, w1b_ref, w2_ref, w2t_ref, o_ref, ot_ref, *, nch):
    r = jnp.concatenate([x_ref[0, pl.ds(j, nch, stride=CMP_STRIDE), :] for j in range(CMP_STRIDE)], axis=1)
    a = _dot((r + pa_ref[0]).astype(BF16), w1a_ref[0])
    b = _dot((r + pb_ref[0]).astype(BF16), w1b_ref[0])
    hid = a + pltpu.roll(b, nch - 1, 0)
    hid = hid * jax.nn.sigmoid(hid)
    for h in range(N_KV):
        hid_h = hid[:, h * CMP_HIDDEN:(h + 1) * CMP_HIDDEN].astype(BF16)
        o_ref[0, 0, h] = _dot(hid_h, w2_ref[0]).astype(o_ref.dtype)
        ot_ref[0, 0, h] = _dot_nt(w2t_ref[0], hid_h).astype(ot_ref.dtype)


def _compress(cmp3, pos_a, pos_b, w1a, w1b, w2, w2t):
    bsz, seq, _ = cmp3.shape
    nch = seq // CMP_STRIDE
    wspec = lambda a: pl.BlockSpec((1,) + a.shape[1:], lambda b, j: (j, 0, 0))
    return pl.pallas_call(
        functools.partial(_compress_kernel, nch=nch),
        grid=(bsz, 2),
        in_specs=[pl.BlockSpec((1, seq, LANES), lambda b, j: (b, 0, j)),
                  wspec(pos_a), wspec(pos_b), wspec(w1a), wspec(w1b), wspec(w2), wspec(w2t)],
        out_specs=[pl.BlockSpec((1, 1, N_KV, nch, LANES), lambda b, j: (b, j, 0, 0, 0)),
                   pl.BlockSpec((1, 1, N_KV, LANES, nch), lambda b, j: (b, j, 0, 0, 0))],
        out_shape=[jax.ShapeDtypeStruct((bsz, 2, N_KV, nch, LANES), BF16),
                   jax.ShapeDtypeStruct((bsz, 2, N_KV, LANES, nch), BF16)],
        compiler_params=pltpu.CompilerParams(dimension_semantics=("parallel", "parallel"),
                                             vmem_limit_bytes=VMEM_LIMIT),
        name="compress",
    )(cmp3, pos_a, pos_b, w1a, w1b, w2, w2t)


def _attn_t_kernel(*refs, n_cmp):
    tiles = [_attn_tile(u, *refs, n_cmp=n_cmp) for u in range(TILES_PER_STEP)]
    while all([next(t) is PHASED for t in tiles]):
        pass
    for t in tiles:
        for _ in t:
            pass


PHASED, SWEEP = "phase done", "ready for the sweep"


def _attn_tile(u, sinks_ref, qa_ref, qb_ref, kc_ref, vct_ref, *rest, n_cmp):
    rows = slice(u * TQ, (u + 1) * TQ)
    n_br = 3
    ks = rest[:n_br * N_KV]
    vts = rest[n_br * N_KV:2 * n_br * N_KV]
    gn_ref, ovt_ref, eye_ref, oa_ref, ob_ref = rest[2 * n_br * N_KV:]
    ksect = lambda branch, h: ks[branch * N_KV + h]
    vsect = lambda branch, h: vts[branch * N_KV + h]
    qi = pl.program_id(1) * TILES_PER_STEP + u
    q0 = pl.multiple_of(qi * TQ, TQ)
    lane = lax.broadcasted_iota(jnp.int32, (1, TQ), 1)
    sub = lax.broadcasted_iota(jnp.int32, (LANES, 1), 0)
    t_row = q0 + lane
    eye = eye_ref[...]
    gates = jax.nn.sigmoid(_dot_nt(eye, gn_ref[0, rows, :]))
    gate = lambda hh, c: gates[3 * hh + c:3 * hh + c + 1, :]
    sub40 = sub[ROWS_LO:ROWS_HI]
    blk = sub40 - FEAT_SEL
    in_rng = (blk >= 0) & (blk < N_SLC_BLK)
    is_pos = (sub40 >= FEAT_POS) & (sub40 < FEAT_SEL)
    zeros_lo = jnp.zeros((LANES - ROWS_HI, TQ), F32)

    def q_t(ref, hh):
        both = _dot_nt(eye, ref[0, rows, (hh // 2) * LANES:(hh // 2 + 1) * LANES])
        return both[(hh % 2) * HEAD_DIM:(hh % 2 + 1) * HEAD_DIM]

    def slope_col(slope):
        hi, mid, lo = _bf16_pieces(slope)
        col = jnp.zeros(sub40.shape, F32)
        for i, v in enumerate([SLC_LEN * hi, SLC_LEN * mid, SLC_LEN * lo, hi, mid, lo]):
            col = jnp.where(sub40 == FEAT_POS + i, v, col)
        return col

    def q_aug_t(q_ts, tails):
        return jnp.concatenate([jnp.concatenate([q, jnp.broadcast_to(t, (ROWS_HI - ROWS_LO, TQ)), zeros_lo], axis=0)
                                for q, t in zip(q_ts, tails)], axis=1).astype(BF16)

    def v_t(ref, first, n):
        return jnp.concatenate([ref[0, 0, first + j] for j in range(n)], axis=1)

    def masked(s, mask):
        return jnp.concatenate([jnp.where(mask, s[:, g * TQ:(g + 1) * TQ], NEG_INF) for g in range(GROUP)], axis=1)

    def store_heads(ref, outs):
        for p in range(N_HEADS // 2):
            pair = jnp.concatenate([outs[2 * p][:HEAD_DIM], outs[2 * p + 1][:HEAD_DIM]], axis=0).astype(BF16)
            ref[0, rows, p * LANES:(p + 1) * LANES] = _dot_nt(eye, pair).astype(ref.dtype)

    nw = NSA_WINDOW // TQ + 1
    w_first = jnp.maximum(qi - (nw - 1), 0)
    w_start = pl.multiple_of(w_first * TQ, TQ)
    nb = (SWA_WINDOW - 1 + TQ - 1) // TQ + 1
    b_first = jnp.maximum(qi - (nb - 1), 0)
    b_start = pl.multiple_of(b_first * TQ, TQ)
    pre = []
    for h in range(N_KV):
        heads = [h * GROUP + g for g in range(GROUP)]
        q_ts = [q_t(qa_ref, hh) for hh in heads]
        scol = [slope_col(SLOPES_NSA[hh]) for hh in heads]
        s_cmp = _dot(kc_ref[0, 0, h], q_aug_t(q_ts, [jnp.zeros((1, 1), F32)] * GROUP))
        s_win = _dot(ksect(1, h)[0, pl.ds(w_start, nw * TQ), :], q_aug_t(q_ts, scol))
        q_b = q_aug_t([q_t(qb_ref, hh) for hh in heads], [slope_col(SLOPES_SWA[hh]) for hh in heads])
        s_swa = _dot(ksect(2, h)[0, pl.ds(b_start, nb * TQ), :], q_b)
        pre.append((q_ts, scol, s_cmp, s_win, s_swa))

    yield PHASED
    def nsa_local(h):
        heads = [h * GROUP + g for g in range(GROUP)]
        _, _, s_all, s_win, _ = pre[h]

        end_c = sub * CMP_STRIDE + (CMP_LEN - 1)
        cmask = (t_row >= end_c) & (sub < n_cmp)
        ps = []
        for g in range(GROUP):
            s = s_all[:, g * TQ:(g + 1) * TQ] + SLOPES_NSA[heads[g]] * end_c.astype(F32)
            s = jnp.where(cmask, s, NEG_INF)
            m = jnp.max(s, axis=0, keepdims=True)
            e = jnp.where(cmask, jnp.exp2(s - m), 0.0)
            z = jnp.sum(e, axis=0, keepdims=True)
            ps.append(e / jnp.where(z > 0, z, 1.0))
        o_cmp = _dot(vct_ref[0, 0, h], jnp.concatenate(ps, axis=1).astype(BF16))

        dist = t_row - (w_start + lax.broadcasted_iota(jnp.int32, (nw * TQ, 1), 0))
        s = masked(s_win, (dist >= 0) & (dist < NSA_WINDOW))
        m = jnp.max(s, axis=0, keepdims=True)
        acc = _dot(v_t(vsect(1, h), w_first, nw), jnp.exp2(s - m).astype(BF16))
        o_win = acc / acc[HEAD_DIM:HEAD_DIM + 1, :]

        psum = ps[0] + ps[1] + ps[2] + ps[3]
        p_hi = psum.astype(BF16)
        p_lo = (psum - p_hi.astype(F32)).astype(BF16)
        imp = (_dot(ovt_ref[...], p_hi) + _dot(ovt_ref[...], p_lo))[ROWS_LO:ROWS_HI]
        part = [gate(hh, 0) * o_cmp[:, g * TQ:(g + 1) * TQ] + gate(hh, 2) * o_win[:, g * TQ:(g + 1) * TQ]
                for g, hh in enumerate(heads)]
        return imp, part

    def nsa_select(h, imp):
        q_ts, scol = pre[h][:2]
        cur = lax.shift_right_logical(t_row, int(np.log2(SLC_LEN)))
        valid = in_rng & (blk * SLC_LEN <= t_row)
        forced = in_rng & ((blk == 0) | (blk == cur) | (blk == cur - 1))
        score = jnp.where(forced, FORCE_SCORE, jnp.where(valid, imp, NEG_INF))
        rank = jnp.zeros(score.shape, F32)
        for i in range(N_SLC_BLK):
            r = FEAT_SEL - ROWS_LO + i
            si = score[r:r + 1, :]
            rank = rank + jnp.where((si > score) | ((si == score) & (blk > i)), 1.0, 0.0)
        sel = in_rng & (rank < SLC_TOPK) & (score > 0.5 * NEG_INF)
        bias_diag = jnp.where(in_rng & jnp.logical_not(sel), NEG_INF, 0.0)
        bias_main = jnp.where(in_rng & jnp.logical_not(sel & (blk < 2 * qi)), NEG_INF, 0.0)

        q_d = q_aug_t(q_ts, [jnp.where(is_pos, scol[g], bias_diag) for g in range(GROUP)])
        q_m = q_aug_t(q_ts, [jnp.where(is_pos, scol[g], bias_main) for g in range(GROUP)])
        return q_m, _dot(ksect(0, h)[0, pl.ds(q0, TQ), :], q_d)

    def nsa_diag(h, q_m, s_diag):
        s = masked(s_diag, q0 + sub <= t_row)
        m0 = jnp.max(s, axis=0, keepdims=True)
        acc0 = _dot(v_t(vsect(0, h), qi, 1), jnp.exp2(s - m0).astype(BF16))
        return q_m, m0, acc0

    local = [nsa_local(h) for h in range(N_KV)]
    yield PHASED

    dist = t_row - (b_start + lax.broadcasted_iota(jnp.int32, (nb * TQ, 1), 0))
    bmask = (dist >= 0) & (dist < SWA_WINDOW)
    outs_b = []
    for h in range(N_KV):
        heads = [h * GROUP + g for g in range(GROUP)]
        s = masked(pre[h][4], bmask)
        sink = jnp.concatenate([sinks_ref[hh] * LOG2E + SLOPES_SWA[hh] * t_row.astype(F32) for hh in heads], axis=1)
        m = jnp.maximum(jnp.max(s, axis=0, keepdims=True), sink)
        acc = _dot(v_t(vsect(2, h), b_first, nb), jnp.exp2(s - m).astype(BF16))
        o_all = acc / (acc[HEAD_DIM:HEAD_DIM + 1, :] + jnp.exp2(sink - m))
        outs_b += [o_all[:, g * TQ:(g + 1) * TQ] for g in range(GROUP)]
    store_heads(ob_ref, outs_b)

    yield PHASED
    selected = [nsa_select(h, local[h][0]) for h in range(N_KV)]
    yield PHASED
    fronts = [nsa_diag(h, *selected[h]) + (local[h][1],) for h in range(N_KV)]
    nblk = KCHUNK // TQ

    def absorb(carry, c, pieces):
        state = list(carry)
        subs = [(j, h) for j in pieces for h in range(N_KV)]
        scores = [_dot(ksect(0, h)[0, pl.ds(pl.multiple_of(c * KCHUNK + j * SUB, SUB), SUB), :], fronts[h][0])
                  for j, h in subs]
        for (j, h), s in zip(subs, scores):
            m_i, acc = state[h]
            m_new = jnp.maximum(m_i, jnp.max(s, axis=0, keepdims=True))
            pv = _dot(v_t(vsect(0, h), c * nblk + j * (SUB // TQ), SUB // TQ), jnp.exp2(s - m_new).astype(BF16))
            state[h] = (m_new, jnp.exp2(m_i - m_new) * acc + pv)
        return tuple(state)

    yield SWEEP
    step_chunk = pl.program_id(1)
    swept = lax.fori_loop(0, step_chunk, lambda c, st: absorb(st, c, range(KCHUNK // SUB)),
                          tuple((f[1], f[2]) for f in fronts))
    if u:
        swept = absorb(swept, step_chunk, range(u * TQ // SUB))
    outs_a = []
    for h in range(N_KV):
        acc = swept[h][1]
        o_slc = acc / acc[HEAD_DIM:HEAD_DIM + 1, :]
        outs_a += [fronts[h][3][g] + gate(h * GROUP + g, 1) * o_slc[:, g * TQ:(g + 1) * TQ] for g in range(GROUP)]
    store_heads(oa_ref, outs_a)


def _attention(proj3, v_t, kc, vc_t, ovt, eye, sinks):
    bsz, seq, _ = proj3.shape
    n_cmp = seq // CMP_STRIDE - CMP_LEN // CMP_STRIDE + 1
    width = N_HEADS * HEAD_DIM
    consts = [ovt, eye]
    tq = TILES_PER_STEP * TQ
    qspec = lambda col: pl.BlockSpec((1, tq, width), lambda b, q, s: (b, q, col))
    cspec = lambda a, j: pl.BlockSpec((1, 1) + a.shape[2:], lambda b, q, s: (b, j, 0, 0, 0))
    in_specs = [qspec(0), qspec(COL_QB // width), cspec(kc, 0), cspec(vc_t, 1)]
    in_specs += [pl.BlockSpec((1, seq, LANES), lambda b, q, s, j=j: (b, 0, COL_KV // LANES + j))
                 for j in range(N_KV_SECT)]
    in_specs += [pl.BlockSpec((1, 1) + v_t.shape[2:], lambda b, q, s, j=j: (j, b, 0, 0, 0)) for j in range(N_KV_SECT)]
    in_specs += [pl.BlockSpec((1, tq, LANES), lambda b, q, s: (b, q, COL_GN // LANES))]
    in_specs += [pl.BlockSpec(c.shape, lambda b, q, s: (0, 0)) for c in consts]
    ospec = pl.BlockSpec((1, tq, width), lambda b, q, s: (b, q, 0))
    return pl.pallas_call(
        functools.partial(_attn_t_kernel, n_cmp=n_cmp),
        grid_spec=pltpu.PrefetchScalarGridSpec(
            num_scalar_prefetch=1, grid=(bsz, seq // tq), in_specs=in_specs, out_specs=[ospec, ospec]),
        out_shape=[jax.ShapeDtypeStruct((bsz, seq, width), BF16)] * 2,
        compiler_params=pltpu.CompilerParams(dimension_semantics=("parallel", "parallel"),
                                             vmem_limit_bytes=VMEM_LIMIT),
        name="attn",
    )(sinks, proj3, proj3, kc, vc_t, *([proj3] * N_KV_SECT), *([v_t] * N_KV_SECT), proj3, *consts)


def _merge_kernel(x_ref, oa_ref, ob_ref, g1_ref, g2_ref, wg_ref, wa_ref, wb_ref, wo_ref,
                  wrh_ref, wrl_ref, br_ref, x2_ref, hn_ref, route_ref, cnt_ref):
    tm = x_ref.shape[0]

    @pl.when(pl.program_id(0) == 0)
    def _():
        cnt_ref[...] = jnp.zeros_like(cnt_ref)

    halves = [slice(i * (tm // 2), (i + 1) * (tm // 2)) for i in range(2)]
    xs = [x_ref[r, :] for r in halves]
    pre = []
    for r, x in zip(halves, xs):
        h = _rms(x, g1_ref[...]).astype(BF16)
        pre.append((_dot(h, wg_ref[...]), _dot(oa_ref[r, :], wa_ref[...]), _dot(ob_ref[r, :], wb_ref[...])))
    mixes = []
    for g_pre, a, b in pre:
        gm = jax.nn.sigmoid(g_pre)
        mixin = gm[:, :D_MODEL] * a + gm[:, D_MODEL:] * b
        mixes.append(_dot(mixin.astype(BF16), wo_ref[...]))
    logit_halves = []
    for r, x, mix in zip(halves, xs, mixes):
        x2 = x + mix
        x2_ref[r, :] = x2
        hn = _rms(x2, g2_ref[...])
        for j in range(ROW_TILE):
            hn_ref[pl.ds(r.start * ROW_TILE + j, tm // 2, stride=ROW_TILE), :] = hn[:, j * LANES:(j + 1) * LANES]
        hn_b = hn.astype(BF16)
        hn_lo = (hn - hn_b.astype(F32)).astype(BF16)
        logit_halves.append(_dot_nt(wrh_ref[...], hn_b) + _dot_nt(wrh_ref[...], hn_lo) + _dot_nt(wrl_ref[...], hn_b))
    bias = br_ref[...]
    logits = jnp.concatenate(logit_halves, axis=1) + jnp.concatenate([bias] * (tm // LANES), axis=1)
    row = lax.broadcasted_iota(jnp.int32, (LANES, 1), 0)
    rowf = row.astype(F32)
    big = float(LANES)
    top = lambda a: jnp.max(a, axis=0, keepdims=True)
    first = lambda hit: jnp.min(jnp.where(hit, rowf, big), axis=0, keepdims=True)
    is_g = (row >= N_EXPERTS) & (row < N_EXPERTS + N_GROUPS)
    gl = jnp.where(is_g, logits, NEG_INF)
    gmax = top(gl)
    grp = first(gl == gmax) - N_EXPERTS
    p_grp = 1.0 / jnp.sum(jnp.where(is_g, jnp.exp(gl - gmax), 0.0), axis=0, keepdims=True)
    in_grp = (rowf >= grp * EPG) & (rowf < grp * EPG + EPG)
    el = jnp.where(in_grp, logits, NEG_INF)
    v0 = top(el)
    i0 = first(el == v0)
    el1 = jnp.where(rowf == i0, NEG_INF, el)
    v1 = top(el1)
    i1 = first(el1 == v1)
    e1 = jnp.exp(v1 - v0)
    w0 = p_grp / (1.0 + e1)
    w1 = p_grp * e1 / (1.0 + e1)

    oh0 = jnp.where(rowf == i0, 1.0, 0.0)
    oh1 = jnp.where(rowf == i1, 1.0, 0.0)
    oh = oh0 + oh1
    r_i = lax.broadcasted_iota(jnp.int32, (tm, tm), 0)
    c_i = lax.broadcasted_iota(jnp.int32, (tm, tm), 1)
    earlier = jnp.where(r_i < c_i, 1.0, 0.0).astype(BF16)
    before = cnt_ref[...] + _dot(oh.astype(BF16), earlier)
    rank0 = jnp.sum(oh0 * before, axis=0, keepdims=True)
    rank1 = jnp.sum(oh1 * before, axis=0, keepdims=True)
    cnt_ref[...] = cnt_ref[...] + jnp.sum(oh, axis=1, keepdims=True)
    row8 = row[:8]
    route = jnp.zeros((8, tm), F32)
    for k, v in enumerate((i0, i1, rank0, rank1, w0, w1)):
        route = jnp.where(row8 == k, v, route)
    route_ref[0] = route


def _merge(x2d, oa, ob, g1, g2, wg, wa, wb, wo, wrh, wrl, br, tm):
    n = x2d.shape[0]
    width = N_HEADS * HEAD_DIM
    row = lambda w: pl.BlockSpec((tm, w), lambda i: (i, 0))
    full = lambda a: pl.BlockSpec(a.shape, lambda i: (0, 0), pipeline_mode=pl.Buffered(1))
    return pl.pallas_call(
        _merge_kernel,
        grid=(n // tm,),
        in_specs=[row(D_MODEL), row(width), row(width), full(g1), full(g2), full(wg), full(wa), full(wb),
                  full(wo), full(wrh), full(wrl), full(br)],
        out_specs=[row(D_MODEL), pl.BlockSpec((tm * ROW_TILE, LANES), lambda i: (i, 0)),
                   pl.BlockSpec((1, 8, tm), lambda i: (i, 0, 0)), pl.BlockSpec((LANES, 1), lambda i: (0, 0))],
        out_shape=[jax.ShapeDtypeStruct((n, D_MODEL), F32), jax.ShapeDtypeStruct((n * ROW_TILE, LANES), F32),
                   jax.ShapeDtypeStruct((n // tm, 8, tm), F32), jax.ShapeDtypeStruct((LANES, 1), F32)],
        compiler_params=pltpu.CompilerParams(dimension_semantics=("arbitrary",),
                                             vmem_limit_bytes=VMEM_LIMIT),
        name="merge",
    )(x2d, oa, ob, g1, g2, wg, wa, wb, wo, wrh, wrl, br)


def _slot_table_kernel(pos_ref, init_ref, tab_ref, sem, *, tt, n):
    step = pl.program_id(0)

    @pl.when(step == 0)
    def _():
        cp = pltpu.make_async_copy(init_ref, tab_ref, sem)
        cp.start()
        cp.wait()

    def chunk(c, row):
        base = pl.multiple_of(c * LANES, LANES)
        for k in range(LANES):
            tab_ref[pos_ref[0, 0, base + k]] = row + k * ROW_TILE
            tab_ref[pos_ref[0, 1, base + k]] = row + (n + k) * ROW_TILE
        return row + LANES * ROW_TILE

    lax.fori_loop(0, tt // LANES, chunk, step * (tt * ROW_TILE))


def _slot_pos_kernel(ps_ref, route_ref, pos_ref):
    route = route_ref[0]
    eid = route[0:2].astype(jnp.int32)
    start = jnp.zeros_like(eid)
    for e in range(N_EXPERTS):
        start = jnp.where(eid == e, ps_ref[e], start)
    pos_ref[0] = start + route[2:4].astype(jnp.int32)


def _slot_pos(pad_start, route):
    nt, _, tt = route.shape
    return pl.pallas_call(
        _slot_pos_kernel,
        grid_spec=pltpu.PrefetchScalarGridSpec(
            num_scalar_prefetch=1, grid=(nt,),
            in_specs=[pl.BlockSpec((1, 8, tt), lambda i, ps: (i, 0, 0))],
            out_specs=pl.BlockSpec((1, 2, tt), lambda i, ps: (i, 0, 0))),
        out_shape=jax.ShapeDtypeStruct((nt, 2, tt), jnp.int32),
        compiler_params=pltpu.CompilerParams(dimension_semantics=("parallel",)),
        name="slot_pos",
    )(pad_start, route)


def _slot_table(pos, init, tt):
    nt = pos.shape[0]
    return pl.pallas_call(
        functools.partial(_slot_table_kernel, tt=tt, n=nt * tt),
        grid=(nt,),
        in_specs=[pl.BlockSpec((1, 2, tt), lambda i: (i, 0, 0), memory_space=pltpu.SMEM),
                  pl.BlockSpec(memory_space=pl.ANY)],
        out_specs=pl.BlockSpec(memory_space=pltpu.SMEM),
        out_shape=jax.ShapeDtypeStruct(init.shape, jnp.int32),
        scratch_shapes=[pltpu.SemaphoreType.DMA(())],
        compiler_params=pltpu.CompilerParams(dimension_semantics=("arbitrary",)),
        name="slot_table",
    )(pos, init)


def _moe_kernel(be_ref, nu_ref, tab_ref, tab_next_ref, tab_ahead_ref, hn_ref, wg_ref, wu_ref, wd_ref, out_ref,
                xin, yout, wg_s, wu_s, wd_s, gsem, ssem, *, tb, n):
    b = pl.program_id(0)
    n_used = nu_ref[0]
    live = b < n_used
    s = lax.rem(b, MOE_SLOTS)

    def gather(tab, slot):
        for j in range(tb):
            t, rows = tab[0, 0, j], n * ROW_TILE
            src = jnp.bitwise_and(t, rows - 1) if rows & (rows - 1) == 0 else lax.rem(t, rows)
            pltpu.make_async_copy(hn_ref.at[_tile_at(src)], xin.at[slot, pl.ds(j * ROW_TILE, ROW_TILE)],
                                  gsem.at[slot]).start(priority=j % 2)

    def wait_rows(sem_ref, slot):
        pltpu.make_async_copy(hn_ref.at[pl.ds(0, tb * ROW_TILE)], xin.at[slot], sem_ref.at[slot]).wait()

    @pl.when(b == 0)
    def _():
        yout[...] = jnp.zeros_like(yout)
        fills = [pltpu.make_async_copy(yout.at[k], out_ref.at[pl.ds((2 * n + k * tb) * ROW_TILE, tb * ROW_TILE)],
                                       ssem.at[k]) for k in range(MOE_SLOTS)]
        for cp in fills:
            cp.start()
        for cp in fills:
            cp.wait()
        gather(tab_ref, 0)
        gather(tab_next_ref, 1)

    @pl.when(live & ((b == 0) | (be_ref[b] != be_ref[jnp.maximum(b - 1, 0)])))
    def _():
        wg_s[...] = wg_ref[0].astype(BF16)
        wu_s[...] = wu_ref[0].astype(BF16)
        wd_s[...] = wd_ref[0].astype(BF16)

    @pl.when(live & (b >= MOE_SLOTS))
    def _():
        wait_rows(ssem, s)

    def block(s):
        wait_rows(gsem, s)
        gather(tab_ahead_ref, (s + 2) % MOE_SLOTS)
        half = tb // 2
        gu = []
        for i in range(2):
            xb = jnp.concatenate([xin[s, pl.ds(i * half * ROW_TILE + j, half, stride=ROW_TILE), :]
                                  for j in range(ROW_TILE)], axis=1).astype(BF16)
            gu.append((_dot(xb, wg_s[...]), _dot(xb, wu_s[...])))
        for i, (g, u) in enumerate(gu):
            y = _dot((g * jax.nn.sigmoid(g) * u).astype(BF16), wd_s[...])
            for j in range(ROW_TILE):
                yout[s, pl.ds(i * half * ROW_TILE + j, half, stride=ROW_TILE), :] = y[:, j * LANES:(j + 1) * LANES]

        for j in range(tb):
            pltpu.make_async_copy(yout.at[s, pl.ds(j * ROW_TILE, ROW_TILE)], out_ref.at[_tile_at(tab_ref[0, 0, j])],
                                  ssem.at[s]).start(priority=j % 2)

    for slot in range(MOE_SLOTS):
        pl.when(live & (s == slot))(functools.partial(block, slot))

    @pl.when(live & (b == n_used - 1))
    def _():
        for k in (1, 2):
            wait_rows(gsem, lax.rem(s + k, MOE_SLOTS))
        wait_rows(ssem, s)
        for k in (1, 2):

            @pl.when(b >= k)
            def _():
                wait_rows(ssem, lax.rem(s + MOE_SLOTS - k, MOE_SLOTS))


def _moe(block_e, n_used, tab, hn, wg, wu, wd, tb, n):
    nblk = tab.shape[0] // tb
    tab2 = tab.reshape(nblk, 1, tb)
    live = lambda b, be, nu: jnp.minimum(b, nu[0] - 1)
    ahead = lambda k: (lambda b, be, nu: (jnp.minimum(b + k, nu[0] - 1), 0, 0))
    wspec = lambda shape: pl.BlockSpec((1,) + shape, lambda b, be, nu: (be[live(b, be, nu)], 0, 0))
    return pl.pallas_call(
        functools.partial(_moe_kernel, tb=tb, n=n),
        grid_spec=pltpu.PrefetchScalarGridSpec(
            num_scalar_prefetch=2, grid=(nblk,),
            in_specs=[pl.BlockSpec((1, 1, tb), ahead(0), memory_space=pltpu.SMEM),
                      pl.BlockSpec((1, 1, tb), ahead(1), memory_space=pltpu.SMEM),
                      pl.BlockSpec((1, 1, tb), ahead(2), memory_space=pltpu.SMEM),
                      pl.BlockSpec(memory_space=pl.ANY),
                      wspec((D_MODEL, EXPERT_FF)), wspec((D_MODEL, EXPERT_FF)), wspec((EXPERT_FF, D_MODEL))],
            out_specs=pl.BlockSpec(memory_space=pl.ANY),
            scratch_shapes=[pltpu.VMEM((MOE_SLOTS, tb * ROW_TILE, LANES), F32),
                            pltpu.VMEM((MOE_SLOTS, tb * ROW_TILE, LANES), F32),
                            pltpu.VMEM((D_MODEL, EXPERT_FF), BF16), pltpu.VMEM((D_MODEL, EXPERT_FF), BF16),
                            pltpu.VMEM((EXPERT_FF, D_MODEL), BF16),
                            pltpu.SemaphoreType.DMA((MOE_SLOTS,)), pltpu.SemaphoreType.DMA((MOE_SLOTS,))]),
        out_shape=jax.ShapeDtypeStruct(((2 * n + MOE_SLOTS * tb) * ROW_TILE, LANES), F32),
        compiler_params=pltpu.CompilerParams(dimension_semantics=("arbitrary",),
                                             vmem_limit_bytes=VMEM_LIMIT),
        name="moe",
    )(block_e, n_used, tab2, tab2, tab2, hn, wg, wu, wd)


def _final_kernel(x2_ref, w_ref, y0_ref, y1_ref, gf_ref, o_ref):
    tc = x2_ref.shape[0]
    w = w_ref[...]
    y = x2_ref[...] + (w[:, 0:1] * _tiles_to_rows(y0_ref, tc) + w[:, 1:2] * _tiles_to_rows(y1_ref, tc))
    o_ref[...] = _rms(y, gf_ref[...])


def _final(x2, w_slot, yslots, gf, tc):
    n = x2.shape[0]
    nt = n // tc
    return pl.pallas_call(
        _final_kernel,
        grid=(nt,),
        in_specs=[pl.BlockSpec((tc, D_MODEL), lambda i: (i, 0)),
                  pl.BlockSpec((tc, 2), lambda i: (i, 0)),
                  pl.BlockSpec((tc * ROW_TILE, LANES), lambda i: (i, 0)),
                  pl.BlockSpec((tc * ROW_TILE, LANES), lambda i: (nt + i, 0)),
                  pl.BlockSpec((1, D_MODEL), lambda i: (0, 0))],
        out_specs=pl.BlockSpec((tc, D_MODEL), lambda i: (i, 0)),
        out_shape=jax.ShapeDtypeStruct((n, D_MODEL), F32),
        compiler_params=pltpu.CompilerParams(dimension_semantics=("parallel",),
                                             vmem_limit_bytes=VMEM_LIMIT),
        name="final",
    )(x2, w_slot, yslots, yslots, gf)


def _tile_at(row):
    return pl.ds(pl.multiple_of(row, ROW_TILE), ROW_TILE)


def _overlap_matrix_t(seq):
    nc = seq // CMP_STRIDE - CMP_LEN // CMP_STRIDE + 1
    ns = seq // SLC_LEN
    c0 = np.arange(nc) * CMP_STRIDE
    s0 = np.arange(ns) * SLC_LEN
    ov = np.clip(np.minimum(c0[:, None] + CMP_LEN, s0[None, :] + SLC_LEN)
                 - np.maximum(c0[:, None], s0[None, :]), 0, None) / CMP_LEN
    out = np.zeros((LANES, LANES), np.float32)
    out[FEAT_SEL:FEAT_SEL + ns, :nc] = ov.T
    return jnp.asarray(out, BF16)


def _position_features(seq):
    pos = np.arange(seq)
    fk = np.zeros((seq, LANES), np.float32)
    fk[:, FEAT_POS:FEAT_POS + 3] = (pos // SLC_LEN)[:, None]
    fk[:, FEAT_POS + 3:FEAT_POS + 6] = (pos % SLC_LEN)[:, None]
    fk[pos, FEAT_SEL + pos // SLC_LEN] = 1.0
    return jnp.asarray(fk)


def _pick_tile(n, pref):
    t = pref
    while n % t:
        t //= 2
    return t


def kernel(x, norm_mix_g, w_in, cmp_pos_k, cmp_w1_k, cmp_w2_k, cmp_pos_v, cmp_w1_v, cmp_w2_v, sinks, w_a, w_b,
           w_o, norm_ffn_g, w_group, b_group, w_expert, b_expert, w_gate_e, w_up_e, w_down_e, norm_final_g):
    bsz, seq, _ = x.shape
    n = bsz * seq
    assert TQ == LANES and seq % KCHUNK == 0 and seq // SLC_LEN <= N_SLC_BLK and seq // CMP_STRIDE <= LANES
    assert TILES_PER_STEP * TQ == KCHUNK and TQ % SUB == 0
    assert seq >= (NSA_WINDOW // TQ + 1) * TQ and w_in.shape[0] == 1
    x2d = x.reshape(n, D_MODEL)

    w = w_in[0]
    scale = HEAD_DIM ** -0.5 * LOG2E
    nsa_w, kvw = N_HEADS * HEAD_DIM, N_KV * HEAD_DIM
    o_qa, o_kva, o_gn = 0, nsa_w, nsa_w + 6 * kvw
    o_qb = o_gn + 3 * N_HEADS
    o_kvb = o_qb + nsa_w
    o_gm = o_kvb + 2 * kvw
    pair = lambda off: w[:, off:off + kvw]
    w_attn = jnp.concatenate(
        [w[:, o_qa:o_qa + nsa_w] * scale, w[:, o_qb:o_qb + nsa_w] * scale,
         pair(o_kva + 2 * kvw), pair(o_kva + 4 * kvw), pair(o_kvb),
         w[:, o_gn:o_gn + 3 * N_HEADS], jnp.zeros((D_MODEL, LANES - 3 * N_HEADS), F32),
         pair(o_kva + 3 * kvw), pair(o_kva + 5 * kvw), pair(o_kvb + kvw),
         w[:, o_kva:o_kva + 2 * kvw]], axis=1).astype(BF16)
    w_gm = w[:, o_gm:o_gm + 2 * D_MODEL].astype(BF16)

    tm = _pick_tile(seq, 512)
    feat_k = _position_features(seq)
    eye = jnp.eye(LANES, dtype=BF16)
    swap = jnp.roll(eye, HEAD_DIM, axis=1)
    proj, v_t, cmp_in = _proj(x2d, norm_mix_g[0][None], w_attn, feat_k, eye, swap, tm)
    proj3 = proj.reshape(bsz, seq, PROJ_W)

    nch = seq // CMP_STRIDE
    pos = jnp.stack([cmp_pos_k[0], cmp_pos_v[0]])
    pos = jnp.broadcast_to(pos[:, :, None, :], (2, CMP_LEN, N_KV, HEAD_DIM))
    pos_a = pos[:, :CMP_STRIDE].reshape(2, 1, CMP_STRIDE * kvw)
    pos_b = pos[:, CMP_STRIDE:].reshape(2, 1, CMP_STRIDE * kvw)
    w1 = jnp.stack([cmp_w1_k[0], cmp_w1_v[0]]).reshape(2, CMP_LEN, HEAD_DIM, CMP_HIDDEN)
    zero = jnp.zeros_like(w1)
    w1 = jnp.stack([jnp.concatenate([w1, zero], axis=-1), jnp.concatenate([zero, w1], axis=-1)], axis=2)
    w1 = w1.reshape(2, CMP_LEN * kvw, N_KV * CMP_HIDDEN).astype(BF16)
    w2 = jnp.pad(jnp.stack([cmp_w2_k[0], cmp_w2_v[0]]), ((0, 0), (0, 0), (0, LANES - HEAD_DIM))).astype(BF16)
    kvc, kvc_t = _compress(cmp_in.reshape(bsz, seq, CMP_W), pos_a, pos_b, w1[:, :CMP_STRIDE * kvw],
                           w1[:, CMP_STRIDE * kvw:], w2, jnp.swapaxes(w2, 1, 2))
    kvc = jnp.pad(kvc, ((0, 0), (0, 0), (0, 0), (0, LANES - nch), (0, 0)))
    kvc_t = jnp.pad(kvc_t, ((0, 0), (0, 0), (0, 0), (0, 0), (0, LANES - nch)))

    o_a, o_b = _attention(proj3, v_t, kvc, kvc_t, _overlap_matrix_t(seq), eye, sinks[0])

    w_r = jnp.concatenate([w_expert[0], w_group[0],
                           jnp.zeros((D_MODEL, LANES - N_EXPERTS - N_GROUPS), F32)], axis=1)
    w_r = w_r.T
    w_rh = w_r.astype(BF16)
    w_rl = (w_r - w_rh.astype(F32)).astype(BF16)
    b_r = jnp.concatenate([b_expert[0], b_group[0], jnp.zeros((LANES - N_EXPERTS - N_GROUPS,), F32)])
    b_r = b_r[:, None] * jnp.ones((1, LANES), F32)
    tt = _pick_tile(n, 1024)
    x2, hn, route, counts = _merge(
        x2d, o_a.reshape(n, nsa_w), o_b.reshape(n, nsa_w), norm_mix_g[0][None], norm_ffn_g[0][None], w_gm,
        w_a[0].astype(BF16), w_b[0].astype(BF16), w_o[0].astype(BF16), w_rh, w_rl, b_r, tt)

    tb = 256
    nblk = -(-(2 * n + N_EXPERTS * (tb - 1)) // tb)
    cnt = counts[:N_EXPERTS, 0].astype(jnp.int32)
    padded = (cnt + tb - 1) // tb * tb
    pad_end = jnp.cumsum(padded)
    pad_start = pad_end - padded
    block_e = jnp.minimum(jnp.sum(pad_end[None, :] <= (jnp.arange(nblk) * tb)[:, None], axis=1), N_EXPERTS - 1)
    n_used = (pad_end[-1:] // tb).astype(jnp.int32)
    pos = _slot_pos(pad_start, route)
    w_slot = jnp.swapaxes(route[:, 4:6, :], 1, 2).reshape(n, 2)
    spare = (2 * n + jnp.arange(nblk * tb, dtype=jnp.int32) % (MOE_SLOTS * tb)) * ROW_TILE
    slot_tab = _slot_table(pos, spare, tt)
    y_slots = _moe(block_e.astype(jnp.int32), n_used, slot_tab, hn, w_gate_e[0], w_up_e[0], w_down_e[0], tb, n)
    out = _final(x2, w_slot, y_slots, norm_final_g[None], tt)
    return out.reshape(bsz, seq, D_MODEL)
```

```python
import functools

import numpy as np
import jax
import jax.numpy as jnp
from jax import lax
from jax.experimental import pallas as pl
from jax.experimental.pallas import tpu as pltpu

F32 = jnp.float32
BF16 = jnp.bfloat16

D_MODEL = 1024
HEAD_DIM = 64
N_HEADS = 8
N_KV = 2
GROUP = N_HEADS // N_KV
CMP_LEN = 32
CMP_STRIDE = 16
CMP_HIDDEN = 256
SLC_LEN = 64
SLC_TOPK = 8
NSA_WINDOW = 256
SWA_WINDOW = 128
N_GROUPS = 4
EPG = 8
N_EXPERTS = N_GROUPS * EPG
EXPERT_FF = 256
RMS_EPS = 1e-6
NEG_INF = -1e30
FORCE_SCORE = 1e9

LANES = 128
TQ = 128
TILES_PER_STEP = 4
KCHUNK = 512
SUB = 128
N_SLC_BLK = LANES // 4
FEAT_POS = HEAD_DIM
FEAT_SEL = HEAD_DIM + 6
ROWS_LO, ROWS_HI = 64, 104
N_KV_SECT = 6
COL_QB = 512
COL_KV = 1024
COL_GN = COL_KV + N_KV_SECT * LANES
PROJ_W = COL_GN + LANES
CMP_W = 2 * LANES
ROW_TILE = D_MODEL // LANES
MOE_SLOTS = 3
VMEM_LIMIT = 56 * 1024 * 1024


LOG2E = float(np.log2(np.e))


def _alibi_slopes():
    n = 2 * N_HEADS
    s = 2.0 ** (-8.0 * np.arange(1, n + 1) / n) * LOG2E
    return [float(v) for v in s[:N_HEADS]], [float(v) for v in s[N_HEADS:]]


SLOPES_SWA, SLOPES_NSA = _alibi_slopes()


def _bf16_pieces(v):
    out, rem = [], np.float32(v)
    for _ in range(3):
        p = np.float32(np.asarray(rem, np.float32).astype(BF16).astype(np.float32))
        out.append(float(p))
        rem = np.float32(rem - p)
    return out


def _rms(x, g):
    return x * lax.rsqrt(jnp.mean(x * x, axis=-1, keepdims=True) + RMS_EPS) * g


def _dot(a, b):
    return jnp.dot(a, b, preferred_element_type=F32)


def _tiles_to_rows(ref, n, lead=()):
    return jnp.concatenate([ref[lead + (pl.ds(j, n, stride=ROW_TILE), slice(None))] for j in range(ROW_TILE)], axis=1)


def _rows_to_tiles(ref, val):
    n = val.shape[0]
    for j in range(ROW_TILE):
        ref[pl.ds(j, n, stride=ROW_TILE), :] = val[:, j * LANES:(j + 1) * LANES]


def _dot_nt(a, b):
    return lax.dot_general(a, b, (((1,), (1,)), ((), ())), preferred_element_type=F32)


def _weight_prep_kernel(w_ref, wa_ref, wg_ref, *, pieces, gm):
    w = w_ref[...]
    cols = [jnp.zeros((w.shape[0], width), F32) if off is None else w[:, off:off + width] * f
            for off, width, f in pieces]
    wa_ref[...] = jnp.concatenate(cols, axis=1).astype(wa_ref.dtype)
    wg_ref[...] = w[:, gm[0]:gm[0] + gm[1]].astype(wg_ref.dtype)


def _weight_prep(w, pieces, gm, rows=128):
    d, wide = w.shape
    wa = sum(width for _, width, _ in pieces)
    return pl.pallas_call(
        functools.partial(_weight_prep_kernel, pieces=pieces, gm=gm),
        grid=(d // rows,),
        in_specs=[pl.BlockSpec((rows, wide), lambda i: (i, 0))],
        out_specs=[pl.BlockSpec((rows, wa), lambda i: (i, 0)), pl.BlockSpec((rows, gm[1]), lambda i: (i, 0))],
        out_shape=[jax.ShapeDtypeStruct((d, wa), BF16), jax.ShapeDtypeStruct((d, gm[1]), BF16)],
        compiler_params=pltpu.CompilerParams(dimension_semantics=("parallel",), vmem_limit_bytes=VMEM_LIMIT),
        name="weight_prep",
    )(w)


def _proj_kernel(x_ref, g_ref, w_ref, fk_ref, eye_ref, swap_ref, o_ref, vt_ref, cmp_ref):
    tm = x_ref.shape[0]
    n_br = N_KV_SECT // N_KV
    col_gn = COL_KV + n_br * LANES
    col_v = col_gn + LANES
    h = _rms(x_ref[...], g_ref[...]).astype(BF16)
    res = _dot(h, w_ref[...])
    o_ref[:, :COL_KV] = res[:, :COL_KV].astype(o_ref.dtype)
    o_ref[:, COL_GN:] = res[:, col_gn:col_v].astype(o_ref.dtype)
    low = lax.broadcasted_iota(jnp.int32, (1, LANES), 1) < HEAD_DIM
    ones_row = jnp.where(lax.broadcasted_iota(jnp.int32, (LANES - HEAD_DIM, tm), 0) == 0, 1.0, 0.0)
    for br in range(n_br):
        k_pair = res[:, COL_KV + br * LANES:COL_KV + (br + 1) * LANES]
        heads = (k_pair, _dot(k_pair.astype(BF16), swap_ref[...]))
        for hd in range(N_KV):
            c0 = COL_KV + (br * N_KV + hd) * LANES
            o_ref[:, c0:c0 + LANES] = jnp.where(low, heads[hd], fk_ref[...]).astype(o_ref.dtype)
        v_pair_t = _dot_nt(eye_ref[...], res[:, col_v + br * LANES:col_v + (br + 1) * LANES].astype(BF16))
        for hd in range(N_KV):
            v_t = jnp.concatenate([v_pair_t[hd * HEAD_DIM:(hd + 1) * HEAD_DIM], ones_row], axis=0)
            for k in range(tm // LANES):
                vt_ref[br * N_KV + hd, 0, k] = v_t[:, k * LANES:(k + 1) * LANES].astype(vt_ref.dtype)
    cmp_ref[...] = res[:, col_v + n_br * LANES:]


def _proj(x2d, g, w, feat_k, eye, swap, tm):
    n = x2d.shape[0]
    seq = feat_k.shape[0]
    nper = seq // tm
    kb = tm // LANES
    return pl.pallas_call(
        _proj_kernel,
        grid=(n // tm,),
        in_specs=[pl.BlockSpec((tm, D_MODEL), lambda i: (i, 0)),
                  pl.BlockSpec((1, D_MODEL), lambda i: (0, 0)),
                  pl.BlockSpec(w.shape, lambda i: (0, 0)),
                  pl.BlockSpec((tm, LANES), lambda i: (i % nper, 0)),
                  pl.BlockSpec((LANES, LANES), lambda i: (0, 0)),
                  pl.BlockSpec((LANES, LANES), lambda i: (0, 0))],
        out_specs=[pl.BlockSpec((tm, PROJ_W), lambda i: (i, 0)),
                   pl.BlockSpec((N_KV_SECT, 1, kb, LANES, LANES), lambda i: (0, i // nper, i % nper, 0, 0)),
                   pl.BlockSpec((tm, CMP_W), lambda i: (i, 0))],
        out_shape=[jax.ShapeDtypeStruct((n, PROJ_W), BF16),
                   jax.ShapeDtypeStruct((N_KV_SECT, n // seq, seq // LANES, LANES, LANES), BF16),
                   jax.ShapeDtypeStruct((n, CMP_W), F32)],
        compiler_params=pltpu.CompilerParams(dimension_semantics=("parallel",),
                                             vmem_limit_bytes=VMEM_LIMIT),
        name="proj",
    )(x2d, g, w, feat_k, eye, swap)


def _compress_kernel(x_ref, pa_ref, pb_ref, w1a_ref, w1b_ref, w2_ref, w2t_ref, o_ref, ot_ref, *, nch):
    r = jnp.concatenate([x_ref[0, pl.ds(j, nch, stride=CMP_STRIDE), :] for j in range(CMP_STRIDE)], axis=1)
    a = _dot((r + pa_ref[0]).astype(BF16), w1a_ref[0])
    b = _dot((r + pb_ref[0]).astype(BF16), w1b_ref[0])
    hid = a + pltpu.roll(b, nch - 1, 0)
    hid = hid * jax.nn.sigmoid(hid)
    for h in range(N_KV):
        hid_h = hid[:, h * CMP_HIDDEN:(h + 1) * CMP_HIDDEN].astype(BF16)
        o_ref[0, 0, h] = _dot(hid_h, w2_ref[0]).astype(o_ref.dtype)
        ot_ref[0, 0, h] = _dot_nt(w2t_ref[0], hid_h).astype(ot_ref.dtype)


def _compress(cmp3, pos_a, pos_b, w1a, w1b, w2, w2t):
    bsz, seq, _ = cmp3.shape
    nch = seq // CMP_STRIDE
    wspec = lambda a: pl.BlockSpec((1,) + a.shape[1:], lambda b, j: (j, 0, 0))
    return pl.pallas_call(
        functools.partial(_compress_kernel, nch=nch),
        grid=(bsz, 2),
        in_specs=[pl.BlockSpec((1, seq, LANES), lambda b, j: (b, 0, j)),
                  wspec(pos_a), wspec(pos_b), wspec(w1a), wspec(w1b), wspec(w2), wspec(w2t)],
        out_specs=[pl.BlockSpec((1, 1, N_KV, nch, LANES), lambda b, j: (b, j, 0, 0, 0)),
                   pl.BlockSpec((1, 1, N_KV, LANES, nch), lambda b, j: (b, j, 0, 0, 0))],
        out_shape=[jax.ShapeDtypeStruct((bsz, 2, N_KV, nch, LANES), BF16),
                   jax.ShapeDtypeStruct((bsz, 2, N_KV, LANES, nch), BF16)],
        compiler_params=pltpu.CompilerParams(dimension_semantics=("parallel", "parallel"),
                                             vmem_limit_bytes=VMEM_LIMIT),
        name="compress",
    )(cmp3, pos_a, pos_b, w1a, w1b, w2, w2t)


def _attn_t_kernel(*refs, n_cmp):
    tiles = [_attn_tile(u, *refs, n_cmp=n_cmp) for u in range(TILES_PER_STEP)]
    while all([next(t) is PHASED for t in tiles]):
        pass
    for t in tiles:
        for _ in t:
            pass


PHASED, SWEEP = "phase done", "ready for the sweep"


def _attn_tile(u, sinks_ref, qa_ref, qb_ref, kc_ref, vct_ref, *rest, n_cmp):
    rows = slice(u * TQ, (u + 1) * TQ)
    n_br = 3
    ks = rest[:n_br * N_KV]
    vts = rest[n_br * N_KV:2 * n_br * N_KV]
    gn_ref, ovt_ref, eye_ref, oa_ref, ob_ref = rest[2 * n_br * N_KV:]
    ksect = lambda branch, h: ks[branch * N_KV + h]
    vsect = lambda branch, h: vts[branch * N_KV + h]
    qi = pl.program_id(1) * TILES_PER_STEP + u
    q0 = pl.multiple_of(qi * TQ, TQ)
    lane = lax.broadcasted_iota(jnp.int32, (1, TQ), 1)
    sub = lax.broadcasted_iota(jnp.int32, (LANES, 1), 0)
    t_row = q0 + lane
    eye = eye_ref[...]
    gates = jax.nn.sigmoid(_dot_nt(eye, gn_ref[0, rows, :]))
    gate = lambda hh, c: gates[3 * hh + c:3 * hh + c + 1, :]
    sub40 = sub[ROWS_LO:ROWS_HI]
    blk = sub40 - FEAT_SEL
    in_rng = (blk >= 0) & (blk < N_SLC_BLK)
    is_pos = (sub40 >= FEAT_POS) & (sub40 < FEAT_SEL)
    zeros_lo = jnp.zeros((LANES - ROWS_HI, TQ), F32)

    def q_t(ref, hh):
        both = _dot_nt(eye, ref[0, rows, (hh // 2) * LANES:(hh // 2 + 1) * LANES])
        return both[(hh % 2) * HEAD_DIM:(hh % 2 + 1) * HEAD_DIM]

    def slope_col(slope):
        hi, mid, lo = _bf16_pieces(slope)
        col = jnp.zeros(sub40.shape, F32)
        for i, v in enumerate([SLC_LEN * hi, SLC_LEN * mid, SLC_LEN * lo, hi, mid, lo]):
            col = jnp.where(sub40 == FEAT_POS + i, v, col)
        return col

    def q_aug_t(q_ts, tails):
        return jnp.concatenate([jnp.concatenate([q, jnp.broadcast_to(t, (ROWS_HI - ROWS_LO, TQ)), zeros_lo], axis=0)
                                for q, t in zip(q_ts, tails)], axis=1).astype(BF16)

    def v_t(ref, first, n):
        return jnp.concatenate([ref[0, 0, first + j] for j in range(n)], axis=1)

    def masked(s, mask):
        return jnp.concatenate([jnp.where(mask, s[:, g * TQ:(g + 1) * TQ], NEG_INF) for g in range(GROUP)], axis=1)

    def store_heads(ref, outs):
        for p in range(N_HEADS // 2):
            pair = jnp.concatenate([outs[2 * p][:HEAD_DIM], outs[2 * p + 1][:HEAD_DIM]], axis=0).astype(BF16)
            ref[0, rows, p * LANES:(p + 1) * LANES] = _dot_nt(eye, pair).astype(ref.dtype)

    nw = NSA_WINDOW // TQ + 1
    w_first = jnp.maximum(qi - (nw - 1), 0)
    w_start = pl.multiple_of(w_first * TQ, TQ)
    nb = (SWA_WINDOW - 1 + TQ - 1) // TQ + 1
    b_first = jnp.maximum(qi - (nb - 1), 0)
    b_start = pl.multiple_of(b_first * TQ, TQ)
    pre = []
    for h in range(N_KV):
        heads = [h * GROUP + g for g in range(GROUP)]
        q_ts = [q_t(qa_ref, hh) for hh in heads]
        scol = [slope_col(SLOPES_NSA[hh]) for hh in heads]
        s_cmp = _dot(kc_ref[0, 0, h], q_aug_t(q_ts, [jnp.zeros((1, 1), F32)] * GROUP))
        s_win = _dot(ksect(1, h)[0, pl.ds(w_start, nw * TQ), :], q_aug_t(q_ts, scol))
        q_b = q_aug_t([q_t(qb_ref, hh) for hh in heads], [slope_col(SLOPES_SWA[hh]) for hh in heads])
        s_swa = _dot(ksect(2, h)[0, pl.ds(b_start, nb * TQ), :], q_b)
        pre.append((q_ts, scol, s_cmp, s_win, s_swa))

    yield PHASED
    def nsa_local(h):
        heads = [h * GROUP + g for g in range(GROUP)]
        _, _, s_all, s_win, _ = pre[h]

        end_c = sub * CMP_STRIDE + (CMP_LEN - 1)
        cmask = (t_row >= end_c) & (sub < n_cmp)
        ps = []
        for g in range(GROUP):
            s = s_all[:, g * TQ:(g + 1) * TQ] + SLOPES_NSA[heads[g]] * end_c.astype(F32)
            s = jnp.where(cmask, s, NEG_INF)
            m = jnp.max(s, axis=0, keepdims=True)
            e = jnp.where(cmask, jnp.exp2(s - m), 0.0)
            z = jnp.sum(e, axis=0, keepdims=True)
            ps.append(e / jnp.where(z > 0, z, 1.0))
        o_cmp = _dot(vct_ref[0, 0, h], jnp.concatenate(ps, axis=1).astype(BF16))

        dist = t_row - (w_start + lax.broadcasted_iota(jnp.int32, (nw * TQ, 1), 0))
        s = masked(s_win, (dist >= 0) & (dist < NSA_WINDOW))
        m = jnp.max(s, axis=0, keepdims=True)
        acc = _dot(v_t(vsect(1, h), w_first, nw), jnp.exp2(s - m).astype(BF16))
        o_win = acc / acc[HEAD_DIM:HEAD_DIM + 1, :]

        psum = ps[0] + ps[1] + ps[2] + ps[3]
        p_hi = psum.astype(BF16)
        p_lo = (psum - p_hi.astype(F32)).astype(BF16)
        imp = (_dot(ovt_ref[...], p_hi) + _dot(ovt_ref[...], p_lo))[ROWS_LO:ROWS_HI]
        part = [gate(hh, 0) * o_cmp[:, g * TQ:(g + 1) * TQ] + gate(hh, 2) * o_win[:, g * TQ:(g + 1) * TQ]
                for g, hh in enumerate(heads)]
        return imp, part

    def nsa_select(h, imp):
        q_ts, scol = pre[h][:2]
        cur = lax.shift_right_logical(t_row, int(np.log2(SLC_LEN)))
        valid = in_rng & (blk * SLC_LEN <= t_row)
        forced = in_rng & ((blk == 0) | (blk == cur) | (blk == cur - 1))
        score = jnp.where(forced, FORCE_SCORE, jnp.where(valid, imp, NEG_INF))
        rank = jnp.zeros(score.shape, F32)
        for i in range(N_SLC_BLK):
            r = FEAT_SEL - ROWS_LO + i
            si = score[r:r + 1, :]
            rank = rank + jnp.where((si > score) | ((si == score) & (blk > i)), 1.0, 0.0)
        sel = in_rng & (rank < SLC_TOPK) & (score > 0.5 * NEG_INF)
        bias_diag = jnp.where(in_rng & jnp.logical_not(sel), NEG_INF, 0.0)
        bias_main = jnp.where(in_rng & jnp.logical_not(sel & (blk < 2 * qi)), NEG_INF, 0.0)

        q_d = q_aug_t(q_ts, [jnp.where(is_pos, scol[g], bias_diag) for g in range(GROUP)])
        q_m = q_aug_t(q_ts, [jnp.where(is_pos, scol[g], bias_main) for g in range(GROUP)])
        return q_m, _dot(ksect(0, h)[0, pl.ds(q0, TQ), :], q_d)

    def nsa_diag(h, q_m, s_diag):
        s = masked(s_diag, q0 + sub <= t_row)
        m0 = jnp.max(s, axis=0, keepdims=True)
        acc0 = _dot(v_t(vsect(0, h), qi, 1), jnp.exp2(s - m0).astype(BF16))
        return q_m, m0, acc0

    local = [nsa_local(h) for h in range(N_KV)]
    yield PHASED

    dist = t_row - (b_start + lax.broadcasted_iota(jnp.int32, (nb * TQ, 1), 0))
    bmask = (dist >= 0) & (dist < SWA_WINDOW)
    outs_b = []
    for h in range(N_KV):
        heads = [h * GROUP + g for g in range(GROUP)]
        s = masked(pre[h][4], bmask)
        sink = jnp.concatenate([sinks_ref[hh] * LOG2E + SLOPES_SWA[hh] * t_row.astype(F32) for hh in heads], axis=1)
        m = jnp.maximum(jnp.max(s, axis=0, keepdims=True), sink)
        acc = _dot(v_t(vsect(2, h), b_first, nb), jnp.exp2(s - m).astype(BF16))
        o_all = acc / (acc[HEAD_DIM:HEAD_DIM + 1, :] + jnp.exp2(sink - m))
        outs_b += [o_all[:, g * TQ:(g + 1) * TQ] for g in range(GROUP)]
    store_heads(ob_ref, outs_b)

    yield PHASED
    selected = [nsa_select(h, local[h][0]) for h in range(N_KV)]
    yield PHASED
    fronts = [nsa_diag(h, *selected[h]) + (local[h][1],) for h in range(N_KV)]
    nblk = KCHUNK // TQ

    def absorb(carry, c, pieces):
        state = list(carry)
        subs = [(j, h) for j in pieces for h in range(N_KV)]
        scores = [_dot(ksect(0, h)[0, pl.ds(pl.multiple_of(c * KCHUNK + j * SUB, SUB), SUB), :], fronts[h][0])
                  for j, h in subs]
        for (j, h), s in zip(subs, scores):
            m_i, acc = state[h]
            m_new = jnp.maximum(m_i, jnp.max(s, axis=0, keepdims=True))
            pv = _dot(v_t(vsect(0, h), c * nblk + j * (SUB // TQ), SUB // TQ), jnp.exp2(s - m_new).astype(BF16))
            state[h] = (m_new, jnp.exp2(m_i - m_new) * acc + pv)
        return tuple(state)

    yield SWEEP
    step_chunk = pl.program_id(1)
    swept = lax.fori_loop(0, step_chunk, lambda c, st: absorb(st, c, range(KCHUNK // SUB)),
                          tuple((f[1], f[2]) for f in fronts))
    if u:
        swept = absorb(swept, step_chunk, range(u * TQ // SUB))
    outs_a = []
    for h in range(N_KV):
        acc = swept[h][1]
        o_slc = acc / acc[HEAD_DIM:HEAD_DIM + 1, :]
        outs_a += [fronts[h][3][g] + gate(h * GROUP + g, 1) * o_slc[:, g * TQ:(g + 1) * TQ] for g in range(GROUP)]
    store_heads(oa_ref, outs_a)


def _attention(proj3, v_t, kc, vc_t, ovt, eye, sinks):
    bsz, seq, _ = proj3.shape
    n_cmp = seq // CMP_STRIDE - CMP_LEN // CMP_STRIDE + 1
    width = N_HEADS * HEAD_DIM
    consts = [ovt, eye]
    tq = TILES_PER_STEP * TQ
    qspec = lambda col: pl.BlockSpec((1, tq, width), lambda b, q, s: (b, q, col))
    cspec = lambda a, j: pl.BlockSpec((1, 1) + a.shape[2:], lambda b, q, s: (b, j, 0, 0, 0))
    in_specs = [qspec(0), qspec(COL_QB // width), cspec(kc, 0), cspec(vc_t, 1)]
    in_specs += [pl.BlockSpec((1, seq, LANES), lambda b, q, s, j=j: (b, 0, COL_KV // LANES + j))
                 for j in range(N_KV_SECT)]
    in_specs += [pl.BlockSpec((1, 1) + v_t.shape[2:], lambda b, q, s, j=j: (j, b, 0, 0, 0)) for j in range(N_KV_SECT)]
    in_specs += [pl.BlockSpec((1, tq, LANES), lambda b, q, s: (b, q, COL_GN // LANES))]
    in_specs += [pl.BlockSpec(c.shape, lambda b, q, s: (0, 0)) for c in consts]
    ospec = pl.BlockSpec((1, tq, width), lambda b, q, s: (b, q, 0))
    return pl.pallas_call(
        functools.partial(_attn_t_kernel, n_cmp=n_cmp),
        grid_spec=pltpu.PrefetchScalarGridSpec(
            num_scalar_prefetch=1, grid=(bsz, seq // tq), in_specs=in_specs, out_specs=[ospec, ospec]),
        out_shape=[jax.ShapeDtypeStruct((bsz, seq, width), BF16)] * 2,
        compiler_params=pltpu.CompilerParams(dimension_semantics=("parallel", "parallel"),
                                             vmem_limit_bytes=VMEM_LIMIT),
        name="attn",
    )(sinks, proj3, proj3, kc, vc_t, *([proj3] * N_KV_SECT), *([v_t] * N_KV_SECT), proj3, *consts)


def _merge_kernel(x_ref, oa_ref, ob_ref, g1_ref, g2_ref, wg_ref, wa_ref, wb_ref, wo_ref,
                  wrh_ref, wrl_ref, br_ref, x2_ref, hn_ref, route_ref, cnt_ref):
    tm = x_ref.shape[0]

    @pl.when(pl.program_id(0) == 0)
    def _():
        cnt_ref[...] = jnp.zeros_like(cnt_ref)

    halves = [slice(i * (tm // 2), (i + 1) * (tm // 2)) for i in range(2)]
    xs = [x_ref[r, :] for r in halves]
    pre = []
    for r, x in zip(halves, xs):
        h = _rms(x, g1_ref[...]).astype(BF16)
        pre.append((_dot(h, wg_ref[...]), _dot(oa_ref[r, :], wa_ref[...]), _dot(ob_ref[r, :], wb_ref[...])))
    mixes = []
    for g_pre, a, b in pre:
        gm = jax.nn.sigmoid(g_pre)
        mixin = gm[:, :D_MODEL] * a + gm[:, D_MODEL:] * b
        mixes.append(_dot(mixin.astype(BF16), wo_ref[...]))
    logit_halves = []
    for r, x, mix in zip(halves, xs, mixes):
        x2 = x + mix
        x2_ref[r, :] = x2
        hn = _rms(x2, g2_ref[...])
        for j in range(ROW_TILE):
            hn_ref[pl.ds(r.start * ROW_TILE + j, tm // 2, stride=ROW_TILE), :] = hn[:, j * LANES:(j + 1) * LANES]
        hn_b = hn.astype(BF16)
        hn_lo = (hn - hn_b.astype(F32)).astype(BF16)
        logit_halves.append(_dot_nt(wrh_ref[...], hn_b) + _dot_nt(wrh_ref[...], hn_lo) + _dot_nt(wrl_ref[...], hn_b))
    bias = br_ref[...]
    logits = jnp.concatenate(logit_halves, axis=1) + jnp.concatenate([bias] * (tm // LANES), axis=1)
    row = lax.broadcasted_iota(jnp.int32, (LANES, 1), 0)
    rowf = row.astype(F32)
    big = float(LANES)
    top = lambda a: jnp.max(a, axis=0, keepdims=True)
    first = lambda hit: jnp.min(jnp.where(hit, rowf, big), axis=0, keepdims=True)
    is_g = (row >= N_EXPERTS) & (row < N_EXPERTS + N_GROUPS)
    gl = jnp.where(is_g, logits, NEG_INF)
    gmax = top(gl)
    grp = first(gl == gmax) - N_EXPERTS
    p_grp = 1.0 / jnp.sum(jnp.where(is_g, jnp.exp(gl - gmax), 0.0), axis=0, keepdims=True)
    in_grp = (rowf >= grp * EPG) & (rowf < grp * EPG + EPG)
    el = jnp.where(in_grp, logits, NEG_INF)
    v0 = top(el)
    i0 = first(el == v0)
    el1 = jnp.where(rowf == i0, NEG_INF, el)
    v1 = top(el1)
    i1 = first(el1 == v1)
    e1 = jnp.exp(v1 - v0)
    w0 = p_grp / (1.0 + e1)
    w1 = p_grp * e1 / (1.0 + e1)

    oh0 = jnp.where(rowf == i0, 1.0, 0.0)
    oh1 = jnp.where(rowf == i1, 1.0, 0.0)
    oh = oh0 + oh1
    r_i = lax.broadcasted_iota(jnp.int32, (tm, tm), 0)
    c_i = lax.broadcasted_iota(jnp.int32, (tm, tm), 1)
    earlier = jnp.where(r_i < c_i, 1.0, 0.0).astype(BF16)
    before = cnt_ref[...] + _dot(oh.astype(BF16), earlier)
    rank0 = jnp.sum(oh0 * before, axis=0, keepdims=True)
    rank1 = jnp.sum(oh1 * before, axis=0, keepdims=True)
    cnt_ref[...] = cnt_ref[...] + jnp.sum(oh, axis=1, keepdims=True)
    row8 = row[:8]
    route = jnp.zeros((8, tm), F32)
    for k, v in enumerate((i0, i1, rank0, rank1, w0, w1)):
        route = jnp.where(row8 == k, v, route)
    route_ref[0] = route


def _merge(x2d, oa, ob, g1, g2, wg, wa, wb, wo, wrh, wrl, br, tm):
    n = x2d.shape[0]
    width = N_HEADS * HEAD_DIM
    row = lambda w: pl.BlockSpec((tm, w), lambda i: (i, 0))
    full = lambda a: pl.BlockSpec(a.shape, lambda i: (0, 0), pipeline_mode=pl.Buffered(1))
    return pl.pallas_call(
        _merge_kernel,
        grid=(n // tm,),
        in_specs=[row(D_MODEL), row(width), row(width), full(g1), full(g2), full(wg), full(wa), full(wb),
                  full(wo), full(wrh), full(wrl), full(br)],
        out_specs=[row(D_MODEL), pl.BlockSpec((tm * ROW_TILE, LANES), lambda i: (i, 0)),
                   pl.BlockSpec((1, 8, tm), lambda i: (i, 0, 0)), pl.BlockSpec((LANES, 1), lambda i: (0, 0))],
        out_shape=[jax.ShapeDtypeStruct((n, D_MODEL), F32), jax.ShapeDtypeStruct((n * ROW_TILE, LANES), F32),
                   jax.ShapeDtypeStruct((n // tm, 8, tm), F32), jax.ShapeDtypeStruct((LANES, 1), F32)],
        compiler_params=pltpu.CompilerParams(dimension_semantics=("arbitrary",),
                                             vmem_limit_bytes=VMEM_LIMIT),
        name="merge",
    )(x2d, oa, ob, g1, g2, wg, wa, wb, wo, wrh, wrl, br)


def _slot_table_kernel(pos_ref, init_ref, tab_ref, sem, *, tt, n):
    step = pl.program_id(0)

    @pl.when(step == 0)
    def _():
        cp = pltpu.make_async_copy(init_ref, tab_ref, sem)
        cp.start()
        cp.wait()

    def chunk(c, row):
        base = pl.multiple_of(c * LANES, LANES)
        for k in range(LANES):
            tab_ref[pos_ref[0, 0, base + k]] = row + k * ROW_TILE
            tab_ref[pos_ref[0, 1, base + k]] = row + (n + k) * ROW_TILE
        return row + LANES * ROW_TILE

    lax.fori_loop(0, tt // LANES, chunk, step * (tt * ROW_TILE))


def _slot_pos_kernel(ps_ref, route_ref, pos_ref):
    route = route_ref[0]
    eid = route[0:2].astype(jnp.int32)
    start = jnp.zeros_like(eid)
    for e in range(N_EXPERTS):
        start = jnp.where(eid == e, ps_ref[e], start)
    pos_ref[0] = start + route[2:4].astype(jnp.int32)


def _slot_pos(pad_start, route):
    nt, _, tt = route.shape
    return pl.pallas_call(
        _slot_pos_kernel,
        grid_spec=pltpu.PrefetchScalarGridSpec(
            num_scalar_prefetch=1, grid=(nt,),
            in_specs=[pl.BlockSpec((1, 8, tt), lambda i, ps: (i, 0, 0))],
            out_specs=pl.BlockSpec((1, 2, tt), lambda i, ps: (i, 0, 0))),
        out_shape=jax.ShapeDtypeStruct((nt, 2, tt), jnp.int32),
        compiler_params=pltpu.CompilerParams(dimension_semantics=("parallel",)),
        name="slot_pos",
    )(pad_start, route)


def _slot_table(pos, init, tt):
    nt = pos.shape[0]
    return pl.pallas_call(
        functools.partial(_slot_table_kernel, tt=tt, n=nt * tt),
        grid=(nt,),
        in_specs=[pl.BlockSpec((1, 2, tt), lambda i: (i, 0, 0), memory_space=pltpu.SMEM),
                  pl.BlockSpec(memory_space=pl.ANY)],
        out_specs=pl.BlockSpec(memory_space=pltpu.SMEM),
        out_shape=jax.ShapeDtypeStruct(init.shape, jnp.int32),
        scratch_shapes=[pltpu.SemaphoreType.DMA(())],
        compiler_params=pltpu.CompilerParams(dimension_semantics=("arbitrary",)),
        name="slot_table",
    )(pos, init)


def _moe_kernel(be_ref, nu_ref, tab_ref, tab_next_ref, tab_ahead_ref, hn_ref, wg_ref, wu_ref, wd_ref, out_ref,
                xin, yout, wg_s, wu_s, wd_s, gsem, ssem, *, tb, n):
    b = pl.program_id(0)
    n_used = nu_ref[0]
    live = b < n_used
    s = lax.rem(b, MOE_SLOTS)

    def gather(tab, slot):
        for j in range(tb):
            t, rows = tab[0, 0, j], n * ROW_TILE
            src = jnp.bitwise_and(t, rows - 1) if rows & (rows - 1) == 0 else lax.rem(t, rows)
            pltpu.make_async_copy(hn_ref.at[_tile_at(src)], xin.at[slot, pl.ds(j * ROW_TILE, ROW_TILE)],
                                  gsem.at[slot]).start(priority=j % 2)

    def wait_rows(sem_ref, slot):
        pltpu.make_async_copy(hn_ref.at[pl.ds(0, tb * ROW_TILE)], xin.at[slot], sem_ref.at[slot]).wait()

    @pl.when(b == 0)
    def _():
        yout[...] = jnp.zeros_like(yout)
        fills = [pltpu.make_async_copy(yout.at[k], out_ref.at[pl.ds((2 * n + k * tb) * ROW_TILE, tb * ROW_TILE)],
                                       ssem.at[k]) for k in range(MOE_SLOTS)]
        for cp in fills:
            cp.start()
        for cp in fills:
            cp.wait()
        gather(tab_ref, 0)
        gather(tab_next_ref, 1)

    @pl.when(live & ((b == 0) | (be_ref[b] != be_ref[jnp.maximum(b - 1, 0)])))
    def _():
        wg_s[...] = wg_ref[0].astype(BF16)
        wu_s[...] = wu_ref[0].astype(BF16)
        wd_s[...] = wd_ref[0].astype(BF16)

    @pl.when(live & (b >= MOE_SLOTS))
    def _():
        wait_rows(ssem, s)

    def block(s):
        wait_rows(gsem, s)
        gather(tab_ahead_ref, (s + 2) % MOE_SLOTS)
        half = tb // 2
        gu = []
        for i in range(2):
            xb = jnp.concatenate([xin[s, pl.ds(i * half * ROW_TILE + j, half, stride=ROW_TILE), :]
                                  for j in range(ROW_TILE)], axis=1).astype(BF16)
            gu.append((_dot(xb, wg_s[...]), _dot(xb, wu_s[...])))
        for i, (g, u) in enumerate(gu):
            y = _dot((g * jax.nn.sigmoid(g) * u).astype(BF16), wd_s[...])
            for j in range(ROW_TILE):
                yout[s, pl.ds(i * half * ROW_TILE + j, half, stride=ROW_TILE), :] = y[:, j * LANES:(j + 1) * LANES]

        for j in range(tb):
            pltpu.make_async_copy(yout.at[s, pl.ds(j * ROW_TILE, ROW_TILE)], out_ref.at[_tile_at(tab_ref[0, 0, j])],
                                  ssem.at[s]).start(priority=j % 2)

    for slot in range(MOE_SLOTS):
        pl.when(live & (s == slot))(functools.partial(block, slot))

    @pl.when(live & (b == n_used - 1))
    def _():
        for k in (1, 2):
            wait_rows(gsem, lax.rem(s + k, MOE_SLOTS))
        wait_rows(ssem, s)
        for k in (1, 2):

            @pl.when(b >= k)
            def _():
                wait_rows(ssem, lax.rem(s + MOE_SLOTS - k, MOE_SLOTS))


def _moe(block_e, n_used, tab, hn, wg, wu, wd, tb, n):
    nblk = tab.shape[0] // tb
    tab2 = tab.reshape(nblk, 1, tb)
    live = lambda b, be, nu: jnp.minimum(b, nu[0] - 1)
    ahead = lambda k: (lambda b, be, nu: (jnp.minimum(b + k, nu[0] - 1), 0, 0))
    wspec = lambda shape: pl.BlockSpec((1,) + shape, lambda b, be, nu: (be[live(b, be, nu)], 0, 0))
    return pl.pallas_call(
        functools.partial(_moe_kernel, tb=tb, n=n),
        grid_spec=pltpu.PrefetchScalarGridSpec(
            num_scalar_prefetch=2, grid=(nblk,),
            in_specs=[pl.BlockSpec((1, 1, tb), ahead(0), memory_space=pltpu.SMEM),
                      pl.BlockSpec((1, 1, tb), ahead(1), memory_space=pltpu.SMEM),
                      pl.BlockSpec((1, 1, tb), ahead(2), memory_space=pltpu.SMEM),
                      pl.BlockSpec(memory_space=pl.ANY),
                      wspec((D_MODEL, EXPERT_FF)), wspec((D_MODEL, EXPERT_FF)), wspec((EXPERT_FF, D_MODEL))],
            out_specs=pl.BlockSpec(memory_space=pl.ANY),
            scratch_shapes=[pltpu.VMEM((MOE_SLOTS, tb * ROW_TILE, LANES), F32),
                            pltpu.VMEM((MOE_SLOTS, tb * ROW_TILE, LANES), F32),
                            pltpu.VMEM((D_MODEL, EXPERT_FF), BF16), pltpu.VMEM((D_MODEL, EXPERT_FF), BF16),
                            pltpu.VMEM((EXPERT_FF, D_MODEL), BF16),
                            pltpu.SemaphoreType.DMA((MOE_SLOTS,)), pltpu.SemaphoreType.DMA((MOE_SLOTS,))]),
        out_shape=jax.ShapeDtypeStruct(((2 * n + MOE_SLOTS * tb) * ROW_TILE, LANES), F32),
        compiler_params=pltpu.CompilerParams(dimension_semantics=("arbitrary",),
                                             vmem_limit_bytes=VMEM_LIMIT),
        name="moe",
    )(block_e, n_used, tab2, tab2, tab2, hn, wg, wu, wd)


def _final_kernel(x2_ref, w_ref, y0_ref, y1_ref, gf_ref, o_ref):
    tc = x2_ref.shape[0]
    w = w_ref[...]
    y = x2_ref[...] + (w[:, 0:1] * _tiles_to_rows(y0_ref, tc) + w[:, 1:2] * _tiles_to_rows(y1_ref, tc))
    o_ref[...] = _rms(y, gf_ref[...])


def _final(x2, w_slot, yslots, gf, tc):
    n = x2.shape[0]
    nt = n // tc
    return pl.pallas_call(
        _final_kernel,
        grid=(nt,),
        in_specs=[pl.BlockSpec((tc, D_MODEL), lambda i: (i, 0)),
                  pl.BlockSpec((tc, 2), lambda i: (i, 0)),
                  pl.BlockSpec((tc * ROW_TILE, LANES), lambda i: (i, 0)),
                  pl.BlockSpec((tc * ROW_TILE, LANES), lambda i: (nt + i, 0)),
                  pl.BlockSpec((1, D_MODEL), lambda i: (0, 0))],
        out_specs=pl.BlockSpec((tc, D_MODEL), lambda i: (i, 0)),
        out_shape=jax.ShapeDtypeStruct((n, D_MODEL), F32),
        compiler_params=pltpu.CompilerParams(dimension_semantics=("parallel",),
                                             vmem_limit_bytes=VMEM_LIMIT),
        name="final",
    )(x2, w_slot, yslots, yslots, gf)


def _tile_at(row):
    return pl.ds(pl.multiple_of(row, ROW_TILE), ROW_TILE)


def _overlap_matrix_t(seq):
    nc = seq // CMP_STRIDE - CMP_LEN // CMP_STRIDE + 1
    ns = seq // SLC_LEN
    c0 = np.arange(nc) * CMP_STRIDE
    s0 = np.arange(ns) * SLC_LEN
    ov = np.clip(np.minimum(c0[:, None] + CMP_LEN, s0[None, :] + SLC_LEN)
                 - np.maximum(c0[:, None], s0[None, :]), 0, None) / CMP_LEN
    out = np.zeros((LANES, LANES), np.float32)
    out[FEAT_SEL:FEAT_SEL + ns, :nc] = ov.T
    return jnp.asarray(out, BF16)


def _position_features(seq):
    pos = np.arange(seq)
    fk = np.zeros((seq, LANES), np.float32)
    fk[:, FEAT_POS:FEAT_POS + 3] = (pos // SLC_LEN)[:, None]
    fk[:, FEAT_POS + 3:FEAT_POS + 6] = (pos % SLC_LEN)[:, None]
    fk[pos, FEAT_SEL + pos // SLC_LEN] = 1.0
    return jnp.asarray(fk)


def _pick_tile(n, pref):
    t = pref
    while n % t:
        t //= 2
    return t


def kernel(x, norm_mix_g, w_in, cmp_pos_k, cmp_w1_k, cmp_w2_k, cmp_pos_v, cmp_w1_v, cmp_w2_v, sinks, w_a, w_b,
           w_o, norm_ffn_g, w_group, b_group, w_expert, b_expert, w_gate_e, w_up_e, w_down_e, norm_final_g):
    bsz, seq, _ = x.shape
    n = bsz * seq
    assert TQ == LANES and seq % KCHUNK == 0 and seq // SLC_LEN <= N_SLC_BLK and seq // CMP_STRIDE <= LANES
    assert TILES_PER_STEP * TQ == KCHUNK and TQ % SUB == 0
    assert seq >= (NSA_WINDOW // TQ + 1) * TQ and w_in.shape[0] == 1
    x2d = x.reshape(n, D_MODEL)

    w = w_in[0]
    scale = HEAD_DIM ** -0.5 * LOG2E
    nsa_w, kvw = N_HEADS * HEAD_DIM, N_KV * HEAD_DIM
    o_qa, o_kva, o_gn = 0, nsa_w, nsa_w + 6 * kvw
    o_qb = o_gn + 3 * N_HEADS
    o_kvb = o_qb + nsa_w
    o_gm = o_kvb + 2 * kvw
    pieces = [(o_qa, nsa_w, scale), (o_qb, nsa_w, scale),
              (o_kva + 2 * kvw, kvw, 1.0), (o_kva + 4 * kvw, kvw, 1.0), (o_kvb, kvw, 1.0),
              (o_gn, 3 * N_HEADS, 1.0), (None, LANES - 3 * N_HEADS, 0.0),
              (o_kva + 3 * kvw, kvw, 1.0), (o_kva + 5 * kvw, kvw, 1.0), (o_kvb + kvw, kvw, 1.0),
              (o_kva, 2 * kvw, 1.0)]
    w_attn, w_gm = _weight_prep(w, pieces, (o_gm, 2 * D_MODEL))

    tm = _pick_tile(seq, 512)
    feat_k = _position_features(seq)
    eye = jnp.eye(LANES, dtype=BF16)
    swap = jnp.roll(eye, HEAD_DIM, axis=1)
    proj, v_t, cmp_in = _proj(x2d, norm_mix_g[0][None], w_attn, feat_k, eye, swap, tm)
    proj3 = proj.reshape(bsz, seq, PROJ_W)

    nch = seq // CMP_STRIDE
    pos = jnp.stack([cmp_pos_k[0], cmp_pos_v[0]])
    pos = jnp.broadcast_to(pos[:, :, None, :], (2, CMP_LEN, N_KV, HEAD_DIM))
    pos_a = pos[:, :CMP_STRIDE].reshape(2, 1, CMP_STRIDE * kvw)
    pos_b = pos[:, CMP_STRIDE:].reshape(2, 1, CMP_STRIDE * kvw)
    w1 = jnp.stack([cmp_w1_k[0], cmp_w1_v[0]]).reshape(2, CMP_LEN, HEAD_DIM, CMP_HIDDEN)
    zero = jnp.zeros_like(w1)
    w1 = jnp.stack([jnp.concatenate([w1, zero], axis=-1), jnp.concatenate([zero, w1], axis=-1)], axis=2)
    w1 = w1.reshape(2, CMP_LEN * kvw, N_KV * CMP_HIDDEN).astype(BF16)
    w2 = jnp.pad(jnp.stack([cmp_w2_k[0], cmp_w2_v[0]]), ((0, 0), (0, 0), (0, LANES - HEAD_DIM))).astype(BF16)
    kvc, kvc_t = _compress(cmp_in.reshape(bsz, seq, CMP_W), pos_a, pos_b, w1[:, :CMP_STRIDE * kvw],
                           w1[:, CMP_STRIDE * kvw:], w2, jnp.swapaxes(w2, 1, 2))
    kvc = jnp.pad(kvc, ((0, 0), (0, 0), (0, 0), (0, LANES - nch), (0, 0)))
    kvc_t = jnp.pad(kvc_t, ((0, 0), (0, 0), (0, 0), (0, 0), (0, LANES - nch)))

    o_a, o_b = _attention(proj3, v_t, kvc, kvc_t, _overlap_matrix_t(seq), eye, sinks[0])

    w_r = jnp.concatenate([w_expert[0], w_group[0],
                           jnp.zeros((D_MODEL, LANES - N_EXPERTS - N_GROUPS), F32)], axis=1)
    w_r = w_r.T
    w_rh = w_r.astype(BF16)
    w_rl = (w_r - w_rh.astype(F32)).astype(BF16)
    b_r = jnp.concatenate([b_expert[0], b_group[0], jnp.zeros((LANES - N_EXPERTS - N_GROUPS,), F32)])
    b_r = b_r[:, None] * jnp.ones((1, LANES), F32)
    tt = _pick_tile(n, 1024)
    x2, hn, route, counts = _merge(
        x2d, o_a.reshape(n, nsa_w), o_b.reshape(n, nsa_w), norm_mix_g[0][None], norm_ffn_g[0][None], w_gm,
        w_a[0].astype(BF16), w_b[0].astype(BF16), w_o[0].astype(BF16), w_rh, w_rl, b_r, tt)

    tb = 256
    nblk = -(-(2 * n + N_EXPERTS * (tb - 1)) // tb)
    cnt = counts[:N_EXPERTS, 0].astype(jnp.int32)
    padded = (cnt + tb - 1) // tb * tb
    pad_end = jnp.cumsum(padded)
    pad_start = pad_end - padded
    block_e = jnp.minimum(jnp.sum(pad_end[None, :] <= (jnp.arange(nblk) * tb)[:, None], axis=1), N_EXPERTS - 1)
    n_used = (pad_end[-1:] // tb).astype(jnp.int32)
    pos = _slot_pos(pad_start, route)
    w_slot = jnp.swapaxes(route[:, 4:6, :], 1, 2).reshape(n, 2)
    spare = (2 * n + jnp.arange(nblk * tb, dtype=jnp.int32) % (MOE_SLOTS * tb)) * ROW_TILE
    slot_tab = _slot_table(pos, spare, tt)
    y_slots = _moe(block_e.astype(jnp.int32), n_used, slot_tab, hn, w_gate_e[0], w_up_e[0], w_down_e[0], tb, n)
    out = _final(x2, w_slot, y_slots, norm_final_g[None], tt)
    return out.reshape(bsz, seq, D_MODEL)
```

```python
import functools

import numpy as np
import jax
import jax.numpy as jnp
from jax import lax
from jax.experimental import pallas as pl
from jax.experimental.pallas import tpu as pltpu

F32 = jnp.float32
BF16 = jnp.bfloat16

D_MODEL = 1024
HEAD_DIM = 64
N_HEADS = 8
N_KV = 2
GROUP = N_HEADS // N_KV
CMP_LEN = 32
CMP_STRIDE = 16
CMP_HIDDEN = 256
SLC_LEN = 64
SLC_TOPK = 8
NSA_WINDOW = 256
SWA_WINDOW = 128
N_GROUPS = 4
EPG = 8
N_EXPERTS = N_GROUPS * EPG
EXPERT_FF = 256
RMS_EPS = 1e-6
NEG_INF = -1e30
FORCE_SCORE = 1e9

LANES = 128
TQ = 128
TILES_PER_STEP = 4
KCHUNK = 512
SUB = 128
N_SLC_BLK = LANES // 4
FEAT_POS = HEAD_DIM
FEAT_SEL = HEAD_DIM + 6
ROWS_LO, ROWS_HI = 64, 104
N_KV_SECT = 6
COL_QB = 512
COL_KV = 1024
COL_GN = COL_KV + N_KV_SECT * LANES
PROJ_W = COL_GN + LANES
CMP_W = 2 * LANES
ROW_TILE = D_MODEL // LANES
MOE_SLOTS = 3
VMEM_LIMIT = 56 * 1024 * 1024


LOG2E = float(np.log2(np.e))


def _alibi_slopes():
    n = 2 * N_HEADS
    s = 2.0 ** (-8.0 * np.arange(1, n + 1) / n) * LOG2E
    return [float(v) for v in s[:N_HEADS]], [float(v) for v in s[N_HEADS:]]


SLOPES_SWA, SLOPES_NSA = _alibi_slopes()


def _bf16_pieces(v):
    out, rem = [], np.float32(v)
    for _ in range(3):
        p = np.float32(np.asarray(rem, np.float32).astype(BF16).astype(np.float32))
        out.append(float(p))
        rem = np.float32(rem - p)
    return out


def _rms(x, g):
    return x * lax.rsqrt(jnp.mean(x * x, axis=-1, keepdims=True) + RMS_EPS) * g


def _dot(a, b):
    return jnp.dot(a, b, preferred_element_type=F32)


def _tiles_to_rows(ref, n, lead=()):
    return jnp.concatenate([ref[lead + (pl.ds(j, n, stride=ROW_TILE), slice(None))] for j in range(ROW_TILE)], axis=1)


def _rows_to_tiles(ref, val):
    n = val.shape[0]
    for j in range(ROW_TILE):
        ref[pl.ds(j, n, stride=ROW_TILE), :] = val[:, j * LANES:(j + 1) * LANES]


def _dot_nt(a, b):
    return lax.dot_general(a, b, (((1,), (1,)), ((), ())), preferred_element_type=F32)


def _weight_prep_kernel(w_ref, wa_ref, wg_ref, *, pieces, gm):
    w = w_ref[0]
    cols = [jnp.zeros((w.shape[0], width), F32) if off is None else w[:, off:off + width] * f
            for off, width, f in pieces]
    wa_ref[...] = jnp.concatenate(cols, axis=1).astype(wa_ref.dtype)
    wg_ref[...] = w[:, gm[0]:gm[0] + gm[1]].astype(wg_ref.dtype)


def _weight_prep(w, pieces, gm, rows=128):
    _, d, wide = w.shape
    wa = sum(width for _, width, _ in pieces)
    return pl.pallas_call(
        functools.partial(_weight_prep_kernel, pieces=pieces, gm=gm),
        grid=(d // rows,),
        in_specs=[pl.BlockSpec((1, rows, wide), lambda i: (0, i, 0))],
        out_specs=[pl.BlockSpec((rows, wa), lambda i: (i, 0)), pl.BlockSpec((rows, gm[1]), lambda i: (i, 0))],
        out_shape=[jax.ShapeDtypeStruct((d, wa), BF16), jax.ShapeDtypeStruct((d, gm[1]), BF16)],
        compiler_params=pltpu.CompilerParams(dimension_semantics=("parallel",), vmem_limit_bytes=VMEM_LIMIT),
        name="weight_prep",
    )(w)


def _proj_kernel(x_ref, g_ref, w_ref, fk_ref, eye_ref, swap_ref, o_ref, vt_ref, cmp_ref):
    tm = x_ref.shape[0]
    n_br = N_KV_SECT // N_KV
    col_gn = COL_KV + n_br * LANES
    col_v = col_gn + LANES
    h = _rms(x_ref[...], g_ref[...]).astype(BF16)
    res = _dot(h, w_ref[...])
    o_ref[:, :COL_KV] = res[:, :COL_KV].astype(o_ref.dtype)
    o_ref[:, COL_GN:] = res[:, col_gn:col_v].astype(o_ref.dtype)
    low = lax.broadcasted_iota(jnp.int32, (1, LANES), 1) < HEAD_DIM
    ones_row = jnp.where(lax.broadcasted_iota(jnp.int32, (LANES - HEAD_DIM, tm), 0) == 0, 1.0, 0.0)
    for br in range(n_br):
        k_pair = res[:, COL_KV + br * LANES:COL_KV + (br + 1) * LANES]
        heads = (k_pair, _dot(k_pair.astype(BF16), swap_ref[...]))
        for hd in range(N_KV):
            c0 = COL_KV + (br * N_KV + hd) * LANES
            o_ref[:, c0:c0 + LANES] = jnp.where(low, heads[hd], fk_ref[...]).astype(o_ref.dtype)
        v_pair_t = _dot_nt(eye_ref[...], res[:, col_v + br * LANES:col_v + (br + 1) * LANES].astype(BF16))
        for hd in range(N_KV):
            v_t = jnp.concatenate([v_pair_t[hd * HEAD_DIM:(hd + 1) * HEAD_DIM], ones_row], axis=0)
            for k in range(tm // LANES):
                vt_ref[br * N_KV + hd, 0, k] = v_t[:, k * LANES:(k + 1) * LANES].astype(vt_ref.dtype)
    cmp_ref[...] = res[:, col_v + n_br * LANES:]


def _proj(x2d, g, w, feat_k, eye, swap, tm):
    n = x2d.shape[0]
    seq = feat_k.shape[0]
    nper = seq // tm
    kb = tm // LANES
    return pl.pallas_call(
        _proj_kernel,
        grid=(n // tm,),
        in_specs=[pl.BlockSpec((tm, D_MODEL), lambda i: (i, 0)),
                  pl.BlockSpec((1, D_MODEL), lambda i: (0, 0)),
                  pl.BlockSpec(w.shape, lambda i: (0, 0)),
                  pl.BlockSpec((tm, LANES), lambda i: (i % nper, 0)),
                  pl.BlockSpec((LANES, LANES), lambda i: (0, 0)),
                  pl.BlockSpec((LANES, LANES), lambda i: (0, 0))],
        out_specs=[pl.BlockSpec((tm, PROJ_W), lambda i: (i, 0)),
                   pl.BlockSpec((N_KV_SECT, 1, kb, LANES, LANES), lambda i: (0, i // nper, i % nper, 0, 0)),
                   pl.BlockSpec((tm, CMP_W), lambda i: (i, 0))],
        out_shape=[jax.ShapeDtypeStruct((n, PROJ_W), BF16),
                   jax.ShapeDtypeStruct((N_KV_SECT, n // seq, seq // LANES, LANES, LANES), BF16),
                   jax.ShapeDtypeStruct((n, CMP_W), F32)],
        compiler_params=pltpu.CompilerParams(dimension_semantics=("parallel",),
                                             vmem_limit_bytes=VMEM_LIMIT),
        name="proj",
    )(x2d, g, w, feat_k, eye, swap)


def _compress_kernel(x_ref, pa_ref, pb_ref, w1a_ref, w1b_ref, w2_ref, w2t_ref, o_ref, ot_ref, *, nch):
    r = jnp.concatenate([x_ref[0, pl.ds(j, nch, stride=CMP_STRIDE), :] for j in range(CMP_STRIDE)], axis=1)
    a = _dot((r + pa_ref[0]).astype(BF16), w1a_ref[0])
    b = _dot((r + pb_ref[0]).astype(BF16), w1b_ref[0])
    hid = a + pltpu.roll(b, nch - 1, 0)
    hid = hid * jax.nn.sigmoid(hid)
    for h in range(N_KV):
        hid_h = hid[:, h * CMP_HIDDEN:(h + 1) * CMP_HIDDEN].astype(BF16)
        o_ref[0, 0, h] = _dot(hid_h, w2_ref[0]).astype(o_ref.dtype)
        ot_ref[0, 0, h] = _dot_nt(w2t_ref[0], hid_h).astype(ot_ref.dtype)


def _compress(cmp3, pos_a, pos_b, w1a, w1b, w2, w2t):
    bsz, seq, _ = cmp3.shape
    nch = seq // CMP_STRIDE
    wspec = lambda a: pl.BlockSpec((1,) + a.shape[1:], lambda b, j: (j, 0, 0))
    return pl.pallas_call(
        functools.partial(_compress_kernel, nch=nch),
        grid=(bsz, 2),
        in_specs=[pl.BlockSpec((1, seq, LANES), lambda b, j: (b, 0, j)),
                  wspec(pos_a), wspec(pos_b), wspec(w1a), wspec(w1b), wspec(w2), wspec(w2t)],
        out_specs=[pl.BlockSpec((1, 1, N_KV, nch, LANES), lambda b, j: (b, j, 0, 0, 0)),
                   pl.BlockSpec((1, 1, N_KV, LANES, nch), lambda b, j: (b, j, 0, 0, 0))],
        out_shape=[jax.ShapeDtypeStruct((bsz, 2, N_KV, nch, LANES), BF16),
                   jax.ShapeDtypeStruct((bsz, 2, N_KV, LANES, nch), BF16)],
        compiler_params=pltpu.CompilerParams(dimension_semantics=("parallel", "parallel"),
                                             vmem_limit_bytes=VMEM_LIMIT),
        name="compress",
    )(cmp3, pos_a, pos_b, w1a, w1b, w2, w2t)


def _attn_t_kernel(*refs, n_cmp):
    tiles = [_attn_tile(u, *refs, n_cmp=n_cmp) for u in range(TILES_PER_STEP)]
    while all([next(t) is PHASED for t in tiles]):
        pass
    for t in tiles:
        for _ in t:
            pass


PHASED, SWEEP = "phase done", "ready for the sweep"


def _attn_tile(u, sinks_ref, qa_ref, qb_ref, kc_ref, vct_ref, *rest, n_cmp):
    rows = slice(u * TQ, (u + 1) * TQ)
    n_br = 3
    ks = rest[:n_br * N_KV]
    vts = rest[n_br * N_KV:2 * n_br * N_KV]
    gn_ref, ovt_ref, eye_ref, oa_ref, ob_ref = rest[2 * n_br * N_KV:]
    ksect = lambda branch, h: ks[branch * N_KV + h]
    vsect = lambda branch, h: vts[branch * N_KV + h]
    qi = pl.program_id(1) * TILES_PER_STEP + u
    q0 = pl.multiple_of(qi * TQ, TQ)
    lane = lax.broadcasted_iota(jnp.int32, (1, TQ), 1)
    sub = lax.broadcasted_iota(jnp.int32, (LANES, 1), 0)
    t_row = q0 + lane
    eye = eye_ref[...]
    gates = jax.nn.sigmoid(_dot_nt(eye, gn_ref[0, rows, :]))
    gate = lambda hh, c: gates[3 * hh + c:3 * hh + c + 1, :]
    sub40 = sub[ROWS_LO:ROWS_HI]
    blk = sub40 - FEAT_SEL
    in_rng = (blk >= 0) & (blk < N_SLC_BLK)
    is_pos = (sub40 >= FEAT_POS) & (sub40 < FEAT_SEL)
    zeros_lo = jnp.zeros((LANES - ROWS_HI, TQ), F32)

    def q_t(ref, hh):
        both = _dot_nt(eye, ref[0, rows, (hh // 2) * LANES:(hh // 2 + 1) * LANES])
        return both[(hh % 2) * HEAD_DIM:(hh % 2 + 1) * HEAD_DIM]

    def slope_col(slope):
        hi, mid, lo = _bf16_pieces(slope)
        col = jnp.zeros(sub40.shape, F32)
        for i, v in enumerate([SLC_LEN * hi, SLC_LEN * mid, SLC_LEN * lo, hi, mid, lo]):
            col = jnp.where(sub40 == FEAT_POS + i, v, col)
        return col

    def q_aug_t(q_ts, tails):
        return jnp.concatenate([jnp.concatenate([q, jnp.broadcast_to(t, (ROWS_HI - ROWS_LO, TQ)), zeros_lo], axis=0)
                                for q, t in zip(q_ts, tails)], axis=1).astype(BF16)

    def v_t(ref, first, n):
        return jnp.concatenate([ref[0, 0, first + j] for j in range(n)], axis=1)

    def masked(s, mask):
        return jnp.concatenate([jnp.where(mask, s[:, g * TQ:(g + 1) * TQ], NEG_INF) for g in range(GROUP)], axis=1)

    def store_heads(ref, outs):
        for p in range(N_HEADS // 2):
            pair = jnp.concatenate([outs[2 * p][:HEAD_DIM], outs[2 * p + 1][:HEAD_DIM]], axis=0).astype(BF16)
            ref[0, rows, p * LANES:(p + 1) * LANES] = _dot_nt(eye, pair).astype(ref.dtype)

    nw = NSA_WINDOW // TQ + 1
    w_first = jnp.maximum(qi - (nw - 1), 0)
    w_start = pl.multiple_of(w_first * TQ, TQ)
    nb = (SWA_WINDOW - 1 + TQ - 1) // TQ + 1
    b_first = jnp.maximum(qi - (nb - 1), 0)
    b_start = pl.multiple_of(b_first * TQ, TQ)
    pre = []
    for h in range(N_KV):
        heads = [h * GROUP + g for g in range(GROUP)]
        q_ts = [q_t(qa_ref, hh) for hh in heads]
        scol = [slope_col(SLOPES_NSA[hh]) for hh in heads]
        s_cmp = _dot(kc_ref[0, 0, h], q_aug_t(q_ts, [jnp.zeros((1, 1), F32)] * GROUP))
        s_win = _dot(ksect(1, h)[0, pl.ds(w_start, nw * TQ), :], q_aug_t(q_ts, scol))
        q_b = q_aug_t([q_t(qb_ref, hh) for hh in heads], [slope_col(SLOPES_SWA[hh]) for hh in heads])
        s_swa = _dot(ksect(2, h)[0, pl.ds(b_start, nb * TQ), :], q_b)
        pre.append((q_ts, scol, s_cmp, s_win, s_swa))

    yield PHASED
    def nsa_local(h):
        heads = [h * GROUP + g for g in range(GROUP)]
        _, _, s_all, s_win, _ = pre[h]

        end_c = sub * CMP_STRIDE + (CMP_LEN - 1)
        cmask = (t_row >= end_c) & (sub < n_cmp)
        ps = []
        for g in range(GROUP):
            s = s_all[:, g * TQ:(g + 1) * TQ] + SLOPES_NSA[heads[g]] * end_c.astype(F32)
            s = jnp.where(cmask, s, NEG_INF)
            m = jnp.max(s, axis=0, keepdims=True)
            e = jnp.where(cmask, jnp.exp2(s - m), 0.0)
            z = jnp.sum(e, axis=0, keepdims=True)
            ps.append(e / jnp.where(z > 0, z, 1.0))
        o_cmp = _dot(vct_ref[0, 0, h], jnp.concatenate(ps, axis=1).astype(BF16))

        dist = t_row - (w_start + lax.broadcasted_iota(jnp.int32, (nw * TQ, 1), 0))
        s = masked(s_win, (dist >= 0) & (dist < NSA_WINDOW))
        m = jnp.max(s, axis=0, keepdims=True)
        acc = _dot(v_t(vsect(1, h), w_first, nw), jnp.exp2(s - m).astype(BF16))
        o_win = acc / acc[HEAD_DIM:HEAD_DIM + 1, :]

        psum = ps[0] + ps[1] + ps[2] + ps[3]
        p_hi = psum.astype(BF16)
        p_lo = (psum - p_hi.astype(F32)).astype(BF16)
        imp = (_dot(ovt_ref[...], p_hi) + _dot(ovt_ref[...], p_lo))[ROWS_LO:ROWS_HI]
        part = [gate(hh, 0) * o_cmp[:, g * TQ:(g + 1) * TQ] + gate(hh, 2) * o_win[:, g * TQ:(g + 1) * TQ]
                for g, hh in enumerate(heads)]
        return imp, part

    def nsa_select(h, imp):
        q_ts, scol = pre[h][:2]
        cur = lax.shift_right_logical(t_row, int(np.log2(SLC_LEN)))
        valid = in_rng & (blk * SLC_LEN <= t_row)
        forced = in_rng & ((blk == 0) | (blk == cur) | (blk == cur - 1))
        score = jnp.where(forced, FORCE_SCORE, jnp.where(valid, imp, NEG_INF))
        rank = jnp.zeros(score.shape, F32)
        for i in range(N_SLC_BLK):
            r = FEAT_SEL - ROWS_LO + i
            si = score[r:r + 1, :]
            rank = rank + jnp.where((si > score) | ((si == score) & (blk > i)), 1.0, 0.0)
        sel = in_rng & (rank < SLC_TOPK) & (score > 0.5 * NEG_INF)
        bias_diag = jnp.where(in_rng & jnp.logical_not(sel), NEG_INF, 0.0)
        bias_main = jnp.where(in_rng & jnp.logical_not(sel & (blk < 2 * qi)), NEG_INF, 0.0)

        q_d = q_aug_t(q_ts, [jnp.where(is_pos, scol[g], bias_diag) for g in range(GROUP)])
        q_m = q_aug_t(q_ts, [jnp.where(is_pos, scol[g], bias_main) for g in range(GROUP)])
        return q_m, _dot(ksect(0, h)[0, pl.ds(q0, TQ), :], q_d)

    def nsa_diag(h, q_m, s_diag):
        s = masked(s_diag, q0 + sub <= t_row)
        m0 = jnp.max(s, axis=0, keepdims=True)
        acc0 = _dot(v_t(vsect(0, h), qi, 1), jnp.exp2(s - m0).astype(BF16))
        return q_m, m0, acc0

    local = [nsa_local(h) for h in range(N_KV)]
    yield PHASED

    dist = t_row - (b_start + lax.broadcasted_iota(jnp.int32, (nb * TQ, 1), 0))
    bmask = (dist >= 0) & (dist < SWA_WINDOW)
    outs_b = []
    for h in range(N_KV):
        heads = [h * GROUP + g for g in range(GROUP)]
        s = masked(pre[h][4], bmask)
        sink = jnp.concatenate([sinks_ref[hh] * LOG2E + SLOPES_SWA[hh] * t_row.astype(F32) for hh in heads], axis=1)
        m = jnp.maximum(jnp.max(s, axis=0, keepdims=True), sink)
        acc = _dot(v_t(vsect(2, h), b_first, nb), jnp.exp2(s - m).astype(BF16))
        o_all = acc / (acc[HEAD_DIM:HEAD_DIM + 1, :] + jnp.exp2(sink - m))
        outs_b += [o_all[:, g * TQ:(g + 1) * TQ] for g in range(GROUP)]
    store_heads(ob_ref, outs_b)

    yield PHASED
    selected = [nsa_select(h, local[h][0]) for h in range(N_KV)]
    yield PHASED
    fronts = [nsa_diag(h, *selected[h]) + (local[h][1],) for h in range(N_KV)]
    nblk = KCHUNK // TQ

    def absorb(carry, c, pieces):
        state = list(carry)
        subs = [(j, h) for j in pieces for h in range(N_KV)]
        scores = [_dot(ksect(0, h)[0, pl.ds(pl.multiple_of(c * KCHUNK + j * SUB, SUB), SUB), :], fronts[h][0])
                  for j, h in subs]
        for (j, h), s in zip(subs, scores):
            m_i, acc = state[h]
            m_new = jnp.maximum(m_i, jnp.max(s, axis=0, keepdims=True))
            pv = _dot(v_t(vsect(0, h), c * nblk + j * (SUB // TQ), SUB // TQ), jnp.exp2(s - m_new).astype(BF16))
            state[h] = (m_new, jnp.exp2(m_i - m_new) * acc + pv)
        return tuple(state)

    yield SWEEP
    step_chunk = pl.program_id(1)
    swept = lax.fori_loop(0, step_chunk, lambda c, st: absorb(st, c, range(KCHUNK // SUB)),
                          tuple((f[1], f[2]) for f in fronts))
    if u:
        swept = absorb(swept, step_chunk, range(u * TQ // SUB))
    outs_a = []
    for h in range(N_KV):
        acc = swept[h][1]
        o_slc = acc / acc[HEAD_DIM:HEAD_DIM + 1, :]
        outs_a += [fronts[h][3][g] + gate(h * GROUP + g, 1) * o_slc[:, g * TQ:(g + 1) * TQ] for g in range(GROUP)]
    store_heads(oa_ref, outs_a)


def _attention(proj3, v_t, kc, vc_t, ovt, eye, sinks):
    bsz, seq, _ = proj3.shape
    n_cmp = seq // CMP_STRIDE - CMP_LEN // CMP_STRIDE + 1
    width = N_HEADS * HEAD_DIM
    consts = [ovt, eye]
    tq = TILES_PER_STEP * TQ
    qspec = lambda col: pl.BlockSpec((1, tq, width), lambda b, q, s: (b, q, col))
    cspec = lambda a, j: pl.BlockSpec((1, 1) + a.shape[2:], lambda b, q, s: (b, j, 0, 0, 0))
    in_specs = [qspec(0), qspec(COL_QB // width), cspec(kc, 0), cspec(vc_t, 1)]
    in_specs += [pl.BlockSpec((1, seq, LANES), lambda b, q, s, j=j: (b, 0, COL_KV // LANES + j))
                 for j in range(N_KV_SECT)]
    in_specs += [pl.BlockSpec((1, 1) + v_t.shape[2:], lambda b, q, s, j=j: (j, b, 0, 0, 0)) for j in range(N_KV_SECT)]
    in_specs += [pl.BlockSpec((1, tq, LANES), lambda b, q, s: (b, q, COL_GN // LANES))]
    in_specs += [pl.BlockSpec(c.shape, lambda b, q, s: (0, 0)) for c in consts]
    ospec = pl.BlockSpec((1, tq, width), lambda b, q, s: (b, q, 0))
    return pl.pallas_call(
        functools.partial(_attn_t_kernel, n_cmp=n_cmp),
        grid_spec=pltpu.PrefetchScalarGridSpec(
            num_scalar_prefetch=1, grid=(bsz, seq // tq), in_specs=in_specs, out_specs=[ospec, ospec]),
        out_shape=[jax.ShapeDtypeStruct((bsz, seq, width), BF16)] * 2,
        compiler_params=pltpu.CompilerParams(dimension_semantics=("parallel", "parallel"),
                                             vmem_limit_bytes=VMEM_LIMIT),
        name="attn",
    )(sinks, proj3, proj3, kc, vc_t, *([proj3] * N_KV_SECT), *([v_t] * N_KV_SECT), proj3, *consts)


def _merge_kernel(x_ref, oa_ref, ob_ref, g1_ref, g2_ref, wg_ref, wa_ref, wb_ref, wo_ref,
                  wrh_ref, wrl_ref, br_ref, x2_ref, hn_ref, route_ref, cnt_ref):
    tm = x_ref.shape[0]

    @pl.when(pl.program_id(0) == 0)
    def _():
        cnt_ref[...] = jnp.zeros_like(cnt_ref)

    halves = [slice(i * (tm // 2), (i + 1) * (tm // 2)) for i in range(2)]
    xs = [x_ref[r, :] for r in halves]
    pre = []
    for r, x in zip(halves, xs):
        h = _rms(x, g1_ref[...]).astype(BF16)
        pre.append((_dot(h, wg_ref[...]), _dot(oa_ref[r, :], wa_ref[...]), _dot(ob_ref[r, :], wb_ref[...])))
    mixes = []
    for g_pre, a, b in pre:
        gm = jax.nn.sigmoid(g_pre)
        mixin = gm[:, :D_MODEL] * a + gm[:, D_MODEL:] * b
        mixes.append(_dot(mixin.astype(BF16), wo_ref[...]))
    logit_halves = []
    for r, x, mix in zip(halves, xs, mixes):
        x2 = x + mix
        x2_ref[r, :] = x2
        hn = _rms(x2, g2_ref[...])
        for j in range(ROW_TILE):
            hn_ref[pl.ds(r.start * ROW_TILE + j, tm // 2, stride=ROW_TILE), :] = hn[:, j * LANES:(j + 1) * LANES]
        hn_b = hn.astype(BF16)
        hn_lo = (hn - hn_b.astype(F32)).astype(BF16)
        logit_halves.append(_dot_nt(wrh_ref[...], hn_b) + _dot_nt(wrh_ref[...], hn_lo) + _dot_nt(wrl_ref[...], hn_b))
    bias = br_ref[...]
    logits = jnp.concatenate(logit_halves, axis=1) + jnp.concatenate([bias] * (tm // LANES), axis=1)
    row = lax.broadcasted_iota(jnp.int32, (LANES, 1), 0)
    rowf = row.astype(F32)
    big = float(LANES)
    top = lambda a: jnp.max(a, axis=0, keepdims=True)
    first = lambda hit: jnp.min(jnp.where(hit, rowf, big), axis=0, keepdims=True)
    is_g = (row >= N_EXPERTS) & (row < N_EXPERTS + N_GROUPS)
    gl = jnp.where(is_g, logits, NEG_INF)
    gmax = top(gl)
    grp = first(gl == gmax) - N_EXPERTS
    p_grp = 1.0 / jnp.sum(jnp.where(is_g, jnp.exp(gl - gmax), 0.0), axis=0, keepdims=True)
    in_grp = (rowf >= grp * EPG) & (rowf < grp * EPG + EPG)
    el = jnp.where(in_grp, logits, NEG_INF)
    v0 = top(el)
    i0 = first(el == v0)
    el1 = jnp.where(rowf == i0, NEG_INF, el)
    v1 = top(el1)
    i1 = first(el1 == v1)
    e1 = jnp.exp(v1 - v0)
    w0 = p_grp / (1.0 + e1)
    w1 = p_grp * e1 / (1.0 + e1)

    oh0 = jnp.where(rowf == i0, 1.0, 0.0)
    oh1 = jnp.where(rowf == i1, 1.0, 0.0)
    oh = oh0 + oh1
    r_i = lax.broadcasted_iota(jnp.int32, (tm, tm), 0)
    c_i = lax.broadcasted_iota(jnp.int32, (tm, tm), 1)
    earlier = jnp.where(r_i < c_i, 1.0, 0.0).astype(BF16)
    before = cnt_ref[...] + _dot(oh.astype(BF16), earlier)
    rank0 = jnp.sum(oh0 * before, axis=0, keepdims=True)
    rank1 = jnp.sum(oh1 * before, axis=0, keepdims=True)
    cnt_ref[...] = cnt_ref[...] + jnp.sum(oh, axis=1, keepdims=True)
    row8 = row[:8]
    route = jnp.zeros((8, tm), F32)
    for k, v in enumerate((i0, i1, rank0, rank1, w0, w1)):
        route = jnp.where(row8 == k, v, route)
    route_ref[0] = route


def _merge(x2d, oa, ob, g1, g2, wg, wa, wb, wo, wrh, wrl, br, tm):
    n = x2d.shape[0]
    width = N_HEADS * HEAD_DIM
    row = lambda w: pl.BlockSpec((tm, w), lambda i: (i, 0))
    full = lambda a: pl.BlockSpec(a.shape, lambda i: (0, 0), pipeline_mode=pl.Buffered(1))
    return pl.pallas_call(
        _merge_kernel,
        grid=(n // tm,),
        in_specs=[row(D_MODEL), row(width), row(width), full(g1), full(g2), full(wg), full(wa), full(wb),
                  full(wo), full(wrh), full(wrl), full(br)],
        out_specs=[row(D_MODEL), pl.BlockSpec((tm * ROW_TILE, LANES), lambda i: (i, 0)),
                   pl.BlockSpec((1, 8, tm), lambda i: (i, 0, 0)), pl.BlockSpec((LANES, 1), lambda i: (0, 0))],
        out_shape=[jax.ShapeDtypeStruct((n, D_MODEL), F32), jax.ShapeDtypeStruct((n * ROW_TILE, LANES), F32),
                   jax.ShapeDtypeStruct((n // tm, 8, tm), F32), jax.ShapeDtypeStruct((LANES, 1), F32)],
        compiler_params=pltpu.CompilerParams(dimension_semantics=("arbitrary",),
                                             vmem_limit_bytes=VMEM_LIMIT),
        name="merge",
    )(x2d, oa, ob, g1, g2, wg, wa, wb, wo, wrh, wrl, br)


def _slot_table_kernel(pos_ref, init_ref, tab_ref, sem, *, tt, n):
    step = pl.program_id(0)

    @pl.when(step == 0)
    def _():
        cp = pltpu.make_async_copy(init_ref, tab_ref, sem)
        cp.start()
        cp.wait()

    def chunk(c, row):
        base = pl.multiple_of(c * LANES, LANES)
        for k in range(LANES):
            tab_ref[pos_ref[0, 0, base + k]] = row + k * ROW_TILE
            tab_ref[pos_ref[0, 1, base + k]] = row + (n + k) * ROW_TILE
        return row + LANES * ROW_TILE

    lax.fori_loop(0, tt // LANES, chunk, step * (tt * ROW_TILE))


def _slot_pos_kernel(ps_ref, route_ref, pos_ref):
    route = route_ref[0]
    eid = route[0:2].astype(jnp.int32)
    start = jnp.zeros_like(eid)
    for e in range(N_EXPERTS):
        start = jnp.where(eid == e, ps_ref[e], start)
    pos_ref[0] = start + route[2:4].astype(jnp.int32)


def _slot_pos(pad_start, route):
    nt, _, tt = route.shape
    return pl.pallas_call(
        _slot_pos_kernel,
        grid_spec=pltpu.PrefetchScalarGridSpec(
            num_scalar_prefetch=1, grid=(nt,),
            in_specs=[pl.BlockSpec((1, 8, tt), lambda i, ps: (i, 0, 0))],
            out_specs=pl.BlockSpec((1, 2, tt), lambda i, ps: (i, 0, 0))),
        out_shape=jax.ShapeDtypeStruct((nt, 2, tt), jnp.int32),
        compiler_params=pltpu.CompilerParams(dimension_semantics=("parallel",)),
        name="slot_pos",
    )(pad_start, route)


def _slot_table(pos, init, tt):
    nt = pos.shape[0]
    return pl.pallas_call(
        functools.partial(_slot_table_kernel, tt=tt, n=nt * tt),
        grid=(nt,),
        in_specs=[pl.BlockSpec((1, 2, tt), lambda i: (i, 0, 0), memory_space=pltpu.SMEM),
                  pl.BlockSpec(memory_space=pl.ANY)],
        out_specs=pl.BlockSpec(memory_space=pltpu.SMEM),
        out_shape=jax.ShapeDtypeStruct(init.shape, jnp.int32),
        scratch_shapes=[pltpu.SemaphoreType.DMA(())],
        compiler_params=pltpu.CompilerParams(dimension_semantics=("arbitrary",)),
        name="slot_table",
    )(pos, init)


def _moe_kernel(be_ref, nu_ref, tab_ref, tab_next_ref, tab_ahead_ref, hn_ref, wg_ref, wu_ref, wd_ref, out_ref,
                xin, yout, wg_s, wu_s, wd_s, gsem, ssem, *, tb, n):
    b = pl.program_id(0)
    n_used = nu_ref[0]
    live = b < n_used
    s = lax.rem(b, MOE_SLOTS)

    def gather(tab, slot):
        for j in range(tb):
            t, rows = tab[0, 0, j], n * ROW_TILE
            src = jnp.bitwise_and(t, rows - 1) if rows & (rows - 1) == 0 else lax.rem(t, rows)
            pltpu.make_async_copy(hn_ref.at[_tile_at(src)], xin.at[slot, pl.ds(j * ROW_TILE, ROW_TILE)],
                                  gsem.at[slot]).start(priority=j % 2)

    def wait_rows(sem_ref, slot):
        pltpu.make_async_copy(hn_ref.at[pl.ds(0, tb * ROW_TILE)], xin.at[slot], sem_ref.at[slot]).wait()

    @pl.when(b == 0)
    def _():
        yout[...] = jnp.zeros_like(yout)
        fills = [pltpu.make_async_copy(yout.at[k], out_ref.at[pl.ds((2 * n + k * tb) * ROW_TILE, tb * ROW_TILE)],
                                       ssem.at[k]) for k in range(MOE_SLOTS)]
        for cp in fills:
            cp.start()
        for cp in fills:
            cp.wait()
        gather(tab_ref, 0)
        gather(tab_next_ref, 1)

    @pl.when(live & ((b == 0) | (be_ref[b] != be_ref[jnp.maximum(b - 1, 0)])))
    def _():
        wg_s[...] = wg_ref[0].astype(BF16)
        wu_s[...] = wu_ref[0].astype(BF16)
        wd_s[...] = wd_ref[0].astype(BF16)

    @pl.when(live & (b >= MOE_SLOTS))
    def _():
        wait_rows(ssem, s)

    def block(s):
        wait_rows(gsem, s)
        gather(tab_ahead_ref, (s + 2) % MOE_SLOTS)
        half = tb // 2
        gu = []
        for i in range(2):
            xb = jnp.concatenate([xin[s, pl.ds(i * half * ROW_TILE + j, half, stride=ROW_TILE), :]
                                  for j in range(ROW_TILE)], axis=1).astype(BF16)
            gu.append((_dot(xb, wg_s[...]), _dot(xb, wu_s[...])))
        for i, (g, u) in enumerate(gu):
            y = _dot((g * jax.nn.sigmoid(g) * u).astype(BF16), wd_s[...])
            for j in range(ROW_TILE):
                yout[s, pl.ds(i * half * ROW_TILE + j, half, stride=ROW_TILE), :] = y[:, j * LANES:(j + 1) * LANES]

        for j in range(tb):
            pltpu.make_async_copy(yout.at[s, pl.ds(j * ROW_TILE, ROW_TILE)], out_ref.at[_tile_at(tab_ref[0, 0, j])],
                                  ssem.at[s]).start(priority=j % 2)

    for slot in range(MOE_SLOTS):
        pl.when(live & (s == slot))(functools.partial(block, slot))

    @pl.when(live & (b == n_used - 1))
    def _():
        for k in (1, 2):
            wait_rows(gsem, lax.rem(s + k, MOE_SLOTS))
        wait_rows(ssem, s)
        for k in (1, 2):

            @pl.when(b >= k)
            def _():
                wait_rows(ssem, lax.rem(s + MOE_SLOTS - k, MOE_SLOTS))


def _moe(block_e, n_used, tab, hn, wg, wu, wd, tb, n):
    nblk = tab.shape[0] // tb
    tab2 = tab.reshape(nblk, 1, tb)
    live = lambda b, be, nu: jnp.minimum(b, nu[0] - 1)
    ahead = lambda k: (lambda b, be, nu: (jnp.minimum(b + k, nu[0] - 1), 0, 0))
    wspec = lambda shape: pl.BlockSpec((1,) + shape, lambda b, be, nu: (be[live(b, be, nu)], 0, 0))
    return pl.pallas_call(
        functools.partial(_moe_kernel, tb=tb, n=n),
        grid_spec=pltpu.PrefetchScalarGridSpec(
            num_scalar_prefetch=2, grid=(nblk,),
            in_specs=[pl.BlockSpec((1, 1, tb), ahead(0), memory_space=pltpu.SMEM),
                      pl.BlockSpec((1, 1, tb), ahead(1), memory_space=pltpu.SMEM),
                      pl.BlockSpec((1, 1, tb), ahead(2), memory_space=pltpu.SMEM),
                      pl.BlockSpec(memory_space=pl.ANY),
                      wspec((D_MODEL, EXPERT_FF)), wspec((D_MODEL, EXPERT_FF)), wspec((EXPERT_FF, D_MODEL))],
            out_specs=pl.BlockSpec(memory_space=pl.ANY),
            scratch_shapes=[pltpu.VMEM((MOE_SLOTS, tb * ROW_TILE, LANES), F32),
                            pltpu.VMEM((MOE_SLOTS, tb * ROW_TILE, LANES), F32),
                            pltpu.VMEM((D_MODEL, EXPERT_FF), BF16), pltpu.VMEM((D_MODEL, EXPERT_FF), BF16),
                            pltpu.VMEM((EXPERT_FF, D_MODEL), BF16),
                            pltpu.SemaphoreType.DMA((MOE_SLOTS,)), pltpu.SemaphoreType.DMA((MOE_SLOTS,))]),
        out_shape=jax.ShapeDtypeStruct(((2 * n + MOE_SLOTS * tb) * ROW_TILE, LANES), F32),
        compiler_params=pltpu.CompilerParams(dimension_semantics=("arbitrary",),
                                             vmem_limit_bytes=VMEM_LIMIT),
        name="moe",
    )(block_e, n_used, tab2, tab2, tab2, hn, wg, wu, wd)


def _final_kernel(x2_ref, w_ref, y0_ref, y1_ref, gf_ref, o_ref):
    tc = x2_ref.shape[0]
    w = w_ref[...]
    y = x2_ref[...] + (w[:, 0:1] * _tiles_to_rows(y0_ref, tc) + w[:, 1:2] * _tiles_to_rows(y1_ref, tc))
    o_ref[...] = _rms(y, gf_ref[...])


def _final(x2, w_slot, yslots, gf, tc):
    n = x2.shape[0]
    nt = n // tc
    return pl.pallas_call(
        _final_kernel,
        grid=(nt,),
        in_specs=[pl.BlockSpec((tc, D_MODEL), lambda i: (i, 0)),
                  pl.BlockSpec((tc, 2), lambda i: (i, 0)),
                  pl.BlockSpec((tc * ROW_TILE, LANES), lambda i: (i, 0)),
                  pl.BlockSpec((tc * ROW_TILE, LANES), lambda i: (nt + i, 0)),
                  pl.BlockSpec((1, D_MODEL), lambda i: (0, 0))],
        out_specs=pl.BlockSpec((tc, D_MODEL), lambda i: (i, 0)),
        out_shape=jax.ShapeDtypeStruct((n, D_MODEL), F32),
        compiler_params=pltpu.CompilerParams(dimension_semantics=("parallel",),
                                             vmem_limit_bytes=VMEM_LIMIT),
        name="final",
    )(x2, w_slot, yslots, yslots, gf)


def _tile_at(row):
    return pl.ds(pl.multiple_of(row, ROW_TILE), ROW_TILE)


def _overlap_matrix_t(seq):
    nc = seq // CMP_STRIDE - CMP_LEN // CMP_STRIDE + 1
    ns = seq // SLC_LEN
    c0 = np.arange(nc) * CMP_STRIDE
    s0 = np.arange(ns) * SLC_LEN
    ov = np.clip(np.minimum(c0[:, None] + CMP_LEN, s0[None, :] + SLC_LEN)
                 - np.maximum(c0[:, None], s0[None, :]), 0, None) / CMP_LEN
    out = np.zeros((LANES, LANES), np.float32)
    out[FEAT_SEL:FEAT_SEL + ns, :nc] = ov.T
    return jnp.asarray(out, BF16)


def _position_features(seq):
    pos = np.arange(seq)
    fk = np.zeros((seq, LANES), np.float32)
    fk[:, FEAT_POS:FEAT_POS + 3] = (pos // SLC_LEN)[:, None]
    fk[:, FEAT_POS + 3:FEAT_POS + 6] = (pos % SLC_LEN)[:, None]
    fk[pos, FEAT_SEL + pos // SLC_LEN] = 1.0
    return jnp.asarray(fk)


def _pick_tile(n, pref):
    t = pref
    while n % t:
        t //= 2
    return t


def kernel(x, norm_mix_g, w_in, cmp_pos_k, cmp_w1_k, cmp_w2_k, cmp_pos_v, cmp_w1_v, cmp_w2_v, sinks, w_a, w_b,
           w_o, norm_ffn_g, w_group, b_group, w_expert, b_expert, w_gate_e, w_up_e, w_down_e, norm_final_g):
    bsz, seq, _ = x.shape
    n = bsz * seq
    assert TQ == LANES and seq % KCHUNK == 0 and seq // SLC_LEN <= N_SLC_BLK and seq // CMP_STRIDE <= LANES
    assert TILES_PER_STEP * TQ == KCHUNK and TQ % SUB == 0
    assert seq >= (NSA_WINDOW // TQ + 1) * TQ and w_in.shape[0] == 1
    x2d = x.reshape(n, D_MODEL)

    scale = HEAD_DIM ** -0.5 * LOG2E
    nsa_w, kvw = N_HEADS * HEAD_DIM, N_KV * HEAD_DIM
    o_qa, o_kva, o_gn = 0, nsa_w, nsa_w + 6 * kvw
    o_qb = o_gn + 3 * N_HEADS
    o_kvb = o_qb + nsa_w
    o_gm = o_kvb + 2 * kvw
    pieces = [(o_qa, nsa_w, scale), (o_qb, nsa_w, scale),
              (o_kva + 2 * kvw, kvw, 1.0), (o_kva + 4 * kvw, kvw, 1.0), (o_kvb, kvw, 1.0),
              (o_gn, 3 * N_HEADS, 1.0), (None, LANES - 3 * N_HEADS, 0.0),
              (o_kva + 3 * kvw, kvw, 1.0), (o_kva + 5 * kvw, kvw, 1.0), (o_kvb + kvw, kvw, 1.0),
              (o_kva, 2 * kvw, 1.0)]
    w_attn, w_gm = _weight_prep(w_in, pieces, (o_gm, 2 * D_MODEL))

    tm = _pick_tile(seq, 512)
    feat_k = _position_features(seq)
    eye = jnp.eye(LANES, dtype=BF16)
    swap = jnp.roll(eye, HEAD_DIM, axis=1)
    proj, v_t, cmp_in = _proj(x2d, norm_mix_g[0][None], w_attn, feat_k, eye, swap, tm)
    proj3 = proj.reshape(bsz, seq, PROJ_W)

    nch = seq // CMP_STRIDE
    pos = jnp.stack([cmp_pos_k[0], cmp_pos_v[0]])
    pos = jnp.broadcast_to(pos[:, :, None, :], (2, CMP_LEN, N_KV, HEAD_DIM))
    pos_a = pos[:, :CMP_STRIDE].reshape(2, 1, CMP_STRIDE * kvw)
    pos_b = pos[:, CMP_STRIDE:].reshape(2, 1, CMP_STRIDE * kvw)
    w1 = jnp.stack([cmp_w1_k[0], cmp_w1_v[0]]).reshape(2, CMP_LEN, HEAD_DIM, CMP_HIDDEN)
    zero = jnp.zeros_like(w1)
    w1 = jnp.stack([jnp.concatenate([w1, zero], axis=-1), jnp.concatenate([zero, w1], axis=-1)], axis=2)
    w1 = w1.reshape(2, CMP_LEN * kvw, N_KV * CMP_HIDDEN).astype(BF16)
    w2 = jnp.pad(jnp.stack([cmp_w2_k[0], cmp_w2_v[0]]), ((0, 0), (0, 0), (0, LANES - HEAD_DIM))).astype(BF16)
    kvc, kvc_t = _compress(cmp_in.reshape(bsz, seq, CMP_W), pos_a, pos_b, w1[:, :CMP_STRIDE * kvw],
                           w1[:, CMP_STRIDE * kvw:], w2, jnp.swapaxes(w2, 1, 2))
    kvc = jnp.pad(kvc, ((0, 0), (0, 0), (0, 0), (0, LANES - nch), (0, 0)))
    kvc_t = jnp.pad(kvc_t, ((0, 0), (0, 0), (0, 0), (0, 0), (0, LANES - nch)))

    o_a, o_b = _attention(proj3, v_t, kvc, kvc_t, _overlap_matrix_t(seq), eye, sinks[0])

    w_r = jnp.concatenate([w_expert[0], w_group[0],
                           jnp.zeros((D_MODEL, LANES - N_EXPERTS - N_GROUPS), F32)], axis=1)
    w_r = w_r.T
    w_rh = w_r.astype(BF16)
    w_rl = (w_r - w_rh.astype(F32)).astype(BF16)
    b_r = jnp.concatenate([b_expert[0], b_group[0], jnp.zeros((LANES - N_EXPERTS - N_GROUPS,), F32)])
    b_r = b_r[:, None] * jnp.ones((1, LANES), F32)
    tt = _pick_tile(n, 1024)
    x2, hn, route, counts = _merge(
        x2d, o_a.reshape(n, nsa_w), o_b.reshape(n, nsa_w), norm_mix_g[0][None], norm_ffn_g[0][None], w_gm,
        w_a[0].astype(BF16), w_b[0].astype(BF16), w_o[0].astype(BF16), w_rh, w_rl, b_r, tt)

    tb = 256
    nblk = -(-(2 * n + N_EXPERTS * (tb - 1)) // tb)
    cnt = counts[:N_EXPERTS, 0].astype(jnp.int32)
    padded = (cnt + tb - 1) // tb * tb
    pad_end = jnp.cumsum(padded)
    pad_start = pad_end - padded
    block_e = jnp.minimum(jnp.sum(pad_end[None, :] <= (jnp.arange(nblk) * tb)[:, None], axis=1), N_EXPERTS - 1)
    n_used = (pad_end[-1:] // tb).astype(jnp.int32)
    pos = _slot_pos(pad_start, route)
    w_slot = jnp.swapaxes(route[:, 4:6, :], 1, 2).reshape(n, 2)
    spare = (2 * n + jnp.arange(nblk * tb, dtype=jnp.int32) % (MOE_SLOTS * tb)) * ROW_TILE
    slot_tab = _slot_table(pos, spare, tt)
    y_slots = _moe(block_e.astype(jnp.int32), n_used, slot_tab, hn, w_gate_e[0], w_up_e[0], w_down_e[0], tb, n)
    out = _final(x2, w_slot, y_slots, norm_final_g[None], tt)
    return out.reshape(bsz, seq, D_MODEL)
```

```python
import functools

import numpy as np
import jax
import jax.numpy as jnp
from jax import lax
from jax.experimental import pallas as pl
from jax.experimental.pallas import tpu as pltpu

F32 = jnp.float32
BF16 = jnp.bfloat16

D_MODEL = 1024
HEAD_DIM = 64
N_HEADS = 8
N_KV = 2
GROUP = N_HEADS // N_KV
CMP_LEN = 32
CMP_STRIDE = 16
CMP_HIDDEN = 256
SLC_LEN = 64
SLC_TOPK = 8
NSA_WINDOW = 256
SWA_WINDOW = 128
N_GROUPS = 4
EPG = 8
N_EXPERTS = N_GROUPS * EPG
EXPERT_FF = 256
RMS_EPS = 1e-6
NEG_INF = -1e30
FORCE_SCORE = 1e9

LANES = 128
TQ = 128
TILES_PER_STEP = 4
KCHUNK = 512
SUB = 128
N_SLC_BLK = LANES // 4
FEAT_POS = HEAD_DIM
FEAT_SEL = HEAD_DIM + 6
ROWS_LO, ROWS_HI = 64, 104
N_KV_SECT = 6
COL_QB = 512
COL_KV = 1024
COL_GN = COL_KV + N_KV_SECT * LANES
PROJ_W = COL_GN + LANES
CMP_W = 2 * LANES
ROW_TILE = D_MODEL // LANES
MOE_SLOTS = 3
VMEM_LIMIT = 56 * 1024 * 1024


LOG2E = float(np.log2(np.e))


def _alibi_slopes():
    n = 2 * N_HEADS
    s = 2.0 ** (-8.0 * np.arange(1, n + 1) / n) * LOG2E
    return [float(v) for v in s[:N_HEADS]], [float(v) for v in s[N_HEADS:]]


SLOPES_SWA, SLOPES_NSA = _alibi_slopes()


def _bf16_pieces(v):
    out, rem = [], np.float32(v)
    for _ in range(3):
        p = np.float32(np.asarray(rem, np.float32).astype(BF16).astype(np.float32))
        out.append(float(p))
        rem = np.float32(rem - p)
    return out


def _rms(x, g):
    return x * lax.rsqrt(jnp.mean(x * x, axis=-1, keepdims=True) + RMS_EPS) * g


def _dot(a, b):
    return jnp.dot(a, b, preferred_element_type=F32)


def _tiles_to_rows(ref, n, lead=()):
    return jnp.concatenate([ref[lead + (pl.ds(j, n, stride=ROW_TILE), slice(None))] for j in range(ROW_TILE)], axis=1)


def _dot_nt(a, b):
    return lax.dot_general(a, b, (((1,), (1,)), ((), ())), preferred_element_type=F32)


def _weight_prep_kernel(w_ref, wa_ref, wg_ref, *, pieces, gm):
    w = w_ref[0]
    cols = [jnp.zeros((w.shape[0], width), F32) if off is None else w[:, off:off + width] * f
            for off, width, f in pieces]
    wa_ref[...] = jnp.concatenate(cols, axis=1).astype(wa_ref.dtype)
    wg_ref[...] = w[:, gm[0]:gm[0] + gm[1]].astype(wg_ref.dtype)


def _weight_prep(w, pieces, gm, rows=128):
    _, d, wide = w.shape
    wa = sum(width for _, width, _ in pieces)
    return pl.pallas_call(
        functools.partial(_weight_prep_kernel, pieces=pieces, gm=gm),
        grid=(d // rows,),
        in_specs=[pl.BlockSpec((1, rows, wide), lambda i: (0, i, 0))],
        out_specs=[pl.BlockSpec((rows, wa), lambda i: (i, 0)), pl.BlockSpec((rows, gm[1]), lambda i: (i, 0))],
        out_shape=[jax.ShapeDtypeStruct((d, wa), BF16), jax.ShapeDtypeStruct((d, gm[1]), BF16)],
        compiler_params=pltpu.CompilerParams(dimension_semantics=("parallel",), vmem_limit_bytes=VMEM_LIMIT),
        name="weight_prep",
    )(w)


def _proj_kernel(x_ref, g_ref, w_ref, fk_ref, eye_ref, swap_ref, o_ref, vt_ref, cmp_ref):
    tm = x_ref.shape[0]
    n_br = N_KV_SECT // N_KV
    col_gn = COL_KV + n_br * LANES
    col_v = col_gn + LANES
    h = _rms(x_ref[...], g_ref[...]).astype(BF16)
    res = _dot(h, w_ref[...])
    o_ref[:, :COL_KV] = res[:, :COL_KV].astype(o_ref.dtype)
    o_ref[:, COL_GN:] = res[:, col_gn:col_v].astype(o_ref.dtype)
    low = lax.broadcasted_iota(jnp.int32, (1, LANES), 1) < HEAD_DIM
    ones_row = jnp.where(lax.broadcasted_iota(jnp.int32, (LANES - HEAD_DIM, tm), 0) == 0, 1.0, 0.0)
    for br in range(n_br):
        k_pair = res[:, COL_KV + br * LANES:COL_KV + (br + 1) * LANES]
        heads = (k_pair, _dot(k_pair.astype(BF16), swap_ref[...]))
        for hd in range(N_KV):
            c0 = COL_KV + (br * N_KV + hd) * LANES
            o_ref[:, c0:c0 + LANES] = jnp.where(low, heads[hd], fk_ref[...]).astype(o_ref.dtype)
        v_pair_t = _dot_nt(eye_ref[...], res[:, col_v + br * LANES:col_v + (br + 1) * LANES].astype(BF16))
        for hd in range(N_KV):
            v_t = jnp.concatenate([v_pair_t[hd * HEAD_DIM:(hd + 1) * HEAD_DIM], ones_row], axis=0)
            for k in range(tm // LANES):
                vt_ref[br * N_KV + hd, 0, k] = v_t[:, k * LANES:(k + 1) * LANES].astype(vt_ref.dtype)
    cmp_ref[...] = res[:, col_v + n_br * LANES:]


def _proj(x2d, g, w, feat_k, eye, swap, tm):
    n = x2d.shape[0]
    seq = feat_k.shape[0]
    nper = seq // tm
    kb = tm // LANES
    return pl.pallas_call(
        _proj_kernel,
        grid=(n // tm,),
        in_specs=[pl.BlockSpec((tm, D_MODEL), lambda i: (i, 0)),
                  pl.BlockSpec((1, D_MODEL), lambda i: (0, 0)),
                  pl.BlockSpec(w.shape, lambda i: (0, 0)),
                  pl.BlockSpec((tm, LANES), lambda i: (i % nper, 0)),
                  pl.BlockSpec((LANES, LANES), lambda i: (0, 0)),
                  pl.BlockSpec((LANES, LANES), lambda i: (0, 0))],
        out_specs=[pl.BlockSpec((tm, PROJ_W), lambda i: (i, 0)),
                   pl.BlockSpec((N_KV_SECT, 1, kb, LANES, LANES), lambda i: (0, i // nper, i % nper, 0, 0)),
                   pl.BlockSpec((tm, CMP_W), lambda i: (i, 0))],
        out_shape=[jax.ShapeDtypeStruct((n, PROJ_W), BF16),
                   jax.ShapeDtypeStruct((N_KV_SECT, n // seq, seq // LANES, LANES, LANES), BF16),
                   jax.ShapeDtypeStruct((n, CMP_W), F32)],
        compiler_params=pltpu.CompilerParams(dimension_semantics=("parallel",),
                                             vmem_limit_bytes=VMEM_LIMIT),
        name="proj",
    )(x2d, g, w, feat_k, eye, swap)


def _compress_kernel(x_ref, pa_ref, pb_ref, w1a_ref, w1b_ref, w2_ref, w2t_ref, o_ref, ot_ref, *, nch):
    r = jnp.concatenate([x_ref[0, pl.ds(j, nch, stride=CMP_STRIDE), :] for j in range(CMP_STRIDE)], axis=1)
    a = _dot((r + pa_ref[0]).astype(BF16), w1a_ref[0])
    b = _dot((r + pb_ref[0]).astype(BF16), w1b_ref[0])
    hid = a + pltpu.roll(b, nch - 1, 0)
    hid = hid * jax.nn.sigmoid(hid)
    for h in range(N_KV):
        hid_h = hid[:, h * CMP_HIDDEN:(h + 1) * CMP_HIDDEN].astype(BF16)
        o_ref[0, 0, h] = _dot(hid_h, w2_ref[0]).astype(o_ref.dtype)
        ot_ref[0, 0, h] = _dot_nt(w2t_ref[0], hid_h).astype(ot_ref.dtype)


def _compress(cmp3, pos_a, pos_b, w1a, w1b, w2, w2t):
    bsz, seq, _ = cmp3.shape
    nch = seq // CMP_STRIDE
    wspec = lambda a: pl.BlockSpec((1,) + a.shape[1:], lambda b, j: (j, 0, 0))
    return pl.pallas_call(
        functools.partial(_compress_kernel, nch=nch),
        grid=(bsz, 2),
        in_specs=[pl.BlockSpec((1, seq, LANES), lambda b, j: (b, 0, j)),
                  wspec(pos_a), wspec(pos_b), wspec(w1a), wspec(w1b), wspec(w2), wspec(w2t)],
        out_specs=[pl.BlockSpec((1, 1, N_KV, nch, LANES), lambda b, j: (b, j, 0, 0, 0)),
                   pl.BlockSpec((1, 1, N_KV, LANES, nch), lambda b, j: (b, j, 0, 0, 0))],
        out_shape=[jax.ShapeDtypeStruct((bsz, 2, N_KV, nch, LANES), BF16),
                   jax.ShapeDtypeStruct((bsz, 2, N_KV, LANES, nch), BF16)],
        compiler_params=pltpu.CompilerParams(dimension_semantics=("parallel", "parallel"),
                                             vmem_limit_bytes=VMEM_LIMIT),
        name="compress",
    )(cmp3, pos_a, pos_b, w1a, w1b, w2, w2t)


def _attn_t_kernel(*refs, n_cmp):
    tiles = [_attn_tile(u, *refs, n_cmp=n_cmp) for u in range(TILES_PER_STEP)]
    while all([next(t) is PHASED for t in tiles]):
        pass
    for t in tiles:
        for _ in t:
            pass


PHASED, SWEEP = "phase done", "ready for the sweep"


def _attn_tile(u, sinks_ref, qa_ref, qb_ref, kc_ref, vct_ref, *rest, n_cmp):
    rows = slice(u * TQ, (u + 1) * TQ)
    n_br = 3
    ks = rest[:n_br * N_KV]
    vts = rest[n_br * N_KV:2 * n_br * N_KV]
    gn_ref, ovt_ref, eye_ref, oa_ref, ob_ref = rest[2 * n_br * N_KV:]
    ksect = lambda branch, h: ks[branch * N_KV + h]
    vsect = lambda branch, h: vts[branch * N_KV + h]
    qi = pl.program_id(1) * TILES_PER_STEP + u
    q0 = pl.multiple_of(qi * TQ, TQ)
    lane = lax.broadcasted_iota(jnp.int32, (1, TQ), 1)
    sub = lax.broadcasted_iota(jnp.int32, (LANES, 1), 0)
    t_row = q0 + lane
    eye = eye_ref[...]
    gates = jax.nn.sigmoid(_dot_nt(eye, gn_ref[0, rows, :]))
    gate = lambda hh, c: gates[3 * hh + c:3 * hh + c + 1, :]
    sub40 = sub[ROWS_LO:ROWS_HI]
    blk = sub40 - FEAT_SEL
    in_rng = (blk >= 0) & (blk < N_SLC_BLK)
    is_pos = (sub40 >= FEAT_POS) & (sub40 < FEAT_SEL)
    zeros_lo = jnp.zeros((LANES - ROWS_HI, TQ), F32)

    def q_t(ref, hh):
        both = _dot_nt(eye, ref[0, rows, (hh // 2) * LANES:(hh // 2 + 1) * LANES])
        return both[(hh % 2) * HEAD_DIM:(hh % 2 + 1) * HEAD_DIM]

    def slope_col(slope):
        hi, mid, lo = _bf16_pieces(slope)
        col = jnp.zeros(sub40.shape, F32)
        for i, v in enumerate([SLC_LEN * hi, SLC_LEN * mid, SLC_LEN * lo, hi, mid, lo]):
            col = jnp.where(sub40 == FEAT_POS + i, v, col)
        return col

    def q_aug_t(q_ts, tails):
        return jnp.concatenate([jnp.concatenate([q, jnp.broadcast_to(t, (ROWS_HI - ROWS_LO, TQ)), zeros_lo], axis=0)
                                for q, t in zip(q_ts, tails)], axis=1).astype(BF16)

    def v_t(ref, first, n):
        return jnp.concatenate([ref[0, 0, first + j] for j in range(n)], axis=1)

    def masked(s, mask):
        return jnp.concatenate([jnp.where(mask, s[:, g * TQ:(g + 1) * TQ], NEG_INF) for g in range(GROUP)], axis=1)

    def store_heads(ref, outs):
        for p in range(N_HEADS // 2):
            pair = jnp.concatenate([outs[2 * p][:HEAD_DIM], outs[2 * p + 1][:HEAD_DIM]], axis=0).astype(BF16)
            ref[0, rows, p * LANES:(p + 1) * LANES] = _dot_nt(eye, pair).astype(ref.dtype)

    nw = NSA_WINDOW // TQ + 1
    w_first = jnp.maximum(qi - (nw - 1), 0)
    w_start = pl.multiple_of(w_first * TQ, TQ)
    nb = (SWA_WINDOW - 1 + TQ - 1) // TQ + 1
    b_first = jnp.maximum(qi - (nb - 1), 0)
    b_start = pl.multiple_of(b_first * TQ, TQ)
    pre = []
    for h in range(N_KV):
        heads = [h * GROUP + g for g in range(GROUP)]
        q_ts = [q_t(qa_ref, hh) for hh in heads]
        scol = [slope_col(SLOPES_NSA[hh]) for hh in heads]
        s_cmp = _dot(kc_ref[0, 0, h], q_aug_t(q_ts, [jnp.zeros((1, 1), F32)] * GROUP))
        s_win = _dot(ksect(1, h)[0, pl.ds(w_start, nw * TQ), :], q_aug_t(q_ts, scol))
        q_b = q_aug_t([q_t(qb_ref, hh) for hh in heads], [slope_col(SLOPES_SWA[hh]) for hh in heads])
        s_swa = _dot(ksect(2, h)[0, pl.ds(b_start, nb * TQ), :], q_b)
        pre.append((q_ts, scol, s_cmp, s_win, s_swa))

    yield PHASED
    def nsa_local(h):
        heads = [h * GROUP + g for g in range(GROUP)]
        _, _, s_all, s_win, _ = pre[h]

        end_c = sub * CMP_STRIDE + (CMP_LEN - 1)
        cmask = (t_row >= end_c) & (sub < n_cmp)
        ps = []
        for g in range(GROUP):
            s = s_all[:, g * TQ:(g + 1) * TQ] + SLOPES_NSA[heads[g]] * end_c.astype(F32)
            s = jnp.where(cmask, s, NEG_INF)
            m = jnp.max(s, axis=0, keepdims=True)
            e = jnp.where(cmask, jnp.exp2(s - m), 0.0)
            z = jnp.sum(e, axis=0, keepdims=True)
            ps.append(e / jnp.where(z > 0, z, 1.0))
        o_cmp = _dot(vct_ref[0, 0, h], jnp.concatenate(ps, axis=1).astype(BF16))

        dist = t_row - (w_start + lax.broadcasted_iota(jnp.int32, (nw * TQ, 1), 0))
        s = masked(s_win, (dist >= 0) & (dist < NSA_WINDOW))
        m = jnp.max(s, axis=0, keepdims=True)
        acc = _dot(v_t(vsect(1, h), w_first, nw), jnp.exp2(s - m).astype(BF16))
        o_win = acc / acc[HEAD_DIM:HEAD_DIM + 1, :]

        psum = ps[0] + ps[1] + ps[2] + ps[3]
        p_hi = psum.astype(BF16)
        p_lo = (psum - p_hi.astype(F32)).astype(BF16)
        imp = (_dot(ovt_ref[...], p_hi) + _dot(ovt_ref[...], p_lo))[ROWS_LO:ROWS_HI]
        part = [gate(hh, 0) * o_cmp[:, g * TQ:(g + 1) * TQ] + gate(hh, 2) * o_win[:, g * TQ:(g + 1) * TQ]
                for g, hh in enumerate(heads)]
        return imp, part

    def nsa_select(h, imp):
        q_ts, scol = pre[h][:2]
        cur = lax.shift_right_logical(t_row, int(np.log2(SLC_LEN)))
        valid = in_rng & (blk * SLC_LEN <= t_row)
        forced = in_rng & ((blk == 0) | (blk == cur) | (blk == cur - 1))
        score = jnp.where(forced, FORCE_SCORE, jnp.where(valid, imp, NEG_INF))
        rank = jnp.zeros(score.shape, F32)
        for i in range(N_SLC_BLK):
            r = FEAT_SEL - ROWS_LO + i
            si = score[r:r + 1, :]
            rank = rank + jnp.where((si > score) | ((si == score) & (blk > i)), 1.0, 0.0)
        sel = in_rng & (rank < SLC_TOPK) & (score > 0.5 * NEG_INF)
        bias_diag = jnp.where(in_rng & jnp.logical_not(sel), NEG_INF, 0.0)
        bias_main = jnp.where(in_rng & jnp.logical_not(sel & (blk < 2 * qi)), NEG_INF, 0.0)

        q_d = q_aug_t(q_ts, [jnp.where(is_pos, scol[g], bias_diag) for g in range(GROUP)])
        q_m = q_aug_t(q_ts, [jnp.where(is_pos, scol[g], bias_main) for g in range(GROUP)])
        return q_m, _dot(ksect(0, h)[0, pl.ds(q0, TQ), :], q_d)

    def nsa_diag(h, q_m, s_diag):
        s = masked(s_diag, q0 + sub <= t_row)
        m0 = jnp.max(s, axis=0, keepdims=True)
        acc0 = _dot(v_t(vsect(0, h), qi, 1), jnp.exp2(s - m0).astype(BF16))
        return q_m, m0, acc0

    local = [nsa_local(h) for h in range(N_KV)]
    yield PHASED

    dist = t_row - (b_start + lax.broadcasted_iota(jnp.int32, (nb * TQ, 1), 0))
    bmask = (dist >= 0) & (dist < SWA_WINDOW)
    outs_b = []
    for h in range(N_KV):
        heads = [h * GROUP + g for g in range(GROUP)]
        s = masked(pre[h][4], bmask)
        sink = jnp.concatenate([sinks_ref[hh] * LOG2E + SLOPES_SWA[hh] * t_row.astype(F32) for hh in heads], axis=1)
        m = jnp.maximum(jnp.max(s, axis=0, keepdims=True), sink)
        acc = _dot(v_t(vsect(2, h), b_first, nb), jnp.exp2(s - m).astype(BF16))
        o_all = acc / (acc[HEAD_DIM:HEAD_DIM + 1, :] + jnp.exp2(sink - m))
        outs_b += [o_all[:, g * TQ:(g + 1) * TQ] for g in range(GROUP)]
    store_heads(ob_ref, outs_b)

    yield PHASED
    selected = [nsa_select(h, local[h][0]) for h in range(N_KV)]
    yield PHASED
    fronts = [nsa_diag(h, *selected[h]) + (local[h][1],) for h in range(N_KV)]
    nblk = KCHUNK // TQ

    def absorb(carry, c, pieces):
        state = list(carry)
        subs = [(j, h) for j in pieces for h in range(N_KV)]
        scores = [_dot(ksect(0, h)[0, pl.ds(pl.multiple_of(c * KCHUNK + j * SUB, SUB), SUB), :], fronts[h][0])
                  for j, h in subs]
        for (j, h), s in zip(subs, scores):
            m_i, acc = state[h]
            m_new = jnp.maximum(m_i, jnp.max(s, axis=0, keepdims=True))
            pv = _dot(v_t(vsect(0, h), c * nblk + j * (SUB // TQ), SUB // TQ), jnp.exp2(s - m_new).astype(BF16))
            state[h] = (m_new, jnp.exp2(m_i - m_new) * acc + pv)
        return tuple(state)

    yield SWEEP
    step_chunk = pl.program_id(1)
    swept = lax.fori_loop(0, step_chunk, lambda c, st: absorb(st, c, range(KCHUNK // SUB)),
                          tuple((f[1], f[2]) for f in fronts))
    if u:
        swept = absorb(swept, step_chunk, range(u * TQ // SUB))
    outs_a = []
    for h in range(N_KV):
        acc = swept[h][1]
        o_slc = acc / acc[HEAD_DIM:HEAD_DIM + 1, :]
        outs_a += [fronts[h][3][g] + gate(h * GROUP + g, 1) * o_slc[:, g * TQ:(g + 1) * TQ] for g in range(GROUP)]
    store_heads(oa_ref, outs_a)


def _attention(proj3, v_t, kc, vc_t, ovt, eye, sinks):
    bsz, seq, _ = proj3.shape
    n_cmp = seq // CMP_STRIDE - CMP_LEN // CMP_STRIDE + 1
    width = N_HEADS * HEAD_DIM
    consts = [ovt, eye]
    tq = TILES_PER_STEP * TQ
    qspec = lambda col: pl.BlockSpec((1, tq, width), lambda b, q, s: (b, q, col))
    cspec = lambda a, j: pl.BlockSpec((1, 1) + a.shape[2:], lambda b, q, s: (b, j, 0, 0, 0))
    in_specs = [qspec(0), qspec(COL_QB // width), cspec(kc, 0), cspec(vc_t, 1)]
    in_specs += [pl.BlockSpec((1, seq, LANES), lambda b, q, s, j=j: (b, 0, COL_KV // LANES + j))
                 for j in range(N_KV_SECT)]
    in_specs += [pl.BlockSpec((1, 1) + v_t.shape[2:], lambda b, q, s, j=j: (j, b, 0, 0, 0)) for j in range(N_KV_SECT)]
    in_specs += [pl.BlockSpec((1, tq, LANES), lambda b, q, s: (b, q, COL_GN // LANES))]
    in_specs += [pl.BlockSpec(c.shape, lambda b, q, s: (0, 0)) for c in consts]
    ospec = pl.BlockSpec((1, tq, width), lambda b, q, s: (b, q, 0))
    return pl.pallas_call(
        functools.partial(_attn_t_kernel, n_cmp=n_cmp),
        grid_spec=pltpu.PrefetchScalarGridSpec(
            num_scalar_prefetch=1, grid=(bsz, seq // tq), in_specs=in_specs, out_specs=[ospec, ospec]),
        out_shape=[jax.ShapeDtypeStruct((bsz, seq, width), BF16)] * 2,
        compiler_params=pltpu.CompilerParams(dimension_semantics=("parallel", "parallel"),
                                             vmem_limit_bytes=VMEM_LIMIT),
        name="attn",
    )(sinks, proj3, proj3, kc, vc_t, *([proj3] * N_KV_SECT), *([v_t] * N_KV_SECT), proj3, *consts)


def _merge_kernel(x_ref, oa_ref, ob_ref, g1_ref, g2_ref, wg_ref, wa_ref, wb_ref, wo_ref,
                  wrh_ref, wrl_ref, br_ref, earlier_ref, x2_ref, hn_ref, route_ref, cnt_ref):
    tm = x_ref.shape[0]

    @pl.when(pl.program_id(0) == 0)
    def _():
        cnt_ref[...] = jnp.zeros_like(cnt_ref)

    halves = [slice(i * (tm // 2), (i + 1) * (tm // 2)) for i in range(2)]
    xs = [x_ref[r, :] for r in halves]
    pre = []
    for r, x in zip(halves, xs):
        h = _rms(x, g1_ref[...]).astype(BF16)
        pre.append((_dot(h, wg_ref[...]), _dot(oa_ref[r, :], wa_ref[...]), _dot(ob_ref[r, :], wb_ref[...])))
    mixes = []
    for g_pre, a, b in pre:
        gm = jax.nn.sigmoid(g_pre)
        mixin = gm[:, :D_MODEL] * a + gm[:, D_MODEL:] * b
        mixes.append(_dot(mixin.astype(BF16), wo_ref[...]))
    logit_halves = []
    for r, x, mix in zip(halves, xs, mixes):
        x2 = x + mix
        x2_ref[r, :] = x2
        hn = _rms(x2, g2_ref[...])
        for j in range(ROW_TILE):
            hn_ref[pl.ds(r.start * ROW_TILE + j, tm // 2, stride=ROW_TILE), :] = hn[:, j * LANES:(j + 1) * LANES]
        hn_b = hn.astype(BF16)
        hn_lo = (hn - hn_b.astype(F32)).astype(BF16)
        logit_halves.append(_dot_nt(wrh_ref[...], hn_b) + _dot_nt(wrh_ref[...], hn_lo) + _dot_nt(wrl_ref[...], hn_b))
    bias = br_ref[...]
    logits = jnp.concatenate(logit_halves, axis=1) + jnp.concatenate([bias] * (tm // LANES), axis=1)
    row = lax.broadcasted_iota(jnp.int32, (LANES, 1), 0)
    rowf = row.astype(F32)
    big = float(LANES)
    top = lambda a: jnp.max(a, axis=0, keepdims=True)
    first = lambda hit: jnp.min(jnp.where(hit, rowf, big), axis=0, keepdims=True)
    is_g = (row >= N_EXPERTS) & (row < N_EXPERTS + N_GROUPS)
    gl = jnp.where(is_g, logits, NEG_INF)
    gmax = top(gl)
    grp = first(gl == gmax) - N_EXPERTS
    p_grp = 1.0 / jnp.sum(jnp.where(is_g, jnp.exp(gl - gmax), 0.0), axis=0, keepdims=True)
    in_grp = (rowf >= grp * EPG) & (rowf < grp * EPG + EPG)
    el = jnp.where(in_grp, logits, NEG_INF)
    v0 = top(el)
    i0 = first(el == v0)
    el1 = jnp.where(rowf == i0, NEG_INF, el)
    v1 = top(el1)
    i1 = first(el1 == v1)
    e1 = jnp.exp(v1 - v0)
    w0 = p_grp / (1.0 + e1)
    w1 = p_grp * e1 / (1.0 + e1)

    oh0 = jnp.where(rowf == i0, 1.0, 0.0)
    oh1 = jnp.where(rowf == i1, 1.0, 0.0)
    oh = oh0 + oh1
    before = cnt_ref[...] + _dot(oh.astype(BF16), earlier_ref[...])
    rank0 = jnp.sum(oh0 * before, axis=0, keepdims=True)
    rank1 = jnp.sum(oh1 * before, axis=0, keepdims=True)
    cnt_ref[...] = cnt_ref[...] + jnp.sum(oh, axis=1, keepdims=True)
    row8 = row[:8]
    route = jnp.zeros((8, tm), F32)
    for k, v in enumerate((i0, i1, rank0, rank1, w0, w1)):
        route = jnp.where(row8 == k, v, route)
    route_ref[0] = route


def _merge(x2d, oa, ob, g1, g2, wg, wa, wb, wo, wrh, wrl, br, tm):
    n = x2d.shape[0]
    width = N_HEADS * HEAD_DIM
    row = lambda w: pl.BlockSpec((tm, w), lambda i: (i, 0))
    full = lambda a: pl.BlockSpec(a.shape, lambda i: (0, 0), pipeline_mode=pl.Buffered(1))
    earlier = jnp.asarray(np.triu(np.ones((tm, tm), np.float32), 1), BF16)
    return pl.pallas_call(
        _merge_kernel,
        grid=(n // tm,),
        in_specs=[row(D_MODEL), row(width), row(width), full(g1), full(g2), full(wg), full(wa), full(wb),
                  full(wo), full(wrh), full(wrl), full(br), full(earlier)],
        out_specs=[row(D_MODEL), pl.BlockSpec((tm * ROW_TILE, LANES), lambda i: (i, 0)),
                   pl.BlockSpec((1, 8, tm), lambda i: (i, 0, 0)), pl.BlockSpec((LANES, 1), lambda i: (0, 0))],
        out_shape=[jax.ShapeDtypeStruct((n, D_MODEL), F32), jax.ShapeDtypeStruct((n * ROW_TILE, LANES), F32),
                   jax.ShapeDtypeStruct((n // tm, 8, tm), F32), jax.ShapeDtypeStruct((LANES, 1), F32)],
        compiler_params=pltpu.CompilerParams(dimension_semantics=("arbitrary",),
                                             vmem_limit_bytes=VMEM_LIMIT),
        name="merge",
    )(x2d, oa, ob, g1, g2, wg, wa, wb, wo, wrh, wrl, br, earlier)


def _slot_table_kernel(pos_ref, init_ref, tab_ref, sem, *, tt, n):
    step = pl.program_id(0)

    @pl.when(step == 0)
    def _():
        cp = pltpu.make_async_copy(init_ref, tab_ref, sem)
        cp.start()
        cp.wait()

    def chunk(c, row):
        base = pl.multiple_of(c * LANES, LANES)
        for k in range(LANES):
            tab_ref[pos_ref[0, 0, base + k]] = row + k * ROW_TILE
            tab_ref[pos_ref[0, 1, base + k]] = row + (n + k) * ROW_TILE
        return row + LANES * ROW_TILE

    lax.fori_loop(0, tt // LANES, chunk, step * (tt * ROW_TILE))


def _slot_pos_kernel(ps_ref, route_ref, pos_ref):
    route = route_ref[0]
    eid = route[0:2].astype(jnp.int32)
    start = jnp.zeros_like(eid)
    for e in range(N_EXPERTS):
        start = jnp.where(eid == e, ps_ref[e], start)
    pos_ref[0] = start + route[2:4].astype(jnp.int32)


def _slot_pos(pad_start, route):
    nt, _, tt = route.shape
    return pl.pallas_call(
        _slot_pos_kernel,
        grid_spec=pltpu.PrefetchScalarGridSpec(
            num_scalar_prefetch=1, grid=(nt,),
            in_specs=[pl.BlockSpec((1, 8, tt), lambda i, ps: (i, 0, 0))],
            out_specs=pl.BlockSpec((1, 2, tt), lambda i, ps: (i, 0, 0))),
        out_shape=jax.ShapeDtypeStruct((nt, 2, tt), jnp.int32),
        compiler_params=pltpu.CompilerParams(dimension_semantics=("parallel",)),
        name="slot_pos",
    )(pad_start, route)


def _slot_table(pos, init, tt):
    nt = pos.shape[0]
    return pl.pallas_call(
        functools.partial(_slot_table_kernel, tt=tt, n=nt * tt),
        grid=(nt,),
        in_specs=[pl.BlockSpec((1, 2, tt), lambda i: (i, 0, 0), memory_space=pltpu.SMEM),
                  pl.BlockSpec(memory_space=pl.ANY)],
        out_specs=pl.BlockSpec(memory_space=pltpu.SMEM),
        out_shape=jax.ShapeDtypeStruct(init.shape, jnp.int32),
        scratch_shapes=[pltpu.SemaphoreType.DMA(())],
        compiler_params=pltpu.CompilerParams(dimension_semantics=("arbitrary",)),
        name="slot_table",
    )(pos, init)


def _moe_kernel(be_ref, nu_ref, tab_ref, tab_next_ref, tab_ahead_ref, hn_ref, wg_ref, wu_ref, wd_ref, out_ref,
                xin, yout, wg_s, wu_s, wd_s, gsem, ssem, *, tb, n):
    b = pl.program_id(0)
    n_used = nu_ref[0]
    live = b < n_used
    s = lax.rem(b, MOE_SLOTS)

    def gather(tab, slot):
        for j in range(tb):
            t, rows = tab[0, 0, j], n * ROW_TILE
            src = jnp.bitwise_and(t, rows - 1) if rows & (rows - 1) == 0 else lax.rem(t, rows)
            pltpu.make_async_copy(hn_ref.at[_tile_at(src)], xin.at[slot, pl.ds(j * ROW_TILE, ROW_TILE)],
                                  gsem.at[slot]).start(priority=j % 2)

    def wait_rows(sem_ref, slot):
        pltpu.make_async_copy(hn_ref.at[pl.ds(0, tb * ROW_TILE)], xin.at[slot], sem_ref.at[slot]).wait()

    @pl.when(b == 0)
    def _():
        yout[...] = jnp.zeros_like(yout)
        fills = [pltpu.make_async_copy(yout.at[k], out_ref.at[pl.ds((2 * n + k * tb) * ROW_TILE, tb * ROW_TILE)],
                                       ssem.at[k]) for k in range(MOE_SLOTS)]
        for cp in fills:
            cp.start()
        for cp in fills:
            cp.wait()
        gather(tab_ref, 0)
        gather(tab_next_ref, 1)

    @pl.when(live & ((b == 0) | (be_ref[b] != be_ref[jnp.maximum(b - 1, 0)])))
    def _():
        wg_s[...] = wg_ref[0].astype(BF16)
        wu_s[...] = wu_ref[0].astype(BF16)
        wd_s[...] = wd_ref[0].astype(BF16)

    @pl.when(live & (b >= MOE_SLOTS))
    def _():
        wait_rows(ssem, s)

    def block(s):
        wait_rows(gsem, s)
        gather(tab_ahead_ref, (s + 2) % MOE_SLOTS)
        half = tb // 2
        gu = []
        for i in range(2):
            xb = jnp.concatenate([xin[s, pl.ds(i * half * ROW_TILE + j, half, stride=ROW_TILE), :]
                                  for j in range(ROW_TILE)], axis=1).astype(BF16)
            gu.append((_dot(xb, wg_s[...]), _dot(xb, wu_s[...])))
        for i, (g, u) in enumerate(gu):
            y = _dot((g * jax.nn.sigmoid(g) * u).astype(BF16), wd_s[...])
            for j in range(ROW_TILE):
                yout[s, pl.ds(i * half * ROW_TILE + j, half, stride=ROW_TILE), :] = y[:, j * LANES:(j + 1) * LANES]

        for j in range(tb):
            pltpu.make_async_copy(yout.at[s, pl.ds(j * ROW_TILE, ROW_TILE)], out_ref.at[_tile_at(tab_ref[0, 0, j])],
                                  ssem.at[s]).start(priority=j % 2)

    for slot in range(MOE_SLOTS):
        pl.when(live & (s == slot))(functools.partial(block, slot))

    @pl.when(live & (b == n_used - 1))
    def _():
        for k in (1, 2):
            wait_rows(gsem, lax.rem(s + k, MOE_SLOTS))
        wait_rows(ssem, s)
        for k in (1, 2):

            @pl.when(b >= k)
            def _():
                wait_rows(ssem, lax.rem(s + MOE_SLOTS - k, MOE_SLOTS))


def _moe(block_e, n_used, tab, hn, wg, wu, wd, tb, n):
    nblk = tab.shape[0] // tb
    tab2 = tab.reshape(nblk, 1, tb)
    live = lambda b, be, nu: jnp.minimum(b, nu[0] - 1)
    ahead = lambda k: (lambda b, be, nu: (jnp.minimum(b + k, nu[0] - 1), 0, 0))
    wspec = lambda shape: pl.BlockSpec((1,) + shape, lambda b, be, nu: (be[live(b, be, nu)], 0, 0))
    return pl.pallas_call(
        functools.partial(_moe_kernel, tb=tb, n=n),
        grid_spec=pltpu.PrefetchScalarGridSpec(
            num_scalar_prefetch=2, grid=(nblk,),
            in_specs=[pl.BlockSpec((1, 1, tb), ahead(0), memory_space=pltpu.SMEM),
                      pl.BlockSpec((1, 1, tb), ahead(1), memory_space=pltpu.SMEM),
                      pl.BlockSpec((1, 1, tb), ahead(2), memory_space=pltpu.SMEM),
                      pl.BlockSpec(memory_space=pl.ANY),
                      wspec((D_MODEL, EXPERT_FF)), wspec((D_MODEL, EXPERT_FF)), wspec((EXPERT_FF, D_MODEL))],
            out_specs=pl.BlockSpec(memory_space=pl.ANY),
            scratch_shapes=[pltpu.VMEM((MOE_SLOTS, tb * ROW_TILE, LANES), F32),
                            pltpu.VMEM((MOE_SLOTS, tb * ROW_TILE, LANES), F32),
                            pltpu.VMEM((D_MODEL, EXPERT_FF), BF16), pltpu.VMEM((D_MODEL, EXPERT_FF), BF16),
                            pltpu.VMEM((EXPERT_FF, D_MODEL), BF16),
                            pltpu.SemaphoreType.DMA((MOE_SLOTS,)), pltpu.SemaphoreType.DMA((MOE_SLOTS,))]),
        out_shape=jax.ShapeDtypeStruct(((2 * n + MOE_SLOTS * tb) * ROW_TILE, LANES), F32),
        compiler_params=pltpu.CompilerParams(dimension_semantics=("arbitrary",),
                                             vmem_limit_bytes=VMEM_LIMIT),
        name="moe",
    )(block_e, n_used, tab2, tab2, tab2, hn, wg, wu, wd)


def _final_kernel(x2_ref, w_ref, y0_ref, y1_ref, gf_ref, o_ref):
    tc = x2_ref.shape[0]
    w = w_ref[...]
    y = x2_ref[...] + (w[:, 0:1] * _tiles_to_rows(y0_ref, tc) + w[:, 1:2] * _tiles_to_rows(y1_ref, tc))
    o_ref[...] = _rms(y, gf_ref[...])


def _final(x2, w_slot, yslots, gf, tc):
    n = x2.shape[0]
    nt = n // tc
    return pl.pallas_call(
        _final_kernel,
        grid=(nt,),
        in_specs=[pl.BlockSpec((tc, D_MODEL), lambda i: (i, 0)),
                  pl.BlockSpec((tc, 2), lambda i: (i, 0)),
                  pl.BlockSpec((tc * ROW_TILE, LANES), lambda i: (i, 0)),
                  pl.BlockSpec((tc * ROW_TILE, LANES), lambda i: (nt + i, 0)),
                  pl.BlockSpec((1, D_MODEL), lambda i: (0, 0))],
        out_specs=pl.BlockSpec((tc, D_MODEL), lambda i: (i, 0)),
        out_shape=jax.ShapeDtypeStruct((n, D_MODEL), F32),
        compiler_params=pltpu.CompilerParams(dimension_semantics=("parallel",),
                                             vmem_limit_bytes=VMEM_LIMIT),
        name="final",
    )(x2, w_slot, yslots, yslots, gf)


def _tile_at(row):
    return pl.ds(pl.multiple_of(row, ROW_TILE), ROW_TILE)


def _overlap_matrix_t(seq):
    nc = seq // CMP_STRIDE - CMP_LEN // CMP_STRIDE + 1
    ns = seq // SLC_LEN
    c0 = np.arange(nc) * CMP_STRIDE
    s0 = np.arange(ns) * SLC_LEN
    ov = np.clip(np.minimum(c0[:, None] + CMP_LEN, s0[None, :] + SLC_LEN)
                 - np.maximum(c0[:, None], s0[None, :]), 0, None) / CMP_LEN
    out = np.zeros((LANES, LANES), np.float32)
    out[FEAT_SEL:FEAT_SEL + ns, :nc] = ov.T
    return jnp.asarray(out, BF16)


def _position_features(seq):
    pos = np.arange(seq)
    fk = np.zeros((seq, LANES), np.float32)
    fk[:, FEAT_POS:FEAT_POS + 3] = (pos // SLC_LEN)[:, None]
    fk[:, FEAT_POS + 3:FEAT_POS + 6] = (pos % SLC_LEN)[:, None]
    fk[pos, FEAT_SEL + pos // SLC_LEN] = 1.0
    return jnp.asarray(fk)


def _pick_tile(n, pref):
    t = pref
    while n % t:
        t //= 2
    return t


def kernel(x, norm_mix_g, w_in, cmp_pos_k, cmp_w1_k, cmp_w2_k, cmp_pos_v, cmp_w1_v, cmp_w2_v, sinks, w_a, w_b,
           w_o, norm_ffn_g, w_group, b_group, w_expert, b_expert, w_gate_e, w_up_e, w_down_e, norm_final_g):
    bsz, seq, _ = x.shape
    n = bsz * seq
    assert TQ == LANES and seq % KCHUNK == 0 and seq // SLC_LEN <= N_SLC_BLK and seq // CMP_STRIDE <= LANES
    assert TILES_PER_STEP * TQ == KCHUNK and TQ % SUB == 0
    assert seq >= (NSA_WINDOW // TQ + 1) * TQ and w_in.shape[0] == 1
    x2d = x.reshape(n, D_MODEL)

    scale = HEAD_DIM ** -0.5 * LOG2E
    nsa_w, kvw = N_HEADS * HEAD_DIM, N_KV * HEAD_DIM
    o_qa, o_kva, o_gn = 0, nsa_w, nsa_w + 6 * kvw
    o_qb = o_gn + 3 * N_HEADS
    o_kvb = o_qb + nsa_w
    o_gm = o_kvb + 2 * kvw
    pieces = [(o_qa, nsa_w, scale), (o_qb, nsa_w, scale),
              (o_kva + 2 * kvw, kvw, 1.0), (o_kva + 4 * kvw, kvw, 1.0), (o_kvb, kvw, 1.0),
              (o_gn, 3 * N_HEADS, 1.0), (None, LANES - 3 * N_HEADS, 0.0),
              (o_kva + 3 * kvw, kvw, 1.0), (o_kva + 5 * kvw, kvw, 1.0), (o_kvb + kvw, kvw, 1.0),
              (o_kva, 2 * kvw, 1.0)]
    w_attn, w_gm = _weight_prep(w_in, pieces, (o_gm, 2 * D_MODEL))

    tm = _pick_tile(seq, 512)
    feat_k = _position_features(seq)
    eye = jnp.eye(LANES, dtype=BF16)
    swap = jnp.roll(eye, HEAD_DIM, axis=1)
    proj, v_t, cmp_in = _proj(x2d, norm_mix_g[0][None], w_attn, feat_k, eye, swap, tm)
    proj3 = proj.reshape(bsz, seq, PROJ_W)

    nch = seq // CMP_STRIDE
    pos = jnp.stack([cmp_pos_k[0], cmp_pos_v[0]])
    pos = jnp.broadcast_to(pos[:, :, None, :], (2, CMP_LEN, N_KV, HEAD_DIM))
    pos_a = pos[:, :CMP_STRIDE].reshape(2, 1, CMP_STRIDE * kvw)
    pos_b = pos[:, CMP_STRIDE:].reshape(2, 1, CMP_STRIDE * kvw)
    w1 = jnp.stack([cmp_w1_k[0], cmp_w1_v[0]]).reshape(2, CMP_LEN, HEAD_DIM, CMP_HIDDEN)
    zero = jnp.zeros_like(w1)
    w1 = jnp.stack([jnp.concatenate([w1, zero], axis=-1), jnp.concatenate([zero, w1], axis=-1)], axis=2)
    w1 = w1.reshape(2, CMP_LEN * kvw, N_KV * CMP_HIDDEN).astype(BF16)
    w2 = jnp.pad(jnp.stack([cmp_w2_k[0], cmp_w2_v[0]]), ((0, 0), (0, 0), (0, LANES - HEAD_DIM))).astype(BF16)
    kvc, kvc_t = _compress(cmp_in.reshape(bsz, seq, CMP_W), pos_a, pos_b, w1[:, :CMP_STRIDE * kvw],
                           w1[:, CMP_STRIDE * kvw:], w2, jnp.swapaxes(w2, 1, 2))
    kvc = jnp.pad(kvc, ((0, 0), (0, 0), (0, 0), (0, LANES - nch), (0, 0)))
    kvc_t = jnp.pad(kvc_t, ((0, 0), (0, 0), (0, 0), (0, 0), (0, LANES - nch)))

    o_a, o_b = _attention(proj3, v_t, kvc, kvc_t, _overlap_matrix_t(seq), eye, sinks[0])

    w_r = jnp.concatenate([w_expert[0], w_group[0],
                           jnp.zeros((D_MODEL, LANES - N_EXPERTS - N_GROUPS), F32)], axis=1)
    w_r = w_r.T
    w_rh = w_r.astype(BF16)
    w_rl = (w_r - w_rh.astype(F32)).astype(BF16)
    b_r = jnp.concatenate([b_expert[0], b_group[0], jnp.zeros((LANES - N_EXPERTS - N_GROUPS,), F32)])
    b_r = b_r[:, None] * jnp.ones((1, LANES), F32)
    tt = _pick_tile(n, 1024)
    x2, hn, route, counts = _merge(
        x2d, o_a.reshape(n, nsa_w), o_b.reshape(n, nsa_w), norm_mix_g[0][None], norm_ffn_g[0][None], w_gm,
        w_a[0].astype(BF16), w_b[0].astype(BF16), w_o[0].astype(BF16), w_rh, w_rl, b_r, tt)

    tb = 256
    nblk = -(-(2 * n + N_EXPERTS * (tb - 1)) // tb)
    cnt = counts[:N_EXPERTS, 0].astype(jnp.int32)
    padded = (cnt + tb - 1) // tb * tb
    pad_end = jnp.cumsum(padded)
    pad_start = pad_end - padded
    block_e = jnp.minimum(jnp.sum(pad_end[None, :] <= (jnp.arange(nblk) * tb)[:, None], axis=1), N_EXPERTS - 1)
    n_used = (pad_end[-1:] // tb).astype(jnp.int32)
    pos = _slot_pos(pad_start, route)
    w_slot = jnp.swapaxes(route[:, 4:6, :], 1, 2).reshape(n, 2)
    spare = (2 * n + jnp.arange(nblk * tb, dtype=jnp.int32) % (MOE_SLOTS * tb)) * ROW_TILE
    slot_tab = _slot_table(pos, spare, tt)
    y_slots = _moe(block_e.astype(jnp.int32), n_used, slot_tab, hn, w_gate_e[0], w_up_e[0], w_down_e[0], tb, n)
    out = _final(x2, w_slot, y_slots, norm_final_g[None], tt)
    return out.reshape(bsz, seq, D_MODEL)
```
